```python
import math
import jax, jax.numpy as jnp
from jax import lax
import numpy as np

D_MODEL = 1024
BATCH = 8
SEQ = 16384
DEPTH = 4

HEAD_DIM = 64
BLOCK = 128
EPS = 1e-6
NEG = -1e30
A_Q_HEADS = 8
A_KV_HEADS = 2
A_GROUP = A_Q_HEADS // A_KV_HEADS
A_WINDOW = 128
B_BRANCHES = ((128, 1), (512, 4), (2048, 16))
B_HEADS_PER_BRANCH = 4
B_HEADS = len(B_BRANCHES) * B_HEADS_PER_BRANCH
NUM_BUCKETS = 32
MAX_DISTANCE = 2048
N_BIAS_HEADS = A_Q_HEADS + B_HEADS
A_IN = (A_Q_HEADS + 2 * A_KV_HEADS) * HEAD_DIM
B_IN = 3 * B_HEADS * HEAD_DIM
AB_IN = A_IN + B_IN
AB_OUT = (A_Q_HEADS + B_HEADS_PER_BRANCH) * HEAD_DIM
C_HEADS = 8
C_NOPE = 64
C_ROPE = 32
C_V = 64
C_Q_RANK = 384
C_KV_RANK = 256
C_DOWN = C_Q_RANK + C_KV_RANK + C_ROPE
ROPE_THETA = 10000.0
D_FF = 4 * D_MODEL
N_EVEN = (DEPTH + 1) // 2
N_ODD = DEPTH // 2

kernel_name = 'hybrid_swa_dilated_mla_trunk'


def rmsnorm(x, g):
    xf = x.astype(jnp.float32)
    y = xf * lax.rsqrt(jnp.mean(xf * xf, axis=-1, keepdims=True) + EPS)
    return (y * g.astype(jnp.float32)).astype(x.dtype)


def t5_bucket(n):
    max_exact = NUM_BUCKETS // 2
    nf = jnp.maximum(n, 1).astype(jnp.float32)
    large = max_exact + (jnp.log(nf / max_exact) / math.log(MAX_DISTANCE / max_exact)
                         * (NUM_BUCKETS - max_exact)).astype(jnp.int32)
    return jnp.where(n < max_exact, n, jnp.minimum(large, NUM_BUCKETS - 1))


def band_bias(table, dilation, max_dist):
    i = jnp.arange(BLOCK)[:, None]
    j = jnp.arange(2 * BLOCK)[None, :]
    dist = i + BLOCK - j
    inband = (dist >= 0) & (dist <= max_dist)
    bias = table[t5_bucket(jnp.maximum(dist, 0) * dilation)].astype(jnp.float32)
    bias = jnp.where(inband[..., None], bias, NEG)
    return bias.transpose(2, 0, 1)


def banded_attention(q, k, v, bias, sinks=None, with_lse=False):
    Bn, L, KVH, G, dh = q.shape
    n = L // BLOCK
    qb = q.reshape(Bn, n, BLOCK, KVH, G, dh)

    def two_blocks(t):
        cur = t.reshape(Bn, n, BLOCK, KVH, dh)
        prev = jnp.pad(cur, ((0, 0), (1, 0), (0, 0), (0, 0), (0, 0)))[:, :-1]
        return jnp.concatenate([prev, cur], axis=2)

    kb, vb = two_blocks(k), two_blocks(v)
    logits = jnp.einsum('bnqhgd,bnkhd->bnhgqk', qb, kb).astype(jnp.float32) * (dh ** -0.5)
    logits = logits + bias.reshape(KVH, G, BLOCK, 2 * BLOCK)
    first = (jnp.arange(n)[:, None] == 0) & (jnp.arange(2 * BLOCK)[None, :] < BLOCK)
    logits = jnp.where(first[None, :, None, None, None, :], NEG, logits)
    m = logits.max(axis=-1)
    if sinks is not None:
        s = sinks.astype(jnp.float32).reshape(KVH, G)[None, None, :, :, None]
        m = jnp.maximum(m, s)
    p = jnp.exp(logits - m[..., None])
    l = p.sum(axis=-1)
    if sinks is not None:
        l = l + jnp.exp(s - m)
    acc = jnp.einsum('bnhgqk,bnkhd->bnqhgd', p.astype(v.dtype), vb)
    o = (acc / l.transpose(0, 1, 4, 2, 3)[..., None].astype(v.dtype)).reshape(Bn, L, KVH, G, dh)
    if with_lse:
        lse = (m + jnp.log(l)).transpose(0, 1, 4, 2, 3).reshape(Bn, L, KVH, G)
        return o, lse
    return o


def dilate(t, d):
    Bn, S = t.shape[:2]
    L = S // d
    t = jnp.moveaxis(t.reshape((Bn, L, d) + t.shape[2:]), 2, 1).reshape((Bn * d, L) + t.shape[2:])
    pad = (-L) % BLOCK
    return jnp.pad(t, [(0, 0), (0, pad)] + [(0, 0)] * (t.ndim - 2))


def undilate(t, Bn, S, d):
    L = S // d
    t = t[:, :L].reshape((Bn, d, L) + t.shape[2:])
    return jnp.moveaxis(t, 1, 2).reshape((Bn, S) + t.shape[3:])


def even_mixer(xn, w_in, sinks, w_out, bias_a, bias_b):
    Bn, S, _ = xn.shape
    proj = xn @ w_in
    pa, pb = proj[..., :A_IN], proj[..., A_IN:]
    qd, kd = A_Q_HEADS * HEAD_DIM, A_KV_HEADS * HEAD_DIM
    qa = pa[..., :qd].reshape(Bn, S, A_KV_HEADS, A_GROUP, HEAD_DIM)
    ka = pa[..., qd:qd + kd].reshape(Bn, S, A_KV_HEADS, HEAD_DIM)
    va = pa[..., qd + kd:].reshape(Bn, S, A_KV_HEADS, HEAD_DIM)
    out_a = banded_attention(qa, ka, va, bias_a, sinks=sinks).reshape(Bn, S, qd)

    qkv_b = pb.reshape(Bn, S, len(B_BRANCHES), 3, B_HEADS_PER_BRANCH, HEAD_DIM)
    outs, lses = [], []
    for g, (window, dil) in enumerate(B_BRANCHES):
        q, k, v = (dilate(qkv_b[:, :, g, i], dil) for i in range(3))
        o, lse = banded_attention(q[:, :, :, None, :], k, v, bias_b[g], with_lse=True)
        outs.append(undilate(o[:, :, :, 0], Bn, S, dil))
        lses.append(undilate(lse[..., 0], Bn, S, dil))
    wts = jax.nn.softmax(jnp.stack(lses), axis=0)
    out_b = jnp.einsum('gbsh,gbshd->bshd', wts.astype(xn.dtype), jnp.stack(outs)).reshape(Bn, S, -1)
    return jnp.concatenate([out_a, out_b], axis=-1) @ w_out


def rope(t):
    S, r = t.shape[1], t.shape[-1]
    inv = ROPE_THETA ** (-jnp.arange(0, r, 2, dtype=jnp.float32) / r)
    ang = jnp.arange(S, dtype=jnp.float32)[:, None] * inv[None, :]
    shape = (1, S) + (1,) * (t.ndim - 3) + (r // 2,)
    cos, sin = jnp.cos(ang).reshape(shape), jnp.sin(ang).reshape(shape)
    t1, t2 = t[..., :r // 2].astype(jnp.float32), t[..., r // 2:].astype(jnp.float32)
    return jnp.concatenate([t1 * cos - t2 * sin, t1 * sin + t2 * cos], axis=-1).astype(t.dtype)


def causal_mla_attention(q_nope, q_rope, k_nope, k_rope, v):
    Bn, S, H, _ = q_nope.shape
    n = S // BLOCK
    scale = (C_NOPE + C_ROPE) ** -0.5
    kpos = jnp.arange(S)

    def block(args):
        qn, qr, i = args
        logits = (jnp.einsum('bqhd,bkhd->bhqk', qn, k_nope)
                  + jnp.einsum('bqhr,bkr->bhqk', qr, k_rope)).astype(jnp.float32) * scale
        qpos = i * BLOCK + jnp.arange(BLOCK)
        logits = jnp.where(kpos[None, :] <= qpos[:, None], logits, NEG)
        p = jax.nn.softmax(logits, axis=-1).astype(v.dtype)
        return jnp.einsum('bhqk,bkhd->bqhd', p, v)

    qn_b = q_nope.reshape(Bn, n, BLOCK, H, C_NOPE).transpose(1, 0, 2, 3, 4)
    qr_b = q_rope.reshape(Bn, n, BLOCK, H, C_ROPE).transpose(1, 0, 2, 3, 4)
    out = lax.map(block, (qn_b, qr_b, jnp.arange(n)))
    return out.transpose(1, 0, 2, 3, 4).reshape(Bn, S, H * C_V)


def mla_mixer(xn, w_down, q_norm, w_uq, kv_norm, w_ukv, w_o):
    Bn, S, _ = xn.shape
    down = xn @ w_down
    c_q = down[..., :C_Q_RANK]
    c_kv = down[..., C_Q_RANK:C_Q_RANK + C_KV_RANK]
    k_rope = rope(down[..., C_Q_RANK + C_KV_RANK:])
    q = (rmsnorm(c_q, q_norm) @ w_uq).reshape(Bn, S, C_HEADS, C_NOPE + C_ROPE)
    kv = (rmsnorm(c_kv, kv_norm) @ w_ukv).reshape(Bn, S, C_HEADS, C_NOPE + C_V)
    out = causal_mla_attention(q[..., :C_NOPE], rope(q[..., C_NOPE:]),
                               kv[..., :C_NOPE], k_rope, kv[..., C_NOPE:])
    return out @ w_o


def squared_relu_mlp(xn, w_up, w_down):
    h = jax.nn.relu(xn @ w_up)
    return (h * h) @ w_down


def _fwd_setup_inputs(seed: int = 0) -> dict:
    key = jax.random.key(seed)
    ks = jax.random.split(key, 16)

    def w(k, shape, fan_in):
        return jax.random.normal(k, shape, jnp.float32) * fan_in ** -0.5

    def gain(k, shape):
        return 1.0 + 0.05 * jax.random.normal(k, shape, jnp.float32)

    return {
        'x': jax.random.normal(ks[0], (BATCH, SEQ, D_MODEL), jnp.float32),
        'rel_bias': 0.5 * jax.random.normal(ks[1], (NUM_BUCKETS, N_BIAS_HEADS), jnp.float32),
        'attn_norm': gain(ks[2], (DEPTH, D_MODEL)),
        'mlp_norm': gain(ks[3], (DEPTH, D_MODEL)),
        'final_norm': gain(ks[4], (D_MODEL,)),
        'w_in_ab': w(ks[5], (N_EVEN, D_MODEL, AB_IN), D_MODEL),
        'sinks': jax.random.normal(ks[6], (N_EVEN, A_Q_HEADS), jnp.float32),
        'w_out_ab': w(ks[7], (N_EVEN, AB_OUT, D_MODEL), AB_OUT),
        'w_down_c': w(ks[8], (N_ODD, D_MODEL, C_DOWN), D_MODEL),
        'q_norm_c': gain(ks[9], (N_ODD, C_Q_RANK)),
        'w_uq_c': w(ks[10], (N_ODD, C_Q_RANK, C_HEADS * (C_NOPE + C_ROPE)), C_Q_RANK),
        'kv_norm_c': gain(ks[11], (N_ODD, C_KV_RANK)),
        'w_ukv_c': w(ks[12], (N_ODD, C_KV_RANK, C_HEADS * (C_NOPE + C_V)), C_KV_RANK),
        'w_o_c': w(ks[13], (N_ODD, C_HEADS * C_V, D_MODEL), C_HEADS * C_V),
        'w_mlp_up': w(ks[14], (DEPTH, D_MODEL, D_FF), D_MODEL),
        'w_mlp_down': w(ks[15], (DEPTH, D_FF, D_MODEL), D_FF),
    }


def _fwd_reference(x, rel_bias, attn_norm, mlp_norm, final_norm, w_in_ab, sinks, w_out_ab,
              w_down_c, q_norm_c, w_uq_c, kv_norm_c, w_ukv_c, w_o_c, w_mlp_up, w_mlp_down):
    bias_a = band_bias(rel_bias[:, :A_Q_HEADS], 1, A_WINDOW - 1)
    bias_b = [band_bias(rel_bias[:, A_Q_HEADS + g * B_HEADS_PER_BRANCH:A_Q_HEADS + (g + 1) * B_HEADS_PER_BRANCH],
                        dil, window // dil)
              for g, (window, dil) in enumerate(B_BRANCHES)]
    h = x
    for layer in range(DEPTH):
        xn = rmsnorm(h, attn_norm[layer])
        if layer % 2 == 0:
            e = layer // 2
            mix = even_mixer(xn, w_in_ab[e], sinks[e], w_out_ab[e], bias_a, bias_b)
        else:
            o = layer // 2
            mix = mla_mixer(xn, w_down_c[o], q_norm_c[o], w_uq_c[o], kv_norm_c[o], w_ukv_c[o], w_o_c[o])
        h = h + mix
        h = h + squared_relu_mlp(rmsnorm(h, mlp_norm[layer]), w_mlp_up[layer], w_mlp_down[layer])
    return rmsnorm(h, final_norm)


import jax as _jax
import jax.numpy as _jnp

TWIN_FORMAT = 'train_step'
FWD_PARAMS = ['x', 'rel_bias', 'attn_norm', 'mlp_norm', 'final_norm', 'w_in_ab', 'sinks', 'w_out_ab', 'w_down_c', 'q_norm_c', 'w_uq_c', 'kv_norm_c', 'w_ukv_c', 'w_o_c', 'w_mlp_up', 'w_mlp_down']
TWIN_WEIGHTS = ['rel_bias', 'attn_norm', 'mlp_norm', 'final_norm', 'w_in_ab', 'sinks', 'w_out_ab', 'w_down_c', 'q_norm_c', 'w_uq_c', 'kv_norm_c', 'w_ukv_c', 'w_o_c', 'w_mlp_up', 'w_mlp_down']
TWIN_DIFF_INPUT = 'x'
TWIN_INPUTS = ['x', 'rel_bias', 'attn_norm', 'mlp_norm', 'final_norm', 'w_in_ab', 'sinks', 'w_out_ab', 'w_down_c', 'q_norm_c', 'w_uq_c', 'kv_norm_c', 'w_ukv_c', 'w_o_c', 'w_mlp_up', 'w_mlp_down', 'loss_target', 'm_rel_bias', 'm_attn_norm', 'm_mlp_norm', 'm_final_norm', 'm_w_in_ab', 'm_sinks', 'm_w_out_ab', 'm_w_down_c', 'm_q_norm_c', 'm_w_uq_c', 'm_kv_norm_c', 'm_w_ukv_c', 'm_w_o_c', 'm_w_mlp_up', 'm_w_mlp_down', 'v_rel_bias', 'v_attn_norm', 'v_mlp_norm', 'v_final_norm', 'v_w_in_ab', 'v_sinks', 'v_w_out_ab', 'v_w_down_c', 'v_q_norm_c', 'v_w_uq_c', 'v_kv_norm_c', 'v_w_ukv_c', 'v_w_o_c', 'v_w_mlp_up', 'v_w_mlp_down']
TWIN_OUTPUTS = ['loss', 'grad_x', 'grad_rel_bias', 'grad_attn_norm', 'grad_mlp_norm', 'grad_final_norm', 'grad_w_in_ab', 'grad_sinks', 'grad_w_out_ab', 'grad_w_down_c', 'grad_q_norm_c', 'grad_w_uq_c', 'grad_kv_norm_c', 'grad_w_ukv_c', 'grad_w_o_c', 'grad_w_mlp_up', 'grad_w_mlp_down', 'delta_rel_bias', 'delta_attn_norm', 'delta_mlp_norm', 'delta_final_norm', 'delta_w_in_ab', 'delta_sinks', 'delta_w_out_ab', 'delta_w_down_c', 'delta_q_norm_c', 'delta_w_uq_c', 'delta_kv_norm_c', 'delta_w_ukv_c', 'delta_w_o_c', 'delta_w_mlp_up', 'delta_w_mlp_down', 'new_m_rel_bias', 'new_m_attn_norm', 'new_m_mlp_norm', 'new_m_final_norm', 'new_m_w_in_ab', 'new_m_sinks', 'new_m_w_out_ab', 'new_m_w_down_c', 'new_m_q_norm_c', 'new_m_w_uq_c', 'new_m_kv_norm_c', 'new_m_w_ukv_c', 'new_m_w_o_c', 'new_m_w_mlp_up', 'new_m_w_mlp_down', 'new_v_rel_bias', 'new_v_attn_norm', 'new_v_mlp_norm', 'new_v_final_norm', 'new_v_w_in_ab', 'new_v_sinks', 'new_v_w_out_ab', 'new_v_w_down_c', 'new_v_q_norm_c', 'new_v_w_uq_c', 'new_v_kv_norm_c', 'new_v_w_ukv_c', 'new_v_w_o_c', 'new_v_w_mlp_up', 'new_v_w_mlp_down']
TWIN_LEAF_KINDS = {'loss': 'loss', 'grad_x': 'grad_x', 'grad_rel_bias': 'grad_w', 'grad_attn_norm': 'grad_w', 'grad_mlp_norm': 'grad_w', 'grad_final_norm': 'grad_w', 'grad_w_in_ab': 'grad_w', 'grad_sinks': 'grad_w', 'grad_w_out_ab': 'grad_w', 'grad_w_down_c': 'grad_w', 'grad_q_norm_c': 'grad_w', 'grad_w_uq_c': 'grad_w', 'grad_kv_norm_c': 'grad_w', 'grad_w_ukv_c': 'grad_w', 'grad_w_o_c': 'grad_w', 'grad_w_mlp_up': 'grad_w', 'grad_w_mlp_down': 'grad_w', 'delta_rel_bias': 'delta_w', 'delta_attn_norm': 'delta_w', 'delta_mlp_norm': 'delta_w', 'delta_final_norm': 'delta_w', 'delta_w_in_ab': 'delta_w', 'delta_sinks': 'delta_w', 'delta_w_out_ab': 'delta_w', 'delta_w_down_c': 'delta_w', 'delta_q_norm_c': 'delta_w', 'delta_w_uq_c': 'delta_w', 'delta_kv_norm_c': 'delta_w', 'delta_w_ukv_c': 'delta_w', 'delta_w_o_c': 'delta_w', 'delta_w_mlp_up': 'delta_w', 'delta_w_mlp_down': 'delta_w', 'new_m_rel_bias': 'new_m', 'new_m_attn_norm': 'new_m', 'new_m_mlp_norm': 'new_m', 'new_m_final_norm': 'new_m', 'new_m_w_in_ab': 'new_m', 'new_m_sinks': 'new_m', 'new_m_w_out_ab': 'new_m', 'new_m_w_down_c': 'new_m', 'new_m_q_norm_c': 'new_m', 'new_m_w_uq_c': 'new_m', 'new_m_kv_norm_c': 'new_m', 'new_m_w_ukv_c': 'new_m', 'new_m_w_o_c': 'new_m', 'new_m_w_mlp_up': 'new_m', 'new_m_w_mlp_down': 'new_m', 'new_v_rel_bias': 'new_v', 'new_v_attn_norm': 'new_v', 'new_v_mlp_norm': 'new_v', 'new_v_final_norm': 'new_v', 'new_v_w_in_ab': 'new_v', 'new_v_sinks': 'new_v', 'new_v_w_out_ab': 'new_v', 'new_v_w_down_c': 'new_v', 'new_v_q_norm_c': 'new_v', 'new_v_w_uq_c': 'new_v', 'new_v_kv_norm_c': 'new_v', 'new_v_w_ukv_c': 'new_v', 'new_v_w_o_c': 'new_v', 'new_v_w_mlp_up': 'new_v', 'new_v_w_mlp_down': 'new_v'}


def _forward(args):
    return _fwd_reference(*[args[k] for k in FWD_PARAMS])


def _output_shape():
    def fwd():
        inp = _fwd_setup_inputs(0)
        return _fwd_reference(*[inp[k] for k in FWD_PARAMS])
    out = _jax.eval_shape(fwd)
    return out.shape, out.dtype

N_MICROBATCH = 1
ADAM_LR = 0.001
ADAM_B1 = 0.9
ADAM_B2 = 0.999
ADAM_EPS = 1e-08
ADAM_WD = 0.01
ADAM_STEP = 10
PER_EXAMPLE_BATCH_AXIS = {'x': 0, 'loss_target': 0}
SHARED_INPUTS = []
_WEIGHT_DTYPES = {'rel_bias': _jnp.float32, 'attn_norm': _jnp.float32, 'mlp_norm': _jnp.float32, 'final_norm': _jnp.float32, 'w_in_ab': _jnp.float32, 'sinks': _jnp.float32, 'w_out_ab': _jnp.float32, 'w_down_c': _jnp.float32, 'q_norm_c': _jnp.float32, 'w_uq_c': _jnp.float32, 'kv_norm_c': _jnp.float32, 'w_ukv_c': _jnp.float32, 'w_o_c': _jnp.float32, 'w_mlp_up': _jnp.float32, 'w_mlp_down': _jnp.float32}
MOMENT_SCALE = {'rel_bias': 9.652084e-02, 'attn_norm': 6.292493e-01, 'mlp_norm': 5.577026e-01, 'final_norm': 1.353183e+02, 'w_in_ab': 2.120671e-01, 'sinks': 9.798491e-02, 'w_out_ab': 5.628743e-01, 'w_down_c': 1.215744e+00, 'q_norm_c': 5.327357e-02, 'w_uq_c': 3.686134e-02, 'kv_norm_c': 2.252320e+00, 'w_ukv_c': 1.043376e+00, 'w_o_c': 9.479698e-01, 'w_mlp_up': 2.693922e-01, 'w_mlp_down': 1.321106e+00}


def _to_microbatches(a, axis):
    t = _jnp.moveaxis(a, axis, 0)
    t = t.reshape((N_MICROBATCH, t.shape[0] // N_MICROBATCH) + t.shape[1:])
    return _jnp.moveaxis(t, 1, axis + 1)


def setup_inputs(seed: int = 0) -> dict:
    inp = _fwd_setup_inputs(seed)
    key = _jax.random.fold_in(_jax.random.key(seed), 7919)
    shape, _ = _output_shape()
    out = dict(inp)
    out["loss_target"] = _jax.random.normal(_jax.random.fold_in(key, 0), shape, _jnp.float32)
    for i, name in enumerate(TWIN_WEIGHTS):
        w = inp[name].astype(_jnp.float32)
        if MOMENT_SCALE is None:
            s = _jnp.sqrt(_jnp.mean(_jnp.square(w)) + 1e-30)
        else:
            s = MOMENT_SCALE[name]
        km, kv = _jax.random.split(_jax.random.fold_in(key, i + 1))
        out[name] = w
        out["m_" + name] = s * _jax.random.normal(km, w.shape, _jnp.float32)
        out["v_" + name] = (s * s) * _jax.random.uniform(kv, w.shape, _jnp.float32, 0.5, 1.5)
    if N_MICROBATCH > 1:
        for name, axis in PER_EXAMPLE_BATCH_AXIS.items():
            out[name] = _to_microbatches(out[name], axis)
    return {'x': out['x'], 'rel_bias': out['rel_bias'], 'attn_norm': out['attn_norm'], 'mlp_norm': out['mlp_norm'], 'final_norm': out['final_norm'], 'w_in_ab': out['w_in_ab'], 'sinks': out['sinks'], 'w_out_ab': out['w_out_ab'], 'w_down_c': out['w_down_c'], 'q_norm_c': out['q_norm_c'], 'w_uq_c': out['w_uq_c'], 'kv_norm_c': out['kv_norm_c'], 'w_ukv_c': out['w_ukv_c'], 'w_o_c': out['w_o_c'], 'w_mlp_up': out['w_mlp_up'], 'w_mlp_down': out['w_mlp_down'], 'loss_target': out['loss_target'], 'm_rel_bias': out['m_rel_bias'], 'm_attn_norm': out['m_attn_norm'], 'm_mlp_norm': out['m_mlp_norm'], 'm_final_norm': out['m_final_norm'], 'm_w_in_ab': out['m_w_in_ab'], 'm_sinks': out['m_sinks'], 'm_w_out_ab': out['m_w_out_ab'], 'm_w_down_c': out['m_w_down_c'], 'm_q_norm_c': out['m_q_norm_c'], 'm_w_uq_c': out['m_w_uq_c'], 'm_kv_norm_c': out['m_kv_norm_c'], 'm_w_ukv_c': out['m_w_ukv_c'], 'm_w_o_c': out['m_w_o_c'], 'm_w_mlp_up': out['m_w_mlp_up'], 'm_w_mlp_down': out['m_w_mlp_down'], 'v_rel_bias': out['v_rel_bias'], 'v_attn_norm': out['v_attn_norm'], 'v_mlp_norm': out['v_mlp_norm'], 'v_final_norm': out['v_final_norm'], 'v_w_in_ab': out['v_w_in_ab'], 'v_sinks': out['v_sinks'], 'v_w_out_ab': out['v_w_out_ab'], 'v_w_down_c': out['v_w_down_c'], 'v_q_norm_c': out['v_q_norm_c'], 'v_w_uq_c': out['v_w_uq_c'], 'v_kv_norm_c': out['v_kv_norm_c'], 'v_w_ukv_c': out['v_w_ukv_c'], 'v_w_o_c': out['v_w_o_c'], 'v_w_mlp_up': out['v_w_mlp_up'], 'v_w_mlp_down': out['v_w_mlp_down']}


def _loss(weights, diff, rest, loss_target):
    with _jax.named_scope("forward"):
        args = {**rest, TWIN_DIFF_INPUT: diff, **{k: w.astype(_WEIGHT_DTYPES[k]) for k, w in weights.items()}}
        y = _forward(args)
    with _jax.named_scope("loss_head"):
        err = _jnp.square(y.astype(_jnp.float32) - loss_target)
        return 0.5 * _jnp.sum(_jnp.mean(err, axis=-1)) if err.ndim else 0.5 * err


def _adamw(w, g, m, v):
    m = ADAM_B1 * m + (1.0 - ADAM_B1) * g
    v = ADAM_B2 * v + (1.0 - ADAM_B2) * _jnp.square(g)
    m_hat = m / (1.0 - ADAM_B1 ** ADAM_STEP)
    v_hat = v / (1.0 - ADAM_B2 ** ADAM_STEP)
    delta = -ADAM_LR * (m_hat / (_jnp.sqrt(v_hat) + ADAM_EPS) + ADAM_WD * w)
    return delta, m, v


def reference(x, rel_bias, attn_norm, mlp_norm, final_norm, w_in_ab, sinks, w_out_ab, w_down_c, q_norm_c, w_uq_c, kv_norm_c, w_ukv_c, w_o_c, w_mlp_up, w_mlp_down, loss_target, m_rel_bias, m_attn_norm, m_mlp_norm, m_final_norm, m_w_in_ab, m_sinks, m_w_out_ab, m_w_down_c, m_q_norm_c, m_w_uq_c, m_kv_norm_c, m_w_ukv_c, m_w_o_c, m_w_mlp_up, m_w_mlp_down, v_rel_bias, v_attn_norm, v_mlp_norm, v_final_norm, v_w_in_ab, v_sinks, v_w_out_ab, v_w_down_c, v_q_norm_c, v_w_uq_c, v_kv_norm_c, v_w_ukv_c, v_w_o_c, v_w_mlp_up, v_w_mlp_down):
    given = dict(x=x, rel_bias=rel_bias, attn_norm=attn_norm, mlp_norm=mlp_norm, final_norm=final_norm, w_in_ab=w_in_ab, sinks=sinks, w_out_ab=w_out_ab, w_down_c=w_down_c, q_norm_c=q_norm_c, w_uq_c=w_uq_c, kv_norm_c=kv_norm_c, w_ukv_c=w_ukv_c, w_o_c=w_o_c, w_mlp_up=w_mlp_up, w_mlp_down=w_mlp_down, loss_target=loss_target, m_rel_bias=m_rel_bias, m_attn_norm=m_attn_norm, m_mlp_norm=m_mlp_norm, m_final_norm=m_final_norm, m_w_in_ab=m_w_in_ab, m_sinks=m_sinks, m_w_out_ab=m_w_out_ab, m_w_down_c=m_w_down_c, m_q_norm_c=m_q_norm_c, m_w_uq_c=m_w_uq_c, m_kv_norm_c=m_kv_norm_c, m_w_ukv_c=m_w_ukv_c, m_w_o_c=m_w_o_c, m_w_mlp_up=m_w_mlp_up, m_w_mlp_down=m_w_mlp_down, v_rel_bias=v_rel_bias, v_attn_norm=v_attn_norm, v_mlp_norm=v_mlp_norm, v_final_norm=v_final_norm, v_w_in_ab=v_w_in_ab, v_sinks=v_sinks, v_w_out_ab=v_w_out_ab, v_w_down_c=v_w_down_c, v_q_norm_c=v_q_norm_c, v_w_uq_c=v_w_uq_c, v_kv_norm_c=v_kv_norm_c, v_w_ukv_c=v_w_ukv_c, v_w_o_c=v_w_o_c, v_w_mlp_up=v_w_mlp_up, v_w_mlp_down=v_w_mlp_down)
    weights = {n: given[n] for n in TWIN_WEIGHTS}
    shared = {n: given[n] for n in SHARED_INPUTS}
    per_example = {n: given[n] for n in ['x']}
    grad_fn = _jax.value_and_grad(_loss, argnums=(0, 1))

    def one_microbatch(ex, loss_target):
        ex = dict(ex)
        diff = ex.pop(TWIN_DIFF_INPUT)
        return grad_fn(weights, diff, {**shared, **ex}, loss_target)

    if N_MICROBATCH == 1:
        loss, (grad_w, grad_x) = one_microbatch(per_example, given["loss_target"])
    else:
        def body(carry, xs):
            loss_sum, grad_sum = carry
            l_k, (gw_k, gx_k) = one_microbatch(xs[0], xs[1])
            with _jax.named_scope("update"):
                return (loss_sum + l_k, _jax.tree.map(_jnp.add, grad_sum, gw_k)), gx_k

        init = (_jnp.zeros((), _jnp.float32), _jax.tree.map(_jnp.zeros_like, weights))
        (loss, grad_w), grad_x = _jax.lax.scan(body, init, (per_example, given["loss_target"]))
    with _jax.named_scope("update"):
        delta_w, new_m, new_v = {}, {}, {}
        for n in TWIN_WEIGHTS:
            delta_w[n], new_m[n], new_v[n] = _adamw(weights[n], grad_w[n], given["m_" + n], given["v_" + n])
    return (loss, grad_x, *[grad_w[n] for n in TWIN_WEIGHTS], *[delta_w[n] for n in TWIN_WEIGHTS],
            *[new_m[n] for n in TWIN_WEIGHTS], *[new_v[n] for n in TWIN_WEIGHTS])
```

```python
import math

import numpy as np
import jax
import jax.numpy as jnp
from jax import lax
from jax.experimental import pallas as pl
from jax.experimental.pallas import tpu as pltpu

F32 = jnp.float32
MXU_DT = jnp.bfloat16

N_DEV = 8
D_MODEL = 1024
DEPTH = 4
HEAD_DIM = 64
BLOCK = 128
EPS = 1e-6
NEG = -1e30
A_Q_HEADS = 8
A_KV_HEADS = 2
A_GROUP = A_Q_HEADS // A_KV_HEADS
A_WINDOW = 128
B_BRANCHES = ((128, 1), (512, 4), (2048, 16))
B_HPB = 4
B_HEADS = len(B_BRANCHES) * B_HPB
NUM_BUCKETS = 32
MAX_DISTANCE = 2048
N_BIAS_HEADS = A_Q_HEADS + B_HEADS
N_BAND_KV = A_KV_HEADS + B_HEADS
A_IN = (A_Q_HEADS + 2 * A_KV_HEADS) * HEAD_DIM
C_HEADS = 8
C_NOPE = 64
C_ROPE = 32
C_QK = C_NOPE + C_ROPE
C_V = 64
C_Q_RANK = 384
C_KV_RANK = 256
C_DOWN = C_Q_RANK + C_KV_RANK + C_ROPE
C_DOWN_PAD = 768
ROPE_THETA = 10000.0
N_CHUNKS = 16
FLASH_T = 512

ADAM_LR = 0.001
ADAM_B1 = 0.9
ADAM_B2 = 0.999
ADAM_EPS = 1e-08
ADAM_WD = 0.01
ADAM_STEP = 10

V7X_VMEM_BYTES = 64 * 1024 * 1024
VMEM_LIMIT = V7X_VMEM_BYTES - 8 * 1024 * 1024


def _pcall(body, **kw):
    return pl.pallas_call(body, **kw)


def _cparams(*sem):
    return pltpu.CompilerParams(dimension_semantics=sem, vmem_limit_bytes=VMEM_LIMIT)


def _exchange(src, all_to_all, name):
    blk = src.shape[-2:]

    def body(src_ref, out_ref, send_sems, recv_sems, local_sem):
        x, y, c = lax.axis_index("x"), lax.axis_index("y"), lax.axis_index("c")
        me = 4 * x + 2 * y + c

        def piece(dev):
            return src_ref.at[dev] if all_to_all else src_ref

        local = pltpu.make_async_copy(piece(me), out_ref.at[me], local_sem)
        local.start()
        copies = []
        for k in range(1, N_DEV):
            px = 1 - x if (k >> 2) & 1 else x
            py = 1 - y if (k >> 1) & 1 else y
            pc = 1 - c if k & 1 else c
            cp = pltpu.make_async_remote_copy(
                src_ref=piece(4 * px + 2 * py + pc), dst_ref=out_ref.at[me],
                send_sem=send_sems.at[k - 1], recv_sem=recv_sems.at[k - 1],
                device_id=(px, py, pc), device_id_type=pl.DeviceIdType.MESH)
            cp.start()
            copies.append(cp)
        for cp in copies:
            cp.wait()
        local.wait()

    return _pcall(
        body, name=name,
        out_shape=jax.ShapeDtypeStruct((N_DEV,) + blk, src.dtype),
        in_specs=[pl.BlockSpec(memory_space=pl.ANY)],
        out_specs=pl.BlockSpec(memory_space=pl.ANY),
        scratch_shapes=[pltpu.SemaphoreType.DMA((N_DEV - 1,)), pltpu.SemaphoreType.DMA((N_DEV - 1,)),
                        pltpu.SemaphoreType.DMA],
    )(src)


def _matmul(a, b, *, trans_b=False, out_dtype=F32, epi=None, extra=None, tm=512, tn=512, name):
    M, K = a.shape
    N = b.shape[0] if trans_b else b.shape[1]
    tm, tn = min(tm, M), min(tn, N)
    assert M % tm == 0 and N % tn == 0 and (b.shape[1] if trans_b else b.shape[0]) == K
    dn = (((1,), (1,)), ((), ())) if trans_b else (((1,), (0,)), ((), ()))

    def body(*refs):
        a_ref, b_ref = refs[0], refs[1]
        o_ref = refs[-1]
        acc = lax.dot_general(a_ref[...].astype(MXU_DT), b_ref[...].astype(MXU_DT), dn,
                              preferred_element_type=F32)
        if epi == 'relu2':
            r = jnp.maximum(acc, 0.0)
            acc = r * r
        elif epi == 'add':
            acc = acc + refs[2][...].astype(F32)
        elif epi == 'dsq':
            acc = acc * (2.0 * jnp.sqrt(refs[2][...].astype(F32)))
        o_ref[...] = acc.astype(out_dtype)

    b_spec = pl.BlockSpec((tn, K), lambda i, j: (j, 0)) if trans_b else pl.BlockSpec((K, tn), lambda i, j: (0, j))
    in_specs = [pl.BlockSpec((tm, K), lambda i, j: (i, 0)), b_spec]
    args = [a, b]
    if extra is not None:
        in_specs.append(pl.BlockSpec((tm, tn), lambda i, j: (i, j)))
        args.append(extra)
    return _pcall(
        body, name=name, grid=(M // tm, N // tn),
        out_shape=jax.ShapeDtypeStruct((M, N), out_dtype),
        in_specs=in_specs, out_specs=pl.BlockSpec((tm, tn), lambda i, j: (i, j)),
        compiler_params=_cparams("parallel", "parallel"),
    )(*args)


def _matmul_tn(a, b, *, tk=512, tn=512, tm=1024, name):
    M, Ka = a.shape
    N = b.shape[1]
    tk, tn, tm = min(tk, Ka), min(tn, N), min(tm, M)
    assert Ka % tk == 0 and N % tn == 0 and M % tm == 0 and b.shape[0] == M

    def body(a_ref, b_ref, o_ref):
        @pl.when(pl.program_id(2) == 0)
        def _():
            o_ref[...] = jnp.zeros_like(o_ref)

        o_ref[...] += lax.dot_general(a_ref[...].astype(MXU_DT), b_ref[...].astype(MXU_DT),
                                      (((0,), (0,)), ((), ())), preferred_element_type=F32)

    return _pcall(
        body, name=name, grid=(Ka // tk, N // tn, M // tm),
        out_shape=jax.ShapeDtypeStruct((Ka, N), F32),
        in_specs=[pl.BlockSpec((tm, tk), lambda i, j, r: (r, i)), pl.BlockSpec((tm, tn), lambda i, j, r: (r, j))],
        out_specs=pl.BlockSpec((tk, tn), lambda i, j, r: (i, j)),
        compiler_params=_cparams("parallel", "parallel", "arbitrary"),
    )(a, b)


def _rmsnorm(x, g, *, out_dtype, name, tr=512):
    S, D = x.shape
    tr = min(tr, S)

    def body(x_ref, g_ref, o_ref):
        xf = x_ref[...].astype(F32)
        r = lax.rsqrt(jnp.mean(xf * xf, axis=-1, keepdims=True) + EPS)
        o_ref[...] = (xf * r * g_ref[...]).astype(out_dtype)

    return _pcall(
        body, name=name, grid=(S // tr,),
        out_shape=jax.ShapeDtypeStruct((S, D), out_dtype),
        in_specs=[pl.BlockSpec((tr, D), lambda i: (i, 0)), pl.BlockSpec((1, D), lambda i: (0, 0))],
        out_specs=pl.BlockSpec((tr, D), lambda i: (i, 0)),
        compiler_params=_cparams("parallel"),
    )(x, g.reshape(1, D))


def _rmsnorm_bwd(x, g, dy, dres, *, name, tr=512):
    S, D = x.shape
    tr = min(tr, S)

    def body(*refs):
        if dres is None:
            x_ref, g_ref, dy_ref, dx_ref, dg_ref = refs
        else:
            x_ref, g_ref, dy_ref, dres_ref, dx_ref, dg_ref = refs

        @pl.when(pl.program_id(0) == 0)
        def _():
            dg_ref[...] = jnp.zeros_like(dg_ref)

        xf = x_ref[...].astype(F32)
        r = lax.rsqrt(jnp.mean(xf * xf, axis=-1, keepdims=True) + EPS)
        xhat = xf * r
        dyf = dy_ref[...].astype(F32)
        dg_ref[...] += jnp.sum(dyf * xhat, axis=0, keepdims=True)
        dyg = dyf * g_ref[...]
        dx = r * (dyg - xhat * jnp.mean(dyg * xhat, axis=-1, keepdims=True))
        if dres is not None:
            dx = dx + dres_ref[...]
        dx_ref[...] = dx

    row = pl.BlockSpec((tr, D), lambda i: (i, 0))
    vec = pl.BlockSpec((1, D), lambda i: (0, 0))
    args = [x, g.reshape(1, D), dy] + ([] if dres is None else [dres])
    return _pcall(
        body, name=name, grid=(S // tr,),
        out_shape=(jax.ShapeDtypeStruct((S, D), F32), jax.ShapeDtypeStruct((1, D), F32)),
        in_specs=[row, vec, row] + ([] if dres is None else [row]),
        out_specs=(row, vec),
        compiler_params=_cparams("arbitrary"),
    )(*args)


def _loss_head(h, t, g, *, tr=512):
    S, D = h.shape
    tr = min(tr, S)

    def body(h_ref, t_ref, g_ref, loss_ref, dh_ref, dg_ref):
        @pl.when(pl.program_id(0) == 0)
        def _():
            dg_ref[...] = jnp.zeros_like(dg_ref)
            loss_ref[...] = jnp.zeros_like(loss_ref)

        xf = h_ref[...]
        r = lax.rsqrt(jnp.mean(xf * xf, axis=-1, keepdims=True) + EPS)
        xhat = xf * r
        e = xhat * g_ref[...] - t_ref[...]
        part = 0.5 * jnp.sum(jnp.mean(e * e, axis=-1, keepdims=True), axis=0, keepdims=True)
        loss_ref[...] += jnp.broadcast_to(part, loss_ref.shape)
        dy = e * (1.0 / D)
        dg_ref[...] += jnp.sum(dy * xhat, axis=0, keepdims=True)
        dyg = dy * g_ref[...]
        dh_ref[...] = r * (dyg - xhat * jnp.mean(dyg * xhat, axis=-1, keepdims=True))

    row = pl.BlockSpec((tr, D), lambda i: (i, 0))
    vec = pl.BlockSpec((1, D), lambda i: (0, 0))
    return _pcall(
        body, name="loss_head", grid=(S // tr,),
        out_shape=(jax.ShapeDtypeStruct((1, 128), F32), jax.ShapeDtypeStruct((S, D), F32),
                   jax.ShapeDtypeStruct((1, D), F32)),
        in_specs=[row, row, vec],
        out_specs=(pl.BlockSpec((1, 128), lambda i: (0, 0)), row, vec),
        compiler_params=_cparams("arbitrary"),
    )(h, t, g.reshape(1, D))


def _band_kv_head(h):
    return jnp.where(h < A_Q_HEADS, h // A_GROUP, h - (A_Q_HEADS - A_KV_HEADS))


def _band_period(h):
    return jnp.where(h < A_Q_HEADS + B_HPB, 16, jnp.where(h < A_Q_HEADS + 2 * B_HPB, 4, 1))


def _band_specs(S):
    ch = S // N_CHUNKS
    nb = ch // BLOCK
    q_spec = pl.BlockSpec((1, ch, HEAD_DIM), lambda h, c: (h, c, 0))
    kv_spec = pl.BlockSpec((1, ch, HEAD_DIM), lambda h, c: (_band_kv_head(h), c, 0))
    kv_prev = pl.BlockSpec((1, BLOCK, HEAD_DIM), lambda h, c: (_band_kv_head(h), jnp.maximum(c * nb - 1, 0), 0))
    bias_spec = pl.BlockSpec((1, BLOCK, 2 * BLOCK), lambda h, c: (h, 0, 0))
    sink_spec = pl.BlockSpec((1, 1, 128), lambda h, c: (h, 0, 0))
    row_spec = pl.BlockSpec((1, 1, nb, BLOCK), lambda h, c: (h, c, 0, 0))
    return ch, nb, q_spec, kv_spec, kv_prev, bias_spec, sink_spec, row_spec


def _eye():
    return lax.broadcasted_iota(jnp.int32, (BLOCK, BLOCK), 0) == lax.broadcasted_iota(jnp.int32, (BLOCK, BLOCK), 1)


_NT = (((1,), (1,)), ((), ()))
_NN = (((1,), (0,)), ((), ()))
_TN = (((0,), (0,)), ((), ()))


def _band_probs(qb, kc, kp, bias_cur, bias_prev, sink, prev_ok):
    scale = HEAD_DIM ** -0.5
    s_cur = lax.dot_general(qb, kc, _NT, preferred_element_type=F32) * scale + bias_cur
    s_prev = lax.dot_general(qb, kp, _NT, preferred_element_type=F32) * scale + bias_prev
    if prev_ok is not None:
        s_prev = jnp.where(prev_ok, s_prev, NEG)
    m = jnp.maximum(jnp.max(s_cur, axis=-1, keepdims=True), jnp.max(s_prev, axis=-1, keepdims=True))
    m = jnp.maximum(m, sink)
    p_cur = jnp.exp(s_cur - m)
    p_prev = jnp.exp(s_prev - m)
    p_sink = jnp.exp(sink - m)
    l = jnp.sum(p_cur, axis=-1, keepdims=True) + jnp.sum(p_prev, axis=-1, keepdims=True) + p_sink
    return p_cur, p_prev, p_sink, m, l


def _banded_fwd(q, k, v, bias, sinks):
    NH, S, dh = q.shape
    ch, nb, q_spec, kv_spec, kv_prev, bias_spec, sink_spec, row_spec = _band_specs(S)

    def body(q_ref, k_ref, kp_ref, v_ref, vp_ref, b_ref, s_ref, o_ref, lse_ref):
        h, c = pl.program_id(0), pl.program_id(1)
        prev_ok = (c % _band_period(h)) != 0
        bias_prev, bias_cur = b_ref[0, :, :BLOCK], b_ref[0, :, BLOCK:]
        sink = s_ref[0, :, :1]
        eye = _eye()
        for j in range(nb):
            rows = slice(j * BLOCK, (j + 1) * BLOCK)
            prows = slice((j - 1) * BLOCK, j * BLOCK)
            kp = kp_ref[0] if j == 0 else k_ref[0, prows, :]
            vp = vp_ref[0] if j == 0 else v_ref[0, prows, :]
            p_cur, p_prev, _, m, l = _band_probs(q_ref[0, rows, :], k_ref[0, rows, :], kp, bias_cur, bias_prev,
                                                 sink, prev_ok if j == 0 else None)
            acc = (lax.dot_general(p_cur.astype(MXU_DT), v_ref[0, rows, :], _NN, preferred_element_type=F32)
                   + lax.dot_general(p_prev.astype(MXU_DT), vp, _NN, preferred_element_type=F32))
            o_ref[0, rows, :] = acc / l
            lse = m + jnp.log(l)
            lse_ref[0, 0, j:j + 1, :] = jnp.sum(jnp.where(eye, lse, 0.0), axis=0, keepdims=True)

    return _pcall(
        body, name="banded_fwd", grid=(NH, N_CHUNKS),
        out_shape=(jax.ShapeDtypeStruct((NH, S, dh), F32), jax.ShapeDtypeStruct((NH, N_CHUNKS, nb, BLOCK), F32)),
        in_specs=[q_spec, kv_spec, kv_prev, kv_spec, kv_prev, bias_spec, sink_spec],
        out_specs=(q_spec, row_spec),
        compiler_params=_cparams("parallel", "parallel"),
    )(q, k, k, v, v, bias, sinks)


def _banded_bwd(q, k, v, bias, sinks, do, dlse):
    NH, S, dh = q.shape
    ch, nb, q_spec, kv_spec, kv_prev, bias_spec, sink_spec, row_spec = _band_specs(S)
    scale = HEAD_DIM ** -0.5

    def body(q_ref, k_ref, kp_ref, v_ref, vp_ref, b_ref, s_ref, do_ref, dl_ref,
             dq_ref, dk_ref, dv_ref, dkh_ref, dvh_ref, db_ref, ds_ref):
        h, c = pl.program_id(0), pl.program_id(1)

        @pl.when(c == 0)
        def _():
            db_ref[...] = jnp.zeros_like(db_ref)
            ds_ref[...] = jnp.zeros_like(ds_ref)

        prev_ok = (c % _band_period(h)) != 0
        bias_prev, bias_cur = b_ref[0, :, :BLOCK], b_ref[0, :, BLOCK:]
        sink = s_ref[0, :, :1]
        eye = _eye()
        dsink = jnp.zeros((1, 1), F32)
        for j in range(nb):
            rows = slice(j * BLOCK, (j + 1) * BLOCK)
            prows = slice((j - 1) * BLOCK, j * BLOCK)
            qb, kc, vc = q_ref[0, rows, :], k_ref[0, rows, :], v_ref[0, rows, :]
            kp = kp_ref[0] if j == 0 else k_ref[0, prows, :]
            vp = vp_ref[0] if j == 0 else v_ref[0, prows, :]
            p_cur, p_prev, p_sink, m, l = _band_probs(qb, kc, kp, bias_cur, bias_prev, sink,
                                                      prev_ok if j == 0 else None)
            inv = 1.0 / l
            p_cur, p_prev, p_sink = p_cur * inv, p_prev * inv, p_sink * inv
            dob = do_ref[0, rows, :].astype(MXU_DT)
            dp_cur = lax.dot_general(dob, vc, _NT, preferred_element_type=F32)
            dp_prev = lax.dot_general(dob, vp, _NT, preferred_element_type=F32)
            delta = (jnp.sum(p_cur * dp_cur, axis=-1, keepdims=True)
                     + jnp.sum(p_prev * dp_prev, axis=-1, keepdims=True))
            dl = jnp.sum(jnp.where(eye, dl_ref[0, 0, j:j + 1, :], 0.0), axis=1, keepdims=True)
            t = dl - delta
            ds_cur = p_cur * (dp_cur + t)
            ds_prev = p_prev * (dp_prev + t)
            dsink = dsink + jnp.sum(p_sink * t, axis=0, keepdims=True)
            db_ref[0, :, :BLOCK] += ds_prev
            db_ref[0, :, BLOCK:] += ds_cur
            dsb_cur = (ds_cur * scale).astype(MXU_DT)
            dsb_prev = (ds_prev * scale).astype(MXU_DT)
            dq_ref[0, rows, :] = (lax.dot_general(dsb_cur, kc, _NN, preferred_element_type=F32)
                                  + lax.dot_general(dsb_prev, kp, _NN, preferred_element_type=F32))
            dk_ref[0, rows, :] = lax.dot_general(dsb_cur, qb, _TN, preferred_element_type=F32)
            dv_ref[0, rows, :] = lax.dot_general(p_cur.astype(MXU_DT), dob, _TN, preferred_element_type=F32)
            dk_prev = lax.dot_general(dsb_prev, qb, _TN, preferred_element_type=F32)
            dv_prev = lax.dot_general(p_prev.astype(MXU_DT), dob, _TN, preferred_element_type=F32)
            if j == 0:
                dkh_ref[0, 0] = dk_prev
                dvh_ref[0, 0] = dv_prev
            else:
                dk_ref[0, prows, :] += dk_prev
                dv_ref[0, prows, :] += dv_prev
        ds_ref[...] += jnp.broadcast_to(dsink, ds_ref.shape)

    halo_spec = pl.BlockSpec((1, 1, BLOCK, HEAD_DIM), lambda h, c: (h, c, 0, 0))
    big = jax.ShapeDtypeStruct((NH, S, dh), F32)
    halo = jax.ShapeDtypeStruct((NH, N_CHUNKS, BLOCK, dh), F32)
    return _pcall(
        body, name="banded_bwd", grid=(NH, N_CHUNKS),
        out_shape=(big, big, big, halo, halo, jax.ShapeDtypeStruct(bias.shape, F32),
                   jax.ShapeDtypeStruct(sinks.shape, F32)),
        in_specs=[q_spec, kv_spec, kv_prev, kv_spec, kv_prev, bias_spec, sink_spec, q_spec, row_spec],
        out_specs=(q_spec, q_spec, q_spec, halo_spec, halo_spec, bias_spec, sink_spec),
        compiler_params=_cparams("arbitrary", "arbitrary"),
    )(q, k, k, v, v, bias, sinks, do, dlse)


def _causal_mask(T):
    return lax.broadcasted_iota(jnp.int32, (T, T), 0) <= lax.broadcasted_iota(jnp.int32, (T, T), 1)


def _flash_fwd(qt, k, vt):
    H, nq, dqk, T = qt.shape
    S = k.shape[1]
    dv = vt.shape[2]
    scale = dqk ** -0.5

    def body(qt_ref, k_ref, vt_ref, ot_ref, lse_ref):
        i = pl.program_id(1)
        qtb = qt_ref[0, 0]

        def step(j, carry, masked):
            m, l, acc = carry
            kb = k_ref[0, pl.ds(pl.multiple_of(j * T, T), T), :]
            st = lax.dot_general(kb, qtb, _NN, preferred_element_type=F32) * scale
            if masked:
                st = jnp.where(_causal_mask(T), st, NEG)
            m_new = jnp.maximum(m, jnp.max(st, axis=0, keepdims=True))
            alpha = jnp.exp(m - m_new)
            p = jnp.exp(st - m_new)
            l = l * alpha + jnp.sum(p, axis=0, keepdims=True)
            acc = acc * alpha + lax.dot_general(vt_ref[0, j], p.astype(MXU_DT), _NN, preferred_element_type=F32)
            return m_new, l, acc

        init = (jnp.full((1, T), NEG, F32), jnp.zeros((1, T), F32), jnp.zeros((dv, T), F32))
        carry = lax.fori_loop(0, i, lambda j, cr: step(j, cr, False), init)
        m, l, acc = step(i, carry, True)
        ot_ref[0, 0] = acc / l
        lse_ref[0, 0] = m + jnp.log(l)

    return _pcall(
        body, name="flash_fwd", grid=(H, nq),
        out_shape=(jax.ShapeDtypeStruct((H, nq, dv, T), F32), jax.ShapeDtypeStruct((H, nq, 1, T), F32)),
        in_specs=[pl.BlockSpec((1, 1, dqk, T), lambda h, i: (h, i, 0, 0)),
                  pl.BlockSpec((1, S, dqk), lambda h, i: (h, 0, 0)),
                  pl.BlockSpec((1, nq, dv, T), lambda h, i: (h, 0, 0, 0))],
        out_specs=(pl.BlockSpec((1, 1, dv, T), lambda h, i: (h, i, 0, 0)),
                   pl.BlockSpec((1, 1, 1, T), lambda h, i: (h, i, 0, 0))),
        compiler_params=_cparams("parallel", "parallel"),
    )(qt, k, vt)


def _flash_delta(ot, dot):
    H, nq, dv, T = ot.shape

    def body(o_ref, do_ref, d_ref):
        d_ref[0, 0] = jnp.sum(o_ref[0, 0] * do_ref[0, 0], axis=0, keepdims=True)

    spec = pl.BlockSpec((1, 1, dv, T), lambda h, i: (h, i, 0, 0))
    return _pcall(
        body, name="flash_delta", grid=(H, nq),
        out_shape=jax.ShapeDtypeStruct((H, nq, 1, T), F32),
        in_specs=[spec, spec], out_specs=pl.BlockSpec((1, 1, 1, T), lambda h, i: (h, i, 0, 0)),
        compiler_params=_cparams("parallel", "parallel"),
    )(ot, dot)


def _flash_bwd(qt, k, kt, v, dot, lse, delta):
    H, nq, dqk, T = qt.shape
    S = k.shape[1]
    dv_ = v.shape[2]
    scale = dqk ** -0.5

    def body(qt_ref, k_ref, kt_ref, v_ref, dot_ref, lse_ref, del_ref, dqt_ref, dk_ref, dv_ref):
        j = pl.program_id(1)

        @pl.when(j == 0)
        def _():
            dqt_ref[...] = jnp.zeros_like(dqt_ref)

        kb, ktb, vb = k_ref[0], kt_ref[0, 0], v_ref[0]

        def step(i, carry, masked):
            dk, dv = carry
            qtb = qt_ref[0, i]
            dob = dot_ref[0, i].astype(MXU_DT)
            st = lax.dot_general(kb, qtb, _NN, preferred_element_type=F32) * scale
            if masked:
                st = jnp.where(_causal_mask(T), st, NEG)
            p = jnp.exp(st - lse_ref[0, i])
            dpt = lax.dot_general(vb, dob, _NN, preferred_element_type=F32)
            dsb = (p * (dpt - del_ref[0, i]) * scale).astype(MXU_DT)
            dv = dv + lax.dot_general(p.astype(MXU_DT), dob, _NT, preferred_element_type=F32)
            dk = dk + lax.dot_general(dsb, qtb, _NT, preferred_element_type=F32)
            dqt_ref[0, i] += lax.dot_general(ktb, dsb, _NN, preferred_element_type=F32)
            return dk, dv

        carry = step(j, (jnp.zeros((T, dqk), F32), jnp.zeros((T, dv_), F32)), True)
        dk, dv = lax.fori_loop(j + 1, nq, lambda i, cr: step(i, cr, False), carry)
        dk_ref[0] = dk
        dv_ref[0] = dv

    whole = lambda d: pl.BlockSpec((1, nq, d, T), lambda h, j: (h, 0, 0, 0))
    return _pcall(
        body, name="flash_bwd", grid=(H, nq),
        out_shape=(jax.ShapeDtypeStruct((H, nq, dqk, T), F32), jax.ShapeDtypeStruct((H, S, dqk), F32),
                   jax.ShapeDtypeStruct((H, S, dv_), F32)),
        in_specs=[whole(dqk),
                  pl.BlockSpec((1, T, dqk), lambda h, j: (h, j, 0)),
                  pl.BlockSpec((1, 1, dqk, T), lambda h, j: (h, j, 0, 0)),
                  pl.BlockSpec((1, T, dv_), lambda h, j: (h, j, 0)),
                  whole(dv_), whole(1), whole(1)],
        out_specs=(whole(dqk), pl.BlockSpec((1, T, dqk), lambda h, j: (h, j, 0)),
                   pl.BlockSpec((1, T, dv_), lambda h, j: (h, j, 0))),
        compiler_params=_cparams("arbitrary", "arbitrary"),
    )(qt, k, kt, v, dot, lse, delta)


def _adamw(parts, w, m, v, *, name, tr=512):
    P, R, C = parts.shape
    tr = min(tr, R)
    assert R % tr == 0

    def body(p_ref, w_ref, m_ref, v_ref, g_ref, d_ref, m2_ref, v2_ref):
        g = p_ref[0].astype(F32)
        for s in range(1, P):
            g = g + p_ref[s].astype(F32)
        m2 = ADAM_B1 * m_ref[...] + (1.0 - ADAM_B1) * g
        v2 = ADAM_B2 * v_ref[...] + (1.0 - ADAM_B2) * jnp.square(g)
        m_hat = m2 / (1.0 - ADAM_B1 ** ADAM_STEP)
        v_hat = v2 / (1.0 - ADAM_B2 ** ADAM_STEP)
        g_ref[...] = g
        d_ref[...] = -ADAM_LR * (m_hat / (jnp.sqrt(v_hat) + ADAM_EPS) + ADAM_WD * w_ref[...])
        m2_ref[...] = m2
        v2_ref[...] = v2

    row = pl.BlockSpec((tr, C), lambda i: (i, 0))
    out = jax.ShapeDtypeStruct((R, C), F32)
    return _pcall(
        body, name=name, grid=(R // tr,),
        out_shape=(out, out, out, out),
        in_specs=[pl.BlockSpec((P, tr, C), lambda i: (0, i, 0)), row, row, row],
        out_specs=(row, row, row, row),
        compiler_params=_cparams("parallel"),
    )(parts, w, m, v)


def _bias_tables():
    i = np.arange(BLOCK)[:, None]
    j = np.arange(2 * BLOCK)[None, :]
    dist = i + BLOCK - j
    out = []
    for dil, max_dist in [(1, A_WINDOW - 1)] + [(d, w // d) for w, d in B_BRANCHES]:
        n = np.maximum(dist, 0) * dil
        max_exact = NUM_BUCKETS // 2
        nf = np.maximum(n, 1).astype(np.float64)
        val = np.log(nf / max_exact) / math.log(MAX_DISTANCE / max_exact) * (NUM_BUCKETS - max_exact)
        inband = (dist >= 0) & (dist <= max_dist)
        frac = np.abs(val - np.round(val))
        last = NUM_BUCKETS - 1 - max_exact
        assert np.all((frac > 2e-5) | (n <= max_exact) | (val >= last) | ~inband)
        large = max_exact + val.astype(np.int64)
        bucket = np.where(n < max_exact, n, np.minimum(large, NUM_BUCKETS - 1))
        onehot = (bucket[..., None] == np.arange(NUM_BUCKETS)).astype(np.float32)
        out.append((onehot.reshape(-1, NUM_BUCKETS), inband))
    return out


def _make_bias(rel_bias):
    tabs = _bias_tables()
    groups = [(0, A_Q_HEADS)] + [(A_Q_HEADS + g * B_HPB, B_HPB) for g in range(len(B_BRANCHES))]
    parts = []
    for (onehot, inband), (h0, nh) in zip(tabs, groups):
        b = jnp.dot(jnp.asarray(onehot), rel_bias[:, h0:h0 + nh], precision=lax.Precision.HIGHEST)
        b = b.reshape(BLOCK, 2 * BLOCK, nh)
        b = jnp.where(jnp.asarray(inband)[..., None], b, NEG)
        parts.append(b.transpose(2, 0, 1))
    return jnp.concatenate(parts, axis=0)


def _dilate_heads(t, d):
    S, H, dh = t.shape
    return t.reshape(S // d, d, H, dh).transpose(2, 1, 0, 3).reshape(H, S, dh)


def _undilate_heads(t, d):
    H, S, dh = t.shape
    return t.reshape(H, d, S // d, dh).transpose(2, 1, 0, 3).reshape(S, H, dh)


def _even_split(proj):
    S = proj.shape[0]
    qd, kd = A_Q_HEADS * HEAD_DIM, A_KV_HEADS * HEAD_DIM
    pa = proj[:, :A_IN]
    qs = [pa[:, :qd].reshape(S, A_Q_HEADS, HEAD_DIM).transpose(1, 0, 2)]
    ks = [pa[:, qd:qd + kd].reshape(S, A_KV_HEADS, HEAD_DIM).transpose(1, 0, 2)]
    vs = [pa[:, qd + kd:].reshape(S, A_KV_HEADS, HEAD_DIM).transpose(1, 0, 2)]
    pb = proj[:, A_IN:].reshape(S, len(B_BRANCHES), 3, B_HPB, HEAD_DIM)
    for g, (_, d) in enumerate(B_BRANCHES):
        qs.append(_dilate_heads(pb[:, g, 0], d))
        ks.append(_dilate_heads(pb[:, g, 1], d))
        vs.append(_dilate_heads(pb[:, g, 2], d))
    return jnp.concatenate(qs, 0), jnp.concatenate(ks, 0), jnp.concatenate(vs, 0)


def _even_merge(dq, dk, dv):
    S = dq.shape[1]
    cols = [dq[:A_Q_HEADS].transpose(1, 0, 2).reshape(S, -1),
            dk[:A_KV_HEADS].transpose(1, 0, 2).reshape(S, -1),
            dv[:A_KV_HEADS].transpose(1, 0, 2).reshape(S, -1)]
    for g, (_, d) in enumerate(B_BRANCHES):
        qh = slice(A_Q_HEADS + g * B_HPB, A_Q_HEADS + (g + 1) * B_HPB)
        kh = slice(A_KV_HEADS + g * B_HPB, A_KV_HEADS + (g + 1) * B_HPB)
        for t in (dq[qh], dk[kh], dv[kh]):
            cols.append(_undilate_heads(t, d).reshape(S, -1))
    return jnp.concatenate(cols, axis=1)


def _even_post(o_all, lse_all):
    S = o_all.shape[1]
    out_a = o_all[:A_Q_HEADS].transpose(1, 0, 2).reshape(S, -1)
    outs, lses = [], []
    for g, (_, d) in enumerate(B_BRANCHES):
        hs = slice(A_Q_HEADS + g * B_HPB, A_Q_HEADS + (g + 1) * B_HPB)
        outs.append(_undilate_heads(o_all[hs], d))
        lses.append(_undilate_heads(lse_all[hs][..., None], d)[..., 0])
    wts = jax.nn.softmax(jnp.stack(lses), axis=0)
    out_b = sum(wts[g][..., None] * outs[g] for g in range(len(B_BRANCHES))).reshape(S, -1)
    return jnp.concatenate([out_a, out_b], axis=-1)


def _rope(t):
    S, r = t.shape[1], t.shape[-1]
    inv = ROPE_THETA ** (-jnp.arange(0, r, 2, dtype=jnp.float32) / r)
    ang = jnp.arange(S, dtype=jnp.float32)[:, None] * inv[None, :]
    shape = (1, S) + (1,) * (t.ndim - 3) + (r // 2,)
    cos, sin = jnp.cos(ang).reshape(shape), jnp.sin(ang).reshape(shape)
    t1, t2 = t[..., :r // 2], t[..., r // 2:]
    return jnp.concatenate([t1 * cos - t2 * sin, t1 * sin + t2 * cos], axis=-1)


def _mla_pre(q_lin, kv_lin, kr_raw):
    S = q_lin.shape[0]
    q = q_lin.reshape(1, S, C_HEADS, C_QK)
    qf = jnp.concatenate([q[..., :C_NOPE], _rope(q[..., C_NOPE:])], axis=-1)[0]
    kv = kv_lin.reshape(S, C_HEADS, C_NOPE + C_V)
    kr = _rope(kr_raw[None])[0]
    kf = jnp.concatenate([kv[..., :C_NOPE], jnp.broadcast_to(kr[:, None, :], (S, C_HEADS, C_ROPE))], axis=-1)
    return qf, kf, kv[..., C_NOPE:]


def _to_tiles_t(t, T):
    S, H, d = t.shape
    return t.reshape(S // T, T, H, d).transpose(2, 0, 3, 1)


def _from_tiles_t(t):
    H, n, d, T = t.shape
    return t.transpose(1, 3, 0, 2).reshape(n * T, H, d)


_BIG = (("w_in_ab", 2), ("w_out_ab", 2), ("w_down_c", 1), ("w_uq_c", 2), ("w_ukv_c", 2), ("w_o_c", 2),
        ("w_mlp_up", 2), ("w_mlp_down", 1))
_ROW_ALIGN = 512


def _pack_rows(arrs):
    rows = [a.reshape(-1, 128) for a in arrs]
    n = sum(r.shape[0] for r in rows)
    pad = (-n) % _ROW_ALIGN
    if pad:
        rows.append(jnp.zeros((pad, 128), rows[0].dtype))
    return jnp.concatenate(rows, axis=0)


def _pack_rows_per_device(arrs):
    rows = [a.reshape(N_DEV, -1, 128) for a in arrs]
    n = sum(r.shape[1] for r in rows)
    pad = (-n) % _ROW_ALIGN
    if pad:
        rows.append(jnp.zeros((N_DEV, pad, 128), rows[0].dtype))
    return jnp.concatenate(rows, axis=1)


def _unpack_rows(buf, shapes):
    out, r0 = [], 0
    for shp in shapes:
        n = math.prod(shp) // 128
        out.append(buf[..., r0:r0 + n, :].reshape(buf.shape[:-2] + tuple(shp)))
        r0 += n
    return out


def _gathered_to_full(g, axis):
    if axis == 2:
        return g.transpose(1, 2, 0, 3).reshape(g.shape[1], g.shape[2], N_DEV * g.shape[3])
    return g.transpose(1, 0, 2, 3).reshape(g.shape[1], N_DEV * g.shape[2], g.shape[3])


def _full_to_shards(t, axis):
    a, b, c = t.shape
    if axis == 2:
        return t.reshape(a, b, N_DEV, c // N_DEV).transpose(2, 0, 1, 3)
    return t.reshape(a, N_DEV, b // N_DEV, c).transpose(1, 0, 2, 3)


def _pad_rows8(a):
    flat = a.reshape(-1)
    n = -(-flat.shape[0] // 1024) * 1024
    return jnp.pad(flat, (0, n - flat.shape[0])).reshape(-1, 128)


def _even_fwd(xn, h, w_in, w_out, bias, sinks_row, l):
    S = xn.shape[0]
    proj = _matmul(xn, w_in, out_dtype=MXU_DT, tm=1024, tn=512, name=f"even_in_{l}")
    q, k, v = _even_split(proj)
    o_all, lse = _banded_fwd(q, k, v, bias, sinks_row)
    attn, post_vjp = jax.vjp(_even_post, o_all, lse.reshape(N_BIAS_HEADS, S))
    attn = attn.astype(MXU_DT)
    h1 = _matmul(attn, w_out, epi='add', extra=h, tm=1024, tn=512, name=f"even_out_{l}")
    return h1, (q, k, v, attn, post_vjp)


def _even_bwd(dh, xn, ctx, w_in, w_out, bias, sinks_row, l):
    q, k, v, attn, post_vjp = ctx
    NH, S, dh_ = q.shape
    nb = S // N_CHUNKS // BLOCK
    d_attn = _matmul(dh, w_out, trans_b=True, tm=1024, tn=768, name=f"even_out_dx_{l}")
    g_w_out = _matmul_tn(attn, dh, tk=768, tn=512, name=f"even_out_dw_{l}")
    do_all, dlse = post_vjp(d_attn)
    dq, dk, dv, dkh, dvh, dbias, dsinks = _banded_bwd(q, k, v, bias, sinks_row, do_all,
                                                      dlse.reshape(NH, N_CHUNKS, nb, BLOCK))

    def fold(t, halo):
        shifted = jnp.concatenate([halo[:, 1:], jnp.zeros_like(halo[:, :1])], axis=1)
        t = t.reshape(NH, N_CHUNKS, nb, BLOCK, dh_)
        t = jnp.concatenate([t[:, :, :nb - 1], t[:, :, nb - 1:] + shifted[:, :, None]], axis=2).reshape(NH, S, dh_)
        ta = t[:A_Q_HEADS].reshape(A_KV_HEADS, A_GROUP, S, dh_).sum(axis=1)
        return jnp.concatenate([ta, t[A_Q_HEADS:]], axis=0)

    dproj = _even_merge(dq, fold(dk, dkh), fold(dv, dvh)).astype(MXU_DT)
    g_w_in = _matmul_tn(xn, dproj, tk=512, tn=1024, name=f"even_in_dw_{l}")
    dxn = _matmul(dproj, w_in, trans_b=True, tm=1024, tn=512, name=f"even_in_dx_{l}")
    return dxn, g_w_in, g_w_out, dbias, dsinks[:A_Q_HEADS, 0, 0]


def _mla_fwd(xn, h, w_down, q_norm, w_uq, kv_norm, w_ukv, w_o, l):
    S = xn.shape[0]
    T = min(FLASH_T, S)
    down = _matmul(xn, w_down, tm=1024, tn=768, name=f"mla_down_{l}")
    c_q, c_kv, kr_raw = down[:, :C_Q_RANK], down[:, C_Q_RANK:C_Q_RANK + C_KV_RANK], down[:, C_Q_RANK + C_KV_RANK:C_DOWN]
    cqn = _rmsnorm(c_q, q_norm, out_dtype=MXU_DT, name=f"mla_qnorm_{l}")
    ckvn = _rmsnorm(c_kv, kv_norm, out_dtype=MXU_DT, name=f"mla_kvnorm_{l}")
    q_lin = _matmul(cqn, w_uq, tm=1024, tn=768, name=f"mla_uq_{l}")
    kv_lin = _matmul(ckvn, w_ukv, tm=1024, tn=1024, name=f"mla_ukv_{l}")
    (qf, kf, vf), pre_vjp = jax.vjp(_mla_pre, q_lin, kv_lin, kr_raw)
    qf, kf, vf = qf.astype(MXU_DT), kf.astype(MXU_DT), vf.astype(MXU_DT)
    qt, kt, vt = _to_tiles_t(qf, T), _to_tiles_t(kf, T), _to_tiles_t(vf, T)
    kn, vn = kf.transpose(1, 0, 2), vf.transpose(1, 0, 2)
    ot, lse = _flash_fwd(qt, kn, vt)
    attn = _from_tiles_t(ot).reshape(S, C_HEADS * C_V).astype(MXU_DT)
    h1 = _matmul(attn, w_o, epi='add', extra=h, tm=1024, tn=512, name=f"mla_o_{l}")
    return h1, (c_q, c_kv, cqn, ckvn, pre_vjp, qt, kn, kt, vn, ot, lse, attn)


def _mla_bwd(dh, xn, ctx, w_down, q_norm, w_uq, kv_norm, w_ukv, w_o, l):
    c_q, c_kv, cqn, ckvn, pre_vjp, qt, kn, kt, vn, ot, lse, attn = ctx
    S = xn.shape[0]
    T = qt.shape[-1]
    d_attn = _matmul(dh, w_o, trans_b=True, tm=1024, tn=512, name=f"mla_o_dx_{l}")
    g_w_o = _matmul_tn(attn, dh, tk=512, tn=512, name=f"mla_o_dw_{l}")
    dot = _to_tiles_t(d_attn.reshape(S, C_HEADS, C_V), T)
    delta = _flash_delta(ot, dot)
    dqt, dk, dv = _flash_bwd(qt, kn, kt, vn, dot, lse, delta)
    dq_lin, dkv_lin, dkr_raw = pre_vjp((_from_tiles_t(dqt), dk.transpose(1, 0, 2), dv.transpose(1, 0, 2)))
    g_w_uq = _matmul_tn(cqn, dq_lin, tk=384, tn=768, name=f"mla_uq_dw_{l}")
    g_w_ukv = _matmul_tn(ckvn, dkv_lin, tk=256, tn=1024, name=f"mla_ukv_dw_{l}")
    dcqn = _matmul(dq_lin, w_uq, trans_b=True, tm=1024, tn=384, name=f"mla_uq_dx_{l}")
    dckvn = _matmul(dkv_lin, w_ukv, trans_b=True, tm=1024, tn=256, name=f"mla_ukv_dx_{l}")
    dc_q, g_q_norm = _rmsnorm_bwd(c_q, q_norm, dcqn, None, name=f"mla_qnorm_bwd_{l}")
    dc_kv, g_kv_norm = _rmsnorm_bwd(c_kv, kv_norm, dckvn, None, name=f"mla_kvnorm_bwd_{l}")
    ddown = jnp.concatenate([dc_q, dc_kv, dkr_raw, jnp.zeros((S, C_DOWN_PAD - C_DOWN), F32)], axis=1).astype(MXU_DT)
    g_w_down = _matmul_tn(xn, ddown, tk=512, tn=768, name=f"mla_down_dw_{l}")[:, :C_DOWN]
    dxn = _matmul(ddown, w_down, trans_b=True, tm=1024, tn=512, name=f"mla_down_dx_{l}")
    return dxn, g_w_down, g_q_norm[0], g_w_uq, g_kv_norm[0], g_w_ukv, g_w_o


def kernel(x, rel_bias, attn_norm, mlp_norm, final_norm, w_in_ab, sinks, w_out_ab, w_down_c, q_norm_c, w_uq_c, kv_norm_c, w_ukv_c, w_o_c, w_mlp_up, w_mlp_down, loss_target, m_rel_bias, m_attn_norm, m_mlp_norm, m_final_norm, m_w_in_ab, m_sinks, m_w_out_ab, m_w_down_c, m_q_norm_c, m_w_uq_c, m_kv_norm_c, m_w_ukv_c, m_w_o_c, m_w_mlp_up, m_w_mlp_down, v_rel_bias, v_attn_norm, v_mlp_norm, v_final_norm, v_w_in_ab, v_sinks, v_w_out_ab, v_w_down_c, v_q_norm_c, v_w_uq_c, v_kv_norm_c, v_w_ukv_c, v_w_o_c, v_w_mlp_up, v_w_mlp_down):
    W = dict(w_in_ab=w_in_ab, w_out_ab=w_out_ab, w_down_c=w_down_c, w_uq_c=w_uq_c, w_ukv_c=w_ukv_c, w_o_c=w_o_c,
             w_mlp_up=w_mlp_up, w_mlp_down=w_mlp_down)
    Mo = dict(w_in_ab=m_w_in_ab, w_out_ab=m_w_out_ab, w_down_c=m_w_down_c, w_uq_c=m_w_uq_c, w_ukv_c=m_w_ukv_c,
              w_o_c=m_w_o_c, w_mlp_up=m_w_mlp_up, w_mlp_down=m_w_mlp_down)
    Vo = dict(w_in_ab=v_w_in_ab, w_out_ab=v_w_out_ab, w_down_c=v_w_down_c, w_uq_c=v_w_uq_c, w_ukv_c=v_w_ukv_c,
              w_o_c=v_w_o_c, w_mlp_up=v_w_mlp_up, w_mlp_down=v_w_mlp_down)
    S = x.shape[1]
    me = 4 * lax.axis_index("x") + 2 * lax.axis_index("y") + lax.axis_index("c")
    big_names = [n for n, _ in _BIG]
    big_shapes = [W[n].shape for n in big_names]

    w_pack = _pack_rows([W[n] for n in big_names])
    gathered = _exchange(w_pack.astype(MXU_DT), False, "gather_weights")
    full = {n: _gathered_to_full(g, ax)
            for (n, ax), g in zip(_BIG, _unpack_rows(gathered, big_shapes))}
    gains = _exchange(jnp.concatenate([_pad_rows8(q_norm_c), _pad_rows8(kv_norm_c)], axis=0), False, "gather_gains")
    n_odd = q_norm_c.shape[0]
    q_norm_full = gains[:, 0].reshape(N_DEV, -1)[:, :q_norm_c.size].reshape(N_DEV, n_odd, -1).transpose(1, 0, 2).reshape(n_odd, C_Q_RANK)
    kv_norm_full = gains[:, 8].reshape(N_DEV, -1)[:, :kv_norm_c.size].reshape(N_DEV, n_odd, -1).transpose(1, 0, 2).reshape(n_odd, C_KV_RANK)
    w_down_pad = jnp.pad(full["w_down_c"], ((0, 0), (0, 0), (0, C_DOWN_PAD - C_DOWN)))

    bias, bias_vjp = jax.vjp(_make_bias, rel_bias)
    sink_rows = [jnp.broadcast_to(jnp.concatenate([sinks[e], jnp.full((B_HEADS,), NEG, F32)])[:, None, None],
                                  (N_BIAS_HEADS, 1, 128)) for e in range(sinks.shape[0])]

    h = x[0]
    saved = []
    for l in range(DEPTH):
        xn = _rmsnorm(h, attn_norm[l], out_dtype=MXU_DT, name=f"attn_norm_{l}")
        if l % 2 == 0:
            e = l // 2
            h1, ctx = _even_fwd(xn, h, full["w_in_ab"][e], full["w_out_ab"][e], bias, sink_rows[e], l)
        else:
            o = l // 2
            h1, ctx = _mla_fwd(xn, h, w_down_pad[o], q_norm_full[o], full["w_uq_c"][o], kv_norm_full[o],
                               full["w_ukv_c"][o], full["w_o_c"][o], l)
        xn2 = _rmsnorm(h1, mlp_norm[l], out_dtype=MXU_DT, name=f"mlp_norm_{l}")
        act = _matmul(xn2, full["w_mlp_up"][l], out_dtype=MXU_DT, epi='relu2', tm=1024, tn=512, name=f"mlp_up_{l}")
        h2 = _matmul(act, full["w_mlp_down"][l], epi='add', extra=h1, tm=512, tn=512, name=f"mlp_down_{l}")
        saved.append((h, xn, h1, xn2, act, ctx))
        h = h2

    loss_row, dh, g_final = _loss_head(h, loss_target[0], final_norm)

    G = {n: [None] * W[n].shape[0] for n in big_names}
    g_attn_norm, g_mlp_norm = [None] * DEPTH, [None] * DEPTH
    g_sinks, g_qn, g_kvn = [None] * sinks.shape[0], [None] * n_odd, [None] * n_odd
    dbias_total = None
    for l in reversed(range(DEPTH)):
        h0, xn, h1, xn2, act, ctx = saved[l]
        du = _matmul(dh, full["w_mlp_down"][l], trans_b=True, out_dtype=MXU_DT, epi='dsq', extra=act,
                     tm=1024, tn=512, name=f"mlp_down_dx_{l}")
        G["w_mlp_down"][l] = _matmul_tn(act, dh, tk=512, tn=1024, name=f"mlp_down_dw_{l}")
        G["w_mlp_up"][l] = _matmul_tn(xn2, du, tk=512, tn=1024, name=f"mlp_up_dw_{l}")
        dxn2 = _matmul(du, full["w_mlp_up"][l], trans_b=True, tm=512, tn=512, name=f"mlp_up_dx_{l}")
        dh, g = _rmsnorm_bwd(h1, mlp_norm[l], dxn2, dh, name=f"mlp_norm_bwd_{l}")
        g_mlp_norm[l] = g[0]
        if l % 2 == 0:
            e = l // 2
            dxn, G["w_in_ab"][e], G["w_out_ab"][e], dbias, g_sinks[e] = _even_bwd(
                dh, xn, ctx, full["w_in_ab"][e], full["w_out_ab"][e], bias, sink_rows[e], l)
            dbias_total = dbias if dbias_total is None else dbias_total + dbias
        else:
            o = l // 2
            dxn, G["w_down_c"][o], g_qn[o], G["w_uq_c"][o], g_kvn[o], G["w_ukv_c"][o], G["w_o_c"][o] = _mla_bwd(
                dh, xn, ctx, w_down_pad[o], q_norm_full[o], full["w_uq_c"][o], kv_norm_full[o],
                full["w_ukv_c"][o], full["w_o_c"][o], l)
        dh, g = _rmsnorm_bwd(h0, attn_norm[l], dxn, dh, name=f"attn_norm_bwd_{l}")
        g_attn_norm[l] = g[0]
    grad_x = dh[None]
    (g_rel_bias,) = bias_vjp(dbias_total)

    send = _pack_rows_per_device([_full_to_shards(jnp.stack(G[n]), ax) for n, ax in _BIG]).astype(MXU_DT)
    landed = _exchange(send, True, "scatter_grads")
    m_pack = _pack_rows([Mo[n] for n in big_names])
    v_pack = _pack_rows([Vo[n] for n in big_names])
    big_out = [_unpack_rows(b, big_shapes) for b in _adamw(landed, w_pack, m_pack, v_pack, name="adamw_sharded")]

    small_g = [g_rel_bias, jnp.stack(g_attn_norm), jnp.stack(g_mlp_norm), g_final[0], jnp.stack(g_sinks),
               jnp.stack(g_qn), jnp.stack(g_kvn), loss_row[0, :1]]
    small_w = [rel_bias, attn_norm, mlp_norm, final_norm, sinks, q_norm_c, kv_norm_c, jnp.zeros((1,), F32)]
    small_m = [m_rel_bias, m_attn_norm, m_mlp_norm, m_final_norm, m_sinks, m_q_norm_c, m_kv_norm_c, jnp.zeros((1,), F32)]
    small_v = [v_rel_bias, v_attn_norm, v_mlp_norm, v_final_norm, v_sinks, v_q_norm_c, v_kv_norm_c, jnp.ones((1,), F32)]
    offs = np.cumsum([0] + [-(-a.size // 1024) * 8 for a in small_g])
    partials = _exchange(jnp.concatenate([_pad_rows8(a) for a in small_g], axis=0), False, "gather_small_grads")

    def mine(i, a_full_shape, local):
        p = partials[:, offs[i]:offs[i + 1]].reshape(N_DEV, -1)[:, :math.prod(a_full_shape)]
        p = p.reshape((N_DEV,) + tuple(a_full_shape))
        if local.shape != tuple(a_full_shape):
            width = local.shape[-1]
            p = lax.dynamic_slice_in_dim(p, me * width, width, axis=p.ndim - 1)
        return jnp.stack([_pad_rows8(p[s]) for s in range(N_DEV)])

    parts_small = jnp.concatenate([mine(i, g.shape, w) for i, (g, w) in enumerate(zip(small_g, small_w))], axis=1)
    pk = lambda arrs: jnp.concatenate([_pad_rows8(a) for a in arrs], axis=0)
    small_out = _adamw(parts_small, pk(small_w), pk(small_m), pk(small_v), name="adamw_small", tr=parts_small.shape[1])
    offs2 = np.cumsum([0] + [-(-a.size // 1024) * 8 for a in small_w])

    def unpack_small(buf):
        return [buf[offs2[i]:offs2[i + 1]].reshape(-1)[:a.size].reshape(a.shape) for i, a in enumerate(small_w)]

    sg, sd, sm, sv = (unpack_small(b) for b in small_out)
    loss = sg[7][0]

    order = ['rel_bias', 'attn_norm', 'mlp_norm', 'final_norm', 'w_in_ab', 'sinks', 'w_out_ab', 'w_down_c', 'q_norm_c',
             'w_uq_c', 'kv_norm_c', 'w_ukv_c', 'w_o_c', 'w_mlp_up', 'w_mlp_down']
    small_idx = {'rel_bias': 0, 'attn_norm': 1, 'mlp_norm': 2, 'final_norm': 3, 'sinks': 4, 'q_norm_c': 5, 'kv_norm_c': 6}

    def pick(kind):
        res = []
        for n in order:
            if n in small_idx:
                res.append((sg, sd, sm, sv)[kind][small_idx[n]])
            else:
                res.append(big_out[kind][big_names.index(n)])
        return res

    return (loss, grad_x, *pick(0), *pick(1), *pick(2), *pick(3))
```

```python
import math

import numpy as np
import jax
import jax.numpy as jnp
from jax import lax
from jax.experimental import pallas as pl
from jax.experimental.pallas import tpu as pltpu

F32 = jnp.float32
MXU_DT = jnp.bfloat16

N_DEV = 8
D_MODEL = 1024
DEPTH = 4
HEAD_DIM = 64
BLOCK = 128
EPS = 1e-6
NEG = -1e30
A_Q_HEADS = 8
A_KV_HEADS = 2
A_GROUP = A_Q_HEADS // A_KV_HEADS
A_WINDOW = 128
B_BRANCHES = ((128, 1), (512, 4), (2048, 16))
B_HPB = 4
B_HEADS = len(B_BRANCHES) * B_HPB
NUM_BUCKETS = 32
MAX_DISTANCE = 2048
N_BIAS_HEADS = A_Q_HEADS + B_HEADS
N_BAND_KV = A_KV_HEADS + B_HEADS
A_IN = (A_Q_HEADS + 2 * A_KV_HEADS) * HEAD_DIM
C_HEADS = 8
C_NOPE = 64
C_ROPE = 32
C_QK = C_NOPE + C_ROPE
C_V = 64
C_Q_RANK = 384
C_KV_RANK = 256
C_DOWN = C_Q_RANK + C_KV_RANK + C_ROPE
C_DOWN_PAD = 768
ROPE_THETA = 10000.0
N_CHUNKS = 16
FLASH_T = 512

ADAM_LR = 0.001
ADAM_B1 = 0.9
ADAM_B2 = 0.999
ADAM_EPS = 1e-08
ADAM_WD = 0.01
ADAM_STEP = 10

V7X_VMEM_BYTES = 64 * 1024 * 1024
VMEM_LIMIT = V7X_VMEM_BYTES - 8 * 1024 * 1024


def _pcall(body, **kw):
    return pl.pallas_call(body, **kw)


def _cparams(*sem):
    return pltpu.CompilerParams(dimension_semantics=sem, vmem_limit_bytes=VMEM_LIMIT)


def _exchange(src, all_to_all, name):
    blk = src.shape[-2:]

    def body(src_ref, out_ref, send_sems, recv_sems, local_sem):
        x, y, c = lax.axis_index("x"), lax.axis_index("y"), lax.axis_index("c")
        me = 4 * x + 2 * y + c

        def piece(dev):
            return src_ref.at[dev] if all_to_all else src_ref

        local = pltpu.make_async_copy(piece(me), out_ref.at[me], local_sem)
        local.start()
        copies = []
        for k in range(1, N_DEV):
            px = 1 - x if (k >> 2) & 1 else x
            py = 1 - y if (k >> 1) & 1 else y
            pc = 1 - c if k & 1 else c
            cp = pltpu.make_async_remote_copy(
                src_ref=piece(4 * px + 2 * py + pc), dst_ref=out_ref.at[me],
                send_sem=send_sems.at[k - 1], recv_sem=recv_sems.at[k - 1],
                device_id=(px, py, pc), device_id_type=pl.DeviceIdType.MESH)
            cp.start()
            copies.append(cp)
        for cp in copies:
            cp.wait()
        local.wait()

    return _pcall(
        body, name=name,
        out_shape=jax.ShapeDtypeStruct((N_DEV,) + blk, src.dtype),
        in_specs=[pl.BlockSpec(memory_space=pl.ANY)],
        out_specs=pl.BlockSpec(memory_space=pl.ANY),
        scratch_shapes=[pltpu.SemaphoreType.DMA((N_DEV - 1,)), pltpu.SemaphoreType.DMA((N_DEV - 1,)),
                        pltpu.SemaphoreType.DMA],
    )(src)


def _matmul(a, b, *, trans_b=False, out_dtype=F32, epi=None, extra=None, tm=512, tn=512, name):
    M, K = a.shape
    N = b.shape[0] if trans_b else b.shape[1]
    tm, tn = min(tm, M), min(tn, N)
    assert M % tm == 0 and N % tn == 0 and (b.shape[1] if trans_b else b.shape[0]) == K
    dn = (((1,), (1,)), ((), ())) if trans_b else (((1,), (0,)), ((), ()))

    def body(*refs):
        a_ref, b_ref = refs[0], refs[1]
        o_ref = refs[-1]
        acc = lax.dot_general(a_ref[...].astype(MXU_DT), b_ref[...].astype(MXU_DT), dn,
                              preferred_element_type=F32)
        if epi == 'relu2':
            r = jnp.maximum(acc, 0.0)
            acc = r * r
        elif epi == 'add':
            acc = acc + refs[2][...].astype(F32)
        elif epi == 'dsq':
            acc = acc * (2.0 * jnp.sqrt(refs[2][...].astype(F32)))
        o_ref[...] = acc.astype(out_dtype)

    b_spec = pl.BlockSpec((tn, K), lambda i, j: (j, 0)) if trans_b else pl.BlockSpec((K, tn), lambda i, j: (0, j))
    in_specs = [pl.BlockSpec((tm, K), lambda i, j: (i, 0)), b_spec]
    args = [a, b]
    if extra is not None:
        in_specs.append(pl.BlockSpec((tm, tn), lambda i, j: (i, j)))
        args.append(extra)
    return _pcall(
        body, name=name, grid=(M // tm, N // tn),
        out_shape=jax.ShapeDtypeStruct((M, N), out_dtype),
        in_specs=in_specs, out_specs=pl.BlockSpec((tm, tn), lambda i, j: (i, j)),
        compiler_params=_cparams("parallel", "parallel"),
    )(*args)


def _matmul_tn(a, b, *, tk=512, tn=512, tm=1024, name):
    M, Ka = a.shape
    N = b.shape[1]
    tk, tn, tm = min(tk, Ka), min(tn, N), min(tm, M)
    assert Ka % tk == 0 and N % tn == 0 and M % tm == 0 and b.shape[0] == M

    def body(a_ref, b_ref, o_ref):
        @pl.when(pl.program_id(2) == 0)
        def _():
            o_ref[...] = jnp.zeros_like(o_ref)

        o_ref[...] += lax.dot_general(a_ref[...].astype(MXU_DT), b_ref[...].astype(MXU_DT),
                                      (((0,), (0,)), ((), ())), preferred_element_type=F32)

    return _pcall(
        body, name=name, grid=(Ka // tk, N // tn, M // tm),
        out_shape=jax.ShapeDtypeStruct((Ka, N), F32),
        in_specs=[pl.BlockSpec((tm, tk), lambda i, j, r: (r, i)), pl.BlockSpec((tm, tn), lambda i, j, r: (r, j))],
        out_specs=pl.BlockSpec((tk, tn), lambda i, j, r: (i, j)),
        compiler_params=_cparams("parallel", "parallel", "arbitrary"),
    )(a, b)


def _rmsnorm(x, g, *, out_dtype, name, tr=512):
    S, D = x.shape
    tr = min(tr, S)

    def body(x_ref, g_ref, o_ref):
        xf = x_ref[...].astype(F32)
        r = lax.rsqrt(jnp.mean(xf * xf, axis=-1, keepdims=True) + EPS)
        o_ref[...] = (xf * r * g_ref[...]).astype(out_dtype)

    return _pcall(
        body, name=name, grid=(S // tr,),
        out_shape=jax.ShapeDtypeStruct((S, D), out_dtype),
        in_specs=[pl.BlockSpec((tr, D), lambda i: (i, 0)), pl.BlockSpec((1, D), lambda i: (0, 0))],
        out_specs=pl.BlockSpec((tr, D), lambda i: (i, 0)),
        compiler_params=_cparams("parallel"),
    )(x, g.reshape(1, D))


def _rmsnorm_bwd(x, g, dy, dres, *, name, tr=512):
    S, D = x.shape
    tr = min(tr, S)

    def body(*refs):
        if dres is None:
            x_ref, g_ref, dy_ref, dx_ref, dg_ref = refs
        else:
            x_ref, g_ref, dy_ref, dres_ref, dx_ref, dg_ref = refs

        @pl.when(pl.program_id(0) == 0)
        def _():
            dg_ref[...] = jnp.zeros_like(dg_ref)

        xf = x_ref[...].astype(F32)
        r = lax.rsqrt(jnp.mean(xf * xf, axis=-1, keepdims=True) + EPS)
        xhat = xf * r
        dyf = dy_ref[...].astype(F32)
        dg_ref[...] += jnp.sum(dyf * xhat, axis=0, keepdims=True)
        dyg = dyf * g_ref[...]
        dx = r * (dyg - xhat * jnp.mean(dyg * xhat, axis=-1, keepdims=True))
        if dres is not None:
            dx = dx + dres_ref[...]
        dx_ref[...] = dx

    row = pl.BlockSpec((tr, D), lambda i: (i, 0))
    vec = pl.BlockSpec((1, D), lambda i: (0, 0))
    args = [x, g.reshape(1, D), dy] + ([] if dres is None else [dres])
    return _pcall(
        body, name=name, grid=(S // tr,),
        out_shape=(jax.ShapeDtypeStruct((S, D), F32), jax.ShapeDtypeStruct((1, D), F32)),
        in_specs=[row, vec, row] + ([] if dres is None else [row]),
        out_specs=(row, vec),
        compiler_params=_cparams("arbitrary"),
    )(*args)


def _loss_head(h, t, g, *, tr=512):
    S, D = h.shape
    tr = min(tr, S)

    def body(h_ref, t_ref, g_ref, loss_ref, dh_ref, dg_ref):
        @pl.when(pl.program_id(0) == 0)
        def _():
            dg_ref[...] = jnp.zeros_like(dg_ref)
            loss_ref[...] = jnp.zeros_like(loss_ref)

        xf = h_ref[...]
        r = lax.rsqrt(jnp.mean(xf * xf, axis=-1, keepdims=True) + EPS)
        xhat = xf * r
        e = xhat * g_ref[...] - t_ref[...]
        part = 0.5 * jnp.sum(jnp.mean(e * e, axis=-1, keepdims=True), axis=0, keepdims=True)
        loss_ref[...] += jnp.broadcast_to(part, loss_ref.shape)
        dy = e * (1.0 / D)
        dg_ref[...] += jnp.sum(dy * xhat, axis=0, keepdims=True)
        dyg = dy * g_ref[...]
        dh_ref[...] = r * (dyg - xhat * jnp.mean(dyg * xhat, axis=-1, keepdims=True))

    row = pl.BlockSpec((tr, D), lambda i: (i, 0))
    vec = pl.BlockSpec((1, D), lambda i: (0, 0))
    return _pcall(
        body, name="loss_head", grid=(S // tr,),
        out_shape=(jax.ShapeDtypeStruct((1, 128), F32), jax.ShapeDtypeStruct((S, D), F32),
                   jax.ShapeDtypeStruct((1, D), F32)),
        in_specs=[row, row, vec],
        out_specs=(pl.BlockSpec((1, 128), lambda i: (0, 0)), row, vec),
        compiler_params=_cparams("arbitrary"),
    )(h, t, g.reshape(1, D))


def _band_kv_head(h):
    return jnp.where(h < A_Q_HEADS, h // A_GROUP, h - (A_Q_HEADS - A_KV_HEADS))


def _band_period(h):
    return jnp.where(h < A_Q_HEADS + B_HPB, 16, jnp.where(h < A_Q_HEADS + 2 * B_HPB, 4, 1))


def _band_specs(S):
    ch = S // N_CHUNKS
    nb = ch // BLOCK
    q_spec = pl.BlockSpec((1, ch, HEAD_DIM), lambda h, c: (h, c, 0))
    kv_spec = pl.BlockSpec((1, ch, HEAD_DIM), lambda h, c: (_band_kv_head(h), c, 0))
    kv_prev = pl.BlockSpec((1, BLOCK, HEAD_DIM), lambda h, c: (_band_kv_head(h), jnp.maximum(c * nb - 1, 0), 0))
    bias_spec = pl.BlockSpec((1, BLOCK, 2 * BLOCK), lambda h, c: (h, 0, 0))
    sink_spec = pl.BlockSpec((1, 1, 128), lambda h, c: (h, 0, 0))
    row_spec = pl.BlockSpec((1, 1, nb, 1, BLOCK), lambda h, c: (h, c, 0, 0, 0))
    return ch, nb, q_spec, kv_spec, kv_prev, bias_spec, sink_spec, row_spec


def _eye():
    return lax.broadcasted_iota(jnp.int32, (BLOCK, BLOCK), 0) == lax.broadcasted_iota(jnp.int32, (BLOCK, BLOCK), 1)


_NT = (((1,), (1,)), ((), ()))
_NN = (((1,), (0,)), ((), ()))
_TN = (((0,), (0,)), ((), ()))


_B_NT = (((2,), (2,)), ((0,), (0,)))
_B_NN = (((2,), (1,)), ((0,), (0,)))


def _bdot(a, b, dn):
    return lax.dot_general(a, b, dn, preferred_element_type=F32)


def _with_prev(first, t3):
    return first[None] if t3.shape[0] == 1 else jnp.concatenate([first[None], t3[:-1]], axis=0)


def _mask_first(s_prev, prev_ok):
    s0 = jnp.where(prev_ok, s_prev[0], NEG)[None]
    return s0 if s_prev.shape[0] == 1 else jnp.concatenate([s0, s_prev[1:]], axis=0)


def _banded_fwd(q, k, v, bias, sinks):
    NH, S, dh = q.shape
    ch, nb, q_spec, kv_spec, kv_prev, bias_spec, sink_spec, row_spec = _band_specs(S)
    scale = HEAD_DIM ** -0.5

    def body(q_ref, k_ref, kp_ref, v_ref, vp_ref, b_ref, s_ref, o_ref, lse_ref):
        h, c = pl.program_id(0), pl.program_id(1)
        prev_ok = (c % _band_period(h)) != 0
        q3, k3, v3 = (r[0].reshape(nb, BLOCK, dh) for r in (q_ref, k_ref, v_ref))
        kp3, vp3 = _with_prev(kp_ref[0], k3), _with_prev(vp_ref[0], v3)
        sink = s_ref[0, :, :1]
        s_cur = _bdot(q3, k3, _B_NT) * scale + b_ref[0, :, BLOCK:][None]
        s_prev = _mask_first(_bdot(q3, kp3, _B_NT) * scale + b_ref[0, :, :BLOCK][None], prev_ok)
        m = jnp.maximum(jnp.max(s_cur, axis=-1, keepdims=True), jnp.max(s_prev, axis=-1, keepdims=True))
        m = jnp.maximum(m, sink)
        p_cur = jnp.exp(s_cur - m)
        p_prev = jnp.exp(s_prev - m)
        l = jnp.sum(p_cur, axis=-1, keepdims=True) + jnp.sum(p_prev, axis=-1, keepdims=True) + jnp.exp(sink - m)
        acc = _bdot(p_cur.astype(MXU_DT), v3, _B_NN) + _bdot(p_prev.astype(MXU_DT), vp3, _B_NN)
        o_ref[0] = (acc / l).reshape(ch, dh)
        lse = m + jnp.log(l)
        lse_ref[0, 0] = jnp.sum(jnp.where(_eye()[None], lse, 0.0), axis=1, keepdims=True)

    return _pcall(
        body, name="banded_fwd", grid=(NH, N_CHUNKS),
        out_shape=(jax.ShapeDtypeStruct((NH, S, dh), F32), jax.ShapeDtypeStruct((NH, N_CHUNKS, nb, 1, BLOCK), F32)),
        in_specs=[q_spec, kv_spec, kv_prev, kv_spec, kv_prev, bias_spec, sink_spec],
        out_specs=(q_spec, row_spec),
        compiler_params=_cparams("parallel", "parallel"),
    )(q, k, k, v, v, bias, sinks)


def _banded_bwd(q, k, v, bias_t, sinks, do, dlse):
    NH, S, dh = q.shape
    ch, nb, q_spec, kv_spec, kv_prev, _, sink_spec, row_spec = _band_specs(S)
    bias_spec = pl.BlockSpec((1, 2 * BLOCK, BLOCK), lambda h, c: (h, 0, 0))
    scale = HEAD_DIM ** -0.5

    def body(q_ref, k_ref, kp_ref, v_ref, vp_ref, b_ref, s_ref, do_ref, dl_ref,
             dq_ref, dk_ref, dv_ref, dkh_ref, dvh_ref, db_ref, ds_ref):
        h, c = pl.program_id(0), pl.program_id(1)

        @pl.when(c == 0)
        def _():
            db_ref[...] = jnp.zeros_like(db_ref)
            ds_ref[...] = jnp.zeros_like(ds_ref)

        prev_ok = (c % _band_period(h)) != 0
        q3, k3, v3 = (r[0].reshape(nb, BLOCK, dh) for r in (q_ref, k_ref, v_ref))
        do3 = do_ref[0].astype(MXU_DT).reshape(nb, BLOCK, dh)
        kp3, vp3 = _with_prev(kp_ref[0], k3), _with_prev(vp_ref[0], v3)
        sink = s_ref[0, :, :1]
        s_cur = _bdot(k3, q3, _B_NT) * scale + b_ref[0, BLOCK:, :][None]
        s_prev = _mask_first(_bdot(kp3, q3, _B_NT) * scale + b_ref[0, :BLOCK, :][None], prev_ok)
        m = jnp.maximum(jnp.max(s_cur, axis=1, keepdims=True), jnp.max(s_prev, axis=1, keepdims=True))
        m = jnp.maximum(m, sink)
        p_cur = jnp.exp(s_cur - m)
        p_prev = jnp.exp(s_prev - m)
        p_sink = jnp.exp(sink - m)
        inv = 1.0 / (jnp.sum(p_cur, axis=1, keepdims=True) + jnp.sum(p_prev, axis=1, keepdims=True) + p_sink)
        p_cur, p_prev, p_sink = p_cur * inv, p_prev * inv, p_sink * inv
        dp_cur = _bdot(v3, do3, _B_NT)
        dp_prev = _bdot(vp3, do3, _B_NT)
        delta = jnp.sum(p_cur * dp_cur, axis=1, keepdims=True) + jnp.sum(p_prev * dp_prev, axis=1, keepdims=True)
        t = dl_ref[0, 0] - delta
        ds_cur = p_cur * (dp_cur + t)
        ds_prev = p_prev * (dp_prev + t)
        dsink = jnp.sum(jnp.sum(p_sink * t, axis=0), axis=-1, keepdims=True)
        ds_ref[...] += jnp.broadcast_to(dsink, ds_ref.shape)
        db_ref[0, :BLOCK, :] += jnp.sum(ds_prev, axis=0)
        db_ref[0, BLOCK:, :] += jnp.sum(ds_cur, axis=0)
        dsb_cur = (ds_cur * scale).astype(MXU_DT)
        dsb_prev = (ds_prev * scale).astype(MXU_DT)
        dk_prev = _bdot(dsb_prev, q3, _B_NN)
        dv_prev = _bdot(p_prev.astype(MXU_DT), do3, _B_NN)

        def shifted(t3):
            z = jnp.zeros((1, BLOCK, dh), F32)
            return z if nb == 1 else jnp.concatenate([t3[1:], z], axis=0)

        dk_ref[0] = (_bdot(dsb_cur, q3, _B_NN) + shifted(dk_prev)).reshape(ch, dh)
        dv_ref[0] = (_bdot(p_cur.astype(MXU_DT), do3, _B_NN) + shifted(dv_prev)).reshape(ch, dh)
        dkh_ref[0, 0] = dk_prev[0]
        dvh_ref[0, 0] = dv_prev[0]
        for b in range(nb):
            dq_ref[0, b * BLOCK:(b + 1) * BLOCK, :] = (
                lax.dot_general(dsb_cur[b], k3[b], _TN, preferred_element_type=F32)
                + lax.dot_general(dsb_prev[b], kp3[b], _TN, preferred_element_type=F32))

    halo_spec = pl.BlockSpec((1, 1, BLOCK, HEAD_DIM), lambda h, c: (h, c, 0, 0))
    big = jax.ShapeDtypeStruct((NH, S, dh), F32)
    halo = jax.ShapeDtypeStruct((NH, N_CHUNKS, BLOCK, dh), F32)
    return _pcall(
        body, name="banded_bwd", grid=(NH, N_CHUNKS),
        out_shape=(big, big, big, halo, halo, jax.ShapeDtypeStruct(bias_t.shape, F32),
                   jax.ShapeDtypeStruct(sinks.shape, F32)),
        in_specs=[q_spec, kv_spec, kv_prev, kv_spec, kv_prev, bias_spec, sink_spec, q_spec, row_spec],
        out_specs=(q_spec, q_spec, q_spec, halo_spec, halo_spec, bias_spec, sink_spec),
        compiler_params=_cparams("arbitrary", "arbitrary"),
    )(q, k, k, v, v, bias_t, sinks, do, dlse)


def _causal_mask(T):
    return lax.broadcasted_iota(jnp.int32, (T, T), 0) <= lax.broadcasted_iota(jnp.int32, (T, T), 1)


LOG2E = math.log2(math.e)
FLASH_SPLIT = 2
FLASH_ONES_ROWS = 16


def _flash_fwd(qt, k, vt1):
    H, nq, dqk, T = qt.shape
    S = k.shape[1]
    dva = vt1.shape[2]
    dv = dva - FLASH_ONES_ROWS
    scale = dqk ** -0.5
    c = scale * LOG2E
    th = T // FLASH_SPLIT

    def body(qt_ref, k_ref, vt_ref, ot_ref, lse_ref, sa_ref, sb_ref):
        i = pl.program_id(1)

        def scores(j):
            kb = k_ref[0, pl.ds(pl.multiple_of(j * T, T), T), :]
            return lax.dot_general(kb, qt_ref[0, 0], _NN, preferred_element_type=F32)

        def softmax_pv(s_ref, j, carry, masked):
            m, acc = carry
            raw = s_ref[...]
            if masked:
                raw = jnp.where(_causal_mask(T), raw, NEG)
            m_new = jnp.maximum(m, jnp.max(raw, axis=0, keepdims=True))
            alpha = jnp.exp2((m - m_new) * c)
            pb = jnp.exp2((raw - m_new) * c).astype(MXU_DT)
            acc = acc * alpha + lax.dot_general(vt_ref[0, j], pb, _NN, preferred_element_type=F32)
            return m_new, acc

        def pair(p, carry):
            j = 2 * p
            sb_ref[...] = scores(j + 1)
            carry = softmax_pv(sa_ref, j, carry, False)
            sa_ref[...] = scores(j + 2)
            return softmax_pv(sb_ref, j + 1, carry, False)

        def even_tail(carry):
            return softmax_pv(sa_ref, i, carry, True)

        def odd_tail(carry):
            sb_ref[...] = scores(i)
            carry = softmax_pv(sa_ref, i - 1, carry, False)
            return softmax_pv(sb_ref, i, carry, True)

        sa_ref[...] = scores(0)
        carry = lax.fori_loop(0, i // 2, pair, (jnp.full((1, T), NEG, F32), jnp.zeros((dva, T), F32)))
        m, acc = lax.cond(i % 2 == 0, even_tail, odd_tail, carry)
        l = acc[dv:dv + 1]
        ot_ref[0, 0] = acc[:dv] / l
        lse_ref[0, 0] = m * scale + jnp.log(l)

    return _pcall(
        body, name="flash_fwd", grid=(H, nq),
        out_shape=(jax.ShapeDtypeStruct((H, nq, dv, T), F32), jax.ShapeDtypeStruct((H, nq, 1, T), F32)),
        in_specs=[pl.BlockSpec((1, 1, dqk, T), lambda h, i: (h, i, 0, 0)),
                  pl.BlockSpec((1, S, dqk), lambda h, i: (h, 0, 0)),
                  pl.BlockSpec((1, nq, dva, T), lambda h, i: (h, 0, 0, 0))],
        out_specs=(pl.BlockSpec((1, 1, dv, T), lambda h, i: (h, i, 0, 0)),
                   pl.BlockSpec((1, 1, 1, T), lambda h, i: (h, i, 0, 0))),
        scratch_shapes=[pltpu.VMEM((T, T), F32), pltpu.VMEM((T, T), F32)],
        compiler_params=_cparams("parallel", "parallel"),
    )(qt, k, vt1)


def _flash_delta(ot, dot):
    H, nq, dv, T = ot.shape

    def body(o_ref, do_ref, d_ref):
        d_ref[0, 0] = jnp.sum(o_ref[0, 0] * do_ref[0, 0], axis=0, keepdims=True)

    spec = pl.BlockSpec((1, 1, dv, T), lambda h, i: (h, i, 0, 0))
    return _pcall(
        body, name="flash_delta", grid=(H, nq),
        out_shape=jax.ShapeDtypeStruct((H, nq, 1, T), F32),
        in_specs=[spec, spec], out_specs=pl.BlockSpec((1, 1, 1, T), lambda h, i: (h, i, 0, 0)),
        compiler_params=_cparams("parallel", "parallel"),
    )(ot, dot)


def _flash_bwd(qt, k, kt, v, dot, lse, delta):
    H, nq, dqk, T = qt.shape
    dv_ = v.shape[2]
    scale = dqk ** -0.5
    c = scale * LOG2E
    th = T // FLASH_SPLIT

    def body(qt_ref, k_ref, kt_ref, v_ref, dot_ref, lse_ref, del_ref, dqt_ref, dkt_ref, dvt_ref,
             sa_ref, pa_ref, sb_ref, pb_ref):
        j = pl.program_id(1)

        @pl.when(j == 0)
        def _():
            dqt_ref[...] = jnp.zeros_like(dqt_ref)

        n_un = nq - 1 - j

        def issue(i, s_ref, dp_ref):
            s_ref[...] = lax.dot_general(k_ref[0], qt_ref[0, i], _NN, preferred_element_type=F32)
            dp_ref[...] = lax.dot_general(v_ref[0], dot_ref[0, i].astype(MXU_DT), _NN, preferred_element_type=F32)

        def consume(i, s_ref, dp_ref, carry, masked):
            dkt, dvt = carry
            raw = s_ref[...]
            if masked:
                raw = jnp.where(_causal_mask(T), raw, NEG)
            p = jnp.exp2(raw * c - lse_ref[0, i] * LOG2E)
            dsb = (p * (dp_ref[...] - del_ref[0, i])).astype(MXU_DT)
            dvt = dvt + lax.dot_general(dot_ref[0, i].astype(MXU_DT), p.astype(MXU_DT), _NT, preferred_element_type=F32)
            dkt = dkt + lax.dot_general(qt_ref[0, i], dsb, _NT, preferred_element_type=F32)
            dqt_ref[0, i] += lax.dot_general(kt_ref[0, 0], dsb, _NN, preferred_element_type=F32) * scale
            return dkt, dvt

        def pair(p, carry):
            i0 = j + 1 + 2 * p
            issue(i0 + 1, sb_ref, pb_ref)
            carry = consume(i0, sa_ref, pa_ref, carry, False)
            issue(jnp.where(2 * p + 2 < n_un, i0 + 2, j), sa_ref, pa_ref)
            return consume(i0 + 1, sb_ref, pb_ref, carry, False)

        def even_tail(carry):
            return consume(j, sa_ref, pa_ref, carry, True)

        def odd_tail(carry):
            issue(j, sb_ref, pb_ref)
            carry = consume(nq - 1, sa_ref, pa_ref, carry, False)
            return consume(j, sb_ref, pb_ref, carry, True)

        issue(jnp.where(n_un > 0, j + 1, j), sa_ref, pa_ref)
        carry = lax.fori_loop(0, n_un // 2, pair, (jnp.zeros((dqk, T), F32), jnp.zeros((dv_, T), F32)))
        dkt, dvt = lax.cond(n_un % 2 == 0, even_tail, odd_tail, carry)
        dkt_ref[0, 0] = dkt * scale
        dvt_ref[0, 0] = dvt

    whole = lambda d: pl.BlockSpec((1, nq, d, T), lambda h, j: (h, 0, 0, 0))
    tile_t = lambda d: pl.BlockSpec((1, 1, d, T), lambda h, j: (h, j, 0, 0))
    return _pcall(
        body, name="flash_bwd", grid=(H, nq),
        out_shape=(jax.ShapeDtypeStruct((H, nq, dqk, T), F32), jax.ShapeDtypeStruct((H, nq, dqk, T), F32),
                   jax.ShapeDtypeStruct((H, nq, dv_, T), F32)),
        in_specs=[whole(dqk),
                  pl.BlockSpec((1, T, dqk), lambda h, j: (h, j, 0)),
                  tile_t(dqk),
                  pl.BlockSpec((1, T, dv_), lambda h, j: (h, j, 0)),
                  whole(dv_), whole(1), whole(1)],
        out_specs=(whole(dqk), tile_t(dqk), tile_t(dv_)),
        scratch_shapes=[pltpu.VMEM((T, T), F32) for _ in range(4)],
        compiler_params=_cparams("arbitrary", "arbitrary"),
    )(qt, k, kt, v, dot, lse, delta)


def _adamw(parts, w, m, v, *, name, tr=512):
    P, R, C = parts.shape
    tr = min(tr, R)
    assert R % tr == 0

    def body(p_ref, w_ref, m_ref, v_ref, g_ref, d_ref, m2_ref, v2_ref):
        g = p_ref[0].astype(F32)
        for s in range(1, P):
            g = g + p_ref[s].astype(F32)
        m2 = ADAM_B1 * m_ref[...] + (1.0 - ADAM_B1) * g
        v2 = ADAM_B2 * v_ref[...] + (1.0 - ADAM_B2) * jnp.square(g)
        m_hat = m2 / (1.0 - ADAM_B1 ** ADAM_STEP)
        v_hat = v2 / (1.0 - ADAM_B2 ** ADAM_STEP)
        g_ref[...] = g
        d_ref[...] = -ADAM_LR * (m_hat / (jnp.sqrt(v_hat) + ADAM_EPS) + ADAM_WD * w_ref[...])
        m2_ref[...] = m2
        v2_ref[...] = v2

    row = pl.BlockSpec((tr, C), lambda i: (i, 0))
    out = jax.ShapeDtypeStruct((R, C), F32)
    return _pcall(
        body, name=name, grid=(R // tr,),
        out_shape=(out, out, out, out),
        in_specs=[pl.BlockSpec((P, tr, C), lambda i: (0, i, 0)), row, row, row],
        out_specs=(row, row, row, row),
        compiler_params=_cparams("parallel"),
    )(parts, w, m, v)


def _bias_tables():
    i = np.arange(BLOCK)[:, None]
    j = np.arange(2 * BLOCK)[None, :]
    dist = i + BLOCK - j
    out = []
    for dil, max_dist in [(1, A_WINDOW - 1)] + [(d, w // d) for w, d in B_BRANCHES]:
        n = np.maximum(dist, 0) * dil
        max_exact = NUM_BUCKETS // 2
        nf = np.maximum(n, 1).astype(np.float64)
        val = np.log(nf / max_exact) / math.log(MAX_DISTANCE / max_exact) * (NUM_BUCKETS - max_exact)
        inband = (dist >= 0) & (dist <= max_dist)
        frac = np.abs(val - np.round(val))
        last = NUM_BUCKETS - 1 - max_exact
        assert np.all((frac > 2e-5) | (n <= max_exact) | (val >= last) | ~inband)
        large = max_exact + val.astype(np.int64)
        bucket = np.where(n < max_exact, n, np.minimum(large, NUM_BUCKETS - 1))
        onehot = (bucket[..., None] == np.arange(NUM_BUCKETS)).astype(np.float32)
        out.append((onehot.reshape(-1, NUM_BUCKETS), inband))
    return out


def _make_bias(rel_bias):
    tabs = _bias_tables()
    groups = [(0, A_Q_HEADS)] + [(A_Q_HEADS + g * B_HPB, B_HPB) for g in range(len(B_BRANCHES))]
    parts = []
    for (onehot, inband), (h0, nh) in zip(tabs, groups):
        b = jnp.dot(jnp.asarray(onehot), rel_bias[:, h0:h0 + nh], precision=lax.Precision.HIGHEST)
        b = b.reshape(BLOCK, 2 * BLOCK, nh)
        b = jnp.where(jnp.asarray(inband)[..., None], b, NEG)
        parts.append(b.transpose(2, 0, 1))
    return jnp.concatenate(parts, axis=0)


def _dilate_heads(t, d):
    S, H, dh = t.shape
    return t.reshape(S // d, d, H, dh).transpose(2, 1, 0, 3).reshape(H, S, dh)


def _undilate_heads(t, d):
    H, S, dh = t.shape
    return t.reshape(H, d, S // d, dh).transpose(2, 1, 0, 3).reshape(S, H, dh)


def _even_split(proj):
    S = proj.shape[0]
    qd, kd = A_Q_HEADS * HEAD_DIM, A_KV_HEADS * HEAD_DIM
    pa = proj[:, :A_IN]
    qs = [pa[:, :qd].reshape(S, A_Q_HEADS, HEAD_DIM).transpose(1, 0, 2)]
    ks = [pa[:, qd:qd + kd].reshape(S, A_KV_HEADS, HEAD_DIM).transpose(1, 0, 2)]
    vs = [pa[:, qd + kd:].reshape(S, A_KV_HEADS, HEAD_DIM).transpose(1, 0, 2)]
    pb = proj[:, A_IN:].reshape(S, len(B_BRANCHES), 3, B_HPB, HEAD_DIM)
    for g, (_, d) in enumerate(B_BRANCHES):
        qs.append(_dilate_heads(pb[:, g, 0], d))
        ks.append(_dilate_heads(pb[:, g, 1], d))
        vs.append(_dilate_heads(pb[:, g, 2], d))
    return jnp.concatenate(qs, 0), jnp.concatenate(ks, 0), jnp.concatenate(vs, 0)


def _even_merge(dq, dk, dv):
    S = dq.shape[1]
    cols = [dq[:A_Q_HEADS].transpose(1, 0, 2).reshape(S, -1),
            dk[:A_KV_HEADS].transpose(1, 0, 2).reshape(S, -1),
            dv[:A_KV_HEADS].transpose(1, 0, 2).reshape(S, -1)]
    for g, (_, d) in enumerate(B_BRANCHES):
        qh = slice(A_Q_HEADS + g * B_HPB, A_Q_HEADS + (g + 1) * B_HPB)
        kh = slice(A_KV_HEADS + g * B_HPB, A_KV_HEADS + (g + 1) * B_HPB)
        for t in (dq[qh], dk[kh], dv[kh]):
            cols.append(_undilate_heads(t, d).reshape(S, -1))
    return jnp.concatenate(cols, axis=1)


def _even_post(o_all, lse_all):
    S = o_all.shape[1]
    out_a = o_all[:A_Q_HEADS].transpose(1, 0, 2).reshape(S, -1)
    outs, lses = [], []
    for g, (_, d) in enumerate(B_BRANCHES):
        hs = slice(A_Q_HEADS + g * B_HPB, A_Q_HEADS + (g + 1) * B_HPB)
        outs.append(_undilate_heads(o_all[hs], d))
        lses.append(_undilate_heads(lse_all[hs][..., None], d)[..., 0])
    wts = jax.nn.softmax(jnp.stack(lses), axis=0)
    out_b = sum(wts[g][..., None] * outs[g] for g in range(len(B_BRANCHES))).reshape(S, -1)
    return jnp.concatenate([out_a, out_b], axis=-1)


def _rope(t):
    S, r = t.shape[1], t.shape[-1]
    inv = ROPE_THETA ** (-jnp.arange(0, r, 2, dtype=jnp.float32) / r)
    ang = jnp.arange(S, dtype=jnp.float32)[:, None] * inv[None, :]
    shape = (1, S) + (1,) * (t.ndim - 3) + (r // 2,)
    cos, sin = jnp.cos(ang).reshape(shape), jnp.sin(ang).reshape(shape)
    t1, t2 = t[..., :r // 2], t[..., r // 2:]
    return jnp.concatenate([t1 * cos - t2 * sin, t1 * sin + t2 * cos], axis=-1)


def _mla_pre(q_lin, kv_lin, kr_raw):
    S = q_lin.shape[0]
    q = q_lin.reshape(1, S, C_HEADS, C_QK)
    qf = jnp.concatenate([q[..., :C_NOPE], _rope(q[..., C_NOPE:])], axis=-1)[0]
    kv = kv_lin.reshape(S, C_HEADS, C_NOPE + C_V)
    kr = _rope(kr_raw[None])[0]
    kf = jnp.concatenate([kv[..., :C_NOPE], jnp.broadcast_to(kr[:, None, :], (S, C_HEADS, C_ROPE))], axis=-1)
    return qf, kf, kv[..., C_NOPE:]


def _to_tiles_t(t, T):
    S, H, d = t.shape
    return t.reshape(S // T, T, H, d).transpose(2, 0, 3, 1)


def _from_tiles_t(t):
    H, n, d, T = t.shape
    return t.transpose(1, 3, 0, 2).reshape(n * T, H, d)


_BIG = (("w_in_ab", 2), ("w_out_ab", 2), ("w_down_c", 1), ("w_uq_c", 2), ("w_ukv_c", 2), ("w_o_c", 2),
        ("w_mlp_up", 2), ("w_mlp_down", 1))
_ROW_ALIGN = 512


def _pack_rows(arrs):
    rows = [a.reshape(-1, 128) for a in arrs]
    n = sum(r.shape[0] for r in rows)
    pad = (-n) % _ROW_ALIGN
    if pad:
        rows.append(jnp.zeros((pad, 128), rows[0].dtype))
    return jnp.concatenate(rows, axis=0)


def _pack_rows_per_device(arrs):
    rows = [a.reshape(N_DEV, -1, 128) for a in arrs]
    n = sum(r.shape[1] for r in rows)
    pad = (-n) % _ROW_ALIGN
    if pad:
        rows.append(jnp.zeros((N_DEV, pad, 128), rows[0].dtype))
    return jnp.concatenate(rows, axis=1)


def _unpack_rows(buf, shapes):
    out, r0 = [], 0
    for shp in shapes:
        n = math.prod(shp) // 128
        out.append(buf[..., r0:r0 + n, :].reshape(buf.shape[:-2] + tuple(shp)))
        r0 += n
    return out


def _gathered_to_full(g, axis):
    if axis == 2:
        return g.transpose(1, 2, 0, 3).reshape(g.shape[1], g.shape[2], N_DEV * g.shape[3])
    return g.transpose(1, 0, 2, 3).reshape(g.shape[1], N_DEV * g.shape[2], g.shape[3])


def _full_to_shards(t, axis):
    a, b, c = t.shape
    if axis == 2:
        return t.reshape(a, b, N_DEV, c // N_DEV).transpose(2, 0, 1, 3)
    return t.reshape(a, N_DEV, b // N_DEV, c).transpose(1, 0, 2, 3)


def _pad_rows8(a):
    flat = a.reshape(-1)
    n = -(-flat.shape[0] // 1024) * 1024
    return jnp.pad(flat, (0, n - flat.shape[0])).reshape(-1, 128)


def _even_fwd(xn, h, w_in, w_out, bias, sinks_row, l):
    S = xn.shape[0]
    proj = _matmul(xn, w_in, out_dtype=MXU_DT, tm=1024, tn=512, name=f"even_in_{l}")
    q, k, v = _even_split(proj)
    o_all, lse = _banded_fwd(q, k, v, bias, sinks_row)
    attn, post_vjp = jax.vjp(_even_post, o_all, lse.reshape(N_BIAS_HEADS, S))
    attn = attn.astype(MXU_DT)
    h1 = _matmul(attn, w_out, epi='add', extra=h, tm=1024, tn=512, name=f"even_out_{l}")
    return h1, (q, k, v, attn, post_vjp)


def _even_bwd(dh, xn, ctx, w_in, w_out, bias, sinks_row, l):
    q, k, v, attn, post_vjp = ctx
    NH, S, dh_ = q.shape
    nb = S // N_CHUNKS // BLOCK
    d_attn = _matmul(dh, w_out, trans_b=True, tm=1024, tn=768, name=f"even_out_dx_{l}")
    g_w_out = _matmul_tn(attn, dh, tk=768, tn=512, name=f"even_out_dw_{l}")
    do_all, dlse = post_vjp(d_attn)
    dq, dk, dv, dkh, dvh, dbias_t, dsinks = _banded_bwd(q, k, v, bias.transpose(0, 2, 1), sinks_row, do_all,
                                                        dlse.reshape(NH, N_CHUNKS, nb, 1, BLOCK))
    dbias = dbias_t.transpose(0, 2, 1)

    def fold(t, halo):
        shifted = jnp.concatenate([halo[:, 1:], jnp.zeros_like(halo[:, :1])], axis=1)
        t = t.reshape(NH, N_CHUNKS, nb, BLOCK, dh_)
        t = jnp.concatenate([t[:, :, :nb - 1], t[:, :, nb - 1:] + shifted[:, :, None]], axis=2).reshape(NH, S, dh_)
        ta = t[:A_Q_HEADS].reshape(A_KV_HEADS, A_GROUP, S, dh_).sum(axis=1)
        return jnp.concatenate([ta, t[A_Q_HEADS:]], axis=0)

    dproj = _even_merge(dq, fold(dk, dkh), fold(dv, dvh)).astype(MXU_DT)
    g_w_in = _matmul_tn(xn, dproj, tk=512, tn=1024, name=f"even_in_dw_{l}")
    dxn = _matmul(dproj, w_in, trans_b=True, tm=1024, tn=512, name=f"even_in_dx_{l}")
    return dxn, g_w_in, g_w_out, dbias, dsinks[:A_Q_HEADS, 0, 0]


def _mla_fwd(xn, h, w_down, q_norm, w_uq, kv_norm, w_ukv, w_o, l):
    S = xn.shape[0]
    T = min(FLASH_T, S)
    down = _matmul(xn, w_down, tm=1024, tn=768, name=f"mla_down_{l}")
    c_q, c_kv, kr_raw = down[:, :C_Q_RANK], down[:, C_Q_RANK:C_Q_RANK + C_KV_RANK], down[:, C_Q_RANK + C_KV_RANK:C_DOWN]
    cqn = _rmsnorm(c_q, q_norm, out_dtype=MXU_DT, name=f"mla_qnorm_{l}")
    ckvn = _rmsnorm(c_kv, kv_norm, out_dtype=MXU_DT, name=f"mla_kvnorm_{l}")
    q_lin = _matmul(cqn, w_uq, tm=1024, tn=768, name=f"mla_uq_{l}")
    kv_lin = _matmul(ckvn, w_ukv, tm=1024, tn=1024, name=f"mla_ukv_{l}")
    (qf, kf, vf), pre_vjp = jax.vjp(_mla_pre, q_lin, kv_lin, kr_raw)
    qf, kf, vf = qf.astype(MXU_DT), kf.astype(MXU_DT), vf.astype(MXU_DT)
    qt, kt, vt = _to_tiles_t(qf, T), _to_tiles_t(kf, T), _to_tiles_t(vf, T)
    kn, vn = kf.transpose(1, 0, 2), vf.transpose(1, 0, 2)
    ones = jnp.concatenate([jnp.ones(vt.shape[:2] + (1, T), MXU_DT),
                            jnp.zeros(vt.shape[:2] + (FLASH_ONES_ROWS - 1, T), MXU_DT)], axis=2)
    ot, lse = _flash_fwd(qt, kn, jnp.concatenate([vt, ones], axis=2))
    attn = _from_tiles_t(ot).reshape(S, C_HEADS * C_V).astype(MXU_DT)
    h1 = _matmul(attn, w_o, epi='add', extra=h, tm=1024, tn=512, name=f"mla_o_{l}")
    return h1, (c_q, c_kv, cqn, ckvn, pre_vjp, qt, kn, kt, vn, ot, lse, attn)


def _mla_bwd(dh, xn, ctx, w_down, q_norm, w_uq, kv_norm, w_ukv, w_o, l):
    c_q, c_kv, cqn, ckvn, pre_vjp, qt, kn, kt, vn, ot, lse, attn = ctx
    S = xn.shape[0]
    T = qt.shape[-1]
    d_attn = _matmul(dh, w_o, trans_b=True, tm=1024, tn=512, name=f"mla_o_dx_{l}")
    g_w_o = _matmul_tn(attn, dh, tk=512, tn=512, name=f"mla_o_dw_{l}")
    dot = _to_tiles_t(d_attn.reshape(S, C_HEADS, C_V), T)
    delta = _flash_delta(ot, dot)
    dqt, dkt, dvt = _flash_bwd(qt, kn, kt, vn, dot, lse, delta)
    dq_lin, dkv_lin, dkr_raw = pre_vjp((_from_tiles_t(dqt), _from_tiles_t(dkt), _from_tiles_t(dvt)))
    g_w_uq = _matmul_tn(cqn, dq_lin, tk=384, tn=768, name=f"mla_uq_dw_{l}")
    g_w_ukv = _matmul_tn(ckvn, dkv_lin, tk=256, tn=1024, name=f"mla_ukv_dw_{l}")
    dcqn = _matmul(dq_lin, w_uq, trans_b=True, tm=1024, tn=384, name=f"mla_uq_dx_{l}")
    dckvn = _matmul(dkv_lin, w_ukv, trans_b=True, tm=1024, tn=256, name=f"mla_ukv_dx_{l}")
    dc_q, g_q_norm = _rmsnorm_bwd(c_q, q_norm, dcqn, None, name=f"mla_qnorm_bwd_{l}")
    dc_kv, g_kv_norm = _rmsnorm_bwd(c_kv, kv_norm, dckvn, None, name=f"mla_kvnorm_bwd_{l}")
    ddown = jnp.concatenate([dc_q, dc_kv, dkr_raw, jnp.zeros((S, C_DOWN_PAD - C_DOWN), F32)], axis=1).astype(MXU_DT)
    g_w_down = _matmul_tn(xn, ddown, tk=512, tn=768, name=f"mla_down_dw_{l}")[:, :C_DOWN]
    dxn = _matmul(ddown, w_down, trans_b=True, tm=1024, tn=512, name=f"mla_down_dx_{l}")
    return dxn, g_w_down, g_q_norm[0], g_w_uq, g_kv_norm[0], g_w_ukv, g_w_o


def kernel(x, rel_bias, attn_norm, mlp_norm, final_norm, w_in_ab, sinks, w_out_ab, w_down_c, q_norm_c, w_uq_c, kv_norm_c, w_ukv_c, w_o_c, w_mlp_up, w_mlp_down, loss_target, m_rel_bias, m_attn_norm, m_mlp_norm, m_final_norm, m_w_in_ab, m_sinks, m_w_out_ab, m_w_down_c, m_q_norm_c, m_w_uq_c, m_kv_norm_c, m_w_ukv_c, m_w_o_c, m_w_mlp_up, m_w_mlp_down, v_rel_bias, v_attn_norm, v_mlp_norm, v_final_norm, v_w_in_ab, v_sinks, v_w_out_ab, v_w_down_c, v_q_norm_c, v_w_uq_c, v_kv_norm_c, v_w_ukv_c, v_w_o_c, v_w_mlp_up, v_w_mlp_down):
    W = dict(w_in_ab=w_in_ab, w_out_ab=w_out_ab, w_down_c=w_down_c, w_uq_c=w_uq_c, w_ukv_c=w_ukv_c, w_o_c=w_o_c,
             w_mlp_up=w_mlp_up, w_mlp_down=w_mlp_down)
    Mo = dict(w_in_ab=m_w_in_ab, w_out_ab=m_w_out_ab, w_down_c=m_w_down_c, w_uq_c=m_w_uq_c, w_ukv_c=m_w_ukv_c,
              w_o_c=m_w_o_c, w_mlp_up=m_w_mlp_up, w_mlp_down=m_w_mlp_down)
    Vo = dict(w_in_ab=v_w_in_ab, w_out_ab=v_w_out_ab, w_down_c=v_w_down_c, w_uq_c=v_w_uq_c, w_ukv_c=v_w_ukv_c,
              w_o_c=v_w_o_c, w_mlp_up=v_w_mlp_up, w_mlp_down=v_w_mlp_down)
    S = x.shape[1]
    me = 4 * lax.axis_index("x") + 2 * lax.axis_index("y") + lax.axis_index("c")
    big_names = [n for n, _ in _BIG]
    big_shapes = [W[n].shape for n in big_names]

    w_pack = _pack_rows([W[n] for n in big_names])
    gathered = _exchange(w_pack.astype(MXU_DT), False, "gather_weights")
    full = {n: _gathered_to_full(g, ax)
            for (n, ax), g in zip(_BIG, _unpack_rows(gathered, big_shapes))}
    gains = _exchange(jnp.concatenate([_pad_rows8(q_norm_c), _pad_rows8(kv_norm_c)], axis=0), False, "gather_gains")
    n_odd = q_norm_c.shape[0]
    q_norm_full = gains[:, 0].reshape(N_DEV, -1)[:, :q_norm_c.size].reshape(N_DEV, n_odd, -1).transpose(1, 0, 2).reshape(n_odd, C_Q_RANK)
    kv_norm_full = gains[:, 8].reshape(N_DEV, -1)[:, :kv_norm_c.size].reshape(N_DEV, n_odd, -1).transpose(1, 0, 2).reshape(n_odd, C_KV_RANK)
    w_down_pad = jnp.pad(full["w_down_c"], ((0, 0), (0, 0), (0, C_DOWN_PAD - C_DOWN)))

    bias, bias_vjp = jax.vjp(_make_bias, rel_bias)
    sink_rows = [jnp.broadcast_to(jnp.concatenate([sinks[e], jnp.full((B_HEADS,), NEG, F32)])[:, None, None],
                                  (N_BIAS_HEADS, 1, 128)) for e in range(sinks.shape[0])]

    h = x[0]
    saved = []
    for l in range(DEPTH):
        xn = _rmsnorm(h, attn_norm[l], out_dtype=MXU_DT, name=f"attn_norm_{l}")
        if l % 2 == 0:
            e = l // 2
            h1, ctx = _even_fwd(xn, h, full["w_in_ab"][e], full["w_out_ab"][e], bias, sink_rows[e], l)
        else:
            o = l // 2
            h1, ctx = _mla_fwd(xn, h, w_down_pad[o], q_norm_full[o], full["w_uq_c"][o], kv_norm_full[o],
                               full["w_ukv_c"][o], full["w_o_c"][o], l)
        xn2 = _rmsnorm(h1, mlp_norm[l], out_dtype=MXU_DT, name=f"mlp_norm_{l}")
        act = _matmul(xn2, full["w_mlp_up"][l], out_dtype=MXU_DT, epi='relu2', tm=1024, tn=512, name=f"mlp_up_{l}")
        h2 = _matmul(act, full["w_mlp_down"][l], epi='add', extra=h1, tm=512, tn=512, name=f"mlp_down_{l}")
        saved.append((h, xn, h1, xn2, act, ctx))
        h = h2

    loss_row, dh, g_final = _loss_head(h, loss_target[0], final_norm)

    G = {n: [None] * W[n].shape[0] for n in big_names}
    g_attn_norm, g_mlp_norm = [None] * DEPTH, [None] * DEPTH
    g_sinks, g_qn, g_kvn = [None] * sinks.shape[0], [None] * n_odd, [None] * n_odd
    dbias_total = None
    for l in reversed(range(DEPTH)):
        h0, xn, h1, xn2, act, ctx = saved[l]
        du = _matmul(dh, full["w_mlp_down"][l], trans_b=True, out_dtype=MXU_DT, epi='dsq', extra=act,
                     tm=1024, tn=512, name=f"mlp_down_dx_{l}")
        G["w_mlp_down"][l] = _matmul_tn(act, dh, tk=512, tn=1024, name=f"mlp_down_dw_{l}")
        G["w_mlp_up"][l] = _matmul_tn(xn2, du, tk=512, tn=1024, name=f"mlp_up_dw_{l}")
        dxn2 = _matmul(du, full["w_mlp_up"][l], trans_b=True, tm=512, tn=512, name=f"mlp_up_dx_{l}")
        dh, g = _rmsnorm_bwd(h1, mlp_norm[l], dxn2, dh, name=f"mlp_norm_bwd_{l}")
        g_mlp_norm[l] = g[0]
        if l % 2 == 0:
            e = l // 2
            dxn, G["w_in_ab"][e], G["w_out_ab"][e], dbias, g_sinks[e] = _even_bwd(
                dh, xn, ctx, full["w_in_ab"][e], full["w_out_ab"][e], bias, sink_rows[e], l)
            dbias_total = dbias if dbias_total is None else dbias_total + dbias
        else:
            o = l // 2
            dxn, G["w_down_c"][o], g_qn[o], G["w_uq_c"][o], g_kvn[o], G["w_ukv_c"][o], G["w_o_c"][o] = _mla_bwd(
                dh, xn, ctx, w_down_pad[o], q_norm_full[o], full["w_uq_c"][o], kv_norm_full[o],
                full["w_ukv_c"][o], full["w_o_c"][o], l)
        dh, g = _rmsnorm_bwd(h0, attn_norm[l], dxn, dh, name=f"attn_norm_bwd_{l}")
        g_attn_norm[l] = g[0]
    grad_x = dh[None]
    (g_rel_bias,) = bias_vjp(dbias_total)

    send = _pack_rows_per_device([_full_to_shards(jnp.stack(G[n]), ax) for n, ax in _BIG]).astype(MXU_DT)
    landed = _exchange(send, True, "scatter_grads")
    m_pack = _pack_rows([Mo[n] for n in big_names])
    v_pack = _pack_rows([Vo[n] for n in big_names])
    big_out = [_unpack_rows(b, big_shapes) for b in _adamw(landed, w_pack, m_pack, v_pack, name="adamw_sharded")]

    small_g = [g_rel_bias, jnp.stack(g_attn_norm), jnp.stack(g_mlp_norm), g_final[0], jnp.stack(g_sinks),
               jnp.stack(g_qn), jnp.stack(g_kvn), loss_row[0, :1]]
    small_w = [rel_bias, attn_norm, mlp_norm, final_norm, sinks, q_norm_c, kv_norm_c, jnp.zeros((1,), F32)]
    small_m = [m_rel_bias, m_attn_norm, m_mlp_norm, m_final_norm, m_sinks, m_q_norm_c, m_kv_norm_c, jnp.zeros((1,), F32)]
    small_v = [v_rel_bias, v_attn_norm, v_mlp_norm, v_final_norm, v_sinks, v_q_norm_c, v_kv_norm_c, jnp.ones((1,), F32)]
    offs = np.cumsum([0] + [-(-a.size // 1024) * 8 for a in small_g])
    partials = _exchange(jnp.concatenate([_pad_rows8(a) for a in small_g], axis=0), False, "gather_small_grads")

    def mine(i, a_full_shape, local):
        p = partials[:, offs[i]:offs[i + 1]].reshape(N_DEV, -1)[:, :math.prod(a_full_shape)]
        p = p.reshape((N_DEV,) + tuple(a_full_shape))
        if local.shape != tuple(a_full_shape):
            width = local.shape[-1]
            p = lax.dynamic_slice_in_dim(p, me * width, width, axis=p.ndim - 1)
        return jnp.stack([_pad_rows8(p[s]) for s in range(N_DEV)])

    parts_small = jnp.concatenate([mine(i, g.shape, w) for i, (g, w) in enumerate(zip(small_g, small_w))], axis=1)
    pk = lambda arrs: jnp.concatenate([_pad_rows8(a) for a in arrs], axis=0)
    small_out = _adamw(parts_small, pk(small_w), pk(small_m), pk(small_v), name="adamw_small", tr=parts_small.shape[1])
    offs2 = np.cumsum([0] + [-(-a.size // 1024) * 8 for a in small_w])

    def unpack_small(buf):
        return [buf[offs2[i]:offs2[i + 1]].reshape(-1)[:a.size].reshape(a.shape) for i, a in enumerate(small_w)]

    sg, sd, sm, sv = (unpack_small(b) for b in small_out)
    loss = sg[7][0]

    order = ['rel_bias', 'attn_norm', 'mlp_norm', 'final_norm', 'w_in_ab', 'sinks', 'w_out_ab', 'w_down_c', 'q_norm_c',
             'w_uq_c', 'kv_norm_c', 'w_ukv_c', 'w_o_c', 'w_mlp_up', 'w_mlp_down']
    small_idx = {'rel_bias': 0, 'attn_norm': 1, 'mlp_norm': 2, 'final_norm': 3, 'sinks': 4, 'q_norm_c': 5, 'kv_norm_c': 6}

    def pick(kind):
        res = []
        for n in order:
            if n in small_idx:
                res.append((sg, sd, sm, sv)[kind][small_idx[n]])
            else:
                res.append(big_out[kind][big_names.index(n)])
        return res

    return (loss, grad_x, *pick(0), *pick(1), *pick(2), *pick(3))
```

```python
import math

import numpy as np
import jax
import jax.numpy as jnp
from jax import lax
from jax.experimental import pallas as pl
from jax.experimental.pallas import tpu as pltpu

F32 = jnp.float32
MXU_DT = jnp.bfloat16

N_DEV = 8
D_MODEL = 1024
DEPTH = 4
HEAD_DIM = 64
BLOCK = 128
EPS = 1e-6
NEG = -1e30
A_Q_HEADS = 8
A_KV_HEADS = 2
A_GROUP = A_Q_HEADS // A_KV_HEADS
A_WINDOW = 128
B_BRANCHES = ((128, 1), (512, 4), (2048, 16))
B_HPB = 4
B_HEADS = len(B_BRANCHES) * B_HPB
NUM_BUCKETS = 32
MAX_DISTANCE = 2048
N_BIAS_HEADS = A_Q_HEADS + B_HEADS
N_BAND_KV = A_KV_HEADS + B_HEADS
A_IN = (A_Q_HEADS + 2 * A_KV_HEADS) * HEAD_DIM
C_HEADS = 8
C_NOPE = 64
C_ROPE = 32
C_QK = C_NOPE + C_ROPE
C_V = 64
C_Q_RANK = 384
C_KV_RANK = 256
C_DOWN = C_Q_RANK + C_KV_RANK + C_ROPE
C_DOWN_PAD = 768
ROPE_THETA = 10000.0
N_CHUNKS = 16
FLASH_T = 512

ADAM_LR = 0.001
ADAM_B1 = 0.9
ADAM_B2 = 0.999
ADAM_EPS = 1e-08
ADAM_WD = 0.01
ADAM_STEP = 10

V7X_VMEM_BYTES = 64 * 1024 * 1024
VMEM_LIMIT = V7X_VMEM_BYTES - 8 * 1024 * 1024


def _pcall(body, **kw):
    return pl.pallas_call(body, **kw)


def _cparams(*sem):
    return pltpu.CompilerParams(dimension_semantics=sem, vmem_limit_bytes=VMEM_LIMIT)


def _exchange(src, all_to_all, name):
    def body(src_ref, out_ref, send_sems, recv_sems, local_sem):
        copies = _exchange_copies(src_ref, out_ref, send_sems, recv_sems, local_sem, all_to_all)
        for cp in copies:
            cp.start()
        _exchange_wait(copies)

    return _pcall(
        body, name=name,
        out_shape=_exchange_out(src),
        in_specs=[pl.BlockSpec(memory_space=pl.ANY)],
        out_specs=pl.BlockSpec(memory_space=pl.ANY),
        scratch_shapes=_exchange_sems(),
    )(src)


def _exchange_out(src):
    return jax.ShapeDtypeStruct((N_DEV,) + src.shape[-2:], src.dtype)


def _exchange_sems():
    return [pltpu.SemaphoreType.DMA((N_DEV - 1,)), pltpu.SemaphoreType.DMA((N_DEV - 1,)), pltpu.SemaphoreType.DMA]


def _exchange_copies(src_ref, out_ref, send_sems, recv_sems, local_sem, all_to_all):
    x, y, c = lax.axis_index("x"), lax.axis_index("y"), lax.axis_index("c")
    me = 4 * x + 2 * y + c

    def piece(dev):
        return src_ref.at[dev] if all_to_all else src_ref

    copies = [pltpu.make_async_copy(piece(me), out_ref.at[me], local_sem)]
    for k in range(1, N_DEV):
        px = 1 - x if (k >> 2) & 1 else x
        py = 1 - y if (k >> 1) & 1 else y
        pc = 1 - c if k & 1 else c
        copies.append(pltpu.make_async_remote_copy(
            src_ref=piece(4 * px + 2 * py + pc), dst_ref=out_ref.at[me],
            send_sem=send_sems.at[k - 1], recv_sem=recv_sems.at[k - 1],
            device_id=(px, py, pc), device_id_type=pl.DeviceIdType.MESH))
    return copies


def _exchange_wait(copies):
    for cp in copies[1:]:
        cp.wait()
    copies[0].wait()


def _matmul(a, b, *, trans_b=False, out_dtype=F32, epi=None, extra=None, tm=512, tn=512, name, ride=None):
    M, K = a.shape
    N = b.shape[0] if trans_b else b.shape[1]
    tm, tn = min(tm, M), min(tn, N)
    assert M % tm == 0 and N % tn == 0 and (b.shape[1] if trans_b else b.shape[0]) == K
    dn = (((1,), (1,)), ((), ())) if trans_b else (((1,), (0,)), ((), ()))
    n_i, n_j = M // tm, N // tn
    n_in = 2 + (extra is not None)

    def body(*refs):
        a_ref, b_ref = refs[0], refs[1]
        o_ref = refs[n_in + (ride is not None)]
        if ride is not None:
            i, j = pl.program_id(0), pl.program_id(1)
            copies = _exchange_copies(refs[n_in], refs[n_in + 2], *refs[n_in + 3:], ride[1])

            @pl.when((i == 0) & (j == 0))
            def _():
                for cp in copies:
                    cp.start()

        acc = lax.dot_general(a_ref[...].astype(MXU_DT), b_ref[...].astype(MXU_DT), dn,
                              preferred_element_type=F32)
        if epi == 'relu2':
            r = jnp.maximum(acc, 0.0)
            acc = r * r
        elif epi == 'add':
            acc = acc + refs[2][...].astype(F32)
        elif epi == 'dsq':
            acc = acc * (2.0 * jnp.sqrt(refs[2][...].astype(F32)))
        o_ref[...] = acc.astype(out_dtype)

        if ride is not None:
            @pl.when((i == n_i - 1) & (j == n_j - 1))
            def _():
                _exchange_wait(copies)

    b_spec = pl.BlockSpec((tn, K), lambda i, j: (j, 0)) if trans_b else pl.BlockSpec((K, tn), lambda i, j: (0, j))
    in_specs = [pl.BlockSpec((tm, K), lambda i, j: (i, 0)), b_spec]
    args = [a, b]
    if extra is not None:
        in_specs.append(pl.BlockSpec((tm, tn), lambda i, j: (i, j)))
        args.append(extra)
    out_shape = jax.ShapeDtypeStruct((M, N), out_dtype)
    out_spec = pl.BlockSpec((tm, tn), lambda i, j: (i, j))
    if ride is None:
        return _pcall(
            body, name=name, grid=(n_i, n_j), out_shape=out_shape, in_specs=in_specs, out_specs=out_spec,
            compiler_params=_cparams("parallel", "parallel"),
        )(*args)
    return _pcall(
        body, name=name, grid=(n_i, n_j),
        out_shape=(out_shape, _exchange_out(ride[0])),
        in_specs=in_specs + [pl.BlockSpec(memory_space=pl.ANY)],
        out_specs=(out_spec, pl.BlockSpec(memory_space=pl.ANY)),
        scratch_shapes=_exchange_sems(),
        compiler_params=_cparams("arbitrary", "arbitrary"),
    )(*args, ride[0])


def _matmul_tn(a, b, *, tk=512, tn=512, tm=1024, name):
    M, Ka = a.shape
    N = b.shape[1]
    tk, tn, tm = min(tk, Ka), min(tn, N), min(tm, M)
    assert Ka % tk == 0 and N % tn == 0 and M % tm == 0 and b.shape[0] == M

    def body(a_ref, b_ref, o_ref):
        @pl.when(pl.program_id(2) == 0)
        def _():
            o_ref[...] = jnp.zeros_like(o_ref)

        o_ref[...] += lax.dot_general(a_ref[...].astype(MXU_DT), b_ref[...].astype(MXU_DT),
                                      (((0,), (0,)), ((), ())), preferred_element_type=F32)

    return _pcall(
        body, name=name, grid=(Ka // tk, N // tn, M // tm),
        out_shape=jax.ShapeDtypeStruct((Ka, N), F32),
        in_specs=[pl.BlockSpec((tm, tk), lambda i, j, r: (r, i)), pl.BlockSpec((tm, tn), lambda i, j, r: (r, j))],
        out_specs=pl.BlockSpec((tk, tn), lambda i, j, r: (i, j)),
        compiler_params=_cparams("parallel", "parallel", "arbitrary"),
    )(a, b)


def _rmsnorm(x, g, *, out_dtype, name, tr=512):
    S, D = x.shape
    tr = min(tr, S)

    def body(x_ref, g_ref, o_ref):
        xf = x_ref[...].astype(F32)
        r = lax.rsqrt(jnp.mean(xf * xf, axis=-1, keepdims=True) + EPS)
        o_ref[...] = (xf * r * g_ref[...]).astype(out_dtype)

    return _pcall(
        body, name=name, grid=(S // tr,),
        out_shape=jax.ShapeDtypeStruct((S, D), out_dtype),
        in_specs=[pl.BlockSpec((tr, D), lambda i: (i, 0)), pl.BlockSpec((1, D), lambda i: (0, 0))],
        out_specs=pl.BlockSpec((tr, D), lambda i: (i, 0)),
        compiler_params=_cparams("parallel"),
    )(x, g.reshape(1, D))


def _rmsnorm_bwd(x, g, dy, dres, *, name, tr=512):
    S, D = x.shape
    tr = min(tr, S)

    def body(*refs):
        if dres is None:
            x_ref, g_ref, dy_ref, dx_ref, dg_ref = refs
        else:
            x_ref, g_ref, dy_ref, dres_ref, dx_ref, dg_ref = refs

        @pl.when(pl.program_id(0) == 0)
        def _():
            dg_ref[...] = jnp.zeros_like(dg_ref)

        xf = x_ref[...].astype(F32)
        r = lax.rsqrt(jnp.mean(xf * xf, axis=-1, keepdims=True) + EPS)
        xhat = xf * r
        dyf = dy_ref[...].astype(F32)
        dg_ref[...] += jnp.sum(dyf * xhat, axis=0, keepdims=True)
        dyg = dyf * g_ref[...]
        dx = r * (dyg - xhat * jnp.mean(dyg * xhat, axis=-1, keepdims=True))
        if dres is not None:
            dx = dx + dres_ref[...]
        dx_ref[...] = dx

    row = pl.BlockSpec((tr, D), lambda i: (i, 0))
    vec = pl.BlockSpec((1, D), lambda i: (0, 0))
    args = [x, g.reshape(1, D), dy] + ([] if dres is None else [dres])
    return _pcall(
        body, name=name, grid=(S // tr,),
        out_shape=(jax.ShapeDtypeStruct((S, D), F32), jax.ShapeDtypeStruct((1, D), F32)),
        in_specs=[row, vec, row] + ([] if dres is None else [row]),
        out_specs=(row, vec),
        compiler_params=_cparams("arbitrary"),
    )(*args)


def _loss_head(h, t, g, *, tr=512):
    S, D = h.shape
    tr = min(tr, S)

    def body(h_ref, t_ref, g_ref, loss_ref, dh_ref, dg_ref):
        @pl.when(pl.program_id(0) == 0)
        def _():
            dg_ref[...] = jnp.zeros_like(dg_ref)
            loss_ref[...] = jnp.zeros_like(loss_ref)

        xf = h_ref[...]
        r = lax.rsqrt(jnp.mean(xf * xf, axis=-1, keepdims=True) + EPS)
        xhat = xf * r
        e = xhat * g_ref[...] - t_ref[...]
        part = 0.5 * jnp.sum(jnp.mean(e * e, axis=-1, keepdims=True), axis=0, keepdims=True)
        loss_ref[...] += jnp.broadcast_to(part, loss_ref.shape)
        dy = e * (1.0 / D)
        dg_ref[...] += jnp.sum(dy * xhat, axis=0, keepdims=True)
        dyg = dy * g_ref[...]
        dh_ref[...] = r * (dyg - xhat * jnp.mean(dyg * xhat, axis=-1, keepdims=True))

    row = pl.BlockSpec((tr, D), lambda i: (i, 0))
    vec = pl.BlockSpec((1, D), lambda i: (0, 0))
    return _pcall(
        body, name="loss_head", grid=(S // tr,),
        out_shape=(jax.ShapeDtypeStruct((1, 128), F32), jax.ShapeDtypeStruct((S, D), F32),
                   jax.ShapeDtypeStruct((1, D), F32)),
        in_specs=[row, row, vec],
        out_specs=(pl.BlockSpec((1, 128), lambda i: (0, 0)), row, vec),
        compiler_params=_cparams("arbitrary"),
    )(h, t, g.reshape(1, D))


def _band_kv_head(h):
    return jnp.where(h < A_Q_HEADS, h // A_GROUP, h - (A_Q_HEADS - A_KV_HEADS))


def _band_period(h):
    return jnp.where(h < A_Q_HEADS + B_HPB, 16, jnp.where(h < A_Q_HEADS + 2 * B_HPB, 4, 1))


def _band_specs(S):
    ch = S // N_CHUNKS
    nb = ch // BLOCK
    q_spec = pl.BlockSpec((1, ch, HEAD_DIM), lambda h, c: (h, c, 0))
    kv_spec = pl.BlockSpec((1, ch, HEAD_DIM), lambda h, c: (_band_kv_head(h), c, 0))
    kv_prev = pl.BlockSpec((1, BLOCK, HEAD_DIM), lambda h, c: (_band_kv_head(h), jnp.maximum(c * nb - 1, 0), 0))
    bias_spec = pl.BlockSpec((1, BLOCK, 2 * BLOCK), lambda h, c: (h, 0, 0))
    sink_spec = pl.BlockSpec((1, 1, 128), lambda h, c: (h, 0, 0))
    row_spec = pl.BlockSpec((1, 1, nb, 1, BLOCK), lambda h, c: (h, c, 0, 0, 0))
    return ch, nb, q_spec, kv_spec, kv_prev, bias_spec, sink_spec, row_spec


def _eye():
    return lax.broadcasted_iota(jnp.int32, (BLOCK, BLOCK), 0) == lax.broadcasted_iota(jnp.int32, (BLOCK, BLOCK), 1)


_NT = (((1,), (1,)), ((), ()))
_NN = (((1,), (0,)), ((), ()))
_TN = (((0,), (0,)), ((), ()))


_B_NT = (((2,), (2,)), ((0,), (0,)))
_B_NN = (((2,), (1,)), ((0,), (0,)))


def _bdot(a, b, dn):
    return lax.dot_general(a, b, dn, preferred_element_type=F32)


def _with_prev(first, t3):
    return first[None] if t3.shape[0] == 1 else jnp.concatenate([first[None], t3[:-1]], axis=0)


def _mask_first(s_prev, prev_ok):
    s0 = jnp.where(prev_ok, s_prev[0], NEG)[None]
    return s0 if s_prev.shape[0] == 1 else jnp.concatenate([s0, s_prev[1:]], axis=0)


def _banded_fwd(q, k, v, bias, sinks):
    NH, S, dh = q.shape
    ch, nb, q_spec, kv_spec, kv_prev, bias_spec, sink_spec, row_spec = _band_specs(S)
    scale = HEAD_DIM ** -0.5

    def body(q_ref, k_ref, kp_ref, v_ref, vp_ref, b_ref, s_ref, o_ref, lse_ref):
        h, c = pl.program_id(0), pl.program_id(1)
        prev_ok = (c % _band_period(h)) != 0
        q3, k3, v3 = (r[0].reshape(nb, BLOCK, dh) for r in (q_ref, k_ref, v_ref))
        kp3, vp3 = _with_prev(kp_ref[0], k3), _with_prev(vp_ref[0], v3)
        sink = s_ref[0, :, :1]
        s_cur = _bdot(q3, k3, _B_NT) * scale + b_ref[0, :, BLOCK:][None]
        s_prev = _mask_first(_bdot(q3, kp3, _B_NT) * scale + b_ref[0, :, :BLOCK][None], prev_ok)
        m = jnp.maximum(jnp.max(s_cur, axis=-1, keepdims=True), jnp.max(s_prev, axis=-1, keepdims=True))
        m = jnp.maximum(m, sink)
        p_cur = jnp.exp(s_cur - m)
        p_prev = jnp.exp(s_prev - m)
        l = jnp.sum(p_cur, axis=-1, keepdims=True) + jnp.sum(p_prev, axis=-1, keepdims=True) + jnp.exp(sink - m)
        acc = _bdot(p_cur.astype(MXU_DT), v3, _B_NN) + _bdot(p_prev.astype(MXU_DT), vp3, _B_NN)
        o_ref[0] = (acc / l).reshape(ch, dh)
        lse = m + jnp.log(l)
        lse_ref[0, 0] = jnp.sum(jnp.where(_eye()[None], lse, 0.0), axis=1, keepdims=True)

    return _pcall(
        body, name="banded_fwd", grid=(NH, N_CHUNKS),
        out_shape=(jax.ShapeDtypeStruct((NH, S, dh), F32), jax.ShapeDtypeStruct((NH, N_CHUNKS, nb, 1, BLOCK), F32)),
        in_specs=[q_spec, kv_spec, kv_prev, kv_spec, kv_prev, bias_spec, sink_spec],
        out_specs=(q_spec, row_spec),
        compiler_params=_cparams("parallel", "parallel"),
    )(q, k, k, v, v, bias, sinks)


def _banded_bwd(q, k, v, bias_t, sinks, do, dlse):
    NH, S, dh = q.shape
    ch, nb, q_spec, kv_spec, kv_prev, _, sink_spec, row_spec = _band_specs(S)
    bias_spec = pl.BlockSpec((1, 2 * BLOCK, BLOCK), lambda h, c: (h, 0, 0))
    scale = HEAD_DIM ** -0.5

    def body(q_ref, k_ref, kp_ref, v_ref, vp_ref, b_ref, s_ref, do_ref, dl_ref,
             dq_ref, dk_ref, dv_ref, dkh_ref, dvh_ref, db_ref, ds_ref):
        h, c = pl.program_id(0), pl.program_id(1)

        @pl.when(c == 0)
        def _():
            db_ref[...] = jnp.zeros_like(db_ref)
            ds_ref[...] = jnp.zeros_like(ds_ref)

        prev_ok = (c % _band_period(h)) != 0
        q3, k3, v3 = (r[0].reshape(nb, BLOCK, dh) for r in (q_ref, k_ref, v_ref))
        do3 = do_ref[0].astype(MXU_DT).reshape(nb, BLOCK, dh)
        kp3, vp3 = _with_prev(kp_ref[0], k3), _with_prev(vp_ref[0], v3)
        sink = s_ref[0, :, :1]
        s_cur = _bdot(k3, q3, _B_NT) * scale + b_ref[0, BLOCK:, :][None]
        s_prev = _mask_first(_bdot(kp3, q3, _B_NT) * scale + b_ref[0, :BLOCK, :][None], prev_ok)
        m = jnp.maximum(jnp.max(s_cur, axis=1, keepdims=True), jnp.max(s_prev, axis=1, keepdims=True))
        m = jnp.maximum(m, sink)
        p_cur = jnp.exp(s_cur - m)
        p_prev = jnp.exp(s_prev - m)
        p_sink = jnp.exp(sink - m)
        inv = 1.0 / (jnp.sum(p_cur, axis=1, keepdims=True) + jnp.sum(p_prev, axis=1, keepdims=True) + p_sink)
        p_cur, p_prev, p_sink = p_cur * inv, p_prev * inv, p_sink * inv
        dp_cur = _bdot(v3, do3, _B_NT)
        dp_prev = _bdot(vp3, do3, _B_NT)
        delta = jnp.sum(p_cur * dp_cur, axis=1, keepdims=True) + jnp.sum(p_prev * dp_prev, axis=1, keepdims=True)
        t = dl_ref[0, 0] - delta
        ds_cur = p_cur * (dp_cur + t)
        ds_prev = p_prev * (dp_prev + t)
        dsink = jnp.sum(jnp.sum(p_sink * t, axis=0), axis=-1, keepdims=True)
        ds_ref[...] += jnp.broadcast_to(dsink, ds_ref.shape)
        db_ref[0, :BLOCK, :] += jnp.sum(ds_prev, axis=0)
        db_ref[0, BLOCK:, :] += jnp.sum(ds_cur, axis=0)
        dsb_cur = (ds_cur * scale).astype(MXU_DT)
        dsb_prev = (ds_prev * scale).astype(MXU_DT)
        dk_prev = _bdot(dsb_prev, q3, _B_NN)
        dv_prev = _bdot(p_prev.astype(MXU_DT), do3, _B_NN)

        def shifted(t3):
            z = jnp.zeros((1, BLOCK, dh), F32)
            return z if nb == 1 else jnp.concatenate([t3[1:], z], axis=0)

        dk_ref[0] = (_bdot(dsb_cur, q3, _B_NN) + shifted(dk_prev)).reshape(ch, dh)
        dv_ref[0] = (_bdot(p_cur.astype(MXU_DT), do3, _B_NN) + shifted(dv_prev)).reshape(ch, dh)
        dkh_ref[0, 0] = dk_prev[0]
        dvh_ref[0, 0] = dv_prev[0]
        for b in range(nb):
            dq_ref[0, b * BLOCK:(b + 1) * BLOCK, :] = (
                lax.dot_general(dsb_cur[b], k3[b], _TN, preferred_element_type=F32)
                + lax.dot_general(dsb_prev[b], kp3[b], _TN, preferred_element_type=F32))

    halo_spec = pl.BlockSpec((1, 1, BLOCK, HEAD_DIM), lambda h, c: (h, c, 0, 0))
    big = jax.ShapeDtypeStruct((NH, S, dh), F32)
    halo = jax.ShapeDtypeStruct((NH, N_CHUNKS, BLOCK, dh), F32)
    return _pcall(
        body, name="banded_bwd", grid=(NH, N_CHUNKS),
        out_shape=(big, big, big, halo, halo, jax.ShapeDtypeStruct(bias_t.shape, F32),
                   jax.ShapeDtypeStruct(sinks.shape, F32)),
        in_specs=[q_spec, kv_spec, kv_prev, kv_spec, kv_prev, bias_spec, sink_spec, q_spec, row_spec],
        out_specs=(q_spec, q_spec, q_spec, halo_spec, halo_spec, bias_spec, sink_spec),
        compiler_params=_cparams("arbitrary", "arbitrary"),
    )(q, k, k, v, v, bias_t, sinks, do, dlse)


def _halo_fold(t, halo, name):
    NH, S, dh = t.shape
    nb = S // N_CHUNKS // BLOCK

    def body(t_ref, h_ref, o_ref):
        keep = pl.program_id(1) < N_CHUNKS - 1
        o_ref[...] = t_ref[...] + jnp.where(keep, h_ref[0], 0.0)

    blk = pl.BlockSpec((1, BLOCK, dh), lambda h, c: (h, c * nb + nb - 1, 0))
    return _pcall(
        body, name=name, grid=(NH, N_CHUNKS),
        out_shape=jax.ShapeDtypeStruct(t.shape, t.dtype),
        in_specs=[blk, pl.BlockSpec((1, 1, BLOCK, dh), lambda h, c: (h, jnp.minimum(c + 1, N_CHUNKS - 1), 0, 0))],
        out_specs=blk, input_output_aliases={0: 0},
        compiler_params=_cparams("parallel", "parallel"),
    )(t, halo)


def _causal_mask(T):
    return lax.broadcasted_iota(jnp.int32, (T, T), 0) <= lax.broadcasted_iota(jnp.int32, (T, T), 1)


LOG2E = math.log2(math.e)
FLASH_SPLIT = 2
FLASH_ONES_ROWS = 16


def _flash_fwd(qt, k, vt1):
    H, nq, dqk, T = qt.shape
    S = k.shape[1]
    dva = vt1.shape[2]
    dv = dva - FLASH_ONES_ROWS
    scale = dqk ** -0.5
    c = scale * LOG2E
    th = T // FLASH_SPLIT

    def body(qt_ref, k_ref, vt_ref, ot_ref, lse_ref, sa_ref, sb_ref):
        i = pl.program_id(1)

        def scores(j):
            kb = k_ref[0, pl.ds(pl.multiple_of(j * T, T), T), :]
            return lax.dot_general(kb, qt_ref[0, 0], _NN, preferred_element_type=F32)

        def softmax_pv(s_ref, j, carry, masked):
            m, acc = carry
            raw = s_ref[...]
            if masked:
                raw = jnp.where(_causal_mask(T), raw, NEG)
            m_new = jnp.maximum(m, jnp.max(raw, axis=0, keepdims=True))
            alpha = jnp.exp2((m - m_new) * c)
            pb = jnp.exp2((raw - m_new) * c).astype(MXU_DT)
            acc = acc * alpha + lax.dot_general(vt_ref[0, j], pb, _NN, preferred_element_type=F32)
            return m_new, acc

        def pair(p, carry):
            j = 2 * p
            sb_ref[...] = scores(j + 1)
            carry = softmax_pv(sa_ref, j, carry, False)
            sa_ref[...] = scores(j + 2)
            return softmax_pv(sb_ref, j + 1, carry, False)

        def even_tail(carry):
            return softmax_pv(sa_ref, i, carry, True)

        def odd_tail(carry):
            sb_ref[...] = scores(i)
            carry = softmax_pv(sa_ref, i - 1, carry, False)
            return softmax_pv(sb_ref, i, carry, True)

        sa_ref[...] = scores(0)
        carry = lax.fori_loop(0, i // 2, pair, (jnp.full((1, T), NEG, F32), jnp.zeros((dva, T), F32)))
        m, acc = lax.cond(i % 2 == 0, even_tail, odd_tail, carry)
        l = acc[dv:dv + 1]
        ot_ref[0, 0] = acc[:dv] / l
        lse_ref[0, 0] = m * scale + jnp.log(l)

    return _pcall(
        body, name="flash_fwd", grid=(H, nq),
        out_shape=(jax.ShapeDtypeStruct((H, nq, dv, T), F32), jax.ShapeDtypeStruct((H, nq, 1, T), F32)),
        in_specs=[pl.BlockSpec((1, 1, dqk, T), lambda h, i: (h, i, 0, 0)),
                  pl.BlockSpec((1, S, dqk), lambda h, i: (h, 0, 0)),
                  pl.BlockSpec((1, nq, dva, T), lambda h, i: (h, 0, 0, 0))],
        out_specs=(pl.BlockSpec((1, 1, dv, T), lambda h, i: (h, i, 0, 0)),
                   pl.BlockSpec((1, 1, 1, T), lambda h, i: (h, i, 0, 0))),
        scratch_shapes=[pltpu.VMEM((T, T), F32), pltpu.VMEM((T, T), F32)],
        compiler_params=_cparams("parallel", "parallel"),
    )(qt, k, vt1)


def _flash_delta(ot, dot):
    H, nq, dv, T = ot.shape

    def body(o_ref, do_ref, d_ref):
        d_ref[0, 0] = jnp.sum(o_ref[0, 0] * do_ref[0, 0], axis=0, keepdims=True)

    spec = pl.BlockSpec((1, 1, dv, T), lambda h, i: (h, i, 0, 0))
    return _pcall(
        body, name="flash_delta", grid=(H, nq),
        out_shape=jax.ShapeDtypeStruct((H, nq, 1, T), F32),
        in_specs=[spec, spec], out_specs=pl.BlockSpec((1, 1, 1, T), lambda h, i: (h, i, 0, 0)),
        compiler_params=_cparams("parallel", "parallel"),
    )(ot, dot)


def _flash_bwd(qt, k, kt, v, dot, lse, delta):
    H, nq, dqk, T = qt.shape
    dv_ = v.shape[2]
    scale = dqk ** -0.5
    c = scale * LOG2E
    th = T // FLASH_SPLIT

    def body(qt_ref, k_ref, kt_ref, v_ref, dot_ref, lse_ref, del_ref, dqt_ref, dkt_ref, dvt_ref,
             sa_ref, pa_ref, sb_ref, pb_ref):
        j = pl.program_id(1)

        @pl.when(j == 0)
        def _():
            dqt_ref[...] = jnp.zeros_like(dqt_ref)

        n_un = nq - 1 - j

        def issue(i, s_ref, dp_ref):
            s_ref[...] = lax.dot_general(k_ref[0], qt_ref[0, i], _NN, preferred_element_type=F32)
            dp_ref[...] = lax.dot_general(v_ref[0], dot_ref[0, i].astype(MXU_DT), _NN, preferred_element_type=F32)

        def consume(i, s_ref, dp_ref, carry, masked):
            dkt, dvt = carry
            raw = s_ref[...]
            if masked:
                raw = jnp.where(_causal_mask(T), raw, NEG)
            p = jnp.exp2(raw * c - lse_ref[0, i] * LOG2E)
            dsb = (p * (dp_ref[...] - del_ref[0, i])).astype(MXU_DT)
            dvt = dvt + lax.dot_general(dot_ref[0, i].astype(MXU_DT), p.astype(MXU_DT), _NT, preferred_element_type=F32)
            dkt = dkt + lax.dot_general(qt_ref[0, i], dsb, _NT, preferred_element_type=F32)
            dqt_ref[0, i] += lax.dot_general(kt_ref[0, 0], dsb, _NN, preferred_element_type=F32) * scale
            return dkt, dvt

        def pair(p, carry):
            i0 = j + 1 + 2 * p
            issue(i0 + 1, sb_ref, pb_ref)
            carry = consume(i0, sa_ref, pa_ref, carry, False)
            issue(jnp.where(2 * p + 2 < n_un, i0 + 2, j), sa_ref, pa_ref)
            return consume(i0 + 1, sb_ref, pb_ref, carry, False)

        def even_tail(carry):
            return consume(j, sa_ref, pa_ref, carry, True)

        def odd_tail(carry):
            issue(j, sb_ref, pb_ref)
            carry = consume(nq - 1, sa_ref, pa_ref, carry, False)
            return consume(j, sb_ref, pb_ref, carry, True)

        issue(jnp.where(n_un > 0, j + 1, j), sa_ref, pa_ref)
        carry = lax.fori_loop(0, n_un // 2, pair, (jnp.zeros((dqk, T), F32), jnp.zeros((dv_, T), F32)))
        dkt, dvt = lax.cond(n_un % 2 == 0, even_tail, odd_tail, carry)
        dkt_ref[0, 0] = dkt * scale
        dvt_ref[0, 0] = dvt

    whole = lambda d: pl.BlockSpec((1, nq, d, T), lambda h, j: (h, 0, 0, 0))
    tile_t = lambda d: pl.BlockSpec((1, 1, d, T), lambda h, j: (h, j, 0, 0))
    return _pcall(
        body, name="flash_bwd", grid=(H, nq),
        out_shape=(jax.ShapeDtypeStruct((H, nq, dqk, T), F32), jax.ShapeDtypeStruct((H, nq, dqk, T), F32),
                   jax.ShapeDtypeStruct((H, nq, dv_, T), F32)),
        in_specs=[whole(dqk),
                  pl.BlockSpec((1, T, dqk), lambda h, j: (h, j, 0)),
                  tile_t(dqk),
                  pl.BlockSpec((1, T, dv_), lambda h, j: (h, j, 0)),
                  whole(dv_), whole(1), whole(1)],
        out_specs=(whole(dqk), tile_t(dqk), tile_t(dv_)),
        scratch_shapes=[pltpu.VMEM((T, T), F32) for _ in range(4)],
        compiler_params=_cparams("arbitrary", "arbitrary"),
    )(qt, k, kt, v, dot, lse, delta)


def _adamw(parts, w, m, v, *, name, tr=512):
    P, R, C = parts.shape
    tr = min(tr, R)
    assert R % tr == 0

    def body(p_ref, w_ref, m_ref, v_ref, g_ref, d_ref, m2_ref, v2_ref):
        g = p_ref[0].astype(F32)
        for s in range(1, P):
            g = g + p_ref[s].astype(F32)
        m2 = ADAM_B1 * m_ref[...] + (1.0 - ADAM_B1) * g
        v2 = ADAM_B2 * v_ref[...] + (1.0 - ADAM_B2) * jnp.square(g)
        m_hat = m2 / (1.0 - ADAM_B1 ** ADAM_STEP)
        v_hat = v2 / (1.0 - ADAM_B2 ** ADAM_STEP)
        g_ref[...] = g
        d_ref[...] = -ADAM_LR * (m_hat / (jnp.sqrt(v_hat) + ADAM_EPS) + ADAM_WD * w_ref[...])
        m2_ref[...] = m2
        v2_ref[...] = v2

    row = pl.BlockSpec((tr, C), lambda i: (i, 0))
    out = jax.ShapeDtypeStruct((R, C), F32)
    return _pcall(
        body, name=name, grid=(R // tr,),
        out_shape=(out, out, out, out),
        in_specs=[pl.BlockSpec((P, tr, C), lambda i: (0, i, 0)), row, row, row],
        out_specs=(row, row, row, row),
        compiler_params=_cparams("parallel"),
    )(parts, w, m, v)


def _bias_tables():
    i = np.arange(BLOCK)[:, None]
    j = np.arange(2 * BLOCK)[None, :]
    dist = i + BLOCK - j
    out = []
    for dil, max_dist in [(1, A_WINDOW - 1)] + [(d, w // d) for w, d in B_BRANCHES]:
        n = np.maximum(dist, 0) * dil
        max_exact = NUM_BUCKETS // 2
        nf = np.maximum(n, 1).astype(np.float64)
        val = np.log(nf / max_exact) / math.log(MAX_DISTANCE / max_exact) * (NUM_BUCKETS - max_exact)
        inband = (dist >= 0) & (dist <= max_dist)
        frac = np.abs(val - np.round(val))
        last = NUM_BUCKETS - 1 - max_exact
        assert np.all((frac > 2e-5) | (n <= max_exact) | (val >= last) | ~inband)
        large = max_exact + val.astype(np.int64)
        bucket = np.where(n < max_exact, n, np.minimum(large, NUM_BUCKETS - 1))
        onehot = (bucket[..., None] == np.arange(NUM_BUCKETS)).astype(np.float32)
        out.append((onehot.reshape(-1, NUM_BUCKETS), inband))
    return out


def _make_bias(rel_bias):
    tabs = _bias_tables()
    groups = [(0, A_Q_HEADS)] + [(A_Q_HEADS + g * B_HPB, B_HPB) for g in range(len(B_BRANCHES))]
    parts = []
    for (onehot, inband), (h0, nh) in zip(tabs, groups):
        b = jnp.dot(jnp.asarray(onehot), rel_bias[:, h0:h0 + nh], precision=lax.Precision.HIGHEST)
        b = b.reshape(BLOCK, 2 * BLOCK, nh)
        b = jnp.where(jnp.asarray(inband)[..., None], b, NEG)
        parts.append(b.transpose(2, 0, 1))
    return jnp.concatenate(parts, axis=0)


def _dilate_heads(t, d):
    S, H, dh = t.shape
    return t.reshape(S // d, d, H, dh).transpose(2, 1, 0, 3).reshape(H, S, dh)


def _undilate_heads(t, d):
    H, S, dh = t.shape
    return t.reshape(H, d, S // d, dh).transpose(2, 1, 0, 3).reshape(S, H, dh)


def _even_split(proj):
    S = proj.shape[0]
    qd, kd = A_Q_HEADS * HEAD_DIM, A_KV_HEADS * HEAD_DIM
    pa = proj[:, :A_IN]
    qs = [pa[:, :qd].reshape(S, A_Q_HEADS, HEAD_DIM).transpose(1, 0, 2)]
    ks = [pa[:, qd:qd + kd].reshape(S, A_KV_HEADS, HEAD_DIM).transpose(1, 0, 2)]
    vs = [pa[:, qd + kd:].reshape(S, A_KV_HEADS, HEAD_DIM).transpose(1, 0, 2)]
    pb = proj[:, A_IN:].reshape(S, len(B_BRANCHES), 3, B_HPB, HEAD_DIM)
    for g, (_, d) in enumerate(B_BRANCHES):
        qs.append(_dilate_heads(pb[:, g, 0], d))
        ks.append(_dilate_heads(pb[:, g, 1], d))
        vs.append(_dilate_heads(pb[:, g, 2], d))
    return jnp.concatenate(qs, 0), jnp.concatenate(ks, 0), jnp.concatenate(vs, 0)


def _even_merge(dq, dk, dv):
    S = dq.shape[1]
    cols = [dq[:A_Q_HEADS].transpose(1, 0, 2).reshape(S, -1),
            dk[:A_KV_HEADS].transpose(1, 0, 2).reshape(S, -1),
            dv[:A_KV_HEADS].transpose(1, 0, 2).reshape(S, -1)]
    for g, (_, d) in enumerate(B_BRANCHES):
        qh = slice(A_Q_HEADS + g * B_HPB, A_Q_HEADS + (g + 1) * B_HPB)
        kh = slice(A_KV_HEADS + g * B_HPB, A_KV_HEADS + (g + 1) * B_HPB)
        for t in (dq[qh], dk[kh], dv[kh]):
            cols.append(_undilate_heads(t, d).reshape(S, -1))
    return jnp.concatenate(cols, axis=1)


def _even_post(o_all, lse_all):
    S = o_all.shape[1]
    out_a = o_all[:A_Q_HEADS].transpose(1, 0, 2).reshape(S, -1)
    outs, lses = [], []
    for g, (_, d) in enumerate(B_BRANCHES):
        hs = slice(A_Q_HEADS + g * B_HPB, A_Q_HEADS + (g + 1) * B_HPB)
        outs.append(_undilate_heads(o_all[hs], d))
        lses.append(_undilate_heads(lse_all[hs][..., None], d)[..., 0])
    wts = jax.nn.softmax(jnp.stack(lses), axis=0)
    out_b = sum(wts[g][..., None] * outs[g] for g in range(len(B_BRANCHES))).reshape(S, -1)
    return jnp.concatenate([out_a, out_b], axis=-1)


def _rope(t):
    S, r = t.shape[1], t.shape[-1]
    inv = ROPE_THETA ** (-jnp.arange(0, r, 2, dtype=jnp.float32) / r)
    ang = jnp.arange(S, dtype=jnp.float32)[:, None] * inv[None, :]
    shape = (1, S) + (1,) * (t.ndim - 3) + (r // 2,)
    cos, sin = jnp.cos(ang).reshape(shape), jnp.sin(ang).reshape(shape)
    t1, t2 = t[..., :r // 2], t[..., r // 2:]
    return jnp.concatenate([t1 * cos - t2 * sin, t1 * sin + t2 * cos], axis=-1)


def _mla_pre(q_lin, kv_lin, kr_raw):
    S = q_lin.shape[0]
    q = q_lin.reshape(1, S, C_HEADS, C_QK)
    qf = jnp.concatenate([q[..., :C_NOPE], _rope(q[..., C_NOPE:])], axis=-1)[0]
    kv = kv_lin.reshape(S, C_HEADS, C_NOPE + C_V)
    kr = _rope(kr_raw[None])[0]
    kf = jnp.concatenate([kv[..., :C_NOPE], jnp.broadcast_to(kr[:, None, :], (S, C_HEADS, C_ROPE))], axis=-1)
    return qf, kf, kv[..., C_NOPE:]


def _to_tiles_t(t, T):
    S, H, d = t.shape
    return t.reshape(S // T, T, H, d).transpose(2, 0, 3, 1)


def _from_tiles_t(t):
    H, n, d, T = t.shape
    return t.transpose(1, 3, 0, 2).reshape(n * T, H, d)


_BIG = (("w_in_ab", 2), ("w_out_ab", 2), ("w_down_c", 1), ("w_uq_c", 2), ("w_ukv_c", 2), ("w_o_c", 2),
        ("w_mlp_up", 2), ("w_mlp_down", 1))
_ROW_ALIGN = 512


def _pack_rows(arrs):
    rows = [a.reshape(-1, 128) for a in arrs]
    n = sum(r.shape[0] for r in rows)
    pad = (-n) % _ROW_ALIGN
    if pad:
        rows.append(jnp.zeros((pad, 128), rows[0].dtype))
    return jnp.concatenate(rows, axis=0)


def _pack_rows_per_device(arrs):
    rows = [a.reshape(N_DEV, -1, 128) for a in arrs]
    n = sum(r.shape[1] for r in rows)
    pad = (-n) % _ROW_ALIGN
    if pad:
        rows.append(jnp.zeros((N_DEV, pad, 128), rows[0].dtype))
    return jnp.concatenate(rows, axis=1)


def _unpack_rows(buf, shapes):
    out, r0 = [], 0
    for shp in shapes:
        n = math.prod(shp) // 128
        out.append(buf[..., r0:r0 + n, :].reshape(buf.shape[:-2] + tuple(shp)))
        r0 += n
    return out


def _layer_tensors(l):
    att = [("w_in_ab", l // 2), ("w_out_ab", l // 2)] if l % 2 == 0 else \
          [("w_down_c", l // 2), ("w_uq_c", l // 2), ("w_ukv_c", l // 2), ("w_o_c", l // 2)]
    return att + [("w_mlp_up", l), ("w_mlp_down", l)]


def _gathered_to_full(g, axis):
    if axis == 2:
        return g.transpose(1, 0, 2).reshape(g.shape[1], N_DEV * g.shape[2])
    return g.reshape(N_DEV * g.shape[1], g.shape[2])


def _full_to_shards(t, axis):
    a, b = t.shape
    if axis == 2:
        return t.reshape(a, N_DEV, b // N_DEV).transpose(1, 0, 2)
    return t.reshape(N_DEV, a // N_DEV, b)


def _pad_rows8(a):
    flat = a.reshape(-1)
    n = -(-flat.shape[0] // 1024) * 1024
    return jnp.pad(flat, (0, n - flat.shape[0])).reshape(-1, 128)


def _even_fwd(xn, h, w_in, w_out, bias, sinks_row, l):
    S = xn.shape[0]
    proj = _matmul(xn, w_in, out_dtype=MXU_DT, tm=1024, tn=512, name=f"even_in_{l}")
    q, k, v = _even_split(proj)
    o_all, lse = _banded_fwd(q, k, v, bias, sinks_row)
    attn, post_vjp = jax.vjp(_even_post, o_all, lse.reshape(N_BIAS_HEADS, S))
    attn = attn.astype(MXU_DT)
    h1 = _matmul(attn, w_out, epi='add', extra=h, tm=1024, tn=512, name=f"even_out_{l}")
    return h1, (q, k, v, attn, post_vjp)


def _even_bwd(dh, xn, ctx, w_in, w_out, bias, sinks_row, l):
    q, k, v, attn, post_vjp = ctx
    NH, S, dh_ = q.shape
    nb = S // N_CHUNKS // BLOCK
    d_attn = _matmul(dh, w_out, trans_b=True, tm=1024, tn=768, name=f"even_out_dx_{l}")
    g_w_out = _matmul_tn(attn, dh, tk=768, tn=512, name=f"even_out_dw_{l}")
    do_all, dlse = post_vjp(d_attn)
    dq, dk, dv, dkh, dvh, dbias_t, dsinks = _banded_bwd(q, k, v, bias.transpose(0, 2, 1), sinks_row, do_all,
                                                        dlse.reshape(NH, N_CHUNKS, nb, 1, BLOCK))
    dbias = dbias_t.transpose(0, 2, 1)

    def fold(t, halo, name):
        t = _halo_fold(t, halo, name)
        ta = t[:A_Q_HEADS].reshape(A_KV_HEADS, A_GROUP, S, dh_).sum(axis=1)
        return jnp.concatenate([ta, t[A_Q_HEADS:]], axis=0)

    dproj = _even_merge(dq, fold(dk, dkh, f"halo_k_{l}"), fold(dv, dvh, f"halo_v_{l}")).astype(MXU_DT)
    g_w_in = _matmul_tn(xn, dproj, tk=512, tn=1024, name=f"even_in_dw_{l}")
    dxn = _matmul(dproj, w_in, trans_b=True, tm=1024, tn=512, name=f"even_in_dx_{l}")
    return dxn, g_w_in, g_w_out, dbias, dsinks[:A_Q_HEADS, 0, 0]


def _mla_fwd(xn, h, w_down, q_norm, w_uq, kv_norm, w_ukv, w_o, l):
    S = xn.shape[0]
    T = min(FLASH_T, S)
    down = _matmul(xn, w_down, tm=1024, tn=768, name=f"mla_down_{l}")
    c_q, c_kv, kr_raw = down[:, :C_Q_RANK], down[:, C_Q_RANK:C_Q_RANK + C_KV_RANK], down[:, C_Q_RANK + C_KV_RANK:C_DOWN]
    cqn = _rmsnorm(c_q, q_norm, out_dtype=MXU_DT, name=f"mla_qnorm_{l}")
    ckvn = _rmsnorm(c_kv, kv_norm, out_dtype=MXU_DT, name=f"mla_kvnorm_{l}")
    q_lin = _matmul(cqn, w_uq, tm=1024, tn=768, name=f"mla_uq_{l}")
    kv_lin = _matmul(ckvn, w_ukv, tm=1024, tn=1024, name=f"mla_ukv_{l}")
    (qf, kf, vf), pre_vjp = jax.vjp(_mla_pre, q_lin, kv_lin, kr_raw)
    qf, kf, vf = qf.astype(MXU_DT), kf.astype(MXU_DT), vf.astype(MXU_DT)
    qt, kt, vt = _to_tiles_t(qf, T), _to_tiles_t(kf, T), _to_tiles_t(vf, T)
    kn, vn = kf.transpose(1, 0, 2), vf.transpose(1, 0, 2)
    ones = jnp.concatenate([jnp.ones(vt.shape[:2] + (1, T), MXU_DT),
                            jnp.zeros(vt.shape[:2] + (FLASH_ONES_ROWS - 1, T), MXU_DT)], axis=2)
    ot, lse = _flash_fwd(qt, kn, jnp.concatenate([vt, ones], axis=2))
    attn = _from_tiles_t(ot).reshape(S, C_HEADS * C_V).astype(MXU_DT)
    h1 = _matmul(attn, w_o, epi='add', extra=h, tm=1024, tn=512, name=f"mla_o_{l}")
    return h1, (c_q, c_kv, cqn, ckvn, pre_vjp, qt, kn, kt, vn, ot, lse, attn)


def _mla_bwd(dh, xn, ctx, w_down, q_norm, w_uq, kv_norm, w_ukv, w_o, l):
    c_q, c_kv, cqn, ckvn, pre_vjp, qt, kn, kt, vn, ot, lse, attn = ctx
    S = xn.shape[0]
    T = qt.shape[-1]
    d_attn = _matmul(dh, w_o, trans_b=True, tm=1024, tn=512, name=f"mla_o_dx_{l}")
    g_w_o = _matmul_tn(attn, dh, tk=512, tn=512, name=f"mla_o_dw_{l}")
    dot = _to_tiles_t(d_attn.reshape(S, C_HEADS, C_V), T)
    delta = _flash_delta(ot, dot)
    dqt, dkt, dvt = _flash_bwd(qt, kn, kt, vn, dot, lse, delta)
    dq_lin, dkv_lin, dkr_raw = pre_vjp((_from_tiles_t(dqt), _from_tiles_t(dkt), _from_tiles_t(dvt)))
    g_w_uq = _matmul_tn(cqn, dq_lin, tk=384, tn=768, name=f"mla_uq_dw_{l}")
    g_w_ukv = _matmul_tn(ckvn, dkv_lin, tk=256, tn=1024, name=f"mla_ukv_dw_{l}")
    dcqn = _matmul(dq_lin, w_uq, trans_b=True, tm=1024, tn=384, name=f"mla_uq_dx_{l}")
    dckvn = _matmul(dkv_lin, w_ukv, trans_b=True, tm=1024, tn=256, name=f"mla_ukv_dx_{l}")
    dc_q, g_q_norm = _rmsnorm_bwd(c_q, q_norm, dcqn, None, name=f"mla_qnorm_bwd_{l}")
    dc_kv, g_kv_norm = _rmsnorm_bwd(c_kv, kv_norm, dckvn, None, name=f"mla_kvnorm_bwd_{l}")
    ddown = jnp.concatenate([dc_q, dc_kv, dkr_raw, jnp.zeros((S, C_DOWN_PAD - C_DOWN), F32)], axis=1).astype(MXU_DT)
    g_w_down = _matmul_tn(xn, ddown, tk=512, tn=768, name=f"mla_down_dw_{l}")[:, :C_DOWN]
    dxn = _matmul(ddown, w_down, trans_b=True, tm=1024, tn=512, name=f"mla_down_dx_{l}")
    return dxn, g_w_down, g_q_norm[0], g_w_uq, g_kv_norm[0], g_w_ukv, g_w_o


def kernel(x, rel_bias, attn_norm, mlp_norm, final_norm, w_in_ab, sinks, w_out_ab, w_down_c, q_norm_c, w_uq_c, kv_norm_c, w_ukv_c, w_o_c, w_mlp_up, w_mlp_down, loss_target, m_rel_bias, m_attn_norm, m_mlp_norm, m_final_norm, m_w_in_ab, m_sinks, m_w_out_ab, m_w_down_c, m_q_norm_c, m_w_uq_c, m_kv_norm_c, m_w_ukv_c, m_w_o_c, m_w_mlp_up, m_w_mlp_down, v_rel_bias, v_attn_norm, v_mlp_norm, v_final_norm, v_w_in_ab, v_sinks, v_w_out_ab, v_w_down_c, v_q_norm_c, v_w_uq_c, v_kv_norm_c, v_w_ukv_c, v_w_o_c, v_w_mlp_up, v_w_mlp_down):
    W = dict(w_in_ab=w_in_ab, w_out_ab=w_out_ab, w_down_c=w_down_c, w_uq_c=w_uq_c, w_ukv_c=w_ukv_c, w_o_c=w_o_c,
             w_mlp_up=w_mlp_up, w_mlp_down=w_mlp_down)
    Mo = dict(w_in_ab=m_w_in_ab, w_out_ab=m_w_out_ab, w_down_c=m_w_down_c, w_uq_c=m_w_uq_c, w_ukv_c=m_w_ukv_c,
              w_o_c=m_w_o_c, w_mlp_up=m_w_mlp_up, w_mlp_down=m_w_mlp_down)
    Vo = dict(w_in_ab=v_w_in_ab, w_out_ab=v_w_out_ab, w_down_c=v_w_down_c, w_uq_c=v_w_uq_c, w_ukv_c=v_w_ukv_c,
              w_o_c=v_w_o_c, w_mlp_up=v_w_mlp_up, w_mlp_down=v_w_mlp_down)
    S = x.shape[1]
    me = 4 * lax.axis_index("x") + 2 * lax.axis_index("y") + lax.axis_index("c")
    axis_of = dict(_BIG)

    def pack(src, l):
        return _pack_rows([src[n][i] for n, i in _layer_tensors(l)])

    def unpack_layer(gathered, l):
        shapes = [W[n].shape[1:] for n, _ in _layer_tensors(l)]
        return {n: _gathered_to_full(g, axis_of[n])
                for (n, _), g in zip(_layer_tensors(l), _unpack_rows(gathered, shapes))}

    w_packs = [pack(W, l) for l in range(DEPTH)]
    gathered = _exchange(w_packs[0].astype(MXU_DT), False, "gather_weights_0")
    gains = _exchange(jnp.concatenate([_pad_rows8(q_norm_c), _pad_rows8(kv_norm_c)], axis=0), False, "gather_gains")
    n_odd = q_norm_c.shape[0]
    q_norm_full = gains[:, 0].reshape(N_DEV, -1)[:, :q_norm_c.size].reshape(N_DEV, n_odd, -1).transpose(1, 0, 2).reshape(n_odd, C_Q_RANK)
    kv_norm_full = gains[:, 8].reshape(N_DEV, -1)[:, :kv_norm_c.size].reshape(N_DEV, n_odd, -1).transpose(1, 0, 2).reshape(n_odd, C_KV_RANK)

    bias, bias_vjp = jax.vjp(_make_bias, rel_bias)
    sink_rows = [jnp.broadcast_to(jnp.concatenate([sinks[e], jnp.full((B_HEADS,), NEG, F32)])[:, None, None],
                                  (N_BIAS_HEADS, 1, 128)) for e in range(sinks.shape[0])]

    h = x[0]
    saved = []
    for l in range(DEPTH):
        full = unpack_layer(gathered, l)
        if l % 2 == 1:
            full["w_down_c"] = jnp.pad(full["w_down_c"], ((0, 0), (0, C_DOWN_PAD - C_DOWN)))
        xn = _rmsnorm(h, attn_norm[l], out_dtype=MXU_DT, name=f"attn_norm_{l}")
        if l % 2 == 0:
            h1, ctx = _even_fwd(xn, h, full["w_in_ab"], full["w_out_ab"], bias, sink_rows[l // 2], l)
        else:
            o = l // 2
            h1, ctx = _mla_fwd(xn, h, full["w_down_c"], q_norm_full[o], full["w_uq_c"], kv_norm_full[o],
                               full["w_ukv_c"], full["w_o_c"], l)
        xn2 = _rmsnorm(h1, mlp_norm[l], out_dtype=MXU_DT, name=f"mlp_norm_{l}")
        if l + 1 < DEPTH:
            act, gathered = _matmul(xn2, full["w_mlp_up"], out_dtype=MXU_DT, epi='relu2', tm=1024, tn=512,
                                    name=f"mlp_up_{l}", ride=(w_packs[l + 1].astype(MXU_DT), False))
        else:
            act = _matmul(xn2, full["w_mlp_up"], out_dtype=MXU_DT, epi='relu2', tm=1024, tn=512, name=f"mlp_up_{l}")
        h2 = _matmul(act, full["w_mlp_down"], epi='add', extra=h1, tm=512, tn=512, name=f"mlp_down_{l}")
        saved.append((h, xn, h1, xn2, act, ctx, full))
        h = h2

    loss_row, dh, g_final = _loss_head(h, loss_target[0], final_norm)

    g_attn_norm, g_mlp_norm = [None] * DEPTH, [None] * DEPTH
    g_sinks, g_qn, g_kvn = [None] * sinks.shape[0], [None] * n_odd, [None] * n_odd
    dbias_total = None
    landed = [None] * DEPTH
    send = None
    for l in reversed(range(DEPTH)):
        h0, xn, h1, xn2, act, ctx, full = saved[l]
        G = {}
        if send is None:
            du = _matmul(dh, full["w_mlp_down"], trans_b=True, out_dtype=MXU_DT, epi='dsq', extra=act,
                         tm=1024, tn=512, name=f"mlp_down_dx_{l}")
        else:
            du, landed[l + 1] = _matmul(dh, full["w_mlp_down"], trans_b=True, out_dtype=MXU_DT, epi='dsq', extra=act,
                                        tm=1024, tn=512, name=f"mlp_down_dx_{l}", ride=(send, True))
        G["w_mlp_down"] = _matmul_tn(act, dh, tk=512, tn=1024, name=f"mlp_down_dw_{l}")
        G["w_mlp_up"] = _matmul_tn(xn2, du, tk=512, tn=1024, name=f"mlp_up_dw_{l}")
        dxn2 = _matmul(du, full["w_mlp_up"], trans_b=True, tm=512, tn=512, name=f"mlp_up_dx_{l}")
        dh, g = _rmsnorm_bwd(h1, mlp_norm[l], dxn2, dh, name=f"mlp_norm_bwd_{l}")
        g_mlp_norm[l] = g[0]
        if l % 2 == 0:
            e = l // 2
            dxn, G["w_in_ab"], G["w_out_ab"], dbias, g_sinks[e] = _even_bwd(
                dh, xn, ctx, full["w_in_ab"], full["w_out_ab"], bias, sink_rows[e], l)
            dbias_total = dbias if dbias_total is None else dbias_total + dbias
        else:
            o = l // 2
            dxn, G["w_down_c"], g_qn[o], G["w_uq_c"], g_kvn[o], G["w_ukv_c"], G["w_o_c"] = _mla_bwd(
                dh, xn, ctx, full["w_down_c"], q_norm_full[o], full["w_uq_c"], kv_norm_full[o],
                full["w_ukv_c"], full["w_o_c"], l)
        dh, g = _rmsnorm_bwd(h0, attn_norm[l], dxn, dh, name=f"attn_norm_bwd_{l}")
        g_attn_norm[l] = g[0]
        send = _pack_rows_per_device([_full_to_shards(G[n], axis_of[n]) for n, _ in _layer_tensors(l)]).astype(MXU_DT)
    landed[0] = _exchange(send, True, "scatter_grads_0")
    grad_x = dh[None]
    (g_rel_bias,) = bias_vjp(dbias_total)

    big = [{}, {}, {}, {}]
    for l in range(DEPTH):
        outs = _adamw(landed[l], w_packs[l], pack(Mo, l), pack(Vo, l), name=f"adamw_{l}")
        shapes = [W[n].shape[1:] for n, _ in _layer_tensors(l)]
        for kind, buf in enumerate(outs):
            for (n, _), t in zip(_layer_tensors(l), _unpack_rows(buf, shapes)):
                big[kind].setdefault(n, []).append(t)
    big_out = [{n: jnp.stack(ts) for n, ts in d.items()} for d in big]

    small_g = [g_rel_bias, jnp.stack(g_attn_norm), jnp.stack(g_mlp_norm), g_final[0], jnp.stack(g_sinks),
               jnp.stack(g_qn), jnp.stack(g_kvn), loss_row[0, :1]]
    small_w = [rel_bias, attn_norm, mlp_norm, final_norm, sinks, q_norm_c, kv_norm_c, jnp.zeros((1,), F32)]
    small_m = [m_rel_bias, m_attn_norm, m_mlp_norm, m_final_norm, m_sinks, m_q_norm_c, m_kv_norm_c, jnp.zeros((1,), F32)]
    small_v = [v_rel_bias, v_attn_norm, v_mlp_norm, v_final_norm, v_sinks, v_q_norm_c, v_kv_norm_c, jnp.ones((1,), F32)]
    offs = np.cumsum([0] + [-(-a.size // 1024) * 8 for a in small_g])
    partials = _exchange(jnp.concatenate([_pad_rows8(a) for a in small_g], axis=0), False, "gather_small_grads")

    def mine(i, a_full_shape, local):
        p = partials[:, offs[i]:offs[i + 1]].reshape(N_DEV, -1)[:, :math.prod(a_full_shape)]
        p = p.reshape((N_DEV,) + tuple(a_full_shape))
        if local.shape != tuple(a_full_shape):
            width = local.shape[-1]
            p = lax.dynamic_slice_in_dim(p, me * width, width, axis=p.ndim - 1)
        return jnp.stack([_pad_rows8(p[s]) for s in range(N_DEV)])

    parts_small = jnp.concatenate([mine(i, g.shape, w) for i, (g, w) in enumerate(zip(small_g, small_w))], axis=1)
    pk = lambda arrs: jnp.concatenate([_pad_rows8(a) for a in arrs], axis=0)
    small_out = _adamw(parts_small, pk(small_w), pk(small_m), pk(small_v), name="adamw_small", tr=parts_small.shape[1])
    offs2 = np.cumsum([0] + [-(-a.size // 1024) * 8 for a in small_w])

    def unpack_small(buf):
        return [buf[offs2[i]:offs2[i + 1]].reshape(-1)[:a.size].reshape(a.shape) for i, a in enumerate(small_w)]

    sg, sd, sm, sv = (unpack_small(b) for b in small_out)
    loss = sg[7][0]

    order = ['rel_bias', 'attn_norm', 'mlp_norm', 'final_norm', 'w_in_ab', 'sinks', 'w_out_ab', 'w_down_c', 'q_norm_c',
             'w_uq_c', 'kv_norm_c', 'w_ukv_c', 'w_o_c', 'w_mlp_up', 'w_mlp_down']
    small_idx = {'rel_bias': 0, 'attn_norm': 1, 'mlp_norm': 2, 'final_norm': 3, 'sinks': 4, 'q_norm_c': 5, 'kv_norm_c': 6}

    def pick(kind):
        res = []
        for n in order:
            if n in small_idx:
                res.append((sg, sd, sm, sv)[kind][small_idx[n]])
            else:
                res.append(big_out[kind][n])
        return res

    return (loss, grad_x, *pick(0), *pick(1), *pick(2), *pick(3))
```

```python
import math

import numpy as np
import jax
import jax.numpy as jnp
from jax import lax
from jax.experimental import pallas as pl
from jax.experimental.pallas import tpu as pltpu

F32 = jnp.float32
MXU_DT = jnp.bfloat16

N_DEV = 8
D_MODEL = 1024
DEPTH = 4
HEAD_DIM = 64
BLOCK = 128
EPS = 1e-6
NEG = -1e30
A_Q_HEADS = 8
A_KV_HEADS = 2
A_GROUP = A_Q_HEADS // A_KV_HEADS
A_WINDOW = 128
B_BRANCHES = ((128, 1), (512, 4), (2048, 16))
B_HPB = 4
B_HEADS = len(B_BRANCHES) * B_HPB
NUM_BUCKETS = 32
MAX_DISTANCE = 2048
N_BIAS_HEADS = A_Q_HEADS + B_HEADS
N_BAND_KV = A_KV_HEADS + B_HEADS
A_IN = (A_Q_HEADS + 2 * A_KV_HEADS) * HEAD_DIM
C_HEADS = 8
C_NOPE = 64
C_ROPE = 32
C_QK = C_NOPE + C_ROPE
C_V = 64
C_Q_RANK = 384
C_KV_RANK = 256
C_DOWN = C_Q_RANK + C_KV_RANK + C_ROPE
C_DOWN_PAD = 768
ROPE_THETA = 10000.0
N_CHUNKS = 16
FLASH_T = 512

ADAM_LR = 0.001
ADAM_B1 = 0.9
ADAM_B2 = 0.999
ADAM_EPS = 1e-08
ADAM_WD = 0.01
ADAM_STEP = 10

V7X_VMEM_BYTES = 64 * 1024 * 1024
VMEM_LIMIT = V7X_VMEM_BYTES - 8 * 1024 * 1024


def _pcall(body, **kw):
    return pl.pallas_call(body, **kw)


def _cparams(*sem):
    return pltpu.CompilerParams(dimension_semantics=sem, vmem_limit_bytes=VMEM_LIMIT)


def _exchange(src, all_to_all, name):
    def body(src_ref, out_ref, send_sems, recv_sems, local_sem):
        copies = _exchange_copies(src_ref, out_ref, send_sems, recv_sems, local_sem, all_to_all)
        for cp in copies:
            cp.start()
        _exchange_wait(copies)

    return _pcall(
        body, name=name,
        out_shape=_exchange_out(src),
        in_specs=[pl.BlockSpec(memory_space=pl.ANY)],
        out_specs=pl.BlockSpec(memory_space=pl.ANY),
        scratch_shapes=_exchange_sems(),
    )(src)


def _exchange_out(src):
    return jax.ShapeDtypeStruct((N_DEV,) + src.shape[-2:], src.dtype)


def _exchange_sems():
    return [pltpu.SemaphoreType.DMA((N_DEV - 1,)), pltpu.SemaphoreType.DMA((N_DEV - 1,)), pltpu.SemaphoreType.DMA]


def _exchange_copies(src_ref, out_ref, send_sems, recv_sems, local_sem, all_to_all):
    x, y, c = lax.axis_index("x"), lax.axis_index("y"), lax.axis_index("c")
    me = 4 * x + 2 * y + c

    def piece(dev):
        return src_ref.at[dev] if all_to_all else src_ref

    copies = [pltpu.make_async_copy(piece(me), out_ref.at[me], local_sem)]
    for k in range(1, N_DEV):
        px = 1 - x if (k >> 2) & 1 else x
        py = 1 - y if (k >> 1) & 1 else y
        pc = 1 - c if k & 1 else c
        copies.append(pltpu.make_async_remote_copy(
            src_ref=piece(4 * px + 2 * py + pc), dst_ref=out_ref.at[me],
            send_sem=send_sems.at[k - 1], recv_sem=recv_sems.at[k - 1],
            device_id=(px, py, pc), device_id_type=pl.DeviceIdType.MESH))
    return copies


def _exchange_wait(copies):
    for cp in copies[1:]:
        cp.wait()
    copies[0].wait()


def _matmul(a, b, *, trans_b=False, out_dtype=F32, epi=None, extra=None, tm=512, tn=512, name, ride=None):
    M, K = a.shape
    N = b.shape[0] if trans_b else b.shape[1]
    tm, tn = min(tm, M), min(tn, N)
    assert M % tm == 0 and N % tn == 0 and (b.shape[1] if trans_b else b.shape[0]) == K
    dn = (((1,), (1,)), ((), ())) if trans_b else (((1,), (0,)), ((), ()))
    n_i, n_j = M // tm, N // tn
    n_in = 2 + (extra is not None)

    def body(*refs):
        a_ref, b_ref = refs[0], refs[1]
        o_ref = refs[n_in + (ride is not None)]
        if ride is not None:
            i, j = pl.program_id(0), pl.program_id(1)
            copies = _exchange_copies(refs[n_in], refs[n_in + 2], *refs[n_in + 3:], ride[1])

            @pl.when((i == 0) & (j == 0))
            def _():
                for cp in copies:
                    cp.start()

        acc = lax.dot_general(a_ref[...].astype(MXU_DT), b_ref[...].astype(MXU_DT), dn,
                              preferred_element_type=F32)
        if epi == 'relu2':
            r = jnp.maximum(acc, 0.0)
            acc = r * r
        elif epi == 'add':
            acc = acc + refs[2][...].astype(F32)
        elif epi == 'dsq':
            acc = acc * (2.0 * jnp.sqrt(refs[2][...].astype(F32)))
        o_ref[...] = acc.astype(out_dtype)

        if ride is not None:
            @pl.when((i == n_i - 1) & (j == n_j - 1))
            def _():
                _exchange_wait(copies)

    b_spec = pl.BlockSpec((tn, K), lambda i, j: (j, 0)) if trans_b else pl.BlockSpec((K, tn), lambda i, j: (0, j))
    in_specs = [pl.BlockSpec((tm, K), lambda i, j: (i, 0)), b_spec]
    args = [a, b]
    if extra is not None:
        in_specs.append(pl.BlockSpec((tm, tn), lambda i, j: (i, j)))
        args.append(extra)
    out_shape = jax.ShapeDtypeStruct((M, N), out_dtype)
    out_spec = pl.BlockSpec((tm, tn), lambda i, j: (i, j))
    if ride is None:
        return _pcall(
            body, name=name, grid=(n_i, n_j), out_shape=out_shape, in_specs=in_specs, out_specs=out_spec,
            compiler_params=_cparams("parallel", "parallel"),
        )(*args)
    return _pcall(
        body, name=name, grid=(n_i, n_j),
        out_shape=(out_shape, _exchange_out(ride[0])),
        in_specs=in_specs + [pl.BlockSpec(memory_space=pl.ANY)],
        out_specs=(out_spec, pl.BlockSpec(memory_space=pl.ANY)),
        scratch_shapes=_exchange_sems(),
        compiler_params=_cparams("arbitrary", "arbitrary"),
    )(*args, ride[0])


def _matmul_tn(a, b, *, tk=512, tn=512, tm=1024, name):
    M, Ka = a.shape
    N = b.shape[1]
    tk, tn, tm = min(tk, Ka), min(tn, N), min(tm, M)
    assert Ka % tk == 0 and N % tn == 0 and M % tm == 0 and b.shape[0] == M

    def body(a_ref, b_ref, o_ref):
        @pl.when(pl.program_id(2) == 0)
        def _():
            o_ref[...] = jnp.zeros_like(o_ref)

        o_ref[...] += lax.dot_general(a_ref[...].astype(MXU_DT), b_ref[...].astype(MXU_DT),
                                      (((0,), (0,)), ((), ())), preferred_element_type=F32)

    return _pcall(
        body, name=name, grid=(Ka // tk, N // tn, M // tm),
        out_shape=jax.ShapeDtypeStruct((Ka, N), F32),
        in_specs=[pl.BlockSpec((tm, tk), lambda i, j, r: (r, i)), pl.BlockSpec((tm, tn), lambda i, j, r: (r, j))],
        out_specs=pl.BlockSpec((tk, tn), lambda i, j, r: (i, j)),
        compiler_params=_cparams("parallel", "parallel", "arbitrary"),
    )(a, b)


def _rmsnorm(x, g, *, out_dtype, name, tr=512):
    S, D = x.shape
    tr = min(tr, S)

    def body(x_ref, g_ref, o_ref):
        xf = x_ref[...].astype(F32)
        r = lax.rsqrt(jnp.mean(xf * xf, axis=-1, keepdims=True) + EPS)
        o_ref[...] = (xf * r * g_ref[...]).astype(out_dtype)

    return _pcall(
        body, name=name, grid=(S // tr,),
        out_shape=jax.ShapeDtypeStruct((S, D), out_dtype),
        in_specs=[pl.BlockSpec((tr, D), lambda i: (i, 0)), pl.BlockSpec((1, D), lambda i: (0, 0))],
        out_specs=pl.BlockSpec((tr, D), lambda i: (i, 0)),
        compiler_params=_cparams("parallel"),
    )(x, g.reshape(1, D))


def _rmsnorm_bwd(x, g, dy, dres, *, name, tr=512):
    S, D = x.shape
    tr = min(tr, S)

    def body(*refs):
        if dres is None:
            x_ref, g_ref, dy_ref, dx_ref, dg_ref = refs
        else:
            x_ref, g_ref, dy_ref, dres_ref, dx_ref, dg_ref = refs

        @pl.when(pl.program_id(0) == 0)
        def _():
            dg_ref[...] = jnp.zeros_like(dg_ref)

        xf = x_ref[...].astype(F32)
        r = lax.rsqrt(jnp.mean(xf * xf, axis=-1, keepdims=True) + EPS)
        xhat = xf * r
        dyf = dy_ref[...].astype(F32)
        dg_ref[...] += jnp.sum(dyf * xhat, axis=0, keepdims=True)
        dyg = dyf * g_ref[...]
        dx = r * (dyg - xhat * jnp.mean(dyg * xhat, axis=-1, keepdims=True))
        if dres is not None:
            dx = dx + dres_ref[...]
        dx_ref[...] = dx

    row = pl.BlockSpec((tr, D), lambda i: (i, 0))
    vec = pl.BlockSpec((1, D), lambda i: (0, 0))
    args = [x, g.reshape(1, D), dy] + ([] if dres is None else [dres])
    return _pcall(
        body, name=name, grid=(S // tr,),
        out_shape=(jax.ShapeDtypeStruct((S, D), F32), jax.ShapeDtypeStruct((1, D), F32)),
        in_specs=[row, vec, row] + ([] if dres is None else [row]),
        out_specs=(row, vec),
        compiler_params=_cparams("arbitrary"),
    )(*args)


def _loss_head(h, t, g, *, tr=512):
    S, D = h.shape
    tr = min(tr, S)

    def body(h_ref, t_ref, g_ref, loss_ref, dh_ref, dg_ref):
        @pl.when(pl.program_id(0) == 0)
        def _():
            dg_ref[...] = jnp.zeros_like(dg_ref)
            loss_ref[...] = jnp.zeros_like(loss_ref)

        xf = h_ref[...]
        r = lax.rsqrt(jnp.mean(xf * xf, axis=-1, keepdims=True) + EPS)
        xhat = xf * r
        e = xhat * g_ref[...] - t_ref[...]
        part = 0.5 * jnp.sum(jnp.mean(e * e, axis=-1, keepdims=True), axis=0, keepdims=True)
        loss_ref[...] += jnp.broadcast_to(part, loss_ref.shape)
        dy = e * (1.0 / D)
        dg_ref[...] += jnp.sum(dy * xhat, axis=0, keepdims=True)
        dyg = dy * g_ref[...]
        dh_ref[...] = r * (dyg - xhat * jnp.mean(dyg * xhat, axis=-1, keepdims=True))

    row = pl.BlockSpec((tr, D), lambda i: (i, 0))
    vec = pl.BlockSpec((1, D), lambda i: (0, 0))
    return _pcall(
        body, name="loss_head", grid=(S // tr,),
        out_shape=(jax.ShapeDtypeStruct((1, 128), F32), jax.ShapeDtypeStruct((S, D), F32),
                   jax.ShapeDtypeStruct((1, D), F32)),
        in_specs=[row, row, vec],
        out_specs=(pl.BlockSpec((1, 128), lambda i: (0, 0)), row, vec),
        compiler_params=_cparams("arbitrary"),
    )(h, t, g.reshape(1, D))


N_PAIRS = N_BIAS_HEADS // 2
A_PAIRS = A_Q_HEADS // 2
PAIR_W = 2 * HEAD_DIM
assert PAIR_W == 128 and A_KV_HEADS * HEAD_DIM == PAIR_W and B_HPB * HEAD_DIM == 2 * PAIR_W


def _pair_period(p):
    return jnp.where(p < A_PAIRS + 2, 16, jnp.where(p < A_PAIRS + 4, 4, 1))


def _pair_cols(p):
    b = jnp.maximum(p - A_PAIRS, 0)
    base, pp = 6 + 6 * (b // 2), b % 2
    is_a = p < A_PAIRS
    return (jnp.where(is_a, p, base + pp), jnp.where(is_a, A_PAIRS, base + 2 + pp),
            jnp.where(is_a, A_PAIRS + 1, base + 4 + pp))


def _band_specs(S):
    ch = S // N_CHUNKS
    nb = ch // BLOCK
    col = lambda i: (lambda p, c: (c, _pair_cols(p)[i]))
    prev = lambda i: (lambda p, c: (jnp.maximum(c * nb - 1, 0), _pair_cols(p)[i]))
    qkv = [pl.BlockSpec((ch, PAIR_W), col(0)), pl.BlockSpec((ch, PAIR_W), col(1)), pl.BlockSpec((BLOCK, PAIR_W), prev(1)),
           pl.BlockSpec((ch, PAIR_W), col(2)), pl.BlockSpec((BLOCK, PAIR_W), prev(2))]
    out_spec = pl.BlockSpec((ch, PAIR_W), lambda p, c: (c, p))
    sink_spec = pl.BlockSpec((2, 1, 128), lambda p, c: (p, 0, 0))
    row_spec = pl.BlockSpec((1, 1, nb, 2, BLOCK), lambda p, c: (p, c, 0, 0, 0))
    return ch, nb, qkv, out_spec, sink_spec, row_spec


def _pair_kv(p, e, ref):
    half = jnp.where(p < A_PAIRS, p // (A_GROUP // 2), e)
    return jnp.where(half == 0, ref[:, :HEAD_DIM], ref[:, HEAD_DIM:])


def _eye():
    return lax.broadcasted_iota(jnp.int32, (BLOCK, BLOCK), 0) == lax.broadcasted_iota(jnp.int32, (BLOCK, BLOCK), 1)


_NT = (((1,), (1,)), ((), ()))
_NN = (((1,), (0,)), ((), ()))
_TN = (((0,), (0,)), ((), ()))


_B_NT = (((2,), (2,)), ((0,), (0,)))
_B_NN = (((2,), (1,)), ((0,), (0,)))


def _bdot(a, b, dn):
    return lax.dot_general(a, b, dn, preferred_element_type=F32)


def _with_prev(first, t3):
    return first[None] if t3.shape[0] == 1 else jnp.concatenate([first[None], t3[:-1]], axis=0)


def _mask_first(s_prev, prev_ok):
    s0 = jnp.where(prev_ok, s_prev[0], NEG)[None]
    return s0 if s_prev.shape[0] == 1 else jnp.concatenate([s0, s_prev[1:]], axis=0)


def _banded_fwd(proj, bias, sinks):
    S = proj.shape[0]
    dh = HEAD_DIM
    ch, nb, qkv, out_spec, sink_spec, row_spec = _band_specs(S)
    bias_spec = pl.BlockSpec((2, BLOCK, 2 * BLOCK), lambda p, c: (p, 0, 0))
    scale = HEAD_DIM ** -0.5

    def body(q_ref, k_ref, kp_ref, v_ref, vp_ref, b_ref, s_ref, o_ref, lse_ref):
        p, c = pl.program_id(0), pl.program_id(1)
        prev_ok = (c % _pair_period(p)) != 0
        for e in range(2):
            lanes = slice(e * dh, (e + 1) * dh)
            q3 = q_ref[:, lanes].reshape(nb, BLOCK, dh)
            k3, v3 = (_pair_kv(p, e, r).reshape(nb, BLOCK, dh) for r in (k_ref, v_ref))
            kp3, vp3 = _with_prev(_pair_kv(p, e, kp_ref), k3), _with_prev(_pair_kv(p, e, vp_ref), v3)
            sink = s_ref[e, :, :1]
            s_cur = _bdot(q3, k3, _B_NT) * scale + b_ref[e, :, BLOCK:][None]
            s_prev = _mask_first(_bdot(q3, kp3, _B_NT) * scale + b_ref[e, :, :BLOCK][None], prev_ok)
            m = jnp.maximum(jnp.max(s_cur, axis=-1, keepdims=True), jnp.max(s_prev, axis=-1, keepdims=True))
            m = jnp.maximum(m, sink)
            p_cur = jnp.exp(s_cur - m)
            p_prev = jnp.exp(s_prev - m)
            l = jnp.sum(p_cur, axis=-1, keepdims=True) + jnp.sum(p_prev, axis=-1, keepdims=True) + jnp.exp(sink - m)
            acc = _bdot(p_cur.astype(MXU_DT), v3, _B_NN) + _bdot(p_prev.astype(MXU_DT), vp3, _B_NN)
            o_ref[:, lanes] = (acc / l).reshape(ch, dh)
            lse = m + jnp.log(l)
            lse_ref[0, 0, :, e:e + 1, :] = jnp.sum(jnp.where(_eye()[None], lse, 0.0), axis=1, keepdims=True)

    return _pcall(
        body, name="banded_fwd", grid=(N_PAIRS, N_CHUNKS),
        out_shape=(jax.ShapeDtypeStruct((S, N_PAIRS * PAIR_W), F32),
                   jax.ShapeDtypeStruct((N_PAIRS, N_CHUNKS, nb, 2, BLOCK), F32)),
        in_specs=qkv + [bias_spec, sink_spec],
        out_specs=(out_spec, row_spec),
        compiler_params=_cparams("parallel", "parallel"),
    )(proj, proj, proj, proj, proj, bias, sinks)


def _banded_bwd(proj, bias_t, sinks, do, dlse):
    S = proj.shape[0]
    dh = HEAD_DIM
    ch, nb, qkv, out_spec, sink_spec, row_spec = _band_specs(S)
    bias_spec = pl.BlockSpec((2, 2 * BLOCK, BLOCK), lambda p, c: (p, 0, 0))
    scale = HEAD_DIM ** -0.5

    def body(q_ref, k_ref, kp_ref, v_ref, vp_ref, b_ref, s_ref, do_ref, dl_ref,
             dq_ref, dk_ref, dv_ref, dkh_ref, dvh_ref, db_ref, ds_ref):
        p, c = pl.program_id(0), pl.program_id(1)

        @pl.when(c == 0)
        def _():
            db_ref[...] = jnp.zeros_like(db_ref)
            ds_ref[...] = jnp.zeros_like(ds_ref)

        prev_ok = (c % _pair_period(p)) != 0
        for e in range(2):
            lanes = slice(e * dh, (e + 1) * dh)
            q3 = q_ref[:, lanes].reshape(nb, BLOCK, dh)
            k3, v3 = (_pair_kv(p, e, r).reshape(nb, BLOCK, dh) for r in (k_ref, v_ref))
            kp3, vp3 = _with_prev(_pair_kv(p, e, kp_ref), k3), _with_prev(_pair_kv(p, e, vp_ref), v3)
            do3 = do_ref[:, lanes].astype(MXU_DT).reshape(nb, BLOCK, dh)
            sink = s_ref[e, :, :1]
            s_cur = _bdot(k3, q3, _B_NT) * scale + b_ref[e, BLOCK:, :][None]
            s_prev = _mask_first(_bdot(kp3, q3, _B_NT) * scale + b_ref[e, :BLOCK, :][None], prev_ok)
            m = jnp.maximum(jnp.max(s_cur, axis=1, keepdims=True), jnp.max(s_prev, axis=1, keepdims=True))
            m = jnp.maximum(m, sink)
            p_cur = jnp.exp(s_cur - m)
            p_prev = jnp.exp(s_prev - m)
            p_sink = jnp.exp(sink - m)
            inv = 1.0 / (jnp.sum(p_cur, axis=1, keepdims=True) + jnp.sum(p_prev, axis=1, keepdims=True) + p_sink)
            p_cur, p_prev, p_sink = p_cur * inv, p_prev * inv, p_sink * inv
            dp_cur = _bdot(v3, do3, _B_NT)
            dp_prev = _bdot(vp3, do3, _B_NT)
            delta = jnp.sum(p_cur * dp_cur, axis=1, keepdims=True) + jnp.sum(p_prev * dp_prev, axis=1, keepdims=True)
            t = dl_ref[0, 0, :, e:e + 1, :] - delta
            ds_cur = p_cur * (dp_cur + t)
            ds_prev = p_prev * (dp_prev + t)
            dsink = jnp.sum(jnp.sum(p_sink * t, axis=0), axis=-1, keepdims=True)
            ds_ref[e] += jnp.broadcast_to(dsink, (1, 128))
            db_ref[e, :BLOCK, :] += jnp.sum(ds_prev, axis=0)
            db_ref[e, BLOCK:, :] += jnp.sum(ds_cur, axis=0)
            dsb_cur = (ds_cur * scale).astype(MXU_DT)
            dsb_prev = (ds_prev * scale).astype(MXU_DT)
            dk_prev = _bdot(dsb_prev, q3, _B_NN)
            dv_prev = _bdot(p_prev.astype(MXU_DT), do3, _B_NN)

            def shifted(t3):
                z = jnp.zeros((1, BLOCK, dh), F32)
                return z if nb == 1 else jnp.concatenate([t3[1:], z], axis=0)

            dk_ref[:, lanes] = (_bdot(dsb_cur, q3, _B_NN) + shifted(dk_prev)).reshape(ch, dh)
            dv_ref[:, lanes] = (_bdot(p_cur.astype(MXU_DT), do3, _B_NN) + shifted(dv_prev)).reshape(ch, dh)
            dkh_ref[0, 0, :, lanes] = dk_prev[0]
            dvh_ref[0, 0, :, lanes] = dv_prev[0]
            for b in range(nb):
                dq_ref[b * BLOCK:(b + 1) * BLOCK, lanes] = (
                    lax.dot_general(dsb_cur[b], k3[b], _TN, preferred_element_type=F32)
                    + lax.dot_general(dsb_prev[b], kp3[b], _TN, preferred_element_type=F32))

    halo_spec = pl.BlockSpec((1, 1, BLOCK, PAIR_W), lambda p, c: (p, c, 0, 0))
    big = jax.ShapeDtypeStruct((S, N_PAIRS * PAIR_W), F32)
    halo = jax.ShapeDtypeStruct((N_PAIRS, N_CHUNKS, BLOCK, PAIR_W), F32)
    return _pcall(
        body, name="banded_bwd", grid=(N_PAIRS, N_CHUNKS),
        out_shape=(big, big, big, halo, halo, jax.ShapeDtypeStruct(bias_t.shape, F32),
                   jax.ShapeDtypeStruct(sinks.shape, F32)),
        in_specs=qkv + [bias_spec, sink_spec, out_spec, row_spec],
        out_specs=(out_spec, out_spec, out_spec, halo_spec, halo_spec, bias_spec, sink_spec),
        compiler_params=_cparams("arbitrary", "arbitrary"),
    )(proj, proj, proj, proj, proj, bias_t, sinks, do, dlse)


def _halo_fold(t, halo, name):
    S, width = t.shape
    nb = S // N_CHUNKS // BLOCK

    def body(t_ref, h_ref, o_ref):
        o_ref[:N_CHUNKS - 1, 0] = t_ref[:N_CHUNKS - 1, 0] + h_ref[0, 1:]
        o_ref[N_CHUNKS - 1:, 0] = t_ref[N_CHUNKS - 1:, 0]

    blk = pl.BlockSpec((N_CHUNKS, 1, BLOCK, PAIR_W), lambda p: (0, nb - 1, 0, p))
    return _pcall(
        body, name=name, grid=(N_PAIRS,),
        out_shape=jax.ShapeDtypeStruct((N_CHUNKS, nb, BLOCK, width), t.dtype),
        in_specs=[blk, pl.BlockSpec((1, N_CHUNKS, BLOCK, PAIR_W), lambda p: (p, 0, 0, 0))],
        out_specs=blk, input_output_aliases={0: 0},
        compiler_params=_cparams("parallel"),
    )(t.reshape(N_CHUNKS, nb, BLOCK, width), halo).reshape(S, width)


def _causal_mask(T):
    return lax.broadcasted_iota(jnp.int32, (T, T), 0) <= lax.broadcasted_iota(jnp.int32, (T, T), 1)


LOG2E = math.log2(math.e)
FLASH_SPLIT = 2
FLASH_ONES_ROWS = 16


def _flash_fwd(qt, k, vt1):
    H, nq, dqk, T = qt.shape
    S = k.shape[1]
    dva = vt1.shape[2]
    dv = dva - FLASH_ONES_ROWS
    scale = dqk ** -0.5
    c = scale * LOG2E
    th = T // FLASH_SPLIT

    def body(qt_ref, k_ref, vt_ref, ot_ref, lse_ref, sa_ref, sb_ref):
        i = pl.program_id(1)

        def scores(j):
            kb = k_ref[0, pl.ds(pl.multiple_of(j * T, T), T), :]
            return lax.dot_general(kb, qt_ref[0, 0], _NN, preferred_element_type=F32)

        def softmax_pv(s_ref, j, carry, masked):
            m, acc = carry
            raw = s_ref[...]
            if masked:
                raw = jnp.where(_causal_mask(T), raw, NEG)
            m_new = jnp.maximum(m, jnp.max(raw, axis=0, keepdims=True))
            alpha = jnp.exp2((m - m_new) * c)
            pb = jnp.exp2((raw - m_new) * c).astype(MXU_DT)
            acc = acc * alpha + lax.dot_general(vt_ref[0, j], pb, _NN, preferred_element_type=F32)
            return m_new, acc

        def pair(p, carry):
            j = 2 * p
            sb_ref[...] = scores(j + 1)
            carry = softmax_pv(sa_ref, j, carry, False)
            sa_ref[...] = scores(j + 2)
            return softmax_pv(sb_ref, j + 1, carry, False)

        def even_tail(carry):
            return softmax_pv(sa_ref, i, carry, True)

        def odd_tail(carry):
            sb_ref[...] = scores(i)
            carry = softmax_pv(sa_ref, i - 1, carry, False)
            return softmax_pv(sb_ref, i, carry, True)

        sa_ref[...] = scores(0)
        carry = lax.fori_loop(0, i // 2, pair, (jnp.full((1, T), NEG, F32), jnp.zeros((dva, T), F32)))
        m, acc = lax.cond(i % 2 == 0, even_tail, odd_tail, carry)
        l = acc[dv:dv + 1]
        ot_ref[0, 0] = acc[:dv] / l
        lse_ref[0, 0] = m * scale + jnp.log(l)

    return _pcall(
        body, name="flash_fwd", grid=(H, nq),
        out_shape=(jax.ShapeDtypeStruct((H, nq, dv, T), F32), jax.ShapeDtypeStruct((H, nq, 1, T), F32)),
        in_specs=[pl.BlockSpec((1, 1, dqk, T), lambda h, i: (h, i, 0, 0)),
                  pl.BlockSpec((1, S, dqk), lambda h, i: (h, 0, 0)),
                  pl.BlockSpec((1, nq, dva, T), lambda h, i: (h, 0, 0, 0))],
        out_specs=(pl.BlockSpec((1, 1, dv, T), lambda h, i: (h, i, 0, 0)),
                   pl.BlockSpec((1, 1, 1, T), lambda h, i: (h, i, 0, 0))),
        scratch_shapes=[pltpu.VMEM((T, T), F32), pltpu.VMEM((T, T), F32)],
        compiler_params=_cparams("parallel", "parallel"),
    )(qt, k, vt1)


def _flash_delta(ot, dot):
    H, nq, dv, T = ot.shape

    def body(o_ref, do_ref, d_ref):
        d_ref[0, 0] = jnp.sum(o_ref[0, 0] * do_ref[0, 0], axis=0, keepdims=True)

    spec = pl.BlockSpec((1, 1, dv, T), lambda h, i: (h, i, 0, 0))
    return _pcall(
        body, name="flash_delta", grid=(H, nq),
        out_shape=jax.ShapeDtypeStruct((H, nq, 1, T), F32),
        in_specs=[spec, spec], out_specs=pl.BlockSpec((1, 1, 1, T), lambda h, i: (h, i, 0, 0)),
        compiler_params=_cparams("parallel", "parallel"),
    )(ot, dot)


def _flash_bwd(qt, k, kt, v, dot, lse, delta):
    H, nq, dqk, T = qt.shape
    dv_ = v.shape[2]
    scale = dqk ** -0.5
    c = scale * LOG2E
    th = T // FLASH_SPLIT

    def body(qt_ref, k_ref, kt_ref, v_ref, dot_ref, lse_ref, del_ref, dqt_ref, dkt_ref, dvt_ref,
             sa_ref, pa_ref, sb_ref, pb_ref):
        j = pl.program_id(1)

        @pl.when(j == 0)
        def _():
            dqt_ref[...] = jnp.zeros_like(dqt_ref)

        n_un = nq - 1 - j

        def issue(i, s_ref, dp_ref):
            s_ref[...] = lax.dot_general(k_ref[0], qt_ref[0, i], _NN, preferred_element_type=F32)
            dp_ref[...] = lax.dot_general(v_ref[0], dot_ref[0, i].astype(MXU_DT), _NN, preferred_element_type=F32)

        def consume(i, s_ref, dp_ref, carry, masked):
            dkt, dvt = carry
            raw = s_ref[...]
            if masked:
                raw = jnp.where(_causal_mask(T), raw, NEG)
            p = jnp.exp2(raw * c - lse_ref[0, i] * LOG2E)
            dsb = (p * (dp_ref[...] - del_ref[0, i])).astype(MXU_DT)
            dvt = dvt + lax.dot_general(dot_ref[0, i].astype(MXU_DT), p.astype(MXU_DT), _NT, preferred_element_type=F32)
            dkt = dkt + lax.dot_general(qt_ref[0, i], dsb, _NT, preferred_element_type=F32)
            dqt_ref[0, i] += lax.dot_general(kt_ref[0, 0], dsb, _NN, preferred_element_type=F32) * scale
            return dkt, dvt

        def pair(p, carry):
            i0 = j + 1 + 2 * p
            issue(i0 + 1, sb_ref, pb_ref)
            carry = consume(i0, sa_ref, pa_ref, carry, False)
            issue(jnp.where(2 * p + 2 < n_un, i0 + 2, j), sa_ref, pa_ref)
            return consume(i0 + 1, sb_ref, pb_ref, carry, False)

        def even_tail(carry):
            return consume(j, sa_ref, pa_ref, carry, True)

        def odd_tail(carry):
            issue(j, sb_ref, pb_ref)
            carry = consume(nq - 1, sa_ref, pa_ref, carry, False)
            return consume(j, sb_ref, pb_ref, carry, True)

        issue(jnp.where(n_un > 0, j + 1, j), sa_ref, pa_ref)
        carry = lax.fori_loop(0, n_un // 2, pair, (jnp.zeros((dqk, T), F32), jnp.zeros((dv_, T), F32)))
        dkt, dvt = lax.cond(n_un % 2 == 0, even_tail, odd_tail, carry)
        dkt_ref[0, 0] = dkt * scale
        dvt_ref[0, 0] = dvt

    whole = lambda d: pl.BlockSpec((1, nq, d, T), lambda h, j: (h, 0, 0, 0))
    tile_t = lambda d: pl.BlockSpec((1, 1, d, T), lambda h, j: (h, j, 0, 0))
    return _pcall(
        body, name="flash_bwd", grid=(H, nq),
        out_shape=(jax.ShapeDtypeStruct((H, nq, dqk, T), F32), jax.ShapeDtypeStruct((H, nq, dqk, T), F32),
                   jax.ShapeDtypeStruct((H, nq, dv_, T), F32)),
        in_specs=[whole(dqk),
                  pl.BlockSpec((1, T, dqk), lambda h, j: (h, j, 0)),
                  tile_t(dqk),
                  pl.BlockSpec((1, T, dv_), lambda h, j: (h, j, 0)),
                  whole(dv_), whole(1), whole(1)],
        out_specs=(whole(dqk), tile_t(dqk), tile_t(dv_)),
        scratch_shapes=[pltpu.VMEM((T, T), F32) for _ in range(4)],
        compiler_params=_cparams("arbitrary", "arbitrary"),
    )(qt, k, kt, v, dot, lse, delta)


def _adamw(parts, w, m, v, *, name, tr=512):
    P, R, C = parts.shape
    tr = min(tr, R)
    assert R % tr == 0

    def body(p_ref, w_ref, m_ref, v_ref, g_ref, d_ref, m2_ref, v2_ref):
        g = p_ref[0].astype(F32)
        for s in range(1, P):
            g = g + p_ref[s].astype(F32)
        m2 = ADAM_B1 * m_ref[...] + (1.0 - ADAM_B1) * g
        v2 = ADAM_B2 * v_ref[...] + (1.0 - ADAM_B2) * jnp.square(g)
        m_hat = m2 / (1.0 - ADAM_B1 ** ADAM_STEP)
        v_hat = v2 / (1.0 - ADAM_B2 ** ADAM_STEP)
        g_ref[...] = g
        d_ref[...] = -ADAM_LR * (m_hat / (jnp.sqrt(v_hat) + ADAM_EPS) + ADAM_WD * w_ref[...])
        m2_ref[...] = m2
        v2_ref[...] = v2

    row = pl.BlockSpec((tr, C), lambda i: (i, 0))
    out = jax.ShapeDtypeStruct((R, C), F32)
    return _pcall(
        body, name=name, grid=(R // tr,),
        out_shape=(out, out, out, out),
        in_specs=[pl.BlockSpec((P, tr, C), lambda i: (0, i, 0)), row, row, row],
        out_specs=(row, row, row, row),
        compiler_params=_cparams("parallel"),
    )(parts, w, m, v)


def _bias_tables():
    i = np.arange(BLOCK)[:, None]
    j = np.arange(2 * BLOCK)[None, :]
    dist = i + BLOCK - j
    out = []
    for dil, max_dist in [(1, A_WINDOW - 1)] + [(d, w // d) for w, d in B_BRANCHES]:
        n = np.maximum(dist, 0) * dil
        max_exact = NUM_BUCKETS // 2
        nf = np.maximum(n, 1).astype(np.float64)
        val = np.log(nf / max_exact) / math.log(MAX_DISTANCE / max_exact) * (NUM_BUCKETS - max_exact)
        inband = (dist >= 0) & (dist <= max_dist)
        frac = np.abs(val - np.round(val))
        last = NUM_BUCKETS - 1 - max_exact
        assert np.all((frac > 2e-5) | (n <= max_exact) | (val >= last) | ~inband)
        large = max_exact + val.astype(np.int64)
        bucket = np.where(n < max_exact, n, np.minimum(large, NUM_BUCKETS - 1))
        onehot = (bucket[..., None] == np.arange(NUM_BUCKETS)).astype(np.float32)
        out.append((onehot.reshape(-1, NUM_BUCKETS), inband))
    return out


def _make_bias(rel_bias):
    tabs = _bias_tables()
    groups = [(0, A_Q_HEADS)] + [(A_Q_HEADS + g * B_HPB, B_HPB) for g in range(len(B_BRANCHES))]
    parts = []
    for (onehot, inband), (h0, nh) in zip(tabs, groups):
        b = jnp.dot(jnp.asarray(onehot), rel_bias[:, h0:h0 + nh], precision=lax.Precision.HIGHEST)
        b = b.reshape(BLOCK, 2 * BLOCK, nh)
        b = jnp.where(jnp.asarray(inband)[..., None], b, NEG)
        parts.append(b.transpose(2, 0, 1))
    return jnp.concatenate(parts, axis=0)


A_W = A_Q_HEADS * HEAD_DIM
B_W = B_HPB * HEAD_DIM


def _perm_rows(x, d):
    return x if d == 1 else x.reshape(x.shape[0] // d, d, -1).transpose(1, 0, 2).reshape(x.shape)


def _unperm_rows(x, d):
    return x if d == 1 else x.reshape(d, x.shape[0] // d, -1).transpose(1, 0, 2).reshape(x.shape)


def _even_post(o_all, lse):
    S = o_all.shape[0]
    outs, lses = [], []
    for g, (_, d) in enumerate(B_BRANCHES):
        w0 = A_W + g * B_W
        outs.append(_unperm_rows(o_all[:, w0:w0 + B_W], d))
        lg = lse[A_PAIRS + 2 * g:A_PAIRS + 2 * g + 2].transpose(1, 2, 4, 0, 3).reshape(S, B_HPB)
        lses.append(_unperm_rows(lg, d))
    wts = jax.nn.softmax(jnp.stack(lses), axis=0)
    widen = jnp.asarray(np.kron(np.eye(B_HPB), np.ones((1, HEAD_DIM))), F32)
    out_b = sum(jnp.dot(wts[g], widen, precision=lax.Precision.HIGHEST) * outs[g] for g in range(len(B_BRANCHES)))
    return jnp.concatenate([o_all[:, :A_W], out_b], axis=-1)


def _rope(t):
    S, r = t.shape[1], t.shape[-1]
    inv = ROPE_THETA ** (-jnp.arange(0, r, 2, dtype=jnp.float32) / r)
    ang = jnp.arange(S, dtype=jnp.float32)[:, None] * inv[None, :]
    shape = (1, S) + (1,) * (t.ndim - 3) + (r // 2,)
    cos, sin = jnp.cos(ang).reshape(shape), jnp.sin(ang).reshape(shape)
    t1, t2 = t[..., :r // 2], t[..., r // 2:]
    return jnp.concatenate([t1 * cos - t2 * sin, t1 * sin + t2 * cos], axis=-1)


def _mla_pre(q_lin, kv_lin, kr_raw):
    S = q_lin.shape[0]
    q = q_lin.reshape(1, S, C_HEADS, C_QK)
    qf = jnp.concatenate([q[..., :C_NOPE], _rope(q[..., C_NOPE:])], axis=-1)[0]
    kv = kv_lin.reshape(S, C_HEADS, C_NOPE + C_V)
    kr = _rope(kr_raw[None])[0]
    kf = jnp.concatenate([kv[..., :C_NOPE], jnp.broadcast_to(kr[:, None, :], (S, C_HEADS, C_ROPE))], axis=-1)
    return qf, kf, kv[..., C_NOPE:]


def _to_tiles_t(t, T):
    S, H, d = t.shape
    return t.reshape(S // T, T, H, d).transpose(2, 0, 3, 1)


def _from_tiles_t(t):
    H, n, d, T = t.shape
    return t.transpose(1, 3, 0, 2).reshape(n * T, H, d)


_BIG = (("w_in_ab", 2), ("w_out_ab", 2), ("w_down_c", 1), ("w_uq_c", 2), ("w_ukv_c", 2), ("w_o_c", 2),
        ("w_mlp_up", 2), ("w_mlp_down", 1))
_ROW_ALIGN = 512


def _pack_rows(arrs):
    rows = [a.reshape(-1, 128) for a in arrs]
    n = sum(r.shape[0] for r in rows)
    pad = (-n) % _ROW_ALIGN
    if pad:
        rows.append(jnp.zeros((pad, 128), rows[0].dtype))
    return jnp.concatenate(rows, axis=0)


def _pack_rows_per_device(arrs):
    rows = [a.reshape(N_DEV, -1, 128) for a in arrs]
    n = sum(r.shape[1] for r in rows)
    pad = (-n) % _ROW_ALIGN
    if pad:
        rows.append(jnp.zeros((N_DEV, pad, 128), rows[0].dtype))
    return jnp.concatenate(rows, axis=1)


def _unpack_rows(buf, shapes):
    out, r0 = [], 0
    for shp in shapes:
        n = math.prod(shp) // 128
        out.append(buf[..., r0:r0 + n, :].reshape(buf.shape[:-2] + tuple(shp)))
        r0 += n
    return out


def _layer_tensors(l):
    att = [("w_in_ab", l // 2), ("w_out_ab", l // 2)] if l % 2 == 0 else \
          [("w_down_c", l // 2), ("w_uq_c", l // 2), ("w_ukv_c", l // 2), ("w_o_c", l // 2)]
    return att + [("w_mlp_up", l), ("w_mlp_down", l)]


def _gathered_to_full(g, axis):
    if axis == 2:
        return g.transpose(1, 0, 2).reshape(g.shape[1], N_DEV * g.shape[2])
    return g.reshape(N_DEV * g.shape[1], g.shape[2])


def _full_to_shards(t, axis):
    a, b = t.shape
    if axis == 2:
        return t.reshape(a, N_DEV, b // N_DEV).transpose(1, 0, 2)
    return t.reshape(N_DEV, a // N_DEV, b)


def _pad_rows8(a):
    flat = a.reshape(-1)
    n = -(-flat.shape[0] // 1024) * 1024
    return jnp.pad(flat, (0, n - flat.shape[0])).reshape(-1, 128)


def _even_fwd(xn, h, w_in, w_out, bias, sinks_row, l):
    c0 = A_IN + 3 * B_W
    parts = [_matmul(xn, w_in[:, :c0], out_dtype=MXU_DT, tm=1024, tn=512, name=f"even_in_{l}")]
    for g, (_, d) in list(enumerate(B_BRANCHES))[1:]:
        cols = slice(A_IN + 3 * B_W * g, A_IN + 3 * B_W * (g + 1))
        parts.append(_matmul(_perm_rows(xn, d), w_in[:, cols], out_dtype=MXU_DT, tm=1024, tn=3 * B_W,
                             name=f"even_in_dil{d}_{l}"))
    proj = jnp.concatenate(parts, axis=1)
    o_all, lse = _banded_fwd(proj, bias, sinks_row)
    attn, post_vjp = jax.vjp(_even_post, o_all, lse)
    attn = attn.astype(MXU_DT)
    h1 = _matmul(attn, w_out, epi='add', extra=h, tm=1024, tn=512, name=f"even_out_{l}")
    return h1, (proj, attn, post_vjp)


def _even_bwd(dh, xn, ctx, w_in, w_out, bias, sinks_row, l):
    proj, attn, post_vjp = ctx
    d_attn = _matmul(dh, w_out, trans_b=True, tm=1024, tn=768, name=f"even_out_dx_{l}")
    g_w_out = _matmul_tn(attn, dh, tk=768, tn=512, name=f"even_out_dw_{l}")
    do_all, dlse = post_vjp(d_attn)
    dq, dk, dv, dkh, dvh, dbias_t, dsinks = _banded_bwd(proj, bias.transpose(0, 2, 1), sinks_row, do_all, dlse)
    dbias = dbias_t.transpose(0, 2, 1)
    dk = _halo_fold(dk, dkh, f"halo_k_{l}")
    dv = _halo_fold(dv, dvh, f"halo_v_{l}")

    def kv_sum(t):
        heads = [t[:, i * HEAD_DIM:(i + 1) * HEAD_DIM] for i in range(A_Q_HEADS)]
        return jnp.concatenate([sum(heads[j * A_GROUP:(j + 1) * A_GROUP]) for j in range(A_KV_HEADS)], axis=1)

    groups = [jnp.concatenate([dq[:, :A_W], kv_sum(dk), kv_sum(dv)], axis=1).astype(MXU_DT)]
    for g, (_, d) in enumerate(B_BRANCHES):
        cols = slice(A_W + g * B_W, A_W + (g + 1) * B_W)
        grp = jnp.concatenate([dq[:, cols], dk[:, cols], dv[:, cols]], axis=1).astype(MXU_DT)
        groups.append(_unperm_rows(grp, d))
    dproj = jnp.concatenate(groups, axis=1)
    g_w_in = _matmul_tn(xn, dproj, tk=512, tn=1024, name=f"even_in_dw_{l}")
    dxn = _matmul(dproj, w_in, trans_b=True, tm=1024, tn=512, name=f"even_in_dx_{l}")
    return dxn, g_w_in, g_w_out, dbias, dsinks[:A_Q_HEADS, 0, 0]


def _mla_fwd(xn, h, w_down, q_norm, w_uq, kv_norm, w_ukv, w_o, l):
    S = xn.shape[0]
    T = min(FLASH_T, S)
    down = _matmul(xn, w_down, tm=1024, tn=768, name=f"mla_down_{l}")
    c_q, c_kv, kr_raw = down[:, :C_Q_RANK], down[:, C_Q_RANK:C_Q_RANK + C_KV_RANK], down[:, C_Q_RANK + C_KV_RANK:C_DOWN]
    cqn = _rmsnorm(c_q, q_norm, out_dtype=MXU_DT, name=f"mla_qnorm_{l}")
    ckvn = _rmsnorm(c_kv, kv_norm, out_dtype=MXU_DT, name=f"mla_kvnorm_{l}")
    q_lin = _matmul(cqn, w_uq, tm=1024, tn=768, name=f"mla_uq_{l}")
    kv_lin = _matmul(ckvn, w_ukv, tm=1024, tn=1024, name=f"mla_ukv_{l}")
    (qf, kf, vf), pre_vjp = jax.vjp(_mla_pre, q_lin, kv_lin, kr_raw)
    qf, kf, vf = qf.astype(MXU_DT), kf.astype(MXU_DT), vf.astype(MXU_DT)
    qt, kt, vt = _to_tiles_t(qf, T), _to_tiles_t(kf, T), _to_tiles_t(vf, T)
    kn, vn = kf.transpose(1, 0, 2), vf.transpose(1, 0, 2)
    ones = jnp.concatenate([jnp.ones(vt.shape[:2] + (1, T), MXU_DT),
                            jnp.zeros(vt.shape[:2] + (FLASH_ONES_ROWS - 1, T), MXU_DT)], axis=2)
    ot, lse = _flash_fwd(qt, kn, jnp.concatenate([vt, ones], axis=2))
    attn = _from_tiles_t(ot).reshape(S, C_HEADS * C_V).astype(MXU_DT)
    h1 = _matmul(attn, w_o, epi='add', extra=h, tm=1024, tn=512, name=f"mla_o_{l}")
    return h1, (c_q, c_kv, cqn, ckvn, pre_vjp, qt, kn, kt, vn, ot, lse, attn)


def _mla_bwd(dh, xn, ctx, w_down, q_norm, w_uq, kv_norm, w_ukv, w_o, l):
    c_q, c_kv, cqn, ckvn, pre_vjp, qt, kn, kt, vn, ot, lse, attn = ctx
    S = xn.shape[0]
    T = qt.shape[-1]
    d_attn = _matmul(dh, w_o, trans_b=True, tm=1024, tn=512, name=f"mla_o_dx_{l}")
    g_w_o = _matmul_tn(attn, dh, tk=512, tn=512, name=f"mla_o_dw_{l}")
    dot = _to_tiles_t(d_attn.reshape(S, C_HEADS, C_V), T)
    delta = _flash_delta(ot, dot)
    dqt, dkt, dvt = _flash_bwd(qt, kn, kt, vn, dot, lse, delta)
    dq_lin, dkv_lin, dkr_raw = pre_vjp((_from_tiles_t(dqt), _from_tiles_t(dkt), _from_tiles_t(dvt)))
    g_w_uq = _matmul_tn(cqn, dq_lin, tk=384, tn=768, name=f"mla_uq_dw_{l}")
    g_w_ukv = _matmul_tn(ckvn, dkv_lin, tk=256, tn=1024, name=f"mla_ukv_dw_{l}")
    dcqn = _matmul(dq_lin, w_uq, trans_b=True, tm=1024, tn=384, name=f"mla_uq_dx_{l}")
    dckvn = _matmul(dkv_lin, w_ukv, trans_b=True, tm=1024, tn=256, name=f"mla_ukv_dx_{l}")
    dc_q, g_q_norm = _rmsnorm_bwd(c_q, q_norm, dcqn, None, name=f"mla_qnorm_bwd_{l}")
    dc_kv, g_kv_norm = _rmsnorm_bwd(c_kv, kv_norm, dckvn, None, name=f"mla_kvnorm_bwd_{l}")
    ddown = jnp.concatenate([dc_q, dc_kv, dkr_raw, jnp.zeros((S, C_DOWN_PAD - C_DOWN), F32)], axis=1).astype(MXU_DT)
    g_w_down = _matmul_tn(xn, ddown, tk=512, tn=768, name=f"mla_down_dw_{l}")[:, :C_DOWN]
    dxn = _matmul(ddown, w_down, trans_b=True, tm=1024, tn=512, name=f"mla_down_dx_{l}")
    return dxn, g_w_down, g_q_norm[0], g_w_uq, g_kv_norm[0], g_w_ukv, g_w_o


def kernel(x, rel_bias, attn_norm, mlp_norm, final_norm, w_in_ab, sinks, w_out_ab, w_down_c, q_norm_c, w_uq_c, kv_norm_c, w_ukv_c, w_o_c, w_mlp_up, w_mlp_down, loss_target, m_rel_bias, m_attn_norm, m_mlp_norm, m_final_norm, m_w_in_ab, m_sinks, m_w_out_ab, m_w_down_c, m_q_norm_c, m_w_uq_c, m_kv_norm_c, m_w_ukv_c, m_w_o_c, m_w_mlp_up, m_w_mlp_down, v_rel_bias, v_attn_norm, v_mlp_norm, v_final_norm, v_w_in_ab, v_sinks, v_w_out_ab, v_w_down_c, v_q_norm_c, v_w_uq_c, v_kv_norm_c, v_w_ukv_c, v_w_o_c, v_w_mlp_up, v_w_mlp_down):
    W = dict(w_in_ab=w_in_ab, w_out_ab=w_out_ab, w_down_c=w_down_c, w_uq_c=w_uq_c, w_ukv_c=w_ukv_c, w_o_c=w_o_c,
             w_mlp_up=w_mlp_up, w_mlp_down=w_mlp_down)
    Mo = dict(w_in_ab=m_w_in_ab, w_out_ab=m_w_out_ab, w_down_c=m_w_down_c, w_uq_c=m_w_uq_c, w_ukv_c=m_w_ukv_c,
              w_o_c=m_w_o_c, w_mlp_up=m_w_mlp_up, w_mlp_down=m_w_mlp_down)
    Vo = dict(w_in_ab=v_w_in_ab, w_out_ab=v_w_out_ab, w_down_c=v_w_down_c, w_uq_c=v_w_uq_c, w_ukv_c=v_w_ukv_c,
              w_o_c=v_w_o_c, w_mlp_up=v_w_mlp_up, w_mlp_down=v_w_mlp_down)
    S = x.shape[1]
    me = 4 * lax.axis_index("x") + 2 * lax.axis_index("y") + lax.axis_index("c")
    axis_of = dict(_BIG)

    def pack(src, l):
        return _pack_rows([src[n][i] for n, i in _layer_tensors(l)])

    def unpack_layer(gathered, l):
        shapes = [W[n].shape[1:] for n, _ in _layer_tensors(l)]
        return {n: _gathered_to_full(g, axis_of[n])
                for (n, _), g in zip(_layer_tensors(l), _unpack_rows(gathered, shapes))}

    w_packs = [pack(W, l) for l in range(DEPTH)]
    gathered = _exchange(w_packs[0].astype(MXU_DT), False, "gather_weights_0")
    gains = _exchange(jnp.concatenate([_pad_rows8(q_norm_c), _pad_rows8(kv_norm_c)], axis=0), False, "gather_gains")
    n_odd = q_norm_c.shape[0]
    q_norm_full = gains[:, 0].reshape(N_DEV, -1)[:, :q_norm_c.size].reshape(N_DEV, n_odd, -1).transpose(1, 0, 2).reshape(n_odd, C_Q_RANK)
    kv_norm_full = gains[:, 8].reshape(N_DEV, -1)[:, :kv_norm_c.size].reshape(N_DEV, n_odd, -1).transpose(1, 0, 2).reshape(n_odd, C_KV_RANK)

    bias, bias_vjp = jax.vjp(_make_bias, rel_bias)
    sink_rows = [jnp.broadcast_to(jnp.concatenate([sinks[e], jnp.full((B_HEADS,), NEG, F32)])[:, None, None],
                                  (N_BIAS_HEADS, 1, 128)) for e in range(sinks.shape[0])]

    h = x[0]
    saved = []
    for l in range(DEPTH):
        full = unpack_layer(gathered, l)
        if l % 2 == 1:
            full["w_down_c"] = jnp.pad(full["w_down_c"], ((0, 0), (0, C_DOWN_PAD - C_DOWN)))
        xn = _rmsnorm(h, attn_norm[l], out_dtype=MXU_DT, name=f"attn_norm_{l}")
        if l % 2 == 0:
            h1, ctx = _even_fwd(xn, h, full["w_in_ab"], full["w_out_ab"], bias, sink_rows[l // 2], l)
        else:
            o = l // 2
            h1, ctx = _mla_fwd(xn, h, full["w_down_c"], q_norm_full[o], full["w_uq_c"], kv_norm_full[o],
                               full["w_ukv_c"], full["w_o_c"], l)
        xn2 = _rmsnorm(h1, mlp_norm[l], out_dtype=MXU_DT, name=f"mlp_norm_{l}")
        if l + 1 < DEPTH:
            act, gathered = _matmul(xn2, full["w_mlp_up"], out_dtype=MXU_DT, epi='relu2', tm=1024, tn=512,
                                    name=f"mlp_up_{l}", ride=(w_packs[l + 1].astype(MXU_DT), False))
        else:
            act = _matmul(xn2, full["w_mlp_up"], out_dtype=MXU_DT, epi='relu2', tm=1024, tn=512, name=f"mlp_up_{l}")
        h2 = _matmul(act, full["w_mlp_down"], epi='add', extra=h1, tm=512, tn=512, name=f"mlp_down_{l}")
        saved.append((h, xn, h1, xn2, act, ctx, full))
        h = h2

    loss_row, dh, g_final = _loss_head(h, loss_target[0], final_norm)

    g_attn_norm, g_mlp_norm = [None] * DEPTH, [None] * DEPTH
    g_sinks, g_qn, g_kvn = [None] * sinks.shape[0], [None] * n_odd, [None] * n_odd
    dbias_total = None
    landed = [None] * DEPTH
    send = None
    for l in reversed(range(DEPTH)):
        h0, xn, h1, xn2, act, ctx, full = saved[l]
        G = {}
        if send is None:
            du = _matmul(dh, full["w_mlp_down"], trans_b=True, out_dtype=MXU_DT, epi='dsq', extra=act,
                         tm=1024, tn=512, name=f"mlp_down_dx_{l}")
        else:
            du, landed[l + 1] = _matmul(dh, full["w_mlp_down"], trans_b=True, out_dtype=MXU_DT, epi='dsq', extra=act,
                                        tm=1024, tn=512, name=f"mlp_down_dx_{l}", ride=(send, True))
        G["w_mlp_down"] = _matmul_tn(act, dh, tk=512, tn=1024, name=f"mlp_down_dw_{l}")
        G["w_mlp_up"] = _matmul_tn(xn2, du, tk=512, tn=1024, name=f"mlp_up_dw_{l}")
        dxn2 = _matmul(du, full["w_mlp_up"], trans_b=True, tm=512, tn=512, name=f"mlp_up_dx_{l}")
        dh, g = _rmsnorm_bwd(h1, mlp_norm[l], dxn2, dh, name=f"mlp_norm_bwd_{l}")
        g_mlp_norm[l] = g[0]
        if l % 2 == 0:
            e = l // 2
            dxn, G["w_in_ab"], G["w_out_ab"], dbias, g_sinks[e] = _even_bwd(
                dh, xn, ctx, full["w_in_ab"], full["w_out_ab"], bias, sink_rows[e], l)
            dbias_total = dbias if dbias_total is None else dbias_total + dbias
        else:
            o = l // 2
            dxn, G["w_down_c"], g_qn[o], G["w_uq_c"], g_kvn[o], G["w_ukv_c"], G["w_o_c"] = _mla_bwd(
                dh, xn, ctx, full["w_down_c"], q_norm_full[o], full["w_uq_c"], kv_norm_full[o],
                full["w_ukv_c"], full["w_o_c"], l)
        dh, g = _rmsnorm_bwd(h0, attn_norm[l], dxn, dh, name=f"attn_norm_bwd_{l}")
        g_attn_norm[l] = g[0]
        send = _pack_rows_per_device([_full_to_shards(G[n], axis_of[n]) for n, _ in _layer_tensors(l)]).astype(MXU_DT)
    landed[0] = _exchange(send, True, "scatter_grads_0")
    grad_x = dh[None]
    (g_rel_bias,) = bias_vjp(dbias_total)

    big = [{}, {}, {}, {}]
    for l in range(DEPTH):
        outs = _adamw(landed[l], w_packs[l], pack(Mo, l), pack(Vo, l), name=f"adamw_{l}")
        shapes = [W[n].shape[1:] for n, _ in _layer_tensors(l)]
        for kind, buf in enumerate(outs):
            for (n, _), t in zip(_layer_tensors(l), _unpack_rows(buf, shapes)):
                big[kind].setdefault(n, []).append(t)
    big_out = [{n: jnp.stack(ts) for n, ts in d.items()} for d in big]

    small_g = [g_rel_bias, jnp.stack(g_attn_norm), jnp.stack(g_mlp_norm), g_final[0], jnp.stack(g_sinks),
               jnp.stack(g_qn), jnp.stack(g_kvn), loss_row[0, :1]]
    small_w = [rel_bias, attn_norm, mlp_norm, final_norm, sinks, q_norm_c, kv_norm_c, jnp.zeros((1,), F32)]
    small_m = [m_rel_bias, m_attn_norm, m_mlp_norm, m_final_norm, m_sinks, m_q_norm_c, m_kv_norm_c, jnp.zeros((1,), F32)]
    small_v = [v_rel_bias, v_attn_norm, v_mlp_norm, v_final_norm, v_sinks, v_q_norm_c, v_kv_norm_c, jnp.ones((1,), F32)]
    offs = np.cumsum([0] + [-(-a.size // 1024) * 8 for a in small_g])
    partials = _exchange(jnp.concatenate([_pad_rows8(a) for a in small_g], axis=0), False, "gather_small_grads")

    def mine(i, a_full_shape, local):
        p = partials[:, offs[i]:offs[i + 1]].reshape(N_DEV, -1)[:, :math.prod(a_full_shape)]
        p = p.reshape((N_DEV,) + tuple(a_full_shape))
        if local.shape != tuple(a_full_shape):
            width = local.shape[-1]
            p = lax.dynamic_slice_in_dim(p, me * width, width, axis=p.ndim - 1)
        return jnp.stack([_pad_rows8(p[s]) for s in range(N_DEV)])

    parts_small = jnp.concatenate([mine(i, g.shape, w) for i, (g, w) in enumerate(zip(small_g, small_w))], axis=1)
    pk = lambda arrs: jnp.concatenate([_pad_rows8(a) for a in arrs], axis=0)
    small_out = _adamw(parts_small, pk(small_w), pk(small_m), pk(small_v), name="adamw_small", tr=parts_small.shape[1])
    offs2 = np.cumsum([0] + [-(-a.size // 1024) * 8 for a in small_w])

    def unpack_small(buf):
        return [buf[offs2[i]:offs2[i + 1]].reshape(-1)[:a.size].reshape(a.shape) for i, a in enumerate(small_w)]

    sg, sd, sm, sv = (unpack_small(b) for b in small_out)
    loss = sg[7][0]

    order = ['rel_bias', 'attn_norm', 'mlp_norm', 'final_norm', 'w_in_ab', 'sinks', 'w_out_ab', 'w_down_c', 'q_norm_c',
             'w_uq_c', 'kv_norm_c', 'w_ukv_c', 'w_o_c', 'w_mlp_up', 'w_mlp_down']
    small_idx = {'rel_bias': 0, 'attn_norm': 1, 'mlp_norm': 2, 'final_norm': 3, 'sinks': 4, 'q_norm_c': 5, 'kv_norm_c': 6}

    def pick(kind):
        res = []
        for n in order:
            if n in small_idx:
                res.append((sg, sd, sm, sv)[kind][small_idx[n]])
            else:
                res.append(big_out[kind][n])
        return res

    return (loss, grad_x, *pick(0), *pick(1), *pick(2), *pick(3))
```

```python
import math

import numpy as np
import jax
import jax.numpy as jnp
from jax import lax
from jax.experimental import pallas as pl
from jax.experimental.pallas import tpu as pltpu

F32 = jnp.float32
MXU_DT = jnp.bfloat16

N_DEV = 8
D_MODEL = 1024
DEPTH = 4
HEAD_DIM = 64
BLOCK = 128
EPS = 1e-6
NEG = -1e30
A_Q_HEADS = 8
A_KV_HEADS = 2
A_GROUP = A_Q_HEADS // A_KV_HEADS
A_WINDOW = 128
B_BRANCHES = ((128, 1), (512, 4), (2048, 16))
B_HPB = 4
B_HEADS = len(B_BRANCHES) * B_HPB
NUM_BUCKETS = 32
MAX_DISTANCE = 2048
N_BIAS_HEADS = A_Q_HEADS + B_HEADS
N_BAND_KV = A_KV_HEADS + B_HEADS
A_IN = (A_Q_HEADS + 2 * A_KV_HEADS) * HEAD_DIM
C_HEADS = 8
C_NOPE = 64
C_ROPE = 32
C_QK = C_NOPE + C_ROPE
C_V = 64
C_Q_RANK = 384
C_KV_RANK = 256
C_DOWN = C_Q_RANK + C_KV_RANK + C_ROPE
C_DOWN_PAD = 768
ROPE_THETA = 10000.0
N_CHUNKS = 16
FLASH_T = 512

ADAM_LR = 0.001
ADAM_B1 = 0.9
ADAM_B2 = 0.999
ADAM_EPS = 1e-08
ADAM_WD = 0.01
ADAM_STEP = 10

V7X_VMEM_BYTES = 64 * 1024 * 1024
VMEM_LIMIT = V7X_VMEM_BYTES - 8 * 1024 * 1024


def _pcall(body, **kw):
    return pl.pallas_call(body, **kw)


def _cparams(*sem):
    return pltpu.CompilerParams(dimension_semantics=sem, vmem_limit_bytes=VMEM_LIMIT)


def _exchange(src, all_to_all, name):
    def body(src_ref, out_ref, send_sems, recv_sems, local_sem):
        copies = _exchange_copies(src_ref, out_ref, send_sems, recv_sems, local_sem, all_to_all)
        for cp in copies:
            cp.start()
        _exchange_wait(copies)

    return _pcall(
        body, name=name,
        out_shape=_exchange_out(src),
        in_specs=[pl.BlockSpec(memory_space=pl.ANY)],
        out_specs=pl.BlockSpec(memory_space=pl.ANY),
        scratch_shapes=_exchange_sems(),
    )(src)


def _exchange_out(src):
    return jax.ShapeDtypeStruct((N_DEV,) + src.shape[-2:], src.dtype)


def _exchange_sems():
    return [pltpu.SemaphoreType.DMA((N_DEV - 1,)), pltpu.SemaphoreType.DMA((N_DEV - 1,)), pltpu.SemaphoreType.DMA]


def _exchange_copies(src_ref, out_ref, send_sems, recv_sems, local_sem, all_to_all):
    x, y, c = lax.axis_index("x"), lax.axis_index("y"), lax.axis_index("c")
    me = 4 * x + 2 * y + c

    def piece(dev):
        return src_ref.at[dev] if all_to_all else src_ref

    copies = [pltpu.make_async_copy(piece(me), out_ref.at[me], local_sem)]
    for k in range(1, N_DEV):
        px = 1 - x if (k >> 2) & 1 else x
        py = 1 - y if (k >> 1) & 1 else y
        pc = 1 - c if k & 1 else c
        copies.append(pltpu.make_async_remote_copy(
            src_ref=piece(4 * px + 2 * py + pc), dst_ref=out_ref.at[me],
            send_sem=send_sems.at[k - 1], recv_sem=recv_sems.at[k - 1],
            device_id=(px, py, pc), device_id_type=pl.DeviceIdType.MESH))
    return copies


def _exchange_wait(copies):
    for cp in copies[1:]:
        cp.wait()
    copies[0].wait()


def _matmul(a, b, *, trans_b=False, out_dtype=F32, epi=None, extra=None, tm=512, tn=512, name, ride=None):
    M, K = a.shape
    N = b.shape[0] if trans_b else b.shape[1]
    tm, tn = min(tm, M), min(tn, N)
    assert M % tm == 0 and N % tn == 0 and (b.shape[1] if trans_b else b.shape[0]) == K
    dn = (((1,), (1,)), ((), ())) if trans_b else (((1,), (0,)), ((), ()))
    n_i, n_j = M // tm, N // tn
    n_in = 2 + (extra is not None)

    def body(*refs):
        a_ref, b_ref = refs[0], refs[1]
        o_ref = refs[n_in + (ride is not None)]
        if ride is not None:
            i, j = pl.program_id(0), pl.program_id(1)
            copies = _exchange_copies(refs[n_in], refs[n_in + 2], *refs[n_in + 3:], ride[1])

            @pl.when((i == 0) & (j == 0))
            def _():
                for cp in copies:
                    cp.start()

        acc = lax.dot_general(a_ref[...].astype(MXU_DT), b_ref[...].astype(MXU_DT), dn,
                              preferred_element_type=F32)
        if epi == 'relu2':
            r = jnp.maximum(acc, 0.0)
            acc = r * r
        elif epi == 'add':
            acc = acc + refs[2][...].astype(F32)
        elif epi == 'dsq':
            acc = acc * (2.0 * jnp.sqrt(refs[2][...].astype(F32)))
        o_ref[...] = acc.astype(out_dtype)

        if ride is not None:
            @pl.when((i == n_i - 1) & (j == n_j - 1))
            def _():
                _exchange_wait(copies)

    b_spec = pl.BlockSpec((tn, K), lambda i, j: (j, 0)) if trans_b else pl.BlockSpec((K, tn), lambda i, j: (0, j))
    in_specs = [pl.BlockSpec((tm, K), lambda i, j: (i, 0)), b_spec]
    args = [a, b]
    if extra is not None:
        in_specs.append(pl.BlockSpec((tm, tn), lambda i, j: (i, j)))
        args.append(extra)
    out_shape = jax.ShapeDtypeStruct((M, N), out_dtype)
    out_spec = pl.BlockSpec((tm, tn), lambda i, j: (i, j))
    if ride is None:
        return _pcall(
            body, name=name, grid=(n_i, n_j), out_shape=out_shape, in_specs=in_specs, out_specs=out_spec,
            compiler_params=_cparams("parallel", "parallel"),
        )(*args)
    return _pcall(
        body, name=name, grid=(n_i, n_j),
        out_shape=(out_shape, _exchange_out(ride[0])),
        in_specs=in_specs + [pl.BlockSpec(memory_space=pl.ANY)],
        out_specs=(out_spec, pl.BlockSpec(memory_space=pl.ANY)),
        scratch_shapes=_exchange_sems(),
        compiler_params=_cparams("arbitrary", "arbitrary"),
    )(*args, ride[0])


def _matmul_tn(a, b, *, tk=512, tn=512, tm=1024, name):
    M, Ka = a.shape
    N = b.shape[1]
    tk, tn, tm = min(tk, Ka), min(tn, N), min(tm, M)
    assert Ka % tk == 0 and N % tn == 0 and M % tm == 0 and b.shape[0] == M

    def body(a_ref, b_ref, o_ref):
        @pl.when(pl.program_id(2) == 0)
        def _():
            o_ref[...] = jnp.zeros_like(o_ref)

        o_ref[...] += lax.dot_general(a_ref[...].astype(MXU_DT), b_ref[...].astype(MXU_DT),
                                      (((0,), (0,)), ((), ())), preferred_element_type=F32)

    return _pcall(
        body, name=name, grid=(Ka // tk, N // tn, M // tm),
        out_shape=jax.ShapeDtypeStruct((Ka, N), F32),
        in_specs=[pl.BlockSpec((tm, tk), lambda i, j, r: (r, i)), pl.BlockSpec((tm, tn), lambda i, j, r: (r, j))],
        out_specs=pl.BlockSpec((tk, tn), lambda i, j, r: (i, j)),
        compiler_params=_cparams("parallel", "parallel", "arbitrary"),
    )(a, b)


def _rmsnorm(x, g, *, out_dtype, name, tr=512):
    S, D = x.shape
    tr = min(tr, S)

    def body(x_ref, g_ref, o_ref):
        xf = x_ref[...].astype(F32)
        r = lax.rsqrt(jnp.mean(xf * xf, axis=-1, keepdims=True) + EPS)
        o_ref[...] = (xf * r * g_ref[...]).astype(out_dtype)

    return _pcall(
        body, name=name, grid=(S // tr,),
        out_shape=jax.ShapeDtypeStruct((S, D), out_dtype),
        in_specs=[pl.BlockSpec((tr, D), lambda i: (i, 0)), pl.BlockSpec((1, D), lambda i: (0, 0))],
        out_specs=pl.BlockSpec((tr, D), lambda i: (i, 0)),
        compiler_params=_cparams("parallel"),
    )(x, g.reshape(1, D))


def _rmsnorm_bwd(x, g, dy, dres, *, name, tr=512):
    S, D = x.shape
    tr = min(tr, S)

    def body(*refs):
        if dres is None:
            x_ref, g_ref, dy_ref, dx_ref, dg_ref = refs
        else:
            x_ref, g_ref, dy_ref, dres_ref, dx_ref, dg_ref = refs

        @pl.when(pl.program_id(0) == 0)
        def _():
            dg_ref[...] = jnp.zeros_like(dg_ref)

        xf = x_ref[...].astype(F32)
        r = lax.rsqrt(jnp.mean(xf * xf, axis=-1, keepdims=True) + EPS)
        xhat = xf * r
        dyf = dy_ref[...].astype(F32)
        dg_ref[...] += jnp.sum(dyf * xhat, axis=0, keepdims=True)
        dyg = dyf * g_ref[...]
        dx = r * (dyg - xhat * jnp.mean(dyg * xhat, axis=-1, keepdims=True))
        if dres is not None:
            dx = dx + dres_ref[...]
        dx_ref[...] = dx

    row = pl.BlockSpec((tr, D), lambda i: (i, 0))
    vec = pl.BlockSpec((1, D), lambda i: (0, 0))
    args = [x, g.reshape(1, D), dy] + ([] if dres is None else [dres])
    return _pcall(
        body, name=name, grid=(S // tr,),
        out_shape=(jax.ShapeDtypeStruct((S, D), F32), jax.ShapeDtypeStruct((1, D), F32)),
        in_specs=[row, vec, row] + ([] if dres is None else [row]),
        out_specs=(row, vec),
        compiler_params=_cparams("arbitrary"),
    )(*args)


def _loss_head(h, t, g, *, tr=512):
    S, D = h.shape
    tr = min(tr, S)

    def body(h_ref, t_ref, g_ref, loss_ref, dh_ref, dg_ref):
        @pl.when(pl.program_id(0) == 0)
        def _():
            dg_ref[...] = jnp.zeros_like(dg_ref)
            loss_ref[...] = jnp.zeros_like(loss_ref)

        xf = h_ref[...]
        r = lax.rsqrt(jnp.mean(xf * xf, axis=-1, keepdims=True) + EPS)
        xhat = xf * r
        e = xhat * g_ref[...] - t_ref[...]
        part = 0.5 * jnp.sum(jnp.mean(e * e, axis=-1, keepdims=True), axis=0, keepdims=True)
        loss_ref[...] += jnp.broadcast_to(part, loss_ref.shape)
        dy = e * (1.0 / D)
        dg_ref[...] += jnp.sum(dy * xhat, axis=0, keepdims=True)
        dyg = dy * g_ref[...]
        dh_ref[...] = r * (dyg - xhat * jnp.mean(dyg * xhat, axis=-1, keepdims=True))

    row = pl.BlockSpec((tr, D), lambda i: (i, 0))
    vec = pl.BlockSpec((1, D), lambda i: (0, 0))
    return _pcall(
        body, name="loss_head", grid=(S // tr,),
        out_shape=(jax.ShapeDtypeStruct((1, 128), F32), jax.ShapeDtypeStruct((S, D), F32),
                   jax.ShapeDtypeStruct((1, D), F32)),
        in_specs=[row, row, vec],
        out_specs=(pl.BlockSpec((1, 128), lambda i: (0, 0)), row, vec),
        compiler_params=_cparams("arbitrary"),
    )(h, t, g.reshape(1, D))


N_PAIRS = N_BIAS_HEADS // 2
A_PAIRS = A_Q_HEADS // 2
PAIR_W = 2 * HEAD_DIM
assert PAIR_W == 128 and A_KV_HEADS * HEAD_DIM == PAIR_W and B_HPB * HEAD_DIM == 2 * PAIR_W


def _pair_period(p):
    return jnp.where(p < A_PAIRS + 2, 16, jnp.where(p < A_PAIRS + 4, 4, 1))


def _pair_cols(p):
    b = jnp.maximum(p - A_PAIRS, 0)
    base, pp = 6 + 6 * (b // 2), b % 2
    is_a = p < A_PAIRS
    return (jnp.where(is_a, p, base + pp), jnp.where(is_a, A_PAIRS, base + 2 + pp),
            jnp.where(is_a, A_PAIRS + 1, base + 4 + pp))


def _band_specs(S):
    ch = S // N_CHUNKS
    nb = ch // BLOCK
    col = lambda i: (lambda p, c: (c, _pair_cols(p)[i]))
    prev = lambda i: (lambda p, c: (jnp.maximum(c * nb - 1, 0), _pair_cols(p)[i]))
    qkv = [pl.BlockSpec((ch, PAIR_W), col(0)), pl.BlockSpec((ch, PAIR_W), col(1)), pl.BlockSpec((BLOCK, PAIR_W), prev(1)),
           pl.BlockSpec((ch, PAIR_W), col(2)), pl.BlockSpec((BLOCK, PAIR_W), prev(2))]
    out_spec = pl.BlockSpec((ch, PAIR_W), lambda p, c: (c, p))
    sink_spec = pl.BlockSpec((2, 1, 128), lambda p, c: (p, 0, 0))
    row_spec = pl.BlockSpec((1, 1, nb, 2, BLOCK), lambda p, c: (p, c, 0, 0, 0))
    return ch, nb, qkv, out_spec, sink_spec, row_spec


def _pair_kv(p, e, ref):
    half = jnp.where(p < A_PAIRS, p // (A_GROUP // 2), e)
    return jnp.where(half == 0, ref[:, :HEAD_DIM], ref[:, HEAD_DIM:])


def _eye():
    return lax.broadcasted_iota(jnp.int32, (BLOCK, BLOCK), 0) == lax.broadcasted_iota(jnp.int32, (BLOCK, BLOCK), 1)


_NT = (((1,), (1,)), ((), ()))
_NN = (((1,), (0,)), ((), ()))
_TN = (((0,), (0,)), ((), ()))


_B_NT = (((2,), (2,)), ((0,), (0,)))
_B_NN = (((2,), (1,)), ((0,), (0,)))


def _bdot(a, b, dn):
    return lax.dot_general(a, b, dn, preferred_element_type=F32)


def _with_prev(first, t3):
    return first[None] if t3.shape[0] == 1 else jnp.concatenate([first[None], t3[:-1]], axis=0)


def _mask_first(s_prev, prev_ok):
    s0 = jnp.where(prev_ok, s_prev[0], NEG)[None]
    return s0 if s_prev.shape[0] == 1 else jnp.concatenate([s0, s_prev[1:]], axis=0)


def _banded_fwd(proj, bias, sinks):
    S = proj.shape[0]
    dh = HEAD_DIM
    ch, nb, qkv, out_spec, sink_spec, row_spec = _band_specs(S)
    bias_spec = pl.BlockSpec((2, BLOCK, 2 * BLOCK), lambda p, c: (p, 0, 0))
    scale = HEAD_DIM ** -0.5

    def body(q_ref, k_ref, kp_ref, v_ref, vp_ref, b_ref, s_ref, o_ref, lse_ref):
        p, c = pl.program_id(0), pl.program_id(1)
        prev_ok = (c % _pair_period(p)) != 0
        for e in range(2):
            lanes = slice(e * dh, (e + 1) * dh)
            q3 = q_ref[:, lanes].reshape(nb, BLOCK, dh)
            k3, v3 = (_pair_kv(p, e, r).reshape(nb, BLOCK, dh) for r in (k_ref, v_ref))
            kp3, vp3 = _with_prev(_pair_kv(p, e, kp_ref), k3), _with_prev(_pair_kv(p, e, vp_ref), v3)
            sink = s_ref[e, :, :1]
            s_cur = _bdot(q3, k3, _B_NT) * scale + b_ref[e, :, BLOCK:][None]
            s_prev = _mask_first(_bdot(q3, kp3, _B_NT) * scale + b_ref[e, :, :BLOCK][None], prev_ok)
            m = jnp.maximum(jnp.max(s_cur, axis=-1, keepdims=True), jnp.max(s_prev, axis=-1, keepdims=True))
            m = jnp.maximum(m, sink)
            p_cur = jnp.exp(s_cur - m)
            p_prev = jnp.exp(s_prev - m)
            l = jnp.sum(p_cur, axis=-1, keepdims=True) + jnp.sum(p_prev, axis=-1, keepdims=True) + jnp.exp(sink - m)
            acc = _bdot(p_cur.astype(MXU_DT), v3, _B_NN) + _bdot(p_prev.astype(MXU_DT), vp3, _B_NN)
            o_ref[:, lanes] = (acc / l).reshape(ch, dh)
            lse = m + jnp.log(l)
            lse_ref[0, 0, :, e:e + 1, :] = jnp.sum(jnp.where(_eye()[None], lse, 0.0), axis=1, keepdims=True)

    return _pcall(
        body, name="banded_fwd", grid=(N_PAIRS, N_CHUNKS),
        out_shape=(jax.ShapeDtypeStruct((S, N_PAIRS * PAIR_W), F32),
                   jax.ShapeDtypeStruct((N_PAIRS, N_CHUNKS, nb, 2, BLOCK), F32)),
        in_specs=qkv + [bias_spec, sink_spec],
        out_specs=(out_spec, row_spec),
        compiler_params=_cparams("parallel", "parallel"),
    )(proj, proj, proj, proj, proj, bias, sinks)


def _banded_bwd(proj, bias_t, sinks, do, dlse):
    S = proj.shape[0]
    dh = HEAD_DIM
    ch, nb, qkv, out_spec, sink_spec, row_spec = _band_specs(S)
    bias_spec = pl.BlockSpec((2, 2 * BLOCK, BLOCK), lambda p, c: (p, 0, 0))
    scale = HEAD_DIM ** -0.5

    def body(q_ref, k_ref, kp_ref, v_ref, vp_ref, b_ref, s_ref, do_ref, dl_ref,
             dq_ref, dk_ref, dv_ref, dkh_ref, dvh_ref, db_ref, ds_ref):
        p, c = pl.program_id(0), pl.program_id(1)

        @pl.when(c == 0)
        def _():
            db_ref[...] = jnp.zeros_like(db_ref)
            ds_ref[...] = jnp.zeros_like(ds_ref)

        prev_ok = (c % _pair_period(p)) != 0
        for e in range(2):
            lanes = slice(e * dh, (e + 1) * dh)
            q3 = q_ref[:, lanes].reshape(nb, BLOCK, dh)
            k3, v3 = (_pair_kv(p, e, r).reshape(nb, BLOCK, dh) for r in (k_ref, v_ref))
            kp3, vp3 = _with_prev(_pair_kv(p, e, kp_ref), k3), _with_prev(_pair_kv(p, e, vp_ref), v3)
            do3 = do_ref[:, lanes].astype(MXU_DT).reshape(nb, BLOCK, dh)
            sink = s_ref[e, :, :1]
            s_cur = _bdot(k3, q3, _B_NT) * scale + b_ref[e, BLOCK:, :][None]
            s_prev = _mask_first(_bdot(kp3, q3, _B_NT) * scale + b_ref[e, :BLOCK, :][None], prev_ok)
            m = jnp.maximum(jnp.max(s_cur, axis=1, keepdims=True), jnp.max(s_prev, axis=1, keepdims=True))
            m = jnp.maximum(m, sink)
            p_cur = jnp.exp(s_cur - m)
            p_prev = jnp.exp(s_prev - m)
            p_sink = jnp.exp(sink - m)
            inv = 1.0 / (jnp.sum(p_cur, axis=1, keepdims=True) + jnp.sum(p_prev, axis=1, keepdims=True) + p_sink)
            p_cur, p_prev, p_sink = p_cur * inv, p_prev * inv, p_sink * inv
            dp_cur = _bdot(v3, do3, _B_NT)
            dp_prev = _bdot(vp3, do3, _B_NT)
            delta = jnp.sum(p_cur * dp_cur, axis=1, keepdims=True) + jnp.sum(p_prev * dp_prev, axis=1, keepdims=True)
            t = dl_ref[0, 0, :, e:e + 1, :] - delta
            ds_cur = p_cur * (dp_cur + t)
            ds_prev = p_prev * (dp_prev + t)
            dsink = jnp.sum(jnp.sum(p_sink * t, axis=0), axis=-1, keepdims=True)
            ds_ref[e] += jnp.broadcast_to(dsink, (1, 128))
            db_ref[e, :BLOCK, :] += jnp.sum(ds_prev, axis=0)
            db_ref[e, BLOCK:, :] += jnp.sum(ds_cur, axis=0)
            dsb_cur = (ds_cur * scale).astype(MXU_DT)
            dsb_prev = (ds_prev * scale).astype(MXU_DT)
            dk_prev = _bdot(dsb_prev, q3, _B_NN)
            dv_prev = _bdot(p_prev.astype(MXU_DT), do3, _B_NN)

            def shifted(t3):
                z = jnp.zeros((1, BLOCK, dh), F32)
                return z if nb == 1 else jnp.concatenate([t3[1:], z], axis=0)

            dk_ref[:, lanes] = (_bdot(dsb_cur, q3, _B_NN) + shifted(dk_prev)).reshape(ch, dh)
            dv_ref[:, lanes] = (_bdot(p_cur.astype(MXU_DT), do3, _B_NN) + shifted(dv_prev)).reshape(ch, dh)
            dkh_ref[0, 0, :, lanes] = dk_prev[0]
            dvh_ref[0, 0, :, lanes] = dv_prev[0]
            for b in range(nb):
                dq_ref[b * BLOCK:(b + 1) * BLOCK, lanes] = (
                    lax.dot_general(dsb_cur[b], k3[b], _TN, preferred_element_type=F32)
                    + lax.dot_general(dsb_prev[b], kp3[b], _TN, preferred_element_type=F32))

    halo_spec = pl.BlockSpec((1, 1, BLOCK, PAIR_W), lambda p, c: (p, c, 0, 0))
    big = jax.ShapeDtypeStruct((S, N_PAIRS * PAIR_W), F32)
    halo = jax.ShapeDtypeStruct((N_PAIRS, N_CHUNKS, BLOCK, PAIR_W), F32)
    return _pcall(
        body, name="banded_bwd", grid=(N_PAIRS, N_CHUNKS),
        out_shape=(big, big, big, halo, halo, jax.ShapeDtypeStruct(bias_t.shape, F32),
                   jax.ShapeDtypeStruct(sinks.shape, F32)),
        in_specs=qkv + [bias_spec, sink_spec, out_spec, row_spec],
        out_specs=(out_spec, out_spec, out_spec, halo_spec, halo_spec, bias_spec, sink_spec),
        compiler_params=_cparams("arbitrary", "arbitrary"),
    )(proj, proj, proj, proj, proj, bias_t, sinks, do, dlse)


def _halo_fold(t, halo, name):
    S, width = t.shape
    nb = S // N_CHUNKS // BLOCK

    def body(t_ref, h_ref, o_ref):
        o_ref[:N_CHUNKS - 1, 0] = t_ref[:N_CHUNKS - 1, 0] + h_ref[0, 1:]
        o_ref[N_CHUNKS - 1:, 0] = t_ref[N_CHUNKS - 1:, 0]

    blk = pl.BlockSpec((N_CHUNKS, 1, BLOCK, PAIR_W), lambda p: (0, nb - 1, 0, p))
    return _pcall(
        body, name=name, grid=(N_PAIRS,),
        out_shape=jax.ShapeDtypeStruct((N_CHUNKS, nb, BLOCK, width), t.dtype),
        in_specs=[blk, pl.BlockSpec((1, N_CHUNKS, BLOCK, PAIR_W), lambda p: (p, 0, 0, 0))],
        out_specs=blk, input_output_aliases={0: 0},
        compiler_params=_cparams("parallel"),
    )(t.reshape(N_CHUNKS, nb, BLOCK, width), halo).reshape(S, width)


def _causal_mask(T):
    return lax.broadcasted_iota(jnp.int32, (T, T), 0) <= lax.broadcasted_iota(jnp.int32, (T, T), 1)


LOG2E = math.log2(math.e)
FLASH_SPLIT = 2
FLASH_ONES_ROWS = 16


def _mla_q_proj(cqn, wq_t, cos_t, sin_t, *, name):
    S, R = cqn.shape
    H, dqk, _ = wq_t.shape
    nq, half, T = cos_t.shape

    def body(x_ref, w_ref, c_ref, s_ref, o_ref):
        qt = lax.dot_general(w_ref[0], x_ref[...], _NT, preferred_element_type=F32)
        t1, t2 = qt[C_NOPE:C_NOPE + half], qt[C_NOPE + half:]
        o_ref[0, 0, :C_NOPE] = qt[:C_NOPE].astype(MXU_DT)
        o_ref[0, 0, C_NOPE:C_NOPE + half] = (t1 * c_ref[0] - t2 * s_ref[0]).astype(MXU_DT)
        o_ref[0, 0, C_NOPE + half:] = (t1 * s_ref[0] + t2 * c_ref[0]).astype(MXU_DT)

    tab = pl.BlockSpec((1, half, T), lambda i, h: (i, 0, 0))
    return _pcall(
        body, name=name, grid=(nq, H),
        out_shape=jax.ShapeDtypeStruct((H, nq, dqk, T), MXU_DT),
        in_specs=[pl.BlockSpec((T, R), lambda i, h: (i, 0)), pl.BlockSpec((1, dqk, R), lambda i, h: (h, 0, 0)), tab, tab],
        out_specs=pl.BlockSpec((1, 1, dqk, T), lambda i, h: (h, i, 0, 0)),
        compiler_params=_cparams("parallel", "parallel"),
    )(cqn, wq_t, cos_t, sin_t)


def _mla_kv_proj(ckvn, wk, wv, kr, kr_t, *, name):
    S, R = ckvn.shape
    H = wk.shape[0]
    nq, dr, T = kr_t.shape
    dqk = C_NOPE + dr
    wk_t, wv_t = wk.transpose(0, 2, 1), wv.transpose(0, 2, 1)

    def body(x_ref, wk_ref, wv_ref, wkt_ref, wvt_ref, kr_ref, krt_ref, kt_ref, vt_ref, kn_ref, vn_ref):
        x = x_ref[...]
        kt_ref[0, 0, :C_NOPE] = lax.dot_general(wkt_ref[0], x, _NT, preferred_element_type=F32).astype(MXU_DT)
        kt_ref[0, 0, C_NOPE:] = krt_ref[0].astype(MXU_DT)
        vt_ref[0, 0, :C_V] = lax.dot_general(wvt_ref[0], x, _NT, preferred_element_type=F32).astype(MXU_DT)
        row = lax.broadcasted_iota(jnp.int32, (FLASH_ONES_ROWS, T), 0)
        vt_ref[0, 0, C_V:] = jnp.where(row == 0, 1.0, 0.0).astype(MXU_DT)
        kn_ref[0, :, :C_NOPE] = lax.dot_general(x, wk_ref[0], _NN, preferred_element_type=F32).astype(MXU_DT)
        kn_ref[0, :, C_NOPE:] = kr_ref[...].astype(MXU_DT)
        vn_ref[0] = lax.dot_general(x, wv_ref[0], _NN, preferred_element_type=F32).astype(MXU_DT)

    w_spec = pl.BlockSpec((1, R, C_NOPE), lambda i, h: (h, 0, 0))
    wt_spec = pl.BlockSpec((1, C_NOPE, R), lambda i, h: (h, 0, 0))
    return _pcall(
        body, name=name, grid=(nq, H),
        out_shape=(jax.ShapeDtypeStruct((H, nq, dqk, T), MXU_DT),
                   jax.ShapeDtypeStruct((H, nq, C_V + FLASH_ONES_ROWS, T), MXU_DT),
                   jax.ShapeDtypeStruct((H, S, dqk), MXU_DT), jax.ShapeDtypeStruct((H, S, C_V), MXU_DT)),
        in_specs=[pl.BlockSpec((T, R), lambda i, h: (i, 0)), w_spec, w_spec, wt_spec, wt_spec,
                  pl.BlockSpec((T, dr), lambda i, h: (i, 0)), pl.BlockSpec((1, dr, T), lambda i, h: (i, 0, 0))],
        out_specs=(pl.BlockSpec((1, 1, dqk, T), lambda i, h: (h, i, 0, 0)),
                   pl.BlockSpec((1, 1, C_V + FLASH_ONES_ROWS, T), lambda i, h: (h, i, 0, 0)),
                   pl.BlockSpec((1, T, dqk), lambda i, h: (h, i, 0)), pl.BlockSpec((1, T, C_V), lambda i, h: (h, i, 0))),
        compiler_params=_cparams("parallel", "parallel"),
    )(ckvn, wk, wv, wk_t, wv_t, kr, kr_t)


def _flash_fwd(qt, k, vt1):
    H, nq, dqk, T = qt.shape
    S = k.shape[1]
    dva = vt1.shape[2]
    dv = dva - FLASH_ONES_ROWS
    scale = dqk ** -0.5
    c = scale * LOG2E
    th = T // FLASH_SPLIT

    def body(qt_ref, k_ref, vt_ref, ot_ref, lse_ref, sa_ref, sb_ref):
        i = pl.program_id(1)

        def scores(j):
            kb = k_ref[0, pl.ds(pl.multiple_of(j * T, T), T), :]
            return lax.dot_general(kb, qt_ref[0, 0], _NN, preferred_element_type=F32)

        def softmax_pv(s_ref, j, carry, masked):
            m, acc = carry
            raw = s_ref[...]
            if masked:
                raw = jnp.where(_causal_mask(T), raw, NEG)
            m_new = jnp.maximum(m, jnp.max(raw, axis=0, keepdims=True))
            alpha = jnp.exp2((m - m_new) * c)
            pb = jnp.exp2((raw - m_new) * c).astype(MXU_DT)
            acc = acc * alpha + lax.dot_general(vt_ref[0, j], pb, _NN, preferred_element_type=F32)
            return m_new, acc

        def pair(p, carry):
            j = 2 * p
            sb_ref[...] = scores(j + 1)
            carry = softmax_pv(sa_ref, j, carry, False)
            sa_ref[...] = scores(j + 2)
            return softmax_pv(sb_ref, j + 1, carry, False)

        def even_tail(carry):
            return softmax_pv(sa_ref, i, carry, True)

        def odd_tail(carry):
            sb_ref[...] = scores(i)
            carry = softmax_pv(sa_ref, i - 1, carry, False)
            return softmax_pv(sb_ref, i, carry, True)

        sa_ref[...] = scores(0)
        carry = lax.fori_loop(0, i // 2, pair, (jnp.full((1, T), NEG, F32), jnp.zeros((dva, T), F32)))
        m, acc = lax.cond(i % 2 == 0, even_tail, odd_tail, carry)
        l = acc[dv:dv + 1]
        ot_ref[0, 0] = acc[:dv] / l
        lse_ref[0, 0] = m * scale + jnp.log(l)

    return _pcall(
        body, name="flash_fwd", grid=(H, nq),
        out_shape=(jax.ShapeDtypeStruct((H, nq, dv, T), F32), jax.ShapeDtypeStruct((H, nq, 1, T), F32)),
        in_specs=[pl.BlockSpec((1, 1, dqk, T), lambda h, i: (h, i, 0, 0)),
                  pl.BlockSpec((1, S, dqk), lambda h, i: (h, 0, 0)),
                  pl.BlockSpec((1, nq, dva, T), lambda h, i: (h, 0, 0, 0))],
        out_specs=(pl.BlockSpec((1, 1, dv, T), lambda h, i: (h, i, 0, 0)),
                   pl.BlockSpec((1, 1, 1, T), lambda h, i: (h, i, 0, 0))),
        scratch_shapes=[pltpu.VMEM((T, T), F32), pltpu.VMEM((T, T), F32)],
        compiler_params=_cparams("parallel", "parallel"),
    )(qt, k, vt1)


def _flash_delta(ot, dot):
    H, nq, dv, T = ot.shape

    def body(o_ref, do_ref, d_ref):
        d_ref[0, 0] = jnp.sum(o_ref[0, 0] * do_ref[0, 0], axis=0, keepdims=True)

    spec = pl.BlockSpec((1, 1, dv, T), lambda h, i: (h, i, 0, 0))
    return _pcall(
        body, name="flash_delta", grid=(H, nq),
        out_shape=jax.ShapeDtypeStruct((H, nq, 1, T), F32),
        in_specs=[spec, spec], out_specs=pl.BlockSpec((1, 1, 1, T), lambda h, i: (h, i, 0, 0)),
        compiler_params=_cparams("parallel", "parallel"),
    )(ot, dot)


def _flash_bwd(qt, k, kt, v, dot, lse, delta):
    H, nq, dqk, T = qt.shape
    dv_ = v.shape[2]
    scale = dqk ** -0.5
    c = scale * LOG2E
    th = T // FLASH_SPLIT

    def body(qt_ref, k_ref, kt_ref, v_ref, dot_ref, lse_ref, del_ref, dqt_ref, dkt_ref, dvt_ref,
             sa_ref, pa_ref, sb_ref, pb_ref):
        j = pl.program_id(1)

        @pl.when(j == 0)
        def _():
            dqt_ref[...] = jnp.zeros_like(dqt_ref)

        n_un = nq - 1 - j

        def issue(i, s_ref, dp_ref):
            s_ref[...] = lax.dot_general(k_ref[0], qt_ref[0, i], _NN, preferred_element_type=F32)
            dp_ref[...] = lax.dot_general(v_ref[0], dot_ref[0, i].astype(MXU_DT), _NN, preferred_element_type=F32)

        def consume(i, s_ref, dp_ref, carry, masked):
            dkt, dvt = carry
            raw = s_ref[...]
            if masked:
                raw = jnp.where(_causal_mask(T), raw, NEG)
            p = jnp.exp2(raw * c - lse_ref[0, i] * LOG2E)
            dsb = (p * (dp_ref[...] - del_ref[0, i])).astype(MXU_DT)
            dvt = dvt + lax.dot_general(dot_ref[0, i].astype(MXU_DT), p.astype(MXU_DT), _NT, preferred_element_type=F32)
            dkt = dkt + lax.dot_general(qt_ref[0, i], dsb, _NT, preferred_element_type=F32)
            dqt_ref[0, i] += lax.dot_general(kt_ref[0, 0], dsb, _NN, preferred_element_type=F32) * scale
            return dkt, dvt

        def pair(p, carry):
            i0 = j + 1 + 2 * p
            issue(i0 + 1, sb_ref, pb_ref)
            carry = consume(i0, sa_ref, pa_ref, carry, False)
            issue(jnp.where(2 * p + 2 < n_un, i0 + 2, j), sa_ref, pa_ref)
            return consume(i0 + 1, sb_ref, pb_ref, carry, False)

        def even_tail(carry):
            return consume(j, sa_ref, pa_ref, carry, True)

        def odd_tail(carry):
            issue(j, sb_ref, pb_ref)
            carry = consume(nq - 1, sa_ref, pa_ref, carry, False)
            return consume(j, sb_ref, pb_ref, carry, True)

        issue(jnp.where(n_un > 0, j + 1, j), sa_ref, pa_ref)
        carry = lax.fori_loop(0, n_un // 2, pair, (jnp.zeros((dqk, T), F32), jnp.zeros((dv_, T), F32)))
        dkt, dvt = lax.cond(n_un % 2 == 0, even_tail, odd_tail, carry)
        dkt_ref[0, 0] = dkt * scale
        dvt_ref[0, 0] = dvt

    whole = lambda d: pl.BlockSpec((1, nq, d, T), lambda h, j: (h, 0, 0, 0))
    tile_t = lambda d: pl.BlockSpec((1, 1, d, T), lambda h, j: (h, j, 0, 0))
    return _pcall(
        body, name="flash_bwd", grid=(H, nq),
        out_shape=(jax.ShapeDtypeStruct((H, nq, dqk, T), F32), jax.ShapeDtypeStruct((H, nq, dqk, T), F32),
                   jax.ShapeDtypeStruct((H, nq, dv_, T), F32)),
        in_specs=[whole(dqk),
                  pl.BlockSpec((1, T, dqk), lambda h, j: (h, j, 0)),
                  tile_t(dqk),
                  pl.BlockSpec((1, T, dv_), lambda h, j: (h, j, 0)),
                  whole(dv_), whole(1), whole(1)],
        out_specs=(whole(dqk), tile_t(dqk), tile_t(dv_)),
        scratch_shapes=[pltpu.VMEM((T, T), F32) for _ in range(4)],
        compiler_params=_cparams("arbitrary", "arbitrary"),
    )(qt, k, kt, v, dot, lse, delta)


def _adamw(parts, w, m, v, *, name, tr=512):
    P, R, C = parts.shape
    tr = min(tr, R)
    assert R % tr == 0

    def body(p_ref, w_ref, m_ref, v_ref, g_ref, d_ref, m2_ref, v2_ref):
        g = p_ref[0].astype(F32)
        for s in range(1, P):
            g = g + p_ref[s].astype(F32)
        m2 = ADAM_B1 * m_ref[...] + (1.0 - ADAM_B1) * g
        v2 = ADAM_B2 * v_ref[...] + (1.0 - ADAM_B2) * jnp.square(g)
        m_hat = m2 / (1.0 - ADAM_B1 ** ADAM_STEP)
        v_hat = v2 / (1.0 - ADAM_B2 ** ADAM_STEP)
        g_ref[...] = g
        d_ref[...] = -ADAM_LR * (m_hat / (jnp.sqrt(v_hat) + ADAM_EPS) + ADAM_WD * w_ref[...])
        m2_ref[...] = m2
        v2_ref[...] = v2

    row = pl.BlockSpec((tr, C), lambda i: (i, 0))
    out = jax.ShapeDtypeStruct((R, C), F32)
    return _pcall(
        body, name=name, grid=(R // tr,),
        out_shape=(out, out, out, out),
        in_specs=[pl.BlockSpec((P, tr, C), lambda i: (0, i, 0)), row, row, row],
        out_specs=(row, row, row, row),
        compiler_params=_cparams("parallel"),
    )(parts, w, m, v)


def _bias_tables():
    i = np.arange(BLOCK)[:, None]
    j = np.arange(2 * BLOCK)[None, :]
    dist = i + BLOCK - j
    out = []
    for dil, max_dist in [(1, A_WINDOW - 1)] + [(d, w // d) for w, d in B_BRANCHES]:
        n = np.maximum(dist, 0) * dil
        max_exact = NUM_BUCKETS // 2
        nf = np.maximum(n, 1).astype(np.float64)
        val = np.log(nf / max_exact) / math.log(MAX_DISTANCE / max_exact) * (NUM_BUCKETS - max_exact)
        inband = (dist >= 0) & (dist <= max_dist)
        frac = np.abs(val - np.round(val))
        last = NUM_BUCKETS - 1 - max_exact
        assert np.all((frac > 2e-5) | (n <= max_exact) | (val >= last) | ~inband)
        large = max_exact + val.astype(np.int64)
        bucket = np.where(n < max_exact, n, np.minimum(large, NUM_BUCKETS - 1))
        onehot = (bucket[..., None] == np.arange(NUM_BUCKETS)).astype(np.float32)
        out.append((onehot.reshape(-1, NUM_BUCKETS), inband))
    return out


def _make_bias(rel_bias):
    tabs = _bias_tables()
    groups = [(0, A_Q_HEADS)] + [(A_Q_HEADS + g * B_HPB, B_HPB) for g in range(len(B_BRANCHES))]
    parts = []
    for (onehot, inband), (h0, nh) in zip(tabs, groups):
        b = jnp.dot(jnp.asarray(onehot), rel_bias[:, h0:h0 + nh], precision=lax.Precision.HIGHEST)
        b = b.reshape(BLOCK, 2 * BLOCK, nh)
        b = jnp.where(jnp.asarray(inband)[..., None], b, NEG)
        parts.append(b.transpose(2, 0, 1))
    return jnp.concatenate(parts, axis=0)


A_W = A_Q_HEADS * HEAD_DIM
B_W = B_HPB * HEAD_DIM


def _perm_rows(x, d):
    return x if d == 1 else x.reshape(x.shape[0] // d, d, -1).transpose(1, 0, 2).reshape(x.shape)


def _unperm_rows(x, d):
    return x if d == 1 else x.reshape(d, x.shape[0] // d, -1).transpose(1, 0, 2).reshape(x.shape)


def _even_post(o_all, lse):
    S = o_all.shape[0]
    outs, lses = [], []
    for g, (_, d) in enumerate(B_BRANCHES):
        w0 = A_W + g * B_W
        outs.append(_unperm_rows(o_all[:, w0:w0 + B_W], d))
        lg = lse[A_PAIRS + 2 * g:A_PAIRS + 2 * g + 2].transpose(1, 2, 4, 0, 3).reshape(S, B_HPB)
        lses.append(_unperm_rows(lg, d))
    wts = jax.nn.softmax(jnp.stack(lses), axis=0)
    widen = jnp.asarray(np.kron(np.eye(B_HPB), np.ones((1, HEAD_DIM))), F32)
    out_b = sum(jnp.dot(wts[g], widen, precision=lax.Precision.HIGHEST) * outs[g] for g in range(len(B_BRANCHES)))
    return jnp.concatenate([o_all[:, :A_W], out_b], axis=-1)


def _rope_tables(S, r):
    inv = ROPE_THETA ** (-jnp.arange(0, r, 2, dtype=jnp.float32) / r)
    ang = jnp.arange(S, dtype=jnp.float32)[:, None] * inv[None, :]
    return jnp.cos(ang), jnp.sin(ang)


def _rope(t):
    S, r = t.shape[1], t.shape[-1]
    shape = (1, S) + (1,) * (t.ndim - 3) + (r // 2,)
    cos, sin = (a.reshape(shape) for a in _rope_tables(S, r))
    t1, t2 = t[..., :r // 2], t[..., r // 2:]
    return jnp.concatenate([t1 * cos - t2 * sin, t1 * sin + t2 * cos], axis=-1)


def _mla_pre(q_lin, kv_lin, kr_raw):
    S = q_lin.shape[0]
    q = q_lin.reshape(1, S, C_HEADS, C_QK)
    qf = jnp.concatenate([q[..., :C_NOPE], _rope(q[..., C_NOPE:])], axis=-1)[0]
    kv = kv_lin.reshape(S, C_HEADS, C_NOPE + C_V)
    kr = _rope(kr_raw[None])[0]
    kf = jnp.concatenate([kv[..., :C_NOPE], jnp.broadcast_to(kr[:, None, :], (S, C_HEADS, C_ROPE))], axis=-1)
    return qf, kf, kv[..., C_NOPE:]


def _to_tiles_t(t, T):
    S, H, d = t.shape
    return t.reshape(S // T, T, H, d).transpose(2, 0, 3, 1)


def _from_tiles_t(t):
    H, n, d, T = t.shape
    return t.transpose(1, 3, 0, 2).reshape(n * T, H, d)


_BIG = (("w_in_ab", 2), ("w_out_ab", 2), ("w_down_c", 1), ("w_uq_c", 2), ("w_ukv_c", 2), ("w_o_c", 2),
        ("w_mlp_up", 2), ("w_mlp_down", 1))
_ROW_ALIGN = 512


def _pack_rows(arrs):
    rows = [a.reshape(-1, 128) for a in arrs]
    n = sum(r.shape[0] for r in rows)
    pad = (-n) % _ROW_ALIGN
    if pad:
        rows.append(jnp.zeros((pad, 128), rows[0].dtype))
    return jnp.concatenate(rows, axis=0)


def _pack_rows_per_device(arrs):
    rows = [a.reshape(N_DEV, -1, 128) for a in arrs]
    n = sum(r.shape[1] for r in rows)
    pad = (-n) % _ROW_ALIGN
    if pad:
        rows.append(jnp.zeros((N_DEV, pad, 128), rows[0].dtype))
    return jnp.concatenate(rows, axis=1)


def _unpack_rows(buf, shapes):
    out, r0 = [], 0
    for shp in shapes:
        n = math.prod(shp) // 128
        out.append(buf[..., r0:r0 + n, :].reshape(buf.shape[:-2] + tuple(shp)))
        r0 += n
    return out


def _layer_tensors(l):
    att = [("w_in_ab", l // 2), ("w_out_ab", l // 2)] if l % 2 == 0 else \
          [("w_down_c", l // 2), ("w_uq_c", l // 2), ("w_ukv_c", l // 2), ("w_o_c", l // 2)]
    return att + [("w_mlp_up", l), ("w_mlp_down", l)]


def _exchange_groups():
    first = _layer_tensors(0)
    return [first[:-2], first[-2:]] + [_layer_tensors(l) for l in range(1, DEPTH)]


def _gathered_to_full(g, axis):
    if axis == 2:
        return g.transpose(1, 0, 2).reshape(g.shape[1], N_DEV * g.shape[2])
    return g.reshape(N_DEV * g.shape[1], g.shape[2])


def _full_to_shards(t, axis):
    a, b = t.shape
    if axis == 2:
        return t.reshape(a, N_DEV, b // N_DEV).transpose(1, 0, 2)
    return t.reshape(N_DEV, a // N_DEV, b)


def _pad_rows8(a):
    flat = a.reshape(-1)
    n = -(-flat.shape[0] // 1024) * 1024
    return jnp.pad(flat, (0, n - flat.shape[0])).reshape(-1, 128)


def _even_fwd(xn, h, w_in, w_out, bias, sinks_row, l, ride=None):
    c0 = A_IN + 3 * B_W
    first = _matmul(xn, w_in[:, :c0], out_dtype=MXU_DT, tm=1024, tn=512, name=f"even_in_{l}", ride=ride)
    first, landed = first if ride is not None else (first, None)
    parts = [first]
    for g, (_, d) in list(enumerate(B_BRANCHES))[1:]:
        cols = slice(A_IN + 3 * B_W * g, A_IN + 3 * B_W * (g + 1))
        parts.append(_matmul(_perm_rows(xn, d), w_in[:, cols], out_dtype=MXU_DT, tm=1024, tn=3 * B_W,
                             name=f"even_in_dil{d}_{l}"))
    proj = jnp.concatenate(parts, axis=1)
    o_all, lse = _banded_fwd(proj, bias, sinks_row)
    attn, post_vjp = jax.vjp(_even_post, o_all, lse)
    attn = attn.astype(MXU_DT)
    h1 = _matmul(attn, w_out, epi='add', extra=h, tm=1024, tn=512, name=f"even_out_{l}")
    return h1, (proj, attn, post_vjp), landed


def _even_bwd(dh, xn, ctx, w_in, w_out, bias, sinks_row, l, ride=None):
    proj, attn, post_vjp = ctx
    d_attn = _matmul(dh, w_out, trans_b=True, tm=1024, tn=768, name=f"even_out_dx_{l}")
    g_w_out = _matmul_tn(attn, dh, tk=768, tn=512, name=f"even_out_dw_{l}")
    do_all, dlse = post_vjp(d_attn)
    dq, dk, dv, dkh, dvh, dbias_t, dsinks = _banded_bwd(proj, bias.transpose(0, 2, 1), sinks_row, do_all, dlse)
    dbias = dbias_t.transpose(0, 2, 1)
    dk = _halo_fold(dk, dkh, f"halo_k_{l}")
    dv = _halo_fold(dv, dvh, f"halo_v_{l}")

    def kv_sum(t):
        heads = [t[:, i * HEAD_DIM:(i + 1) * HEAD_DIM] for i in range(A_Q_HEADS)]
        return jnp.concatenate([sum(heads[j * A_GROUP:(j + 1) * A_GROUP]) for j in range(A_KV_HEADS)], axis=1)

    groups = [jnp.concatenate([dq[:, :A_W], kv_sum(dk), kv_sum(dv)], axis=1).astype(MXU_DT)]
    for g, (_, d) in enumerate(B_BRANCHES):
        cols = slice(A_W + g * B_W, A_W + (g + 1) * B_W)
        grp = jnp.concatenate([dq[:, cols], dk[:, cols], dv[:, cols]], axis=1).astype(MXU_DT)
        groups.append(_unperm_rows(grp, d))
    dproj = jnp.concatenate(groups, axis=1)
    g_w_in = _matmul_tn(xn, dproj, tk=512, tn=1024, name=f"even_in_dw_{l}")
    dxn = _matmul(dproj, w_in, trans_b=True, tm=1024, tn=512, name=f"even_in_dx_{l}", ride=ride)
    dxn, landed = dxn if ride is not None else (dxn, None)
    return dxn, g_w_in, g_w_out, dbias, dsinks[:A_Q_HEADS, 0, 0], landed


def _mla_fwd(xn, h, w_down, q_norm, w_uq, kv_norm, w_ukv, w_o, l):
    S = xn.shape[0]
    T = min(FLASH_T, S)
    down = _matmul(xn, w_down, tm=1024, tn=768, name=f"mla_down_{l}")
    c_q, c_kv, kr_raw = down[:, :C_Q_RANK], down[:, C_Q_RANK:C_Q_RANK + C_KV_RANK], down[:, C_Q_RANK + C_KV_RANK:C_DOWN]
    cqn = _rmsnorm(c_q, q_norm, out_dtype=MXU_DT, name=f"mla_qnorm_{l}")
    ckvn = _rmsnorm(c_kv, kv_norm, out_dtype=MXU_DT, name=f"mla_kvnorm_{l}")
    cos, sin = _rope_tables(S, C_ROPE)
    to_t = lambda t: t.reshape(S // T, T, -1).transpose(0, 2, 1)
    qt = _mla_q_proj(cqn, w_uq.T.reshape(C_HEADS, C_QK, C_Q_RANK), to_t(cos), to_t(sin), name=f"mla_uq_{l}")
    w_kv = w_ukv.reshape(C_KV_RANK, C_HEADS, C_NOPE + C_V).transpose(1, 0, 2)
    kr = _rope(kr_raw[None])[0]
    kt, vt1, kn, vn = _mla_kv_proj(ckvn, w_kv[..., :C_NOPE], w_kv[..., C_NOPE:], kr, to_t(kr), name=f"mla_ukv_{l}")
    ot, lse = _flash_fwd(qt, kn, vt1)
    attn = _from_tiles_t(ot).reshape(S, C_HEADS * C_V).astype(MXU_DT)
    h1 = _matmul(attn, w_o, epi='add', extra=h, tm=1024, tn=512, name=f"mla_o_{l}")
    return h1, (c_q, c_kv, cqn, ckvn, qt, kn, kt, vn, ot, lse, attn)


def _mla_bwd(dh, xn, ctx, w_down, q_norm, w_uq, kv_norm, w_ukv, w_o, l):
    c_q, c_kv, cqn, ckvn, qt, kn, kt, vn, ot, lse, attn = ctx
    S = xn.shape[0]
    T = qt.shape[-1]
    _, pre_vjp = jax.vjp(_mla_pre, jnp.zeros((S, C_HEADS * C_QK), F32), jnp.zeros((S, C_HEADS * (C_NOPE + C_V)), F32),
                         jnp.zeros((S, C_ROPE), F32))
    d_attn = _matmul(dh, w_o, trans_b=True, tm=1024, tn=512, name=f"mla_o_dx_{l}")
    g_w_o = _matmul_tn(attn, dh, tk=512, tn=512, name=f"mla_o_dw_{l}")
    dot = _to_tiles_t(d_attn.reshape(S, C_HEADS, C_V), T)
    delta = _flash_delta(ot, dot)
    dqt, dkt, dvt = _flash_bwd(qt, kn, kt, vn, dot, lse, delta)
    dq_lin, dkv_lin, dkr_raw = pre_vjp((_from_tiles_t(dqt), _from_tiles_t(dkt), _from_tiles_t(dvt)))
    g_w_uq = _matmul_tn(cqn, dq_lin, tk=384, tn=768, name=f"mla_uq_dw_{l}")
    g_w_ukv = _matmul_tn(ckvn, dkv_lin, tk=256, tn=1024, name=f"mla_ukv_dw_{l}")
    dcqn = _matmul(dq_lin, w_uq, trans_b=True, tm=1024, tn=384, name=f"mla_uq_dx_{l}")
    dckvn = _matmul(dkv_lin, w_ukv, trans_b=True, tm=1024, tn=256, name=f"mla_ukv_dx_{l}")
    dc_q, g_q_norm = _rmsnorm_bwd(c_q, q_norm, dcqn, None, name=f"mla_qnorm_bwd_{l}")
    dc_kv, g_kv_norm = _rmsnorm_bwd(c_kv, kv_norm, dckvn, None, name=f"mla_kvnorm_bwd_{l}")
    ddown = jnp.concatenate([dc_q, dc_kv, dkr_raw, jnp.zeros((S, C_DOWN_PAD - C_DOWN), F32)], axis=1).astype(MXU_DT)
    g_w_down = _matmul_tn(xn, ddown, tk=512, tn=768, name=f"mla_down_dw_{l}")[:, :C_DOWN]
    dxn = _matmul(ddown, w_down, trans_b=True, tm=1024, tn=512, name=f"mla_down_dx_{l}")
    return dxn, g_w_down, g_q_norm[0], g_w_uq, g_kv_norm[0], g_w_ukv, g_w_o


def kernel(x, rel_bias, attn_norm, mlp_norm, final_norm, w_in_ab, sinks, w_out_ab, w_down_c, q_norm_c, w_uq_c, kv_norm_c, w_ukv_c, w_o_c, w_mlp_up, w_mlp_down, loss_target, m_rel_bias, m_attn_norm, m_mlp_norm, m_final_norm, m_w_in_ab, m_sinks, m_w_out_ab, m_w_down_c, m_q_norm_c, m_w_uq_c, m_kv_norm_c, m_w_ukv_c, m_w_o_c, m_w_mlp_up, m_w_mlp_down, v_rel_bias, v_attn_norm, v_mlp_norm, v_final_norm, v_w_in_ab, v_sinks, v_w_out_ab, v_w_down_c, v_q_norm_c, v_w_uq_c, v_kv_norm_c, v_w_ukv_c, v_w_o_c, v_w_mlp_up, v_w_mlp_down):
    W = dict(w_in_ab=w_in_ab, w_out_ab=w_out_ab, w_down_c=w_down_c, w_uq_c=w_uq_c, w_ukv_c=w_ukv_c, w_o_c=w_o_c,
             w_mlp_up=w_mlp_up, w_mlp_down=w_mlp_down)
    Mo = dict(w_in_ab=m_w_in_ab, w_out_ab=m_w_out_ab, w_down_c=m_w_down_c, w_uq_c=m_w_uq_c, w_ukv_c=m_w_ukv_c,
              w_o_c=m_w_o_c, w_mlp_up=m_w_mlp_up, w_mlp_down=m_w_mlp_down)
    Vo = dict(w_in_ab=v_w_in_ab, w_out_ab=v_w_out_ab, w_down_c=v_w_down_c, w_uq_c=v_w_uq_c, w_ukv_c=v_w_ukv_c,
              w_o_c=v_w_o_c, w_mlp_up=v_w_mlp_up, w_mlp_down=v_w_mlp_down)
    S = x.shape[1]
    me = 4 * lax.axis_index("x") + 2 * lax.axis_index("y") + lax.axis_index("c")
    axis_of = dict(_BIG)

    groups = _exchange_groups()

    def pack(src, gi):
        return _pack_rows([src[n][i] for n, i in groups[gi]])

    def unpack_group(gathered, gi):
        shapes = [W[n].shape[1:] for n, _ in groups[gi]]
        return {n: _gathered_to_full(g, axis_of[n])
                for (n, _), g in zip(groups[gi], _unpack_rows(gathered, shapes))}

    def send_of(G, gi):
        return _pack_rows_per_device([_full_to_shards(G[n], axis_of[n]) for n, _ in groups[gi]]).astype(MXU_DT)

    w_packs = [pack(W, gi) for gi in range(len(groups))]
    gathered = _exchange(w_packs[0].astype(MXU_DT), False, "gather_weights_0")
    gains = _exchange(jnp.concatenate([_pad_rows8(q_norm_c), _pad_rows8(kv_norm_c)], axis=0), False, "gather_gains")
    n_odd = q_norm_c.shape[0]
    q_norm_full = gains[:, 0].reshape(N_DEV, -1)[:, :q_norm_c.size].reshape(N_DEV, n_odd, -1).transpose(1, 0, 2).reshape(n_odd, C_Q_RANK)
    kv_norm_full = gains[:, 8].reshape(N_DEV, -1)[:, :kv_norm_c.size].reshape(N_DEV, n_odd, -1).transpose(1, 0, 2).reshape(n_odd, C_KV_RANK)

    bias, bias_vjp = jax.vjp(_make_bias, rel_bias)
    sink_rows = [jnp.broadcast_to(jnp.concatenate([sinks[e], jnp.full((B_HEADS,), NEG, F32)])[:, None, None],
                                  (N_BIAS_HEADS, 1, 128)) for e in range(sinks.shape[0])]

    h = x[0]
    saved = []
    for l in range(DEPTH):
        full = unpack_group(gathered, 0 if l == 0 else l + 1)
        if l % 2 == 1:
            full["w_down_c"] = jnp.pad(full["w_down_c"], ((0, 0), (0, C_DOWN_PAD - C_DOWN)))
        xn = _rmsnorm(h, attn_norm[l], out_dtype=MXU_DT, name=f"attn_norm_{l}")
        if l == 0:
            h1, ctx, gathered_mlp = _even_fwd(xn, h, full["w_in_ab"], full["w_out_ab"], bias, sink_rows[0], l,
                                              ride=(w_packs[1].astype(MXU_DT), False))
            full.update(unpack_group(gathered_mlp, 1))
        elif l % 2 == 0:
            h1, ctx, _ = _even_fwd(xn, h, full["w_in_ab"], full["w_out_ab"], bias, sink_rows[l // 2], l)
        else:
            o = l // 2
            h1, ctx = _mla_fwd(xn, h, full["w_down_c"], q_norm_full[o], full["w_uq_c"], kv_norm_full[o],
                               full["w_ukv_c"], full["w_o_c"], l)
        xn2 = _rmsnorm(h1, mlp_norm[l], out_dtype=MXU_DT, name=f"mlp_norm_{l}")
        if l + 1 < DEPTH:
            act, gathered = _matmul(xn2, full["w_mlp_up"], out_dtype=MXU_DT, epi='relu2', tm=1024, tn=512,
                                    name=f"mlp_up_{l}", ride=(w_packs[l + 2].astype(MXU_DT), False))
        else:
            act = _matmul(xn2, full["w_mlp_up"], out_dtype=MXU_DT, epi='relu2', tm=1024, tn=512, name=f"mlp_up_{l}")
        h2 = _matmul(act, full["w_mlp_down"], epi='add', extra=h1, tm=512, tn=512, name=f"mlp_down_{l}")
        saved.append((h, xn, h1, xn2, act, ctx, full))
        h = h2

    loss_row, dh, g_final = _loss_head(h, loss_target[0], final_norm)

    g_attn_norm, g_mlp_norm = [None] * DEPTH, [None] * DEPTH
    g_sinks, g_qn, g_kvn = [None] * sinks.shape[0], [None] * n_odd, [None] * n_odd
    dbias_total = None
    landed = [None] * len(groups)
    send = None
    for l in reversed(range(DEPTH)):
        h0, xn, h1, xn2, act, ctx, full = saved[l]
        G = {}
        if send is None:
            du = _matmul(dh, full["w_mlp_down"], trans_b=True, out_dtype=MXU_DT, epi='dsq', extra=act,
                         tm=1024, tn=512, name=f"mlp_down_dx_{l}")
        else:
            du, landed[l + 2] = _matmul(dh, full["w_mlp_down"], trans_b=True, out_dtype=MXU_DT, epi='dsq', extra=act,
                                        tm=1024, tn=512, name=f"mlp_down_dx_{l}", ride=(send, True))
        G["w_mlp_down"] = _matmul_tn(act, dh, tk=512, tn=1024, name=f"mlp_down_dw_{l}")
        G["w_mlp_up"] = _matmul_tn(xn2, du, tk=512, tn=1024, name=f"mlp_up_dw_{l}")
        dxn2 = _matmul(du, full["w_mlp_up"], trans_b=True, tm=512, tn=512, name=f"mlp_up_dx_{l}")
        dh, g = _rmsnorm_bwd(h1, mlp_norm[l], dxn2, dh, name=f"mlp_norm_bwd_{l}")
        g_mlp_norm[l] = g[0]
        if l % 2 == 0:
            e = l // 2
            dxn, G["w_in_ab"], G["w_out_ab"], dbias, g_sinks[e], landed_mlp = _even_bwd(
                dh, xn, ctx, full["w_in_ab"], full["w_out_ab"], bias, sink_rows[e], l,
                ride=(send_of(G, 1), True) if l == 0 else None)
            if l == 0:
                landed[1] = landed_mlp
            dbias_total = dbias if dbias_total is None else dbias_total + dbias
        else:
            o = l // 2
            dxn, G["w_down_c"], g_qn[o], G["w_uq_c"], g_kvn[o], G["w_ukv_c"], G["w_o_c"] = _mla_bwd(
                dh, xn, ctx, full["w_down_c"], q_norm_full[o], full["w_uq_c"], kv_norm_full[o],
                full["w_ukv_c"], full["w_o_c"], l)
        dh, g = _rmsnorm_bwd(h0, attn_norm[l], dxn, dh, name=f"attn_norm_bwd_{l}")
        g_attn_norm[l] = g[0]
        send = send_of(G, 0 if l == 0 else l + 1)
    landed[0] = _exchange(send, True, "scatter_grads_0")
    grad_x = dh[None]
    (g_rel_bias,) = bias_vjp(dbias_total)

    big = [{}, {}, {}, {}]
    for gi in range(len(groups)):
        outs = _adamw(landed[gi], w_packs[gi], pack(Mo, gi), pack(Vo, gi), name=f"adamw_{gi}")
        shapes = [W[n].shape[1:] for n, _ in groups[gi]]
        for kind, buf in enumerate(outs):
            for (n, _), t in zip(groups[gi], _unpack_rows(buf, shapes)):
                big[kind].setdefault(n, []).append(t)
    big_out = [{n: jnp.stack(ts) for n, ts in d.items()} for d in big]

    small_g = [g_rel_bias, jnp.stack(g_attn_norm), jnp.stack(g_mlp_norm), g_final[0], jnp.stack(g_sinks),
               jnp.stack(g_qn), jnp.stack(g_kvn), loss_row[0, :1]]
    small_w = [rel_bias, attn_norm, mlp_norm, final_norm, sinks, q_norm_c, kv_norm_c, jnp.zeros((1,), F32)]
    small_m = [m_rel_bias, m_attn_norm, m_mlp_norm, m_final_norm, m_sinks, m_q_norm_c, m_kv_norm_c, jnp.zeros((1,), F32)]
    small_v = [v_rel_bias, v_attn_norm, v_mlp_norm, v_final_norm, v_sinks, v_q_norm_c, v_kv_norm_c, jnp.ones((1,), F32)]
    offs = np.cumsum([0] + [-(-a.size // 1024) * 8 for a in small_g])
    partials = _exchange(jnp.concatenate([_pad_rows8(a) for a in small_g], axis=0), False, "gather_small_grads")

    def mine(i, a_full_shape, local):
        p = partials[:, offs[i]:offs[i + 1]].reshape(N_DEV, -1)[:, :math.prod(a_full_shape)]
        p = p.reshape((N_DEV,) + tuple(a_full_shape))
        if local.shape != tuple(a_full_shape):
            width = local.shape[-1]
            p = lax.dynamic_slice_in_dim(p, me * width, width, axis=p.ndim - 1)
        return jnp.stack([_pad_rows8(p[s]) for s in range(N_DEV)])

    parts_small = jnp.concatenate([mine(i, g.shape, w) for i, (g, w) in enumerate(zip(small_g, small_w))], axis=1)
    pk = lambda arrs: jnp.concatenate([_pad_rows8(a) for a in arrs], axis=0)
    small_out = _adamw(parts_small, pk(small_w), pk(small_m), pk(small_v), name="adamw_small", tr=parts_small.shape[1])
    offs2 = np.cumsum([0] + [-(-a.size // 1024) * 8 for a in small_w])

    def unpack_small(buf):
        return [buf[offs2[i]:offs2[i + 1]].reshape(-1)[:a.size].reshape(a.shape) for i, a in enumerate(small_w)]

    sg, sd, sm, sv = (unpack_small(b) for b in small_out)
    loss = sg[7][0]

    order = ['rel_bias', 'attn_norm', 'mlp_norm', 'final_norm', 'w_in_ab', 'sinks', 'w_out_ab', 'w_down_c', 'q_norm_c',
             'w_uq_c', 'kv_norm_c', 'w_ukv_c', 'w_o_c', 'w_mlp_up', 'w_mlp_down']
    small_idx = {'rel_bias': 0, 'attn_norm': 1, 'mlp_norm': 2, 'final_norm': 3, 'sinks': 4, 'q_norm_c': 5, 'kv_norm_c': 6}

    def pick(kind):
        res = []
        for n in order:
            if n in small_idx:
                res.append((sg, sd, sm, sv)[kind][small_idx[n]])
            else:
                res.append(big_out[kind][n])
        return res

    return (loss, grad_x, *pick(0), *pick(1), *pick(2), *pick(3))
```

```python
import math

import numpy as np
import jax
import jax.numpy as jnp
from jax import lax
from jax.experimental import pallas as pl
from jax.experimental.pallas import tpu as pltpu

F32 = jnp.float32
MXU_DT = jnp.bfloat16

N_DEV = 8
D_MODEL = 1024
DEPTH = 4
HEAD_DIM = 64
BLOCK = 128
EPS = 1e-6
NEG = -1e30
A_Q_HEADS = 8
A_KV_HEADS = 2
A_GROUP = A_Q_HEADS // A_KV_HEADS
A_WINDOW = 128
B_BRANCHES = ((128, 1), (512, 4), (2048, 16))
B_HPB = 4
B_HEADS = len(B_BRANCHES) * B_HPB
NUM_BUCKETS = 32
MAX_DISTANCE = 2048
N_BIAS_HEADS = A_Q_HEADS + B_HEADS
N_BAND_KV = A_KV_HEADS + B_HEADS
A_IN = (A_Q_HEADS + 2 * A_KV_HEADS) * HEAD_DIM
C_HEADS = 8
C_NOPE = 64
C_ROPE = 32
C_QK = C_NOPE + C_ROPE
C_V = 64
C_Q_RANK = 384
C_KV_RANK = 256
C_DOWN = C_Q_RANK + C_KV_RANK + C_ROPE
C_DOWN_PAD = 768
ROPE_THETA = 10000.0
N_CHUNKS = 16
FLASH_T = 512

ADAM_LR = 0.001
ADAM_B1 = 0.9
ADAM_B2 = 0.999
ADAM_EPS = 1e-08
ADAM_WD = 0.01
ADAM_STEP = 10

V7X_VMEM_BYTES = 64 * 1024 * 1024
VMEM_LIMIT = V7X_VMEM_BYTES - 8 * 1024 * 1024


def _pcall(body, **kw):
    return pl.pallas_call(body, **kw)


def _cparams(*sem):
    return pltpu.CompilerParams(dimension_semantics=sem, vmem_limit_bytes=VMEM_LIMIT)


def _exchange(src, all_to_all, name):
    def body(src_ref, out_ref, send_sems, recv_sems, local_sem):
        copies = _exchange_copies(src_ref, out_ref, send_sems, recv_sems, local_sem, all_to_all)
        for cp in copies:
            cp.start()
        _exchange_wait(copies)

    return _pcall(
        body, name=name,
        out_shape=_exchange_out(src),
        in_specs=[pl.BlockSpec(memory_space=pl.ANY)],
        out_specs=pl.BlockSpec(memory_space=pl.ANY),
        scratch_shapes=_exchange_sems(),
    )(src)


def _exchange_out(src):
    return jax.ShapeDtypeStruct((N_DEV,) + src.shape[-2:], src.dtype)


def _exchange_sems():
    return [pltpu.SemaphoreType.DMA((N_DEV - 1,)), pltpu.SemaphoreType.DMA((N_DEV - 1,)), pltpu.SemaphoreType.DMA]


def _exchange_copies(src_ref, out_ref, send_sems, recv_sems, local_sem, all_to_all):
    x, y, c = lax.axis_index("x"), lax.axis_index("y"), lax.axis_index("c")
    me = 4 * x + 2 * y + c

    def piece(dev):
        return src_ref.at[dev] if all_to_all else src_ref

    copies = [pltpu.make_async_copy(piece(me), out_ref.at[me], local_sem)]
    for k in range(1, N_DEV):
        px = 1 - x if (k >> 2) & 1 else x
        py = 1 - y if (k >> 1) & 1 else y
        pc = 1 - c if k & 1 else c
        copies.append(pltpu.make_async_remote_copy(
            src_ref=piece(4 * px + 2 * py + pc), dst_ref=out_ref.at[me],
            send_sem=send_sems.at[k - 1], recv_sem=recv_sems.at[k - 1],
            device_id=(px, py, pc), device_id_type=pl.DeviceIdType.MESH))
    return copies


def _exchange_wait(copies):
    for cp in copies[1:]:
        cp.wait()
    copies[0].wait()


def _matmul(a, b, *, trans_b=False, out_dtype=F32, epi=None, extra=None, tm=512, tn=512, name, ride=None):
    M, K = a.shape
    N = b.shape[0] if trans_b else b.shape[1]
    tm, tn = min(tm, M), min(tn, N)
    assert M % tm == 0 and N % tn == 0 and (b.shape[1] if trans_b else b.shape[0]) == K
    dn = (((1,), (1,)), ((), ())) if trans_b else (((1,), (0,)), ((), ()))
    n_i, n_j = M // tm, N // tn
    n_in = 2 + (extra is not None)

    def body(*refs):
        a_ref, b_ref = refs[0], refs[1]
        o_ref = refs[n_in + (ride is not None)]
        if ride is not None:
            i, j = pl.program_id(0), pl.program_id(1)
            copies = _exchange_copies(refs[n_in], refs[n_in + 2], *refs[n_in + 3:], ride[1])

            @pl.when((i == 0) & (j == 0))
            def _():
                for cp in copies:
                    cp.start()

        acc = lax.dot_general(a_ref[...].astype(MXU_DT), b_ref[...].astype(MXU_DT), dn,
                              preferred_element_type=F32)
        if epi == 'relu2':
            r = jnp.maximum(acc, 0.0)
            acc = r * r
        elif epi == 'add':
            acc = acc + refs[2][...].astype(F32)
        elif epi == 'dsq':
            acc = acc * (2.0 * jnp.sqrt(refs[2][...].astype(F32)))
        o_ref[...] = acc.astype(out_dtype)

        if ride is not None:
            @pl.when((i == n_i - 1) & (j == n_j - 1))
            def _():
                _exchange_wait(copies)

    b_spec = pl.BlockSpec((tn, K), lambda i, j: (j, 0)) if trans_b else pl.BlockSpec((K, tn), lambda i, j: (0, j))
    in_specs = [pl.BlockSpec((tm, K), lambda i, j: (i, 0)), b_spec]
    args = [a, b]
    if extra is not None:
        in_specs.append(pl.BlockSpec((tm, tn), lambda i, j: (i, j)))
        args.append(extra)
    out_shape = jax.ShapeDtypeStruct((M, N), out_dtype)
    out_spec = pl.BlockSpec((tm, tn), lambda i, j: (i, j))
    if ride is None:
        return _pcall(
            body, name=name, grid=(n_i, n_j), out_shape=out_shape, in_specs=in_specs, out_specs=out_spec,
            compiler_params=_cparams("parallel", "parallel"),
        )(*args)
    return _pcall(
        body, name=name, grid=(n_i, n_j),
        out_shape=(out_shape, _exchange_out(ride[0])),
        in_specs=in_specs + [pl.BlockSpec(memory_space=pl.ANY)],
        out_specs=(out_spec, pl.BlockSpec(memory_space=pl.ANY)),
        scratch_shapes=_exchange_sems(),
        compiler_params=_cparams("arbitrary", "arbitrary"),
    )(*args, ride[0])


def _matmul_tn(a, b, *, tk=512, tn=512, tm=1024, name):
    M, Ka = a.shape
    N = b.shape[1]
    tk, tn, tm = min(tk, Ka), min(tn, N), min(tm, M)
    assert Ka % tk == 0 and N % tn == 0 and M % tm == 0 and b.shape[0] == M

    def body(a_ref, b_ref, o_ref):
        @pl.when(pl.program_id(2) == 0)
        def _():
            o_ref[...] = jnp.zeros_like(o_ref)

        o_ref[...] += lax.dot_general(a_ref[...].astype(MXU_DT), b_ref[...].astype(MXU_DT),
                                      (((0,), (0,)), ((), ())), preferred_element_type=F32)

    return _pcall(
        body, name=name, grid=(Ka // tk, N // tn, M // tm),
        out_shape=jax.ShapeDtypeStruct((Ka, N), F32),
        in_specs=[pl.BlockSpec((tm, tk), lambda i, j, r: (r, i)), pl.BlockSpec((tm, tn), lambda i, j, r: (r, j))],
        out_specs=pl.BlockSpec((tk, tn), lambda i, j, r: (i, j)),
        compiler_params=_cparams("parallel", "parallel", "arbitrary"),
    )(a, b)


def _rmsnorm(x, g, *, out_dtype, name, tr=512):
    S, D = x.shape
    tr = min(tr, S)

    def body(x_ref, g_ref, o_ref):
        xf = x_ref[...].astype(F32)
        r = lax.rsqrt(jnp.mean(xf * xf, axis=-1, keepdims=True) + EPS)
        o_ref[...] = (xf * r * g_ref[...]).astype(out_dtype)

    return _pcall(
        body, name=name, grid=(S // tr,),
        out_shape=jax.ShapeDtypeStruct((S, D), out_dtype),
        in_specs=[pl.BlockSpec((tr, D), lambda i: (i, 0)), pl.BlockSpec((1, D), lambda i: (0, 0))],
        out_specs=pl.BlockSpec((tr, D), lambda i: (i, 0)),
        compiler_params=_cparams("parallel"),
    )(x, g.reshape(1, D))


def _rmsnorm_bwd(x, g, dy, dres, *, name, tr=512):
    S, D = x.shape
    tr = min(tr, S)

    def body(*refs):
        if dres is None:
            x_ref, g_ref, dy_ref, dx_ref, dg_ref = refs
        else:
            x_ref, g_ref, dy_ref, dres_ref, dx_ref, dg_ref = refs

        @pl.when(pl.program_id(0) == 0)
        def _():
            dg_ref[...] = jnp.zeros_like(dg_ref)

        xf = x_ref[...].astype(F32)
        r = lax.rsqrt(jnp.mean(xf * xf, axis=-1, keepdims=True) + EPS)
        xhat = xf * r
        dyf = dy_ref[...].astype(F32)
        dg_ref[...] += jnp.sum(dyf * xhat, axis=0, keepdims=True)
        dyg = dyf * g_ref[...]
        dx = r * (dyg - xhat * jnp.mean(dyg * xhat, axis=-1, keepdims=True))
        if dres is not None:
            dx = dx + dres_ref[...]
        dx_ref[...] = dx

    row = pl.BlockSpec((tr, D), lambda i: (i, 0))
    vec = pl.BlockSpec((1, D), lambda i: (0, 0))
    args = [x, g.reshape(1, D), dy] + ([] if dres is None else [dres])
    return _pcall(
        body, name=name, grid=(S // tr,),
        out_shape=(jax.ShapeDtypeStruct((S, D), F32), jax.ShapeDtypeStruct((1, D), F32)),
        in_specs=[row, vec, row] + ([] if dres is None else [row]),
        out_specs=(row, vec),
        compiler_params=_cparams("arbitrary"),
    )(*args)


def _loss_head(h, t, g, *, tr=512):
    S, D = h.shape
    tr = min(tr, S)

    def body(h_ref, t_ref, g_ref, loss_ref, dh_ref, dg_ref):
        @pl.when(pl.program_id(0) == 0)
        def _():
            dg_ref[...] = jnp.zeros_like(dg_ref)
            loss_ref[...] = jnp.zeros_like(loss_ref)

        xf = h_ref[...]
        r = lax.rsqrt(jnp.mean(xf * xf, axis=-1, keepdims=True) + EPS)
        xhat = xf * r
        e = xhat * g_ref[...] - t_ref[...]
        part = 0.5 * jnp.sum(jnp.mean(e * e, axis=-1, keepdims=True), axis=0, keepdims=True)
        loss_ref[...] += jnp.broadcast_to(part, loss_ref.shape)
        dy = e * (1.0 / D)
        dg_ref[...] += jnp.sum(dy * xhat, axis=0, keepdims=True)
        dyg = dy * g_ref[...]
        dh_ref[...] = r * (dyg - xhat * jnp.mean(dyg * xhat, axis=-1, keepdims=True))

    row = pl.BlockSpec((tr, D), lambda i: (i, 0))
    vec = pl.BlockSpec((1, D), lambda i: (0, 0))
    return _pcall(
        body, name="loss_head", grid=(S // tr,),
        out_shape=(jax.ShapeDtypeStruct((1, 128), F32), jax.ShapeDtypeStruct((S, D), F32),
                   jax.ShapeDtypeStruct((1, D), F32)),
        in_specs=[row, row, vec],
        out_specs=(pl.BlockSpec((1, 128), lambda i: (0, 0)), row, vec),
        compiler_params=_cparams("arbitrary"),
    )(h, t, g.reshape(1, D))


N_PAIRS = N_BIAS_HEADS // 2
A_PAIRS = A_Q_HEADS // 2
PAIR_W = 2 * HEAD_DIM
assert PAIR_W == 128 and A_KV_HEADS * HEAD_DIM == PAIR_W and B_HPB * HEAD_DIM == 2 * PAIR_W


def _pair_period(p):
    return jnp.where(p < A_PAIRS + 2, 16, jnp.where(p < A_PAIRS + 4, 4, 1))


def _pair_cols(p):
    b = jnp.maximum(p - A_PAIRS, 0)
    base, pp = 6 + 6 * (b // 2), b % 2
    is_a = p < A_PAIRS
    return (jnp.where(is_a, p, base + pp), jnp.where(is_a, A_PAIRS, base + 2 + pp),
            jnp.where(is_a, A_PAIRS + 1, base + 4 + pp))


def _band_specs(S):
    ch = S // N_CHUNKS
    nb = ch // BLOCK
    col = lambda i: (lambda p, c: (c, _pair_cols(p)[i]))
    prev = lambda i: (lambda p, c: (jnp.maximum(c * nb - 1, 0), _pair_cols(p)[i]))
    qkv = [pl.BlockSpec((ch, PAIR_W), col(0)), pl.BlockSpec((ch, PAIR_W), col(1)), pl.BlockSpec((BLOCK, PAIR_W), prev(1)),
           pl.BlockSpec((ch, PAIR_W), col(2)), pl.BlockSpec((BLOCK, PAIR_W), prev(2))]
    out_spec = pl.BlockSpec((ch, PAIR_W), lambda p, c: (c, p))
    sink_spec = pl.BlockSpec((2, 1, 128), lambda p, c: (p, 0, 0))
    row_spec = pl.BlockSpec((1, 1, nb, 2, BLOCK), lambda p, c: (p, c, 0, 0, 0))
    return ch, nb, qkv, out_spec, sink_spec, row_spec


def _pair_kv(p, e, ref):
    half = jnp.where(p < A_PAIRS, p // (A_GROUP // 2), e)
    return jnp.where(half == 0, ref[:, :HEAD_DIM], ref[:, HEAD_DIM:])


def _eye():
    return lax.broadcasted_iota(jnp.int32, (BLOCK, BLOCK), 0) == lax.broadcasted_iota(jnp.int32, (BLOCK, BLOCK), 1)


_NT = (((1,), (1,)), ((), ()))
_NN = (((1,), (0,)), ((), ()))
_TN = (((0,), (0,)), ((), ()))


_B_NT = (((2,), (2,)), ((0,), (0,)))
_B_NN = (((2,), (1,)), ((0,), (0,)))


def _bdot(a, b, dn):
    return lax.dot_general(a, b, dn, preferred_element_type=F32)


def _with_prev(first, t3):
    return first[None] if t3.shape[0] == 1 else jnp.concatenate([first[None], t3[:-1]], axis=0)


def _mask_first(s_prev, prev_ok):
    s0 = jnp.where(prev_ok, s_prev[0], NEG)[None]
    return s0 if s_prev.shape[0] == 1 else jnp.concatenate([s0, s_prev[1:]], axis=0)


def _banded_fwd(proj, bias, sinks):
    S = proj.shape[0]
    dh = HEAD_DIM
    ch, nb, qkv, out_spec, sink_spec, row_spec = _band_specs(S)
    bias_spec = pl.BlockSpec((2, BLOCK, 2 * BLOCK), lambda p, c: (p, 0, 0))
    scale = HEAD_DIM ** -0.5

    def body(q_ref, k_ref, kp_ref, v_ref, vp_ref, b_ref, s_ref, o_ref, lse_ref):
        p, c = pl.program_id(0), pl.program_id(1)
        prev_ok = (c % _pair_period(p)) != 0
        for e in range(2):
            lanes = slice(e * dh, (e + 1) * dh)
            q3 = q_ref[:, lanes].reshape(nb, BLOCK, dh)
            k3, v3 = (_pair_kv(p, e, r).reshape(nb, BLOCK, dh) for r in (k_ref, v_ref))
            kp3, vp3 = _with_prev(_pair_kv(p, e, kp_ref), k3), _with_prev(_pair_kv(p, e, vp_ref), v3)
            sink = s_ref[e, :, :1]
            s_cur = _bdot(q3, k3, _B_NT) * scale + b_ref[e, :, BLOCK:][None]
            s_prev = _mask_first(_bdot(q3, kp3, _B_NT) * scale + b_ref[e, :, :BLOCK][None], prev_ok)
            m = jnp.maximum(jnp.max(s_cur, axis=-1, keepdims=True), jnp.max(s_prev, axis=-1, keepdims=True))
            m = jnp.maximum(m, sink)
            p_cur = jnp.exp(s_cur - m)
            p_prev = jnp.exp(s_prev - m)
            l = jnp.sum(p_cur, axis=-1, keepdims=True) + jnp.sum(p_prev, axis=-1, keepdims=True) + jnp.exp(sink - m)
            acc = _bdot(p_cur.astype(MXU_DT), v3, _B_NN) + _bdot(p_prev.astype(MXU_DT), vp3, _B_NN)
            o_ref[:, lanes] = (acc / l).reshape(ch, dh)
            lse = m + jnp.log(l)
            lse_ref[0, 0, :, e:e + 1, :] = jnp.sum(jnp.where(_eye()[None], lse, 0.0), axis=1, keepdims=True)

    return _pcall(
        body, name="banded_fwd", grid=(N_PAIRS, N_CHUNKS),
        out_shape=(jax.ShapeDtypeStruct((S, N_PAIRS * PAIR_W), F32),
                   jax.ShapeDtypeStruct((N_PAIRS, N_CHUNKS, nb, 2, BLOCK), F32)),
        in_specs=qkv + [bias_spec, sink_spec],
        out_specs=(out_spec, row_spec),
        compiler_params=_cparams("parallel", "parallel"),
    )(proj, proj, proj, proj, proj, bias, sinks)


def _banded_bwd(proj, bias_t, sinks, do, dlse):
    S = proj.shape[0]
    dh = HEAD_DIM
    ch, nb, qkv, out_spec, sink_spec, row_spec = _band_specs(S)
    bias_spec = pl.BlockSpec((2, 2 * BLOCK, BLOCK), lambda p, c: (p, 0, 0))
    scale = HEAD_DIM ** -0.5

    def body(q_ref, k_ref, kp_ref, v_ref, vp_ref, b_ref, s_ref, do_ref, dl_ref,
             dq_ref, dk_ref, dv_ref, dkh_ref, dvh_ref, db_ref, ds_ref):
        p, c = pl.program_id(0), pl.program_id(1)

        @pl.when(c == 0)
        def _():
            db_ref[...] = jnp.zeros_like(db_ref)
            ds_ref[...] = jnp.zeros_like(ds_ref)

        prev_ok = (c % _pair_period(p)) != 0
        for e in range(2):
            lanes = slice(e * dh, (e + 1) * dh)
            q3 = q_ref[:, lanes].reshape(nb, BLOCK, dh)
            k3, v3 = (_pair_kv(p, e, r).reshape(nb, BLOCK, dh) for r in (k_ref, v_ref))
            kp3, vp3 = _with_prev(_pair_kv(p, e, kp_ref), k3), _with_prev(_pair_kv(p, e, vp_ref), v3)
            do3 = do_ref[:, lanes].astype(MXU_DT).reshape(nb, BLOCK, dh)
            sink = s_ref[e, :, :1]
            s_cur = _bdot(k3, q3, _B_NT) * scale + b_ref[e, BLOCK:, :][None]
            s_prev = _mask_first(_bdot(kp3, q3, _B_NT) * scale + b_ref[e, :BLOCK, :][None], prev_ok)
            m = jnp.maximum(jnp.max(s_cur, axis=1, keepdims=True), jnp.max(s_prev, axis=1, keepdims=True))
            m = jnp.maximum(m, sink)
            p_cur = jnp.exp(s_cur - m)
            p_prev = jnp.exp(s_prev - m)
            p_sink = jnp.exp(sink - m)
            inv = 1.0 / (jnp.sum(p_cur, axis=1, keepdims=True) + jnp.sum(p_prev, axis=1, keepdims=True) + p_sink)
            p_cur, p_prev, p_sink = p_cur * inv, p_prev * inv, p_sink * inv
            dp_cur = _bdot(v3, do3, _B_NT)
            dp_prev = _bdot(vp3, do3, _B_NT)
            delta = jnp.sum(p_cur * dp_cur, axis=1, keepdims=True) + jnp.sum(p_prev * dp_prev, axis=1, keepdims=True)
            t = dl_ref[0, 0, :, e:e + 1, :] - delta
            ds_cur = p_cur * (dp_cur + t)
            ds_prev = p_prev * (dp_prev + t)
            dsink = jnp.sum(jnp.sum(p_sink * t, axis=0), axis=-1, keepdims=True)
            ds_ref[e] += jnp.broadcast_to(dsink, (1, 128))
            db_ref[e, :BLOCK, :] += jnp.sum(ds_prev, axis=0)
            db_ref[e, BLOCK:, :] += jnp.sum(ds_cur, axis=0)
            dsb_cur = (ds_cur * scale).astype(MXU_DT)
            dsb_prev = (ds_prev * scale).astype(MXU_DT)
            dk_prev = _bdot(dsb_prev, q3, _B_NN)
            dv_prev = _bdot(p_prev.astype(MXU_DT), do3, _B_NN)

            def shifted(t3):
                z = jnp.zeros((1, BLOCK, dh), F32)
                return z if nb == 1 else jnp.concatenate([t3[1:], z], axis=0)

            dk_ref[:, lanes] = (_bdot(dsb_cur, q3, _B_NN) + shifted(dk_prev)).reshape(ch, dh)
            dv_ref[:, lanes] = (_bdot(p_cur.astype(MXU_DT), do3, _B_NN) + shifted(dv_prev)).reshape(ch, dh)
            dkh_ref[0, 0, :, lanes] = dk_prev[0]
            dvh_ref[0, 0, :, lanes] = dv_prev[0]
            for b in range(nb):
                dq_ref[b * BLOCK:(b + 1) * BLOCK, lanes] = (
                    lax.dot_general(dsb_cur[b], k3[b], _TN, preferred_element_type=F32)
                    + lax.dot_general(dsb_prev[b], kp3[b], _TN, preferred_element_type=F32))

    halo_spec = pl.BlockSpec((1, 1, BLOCK, PAIR_W), lambda p, c: (p, c, 0, 0))
    big = jax.ShapeDtypeStruct((S, N_PAIRS * PAIR_W), F32)
    halo = jax.ShapeDtypeStruct((N_PAIRS, N_CHUNKS, BLOCK, PAIR_W), F32)
    return _pcall(
        body, name="banded_bwd", grid=(N_PAIRS, N_CHUNKS),
        out_shape=(big, big, big, halo, halo, jax.ShapeDtypeStruct(bias_t.shape, F32),
                   jax.ShapeDtypeStruct(sinks.shape, F32)),
        in_specs=qkv + [bias_spec, sink_spec, out_spec, row_spec],
        out_specs=(out_spec, out_spec, out_spec, halo_spec, halo_spec, bias_spec, sink_spec),
        compiler_params=_cparams("arbitrary", "arbitrary"),
    )(proj, proj, proj, proj, proj, bias_t, sinks, do, dlse)


def _halo_fold(t, halo, name):
    S, width = t.shape
    nb = S // N_CHUNKS // BLOCK

    def body(t_ref, h_ref, o_ref):
        o_ref[:N_CHUNKS - 1, 0] = t_ref[:N_CHUNKS - 1, 0] + h_ref[0, 1:]
        o_ref[N_CHUNKS - 1:, 0] = t_ref[N_CHUNKS - 1:, 0]

    blk = pl.BlockSpec((N_CHUNKS, 1, BLOCK, PAIR_W), lambda p: (0, nb - 1, 0, p))
    return _pcall(
        body, name=name, grid=(N_PAIRS,),
        out_shape=jax.ShapeDtypeStruct((N_CHUNKS, nb, BLOCK, width), t.dtype),
        in_specs=[blk, pl.BlockSpec((1, N_CHUNKS, BLOCK, PAIR_W), lambda p: (p, 0, 0, 0))],
        out_specs=blk, input_output_aliases={0: 0},
        compiler_params=_cparams("parallel"),
    )(t.reshape(N_CHUNKS, nb, BLOCK, width), halo).reshape(S, width)


def _causal_mask(T):
    return lax.broadcasted_iota(jnp.int32, (T, T), 0) <= lax.broadcasted_iota(jnp.int32, (T, T), 1)


LOG2E = math.log2(math.e)
FLASH_SPLIT = 2
FLASH_ONES_ROWS = 16


def _mla_q_proj(cqn, wq_t, cos_t, sin_t, *, name):
    S, R = cqn.shape
    H, dqk, _ = wq_t.shape
    nq, half, T = cos_t.shape

    def body(x_ref, w_ref, c_ref, s_ref, o_ref):
        x = x_ref[...]
        for h in range(H):
            qt = lax.dot_general(w_ref[h], x, _NT, preferred_element_type=F32)
            t1, t2 = qt[C_NOPE:C_NOPE + half], qt[C_NOPE + half:]
            o_ref[h, 0, :C_NOPE] = qt[:C_NOPE].astype(MXU_DT)
            o_ref[h, 0, C_NOPE:C_NOPE + half] = (t1 * c_ref[0] - t2 * s_ref[0]).astype(MXU_DT)
            o_ref[h, 0, C_NOPE + half:] = (t1 * s_ref[0] + t2 * c_ref[0]).astype(MXU_DT)

    tab = pl.BlockSpec((1, half, T), lambda i: (i, 0, 0))
    return _pcall(
        body, name=name, grid=(nq,),
        out_shape=jax.ShapeDtypeStruct((H, nq, dqk, T), MXU_DT),
        in_specs=[pl.BlockSpec((T, R), lambda i: (i, 0)), pl.BlockSpec((H, dqk, R), lambda i: (0, 0, 0)), tab, tab],
        out_specs=pl.BlockSpec((H, 1, dqk, T), lambda i: (0, i, 0, 0)),
        compiler_params=_cparams("parallel"),
    )(cqn, wq_t, cos_t, sin_t)


def _mla_q_proj_bwd(dqt, cqn, wq_t, cos_t, sin_t, *, name):
    S, R = cqn.shape
    H, dqk, _ = wq_t.shape
    nq, half, T = cos_t.shape

    def body(g_ref, x_ref, w_ref, c_ref, s_ref, dx_ref, dw_ref):
        @pl.when(pl.program_id(0) == 0)
        def _():
            dw_ref[...] = jnp.zeros_like(dw_ref)

        x = x_ref[...]
        acc = jnp.zeros((T, R), F32)
        for h in range(H):
            g = g_ref[h, 0]
            g1, g2 = g[C_NOPE:C_NOPE + half], g[C_NOPE + half:]
            gq = jnp.concatenate([g[:C_NOPE], g1 * c_ref[0] + g2 * s_ref[0], g2 * c_ref[0] - g1 * s_ref[0]],
                                 axis=0).astype(MXU_DT)
            acc = acc + lax.dot_general(gq, w_ref[h], _TN, preferred_element_type=F32)
            dw_ref[h] += lax.dot_general(gq, x, _NN, preferred_element_type=F32)
        dx_ref[...] = acc

    tab = pl.BlockSpec((1, half, T), lambda i: (i, 0, 0))
    whole = pl.BlockSpec((H, dqk, R), lambda i: (0, 0, 0))
    return _pcall(
        body, name=name, grid=(nq,),
        out_shape=(jax.ShapeDtypeStruct((S, R), F32), jax.ShapeDtypeStruct((H, dqk, R), F32)),
        in_specs=[pl.BlockSpec((H, 1, dqk, T), lambda i: (0, i, 0, 0)), pl.BlockSpec((T, R), lambda i: (i, 0)), whole, tab, tab],
        out_specs=(pl.BlockSpec((T, R), lambda i: (i, 0)), whole),
        compiler_params=_cparams("arbitrary"),
    )(dqt, cqn, wq_t, cos_t, sin_t)


def _mla_kv_proj(ckvn, wk, wv, kr, kr_t, *, name):
    S, R = ckvn.shape
    H = wk.shape[0]
    nq, dr, T = kr_t.shape
    dqk = C_NOPE + dr
    wk_t, wv_t = wk.transpose(0, 2, 1), wv.transpose(0, 2, 1)

    def body(x_ref, wk_ref, wv_ref, wkt_ref, wvt_ref, kr_ref, krt_ref, kt_ref, vt_ref, kn_ref, vn_ref):
        x = x_ref[...]
        krt, krn = krt_ref[0].astype(MXU_DT), kr_ref[...].astype(MXU_DT)
        ones = jnp.where(lax.broadcasted_iota(jnp.int32, (FLASH_ONES_ROWS, T), 0) == 0, 1.0, 0.0).astype(MXU_DT)
        for h in range(H):
            kt_ref[h, 0, :C_NOPE] = lax.dot_general(wkt_ref[h], x, _NT, preferred_element_type=F32).astype(MXU_DT)
            kt_ref[h, 0, C_NOPE:] = krt
            vt_ref[h, 0, :C_V] = lax.dot_general(wvt_ref[h], x, _NT, preferred_element_type=F32).astype(MXU_DT)
            vt_ref[h, 0, C_V:] = ones
            kn_ref[h, :, :C_NOPE] = lax.dot_general(x, wk_ref[h], _NN, preferred_element_type=F32).astype(MXU_DT)
            kn_ref[h, :, C_NOPE:] = krn
            vn_ref[h] = lax.dot_general(x, wv_ref[h], _NN, preferred_element_type=F32).astype(MXU_DT)

    w_spec = pl.BlockSpec((H, R, C_NOPE), lambda i: (0, 0, 0))
    wt_spec = pl.BlockSpec((H, C_NOPE, R), lambda i: (0, 0, 0))
    return _pcall(
        body, name=name, grid=(nq,),
        out_shape=(jax.ShapeDtypeStruct((H, nq, dqk, T), MXU_DT),
                   jax.ShapeDtypeStruct((H, nq, C_V + FLASH_ONES_ROWS, T), MXU_DT),
                   jax.ShapeDtypeStruct((H, S, dqk), MXU_DT), jax.ShapeDtypeStruct((H, S, C_V), MXU_DT)),
        in_specs=[pl.BlockSpec((T, R), lambda i: (i, 0)), w_spec, w_spec, wt_spec, wt_spec,
                  pl.BlockSpec((T, dr), lambda i: (i, 0)), pl.BlockSpec((1, dr, T), lambda i: (i, 0, 0))],
        out_specs=(pl.BlockSpec((H, 1, dqk, T), lambda i: (0, i, 0, 0)),
                   pl.BlockSpec((H, 1, C_V + FLASH_ONES_ROWS, T), lambda i: (0, i, 0, 0)),
                   pl.BlockSpec((H, T, dqk), lambda i: (0, i, 0)), pl.BlockSpec((H, T, C_V), lambda i: (0, i, 0))),
        compiler_params=_cparams("parallel"),
    )(ckvn, wk, wv, wk_t, wv_t, kr, kr_t)


def _mla_kv_proj_bwd(dkt, dvt, ckvn, wk, wv, *, name):
    S, R = ckvn.shape
    H = wk.shape[0]
    _, nq, dqk, T = dkt.shape
    dr = dqk - C_NOPE
    wk_t, wv_t = wk.transpose(0, 2, 1), wv.transpose(0, 2, 1)

    def body(gk_ref, gv_ref, x_ref, wkt_ref, wvt_ref, dx_ref, dwk_ref, dwv_ref, dkr_ref):
        @pl.when(pl.program_id(0) == 0)
        def _():
            dwk_ref[...] = jnp.zeros_like(dwk_ref)
            dwv_ref[...] = jnp.zeros_like(dwv_ref)

        x = x_ref[...]
        acc = jnp.zeros((T, R), F32)
        dkr = jnp.zeros((dr, T), F32)
        for h in range(H):
            gk = gk_ref[h, 0, :C_NOPE].astype(MXU_DT)
            gv = gv_ref[h, 0].astype(MXU_DT)
            acc = acc + (lax.dot_general(gk, wkt_ref[h], _TN, preferred_element_type=F32)
                         + lax.dot_general(gv, wvt_ref[h], _TN, preferred_element_type=F32))
            dwk_ref[h] += lax.dot_general(gk, x, _NN, preferred_element_type=F32)
            dwv_ref[h] += lax.dot_general(gv, x, _NN, preferred_element_type=F32)
            dkr = dkr + gk_ref[h, 0, C_NOPE:]
        dx_ref[...] = acc
        dkr_ref[0] = dkr

    wt_spec = pl.BlockSpec((H, C_NOPE, R), lambda i: (0, 0, 0))
    return _pcall(
        body, name=name, grid=(nq,),
        out_shape=(jax.ShapeDtypeStruct((S, R), F32), jax.ShapeDtypeStruct((H, C_NOPE, R), F32),
                   jax.ShapeDtypeStruct((H, C_V, R), F32), jax.ShapeDtypeStruct((nq, dr, T), F32)),
        in_specs=[pl.BlockSpec((H, 1, dqk, T), lambda i: (0, i, 0, 0)), pl.BlockSpec((H, 1, C_V, T), lambda i: (0, i, 0, 0)),
                  pl.BlockSpec((T, R), lambda i: (i, 0)), wt_spec, wt_spec],
        out_specs=(pl.BlockSpec((T, R), lambda i: (i, 0)), wt_spec, wt_spec, pl.BlockSpec((1, dr, T), lambda i: (i, 0, 0))),
        compiler_params=_cparams("arbitrary"),
    )(dkt, dvt, ckvn, wk_t, wv_t)


def _mla_out_proj(ot, w_o, h, *, name):
    H, nq, dv, T = ot.shape
    D = w_o.shape[2]

    def body(o_ref, w_ref, h_ref, out_ref):
        acc = h_ref[...]
        for hd in range(H):
            acc = acc + lax.dot_general(o_ref[hd, 0].astype(MXU_DT), w_ref[hd], _TN, preferred_element_type=F32)
        out_ref[...] = acc

    row = pl.BlockSpec((T, D), lambda i: (i, 0))
    return _pcall(
        body, name=name, grid=(nq,),
        out_shape=jax.ShapeDtypeStruct(h.shape, F32),
        in_specs=[pl.BlockSpec((H, 1, dv, T), lambda i: (0, i, 0, 0)), pl.BlockSpec((H, dv, D), lambda i: (0, 0, 0)), row],
        out_specs=row,
        compiler_params=_cparams("parallel"),
    )(ot, w_o, h)


def _mla_out_proj_bwd(dh, ot, w_o, *, name):
    H, nq, dv, T = ot.shape
    D = w_o.shape[2]

    def body(dh_ref, o_ref, w_ref, dot_ref, del_ref, dw_ref):
        @pl.when(pl.program_id(0) == 0)
        def _():
            dw_ref[...] = jnp.zeros_like(dw_ref)

        dhb = dh_ref[...].astype(MXU_DT)
        for hd in range(H):
            o = o_ref[hd, 0]
            d = lax.dot_general(w_ref[hd], dhb, _NT, preferred_element_type=F32)
            dot_ref[hd, 0] = d
            del_ref[hd, 0] = jnp.sum(d * o, axis=0, keepdims=True)
            dw_ref[hd] += lax.dot_general(o.astype(MXU_DT), dhb, _NN, preferred_element_type=F32)

    tile = pl.BlockSpec((H, 1, dv, T), lambda i: (0, i, 0, 0))
    whole = pl.BlockSpec((H, dv, D), lambda i: (0, 0, 0))
    return _pcall(
        body, name=name, grid=(nq,),
        out_shape=(jax.ShapeDtypeStruct(ot.shape, F32), jax.ShapeDtypeStruct((H, nq, 1, T), F32),
                   jax.ShapeDtypeStruct(w_o.shape, F32)),
        in_specs=[pl.BlockSpec((T, D), lambda i: (i, 0)), tile, whole],
        out_specs=(tile, pl.BlockSpec((H, 1, 1, T), lambda i: (0, i, 0, 0)), whole),
        compiler_params=_cparams("arbitrary"),
    )(dh, ot, w_o)


def _flash_fwd(qt, k, vt1):
    H, nq, dqk, T = qt.shape
    S = k.shape[1]
    dva = vt1.shape[2]
    dv = dva - FLASH_ONES_ROWS
    scale = dqk ** -0.5
    c = scale * LOG2E
    th = T // FLASH_SPLIT

    def body(qt_ref, k_ref, vt_ref, ot_ref, lse_ref, sa_ref, sb_ref):
        i = pl.program_id(1)

        def scores(j):
            kb = k_ref[0, pl.ds(pl.multiple_of(j * T, T), T), :]
            return lax.dot_general(kb, qt_ref[0, 0], _NN, preferred_element_type=F32)

        def softmax_pv(s_ref, j, carry, masked):
            m, acc = carry
            raw = s_ref[...]
            if masked:
                raw = jnp.where(_causal_mask(T), raw, NEG)
            m_new = jnp.maximum(m, jnp.max(raw, axis=0, keepdims=True))
            alpha = jnp.exp2((m - m_new) * c)
            pb = jnp.exp2((raw - m_new) * c).astype(MXU_DT)
            acc = acc * alpha + lax.dot_general(vt_ref[0, j], pb, _NN, preferred_element_type=F32)
            return m_new, acc

        def pair(p, carry):
            j = 2 * p
            sb_ref[...] = scores(j + 1)
            carry = softmax_pv(sa_ref, j, carry, False)
            sa_ref[...] = scores(j + 2)
            return softmax_pv(sb_ref, j + 1, carry, False)

        def even_tail(carry):
            return softmax_pv(sa_ref, i, carry, True)

        def odd_tail(carry):
            sb_ref[...] = scores(i)
            carry = softmax_pv(sa_ref, i - 1, carry, False)
            return softmax_pv(sb_ref, i, carry, True)

        sa_ref[...] = scores(0)
        carry = lax.fori_loop(0, i // 2, pair, (jnp.full((1, T), NEG, F32), jnp.zeros((dva, T), F32)))
        m, acc = lax.cond(i % 2 == 0, even_tail, odd_tail, carry)
        l = acc[dv:dv + 1]
        ot_ref[0, 0] = acc[:dv] / l
        lse_ref[0, 0] = m * scale + jnp.log(l)

    return _pcall(
        body, name="flash_fwd", grid=(H, nq),
        out_shape=(jax.ShapeDtypeStruct((H, nq, dv, T), F32), jax.ShapeDtypeStruct((H, nq, 1, T), F32)),
        in_specs=[pl.BlockSpec((1, 1, dqk, T), lambda h, i: (h, i, 0, 0)),
                  pl.BlockSpec((1, S, dqk), lambda h, i: (h, 0, 0)),
                  pl.BlockSpec((1, nq, dva, T), lambda h, i: (h, 0, 0, 0))],
        out_specs=(pl.BlockSpec((1, 1, dv, T), lambda h, i: (h, i, 0, 0)),
                   pl.BlockSpec((1, 1, 1, T), lambda h, i: (h, i, 0, 0))),
        scratch_shapes=[pltpu.VMEM((T, T), F32), pltpu.VMEM((T, T), F32)],
        compiler_params=_cparams("parallel", "parallel"),
    )(qt, k, vt1)


def _flash_delta(ot, dot):
    H, nq, dv, T = ot.shape

    def body(o_ref, do_ref, d_ref):
        d_ref[0, 0] = jnp.sum(o_ref[0, 0] * do_ref[0, 0], axis=0, keepdims=True)

    spec = pl.BlockSpec((1, 1, dv, T), lambda h, i: (h, i, 0, 0))
    return _pcall(
        body, name="flash_delta", grid=(H, nq),
        out_shape=jax.ShapeDtypeStruct((H, nq, 1, T), F32),
        in_specs=[spec, spec], out_specs=pl.BlockSpec((1, 1, 1, T), lambda h, i: (h, i, 0, 0)),
        compiler_params=_cparams("parallel", "parallel"),
    )(ot, dot)


def _flash_bwd(qt, k, kt, v, dot, lse, delta):
    H, nq, dqk, T = qt.shape
    dv_ = v.shape[2]
    scale = dqk ** -0.5
    c = scale * LOG2E
    th = T // FLASH_SPLIT

    def body(qt_ref, k_ref, kt_ref, v_ref, dot_ref, lse_ref, del_ref, dqt_ref, dkt_ref, dvt_ref,
             sa_ref, pa_ref, sb_ref, pb_ref):
        j = pl.program_id(1)

        @pl.when(j == 0)
        def _():
            dqt_ref[...] = jnp.zeros_like(dqt_ref)

        n_un = nq - 1 - j

        def issue(i, s_ref, dp_ref):
            s_ref[...] = lax.dot_general(k_ref[0], qt_ref[0, i], _NN, preferred_element_type=F32)
            dp_ref[...] = lax.dot_general(v_ref[0], dot_ref[0, i].astype(MXU_DT), _NN, preferred_element_type=F32)

        def consume(i, s_ref, dp_ref, carry, masked):
            dkt, dvt = carry
            raw = s_ref[...]
            if masked:
                raw = jnp.where(_causal_mask(T), raw, NEG)
            p = jnp.exp2(raw * c - lse_ref[0, i] * LOG2E)
            dsb = (p * (dp_ref[...] - del_ref[0, i])).astype(MXU_DT)
            dvt = dvt + lax.dot_general(dot_ref[0, i].astype(MXU_DT), p.astype(MXU_DT), _NT, preferred_element_type=F32)
            dkt = dkt + lax.dot_general(qt_ref[0, i], dsb, _NT, preferred_element_type=F32)
            dqt_ref[0, i] += lax.dot_general(kt_ref[0, 0], dsb, _NN, preferred_element_type=F32) * scale
            return dkt, dvt

        def pair(p, carry):
            i0 = j + 1 + 2 * p
            issue(i0 + 1, sb_ref, pb_ref)
            carry = consume(i0, sa_ref, pa_ref, carry, False)
            issue(jnp.where(2 * p + 2 < n_un, i0 + 2, j), sa_ref, pa_ref)
            return consume(i0 + 1, sb_ref, pb_ref, carry, False)

        def even_tail(carry):
            return consume(j, sa_ref, pa_ref, carry, True)

        def odd_tail(carry):
            issue(j, sb_ref, pb_ref)
            carry = consume(nq - 1, sa_ref, pa_ref, carry, False)
            return consume(j, sb_ref, pb_ref, carry, True)

        issue(jnp.where(n_un > 0, j + 1, j), sa_ref, pa_ref)
        carry = lax.fori_loop(0, n_un // 2, pair, (jnp.zeros((dqk, T), F32), jnp.zeros((dv_, T), F32)))
        dkt, dvt = lax.cond(n_un % 2 == 0, even_tail, odd_tail, carry)
        dkt_ref[0, 0] = dkt * scale
        dvt_ref[0, 0] = dvt

    whole = lambda d: pl.BlockSpec((1, nq, d, T), lambda h, j: (h, 0, 0, 0))
    tile_t = lambda d: pl.BlockSpec((1, 1, d, T), lambda h, j: (h, j, 0, 0))
    return _pcall(
        body, name="flash_bwd", grid=(H, nq),
        out_shape=(jax.ShapeDtypeStruct((H, nq, dqk, T), F32), jax.ShapeDtypeStruct((H, nq, dqk, T), F32),
                   jax.ShapeDtypeStruct((H, nq, dv_, T), F32)),
        in_specs=[whole(dqk),
                  pl.BlockSpec((1, T, dqk), lambda h, j: (h, j, 0)),
                  tile_t(dqk),
                  pl.BlockSpec((1, T, dv_), lambda h, j: (h, j, 0)),
                  whole(dv_), whole(1), whole(1)],
        out_specs=(whole(dqk), tile_t(dqk), tile_t(dv_)),
        scratch_shapes=[pltpu.VMEM((T, T), F32) for _ in range(4)],
        compiler_params=_cparams("arbitrary", "arbitrary"),
    )(qt, k, kt, v, dot, lse, delta)


def _adamw(parts, w, m, v, *, name, tr=512):
    P, R, C = parts.shape
    tr = min(tr, R)
    assert R % tr == 0

    def body(p_ref, w_ref, m_ref, v_ref, g_ref, d_ref, m2_ref, v2_ref):
        g = p_ref[0].astype(F32)
        for s in range(1, P):
            g = g + p_ref[s].astype(F32)
        m2 = ADAM_B1 * m_ref[...] + (1.0 - ADAM_B1) * g
        v2 = ADAM_B2 * v_ref[...] + (1.0 - ADAM_B2) * jnp.square(g)
        m_hat = m2 / (1.0 - ADAM_B1 ** ADAM_STEP)
        v_hat = v2 / (1.0 - ADAM_B2 ** ADAM_STEP)
        g_ref[...] = g
        d_ref[...] = -ADAM_LR * (m_hat / (jnp.sqrt(v_hat) + ADAM_EPS) + ADAM_WD * w_ref[...])
        m2_ref[...] = m2
        v2_ref[...] = v2

    row = pl.BlockSpec((tr, C), lambda i: (i, 0))
    out = jax.ShapeDtypeStruct((R, C), F32)
    return _pcall(
        body, name=name, grid=(R // tr,),
        out_shape=(out, out, out, out),
        in_specs=[pl.BlockSpec((P, tr, C), lambda i: (0, i, 0)), row, row, row],
        out_specs=(row, row, row, row),
        compiler_params=_cparams("parallel"),
    )(parts, w, m, v)


def _bias_tables():
    i = np.arange(BLOCK)[:, None]
    j = np.arange(2 * BLOCK)[None, :]
    dist = i + BLOCK - j
    out = []
    for dil, max_dist in [(1, A_WINDOW - 1)] + [(d, w // d) for w, d in B_BRANCHES]:
        n = np.maximum(dist, 0) * dil
        max_exact = NUM_BUCKETS // 2
        nf = np.maximum(n, 1).astype(np.float64)
        val = np.log(nf / max_exact) / math.log(MAX_DISTANCE / max_exact) * (NUM_BUCKETS - max_exact)
        inband = (dist >= 0) & (dist <= max_dist)
        frac = np.abs(val - np.round(val))
        last = NUM_BUCKETS - 1 - max_exact
        assert np.all((frac > 2e-5) | (n <= max_exact) | (val >= last) | ~inband)
        large = max_exact + val.astype(np.int64)
        bucket = np.where(n < max_exact, n, np.minimum(large, NUM_BUCKETS - 1))
        onehot = (bucket[..., None] == np.arange(NUM_BUCKETS)).astype(np.float32)
        out.append((onehot.reshape(-1, NUM_BUCKETS), inband))
    return out


def _make_bias(rel_bias):
    tabs = _bias_tables()
    groups = [(0, A_Q_HEADS)] + [(A_Q_HEADS + g * B_HPB, B_HPB) for g in range(len(B_BRANCHES))]
    parts = []
    for (onehot, inband), (h0, nh) in zip(tabs, groups):
        b = jnp.dot(jnp.asarray(onehot), rel_bias[:, h0:h0 + nh], precision=lax.Precision.HIGHEST)
        b = b.reshape(BLOCK, 2 * BLOCK, nh)
        b = jnp.where(jnp.asarray(inband)[..., None], b, NEG)
        parts.append(b.transpose(2, 0, 1))
    return jnp.concatenate(parts, axis=0)


A_W = A_Q_HEADS * HEAD_DIM
B_W = B_HPB * HEAD_DIM


def _perm_rows(x, d):
    return x if d == 1 else x.reshape(x.shape[0] // d, d, -1).transpose(1, 0, 2).reshape(x.shape)


def _unperm_rows(x, d):
    return x if d == 1 else x.reshape(d, x.shape[0] // d, -1).transpose(1, 0, 2).reshape(x.shape)


def _even_post(o_all, lse):
    S = o_all.shape[0]
    outs, lses = [], []
    for g, (_, d) in enumerate(B_BRANCHES):
        w0 = A_W + g * B_W
        outs.append(_unperm_rows(o_all[:, w0:w0 + B_W], d))
        lg = lse[A_PAIRS + 2 * g:A_PAIRS + 2 * g + 2].transpose(1, 2, 4, 0, 3).reshape(S, B_HPB)
        lses.append(_unperm_rows(lg, d))
    wts = jax.nn.softmax(jnp.stack(lses), axis=0)
    widen = jnp.asarray(np.kron(np.eye(B_HPB), np.ones((1, HEAD_DIM))), F32)
    out_b = sum(jnp.dot(wts[g], widen, precision=lax.Precision.HIGHEST) * outs[g] for g in range(len(B_BRANCHES)))
    return jnp.concatenate([o_all[:, :A_W], out_b], axis=-1)


def _rope_tables(S, r):
    inv = ROPE_THETA ** (-jnp.arange(0, r, 2, dtype=jnp.float32) / r)
    ang = jnp.arange(S, dtype=jnp.float32)[:, None] * inv[None, :]
    return jnp.cos(ang), jnp.sin(ang)


def _rope(t):
    S, r = t.shape[1], t.shape[-1]
    shape = (1, S) + (1,) * (t.ndim - 3) + (r // 2,)
    cos, sin = (a.reshape(shape) for a in _rope_tables(S, r))
    t1, t2 = t[..., :r // 2], t[..., r // 2:]
    return jnp.concatenate([t1 * cos - t2 * sin, t1 * sin + t2 * cos], axis=-1)


def _mla_pre(q_lin, kv_lin, kr_raw):
    S = q_lin.shape[0]
    q = q_lin.reshape(1, S, C_HEADS, C_QK)
    qf = jnp.concatenate([q[..., :C_NOPE], _rope(q[..., C_NOPE:])], axis=-1)[0]
    kv = kv_lin.reshape(S, C_HEADS, C_NOPE + C_V)
    kr = _rope(kr_raw[None])[0]
    kf = jnp.concatenate([kv[..., :C_NOPE], jnp.broadcast_to(kr[:, None, :], (S, C_HEADS, C_ROPE))], axis=-1)
    return qf, kf, kv[..., C_NOPE:]


def _to_tiles_t(t, T):
    S, H, d = t.shape
    return t.reshape(S // T, T, H, d).transpose(2, 0, 3, 1)


def _from_tiles_t(t):
    H, n, d, T = t.shape
    return t.transpose(1, 3, 0, 2).reshape(n * T, H, d)


_BIG = (("w_in_ab", 2), ("w_out_ab", 2), ("w_down_c", 1), ("w_uq_c", 2), ("w_ukv_c", 2), ("w_o_c", 2),
        ("w_mlp_up", 2), ("w_mlp_down", 1))
_ROW_ALIGN = 512


def _pack_rows(arrs):
    rows = [a.reshape(-1, 128) for a in arrs]
    n = sum(r.shape[0] for r in rows)
    pad = (-n) % _ROW_ALIGN
    if pad:
        rows.append(jnp.zeros((pad, 128), rows[0].dtype))
    return jnp.concatenate(rows, axis=0)


def _pack_rows_per_device(arrs):
    rows = [a.reshape(N_DEV, -1, 128) for a in arrs]
    n = sum(r.shape[1] for r in rows)
    pad = (-n) % _ROW_ALIGN
    if pad:
        rows.append(jnp.zeros((N_DEV, pad, 128), rows[0].dtype))
    return jnp.concatenate(rows, axis=1)


def _unpack_rows(buf, shapes):
    out, r0 = [], 0
    for shp in shapes:
        n = math.prod(shp) // 128
        out.append(buf[..., r0:r0 + n, :].reshape(buf.shape[:-2] + tuple(shp)))
        r0 += n
    return out


def _layer_tensors(l):
    att = [("w_in_ab", l // 2), ("w_out_ab", l // 2)] if l % 2 == 0 else \
          [("w_down_c", l // 2), ("w_uq_c", l // 2), ("w_ukv_c", l // 2), ("w_o_c", l // 2)]
    return att + [("w_mlp_up", l), ("w_mlp_down", l)]


def _exchange_groups():
    first = _layer_tensors(0)
    return [first[:-2], first[-2:]] + [_layer_tensors(l) for l in range(1, DEPTH)]


def _gathered_to_full(g, axis):
    if axis == 2:
        return g.transpose(1, 0, 2).reshape(g.shape[1], N_DEV * g.shape[2])
    return g.reshape(N_DEV * g.shape[1], g.shape[2])


def _full_to_shards(t, axis):
    a, b = t.shape
    if axis == 2:
        return t.reshape(a, N_DEV, b // N_DEV).transpose(1, 0, 2)
    return t.reshape(N_DEV, a // N_DEV, b)


def _pad_rows8(a):
    flat = a.reshape(-1)
    n = -(-flat.shape[0] // 1024) * 1024
    return jnp.pad(flat, (0, n - flat.shape[0])).reshape(-1, 128)


def _even_fwd(xn, h, w_in, w_out, bias, sinks_row, l, ride=None):
    c0 = A_IN + 3 * B_W
    first = _matmul(xn, w_in[:, :c0], out_dtype=MXU_DT, tm=1024, tn=512, name=f"even_in_{l}", ride=ride)
    first, landed = first if ride is not None else (first, None)
    parts = [first]
    for g, (_, d) in list(enumerate(B_BRANCHES))[1:]:
        cols = slice(A_IN + 3 * B_W * g, A_IN + 3 * B_W * (g + 1))
        parts.append(_matmul(_perm_rows(xn, d), w_in[:, cols], out_dtype=MXU_DT, tm=1024, tn=3 * B_W,
                             name=f"even_in_dil{d}_{l}"))
    proj = jnp.concatenate(parts, axis=1)
    o_all, lse = _banded_fwd(proj, bias, sinks_row)
    attn, post_vjp = jax.vjp(_even_post, o_all, lse)
    attn = attn.astype(MXU_DT)
    h1 = _matmul(attn, w_out, epi='add', extra=h, tm=1024, tn=512, name=f"even_out_{l}")
    return h1, (proj, attn, post_vjp), landed


def _even_bwd(dh, xn, ctx, w_in, w_out, bias, sinks_row, l, ride=None):
    proj, attn, post_vjp = ctx
    d_attn = _matmul(dh, w_out, trans_b=True, tm=1024, tn=768, name=f"even_out_dx_{l}")
    g_w_out = _matmul_tn(attn, dh, tk=768, tn=512, name=f"even_out_dw_{l}")
    do_all, dlse = post_vjp(d_attn)
    dq, dk, dv, dkh, dvh, dbias_t, dsinks = _banded_bwd(proj, bias.transpose(0, 2, 1), sinks_row, do_all, dlse)
    dbias = dbias_t.transpose(0, 2, 1)
    dk = _halo_fold(dk, dkh, f"halo_k_{l}")
    dv = _halo_fold(dv, dvh, f"halo_v_{l}")

    def kv_sum(t):
        heads = [t[:, i * HEAD_DIM:(i + 1) * HEAD_DIM] for i in range(A_Q_HEADS)]
        return jnp.concatenate([sum(heads[j * A_GROUP:(j + 1) * A_GROUP]) for j in range(A_KV_HEADS)], axis=1)

    groups = [jnp.concatenate([dq[:, :A_W], kv_sum(dk), kv_sum(dv)], axis=1).astype(MXU_DT)]
    for g, (_, d) in enumerate(B_BRANCHES):
        cols = slice(A_W + g * B_W, A_W + (g + 1) * B_W)
        grp = jnp.concatenate([dq[:, cols], dk[:, cols], dv[:, cols]], axis=1).astype(MXU_DT)
        groups.append(_unperm_rows(grp, d))
    dproj = jnp.concatenate(groups, axis=1)
    g_w_in = _matmul_tn(xn, dproj, tk=512, tn=1024, name=f"even_in_dw_{l}")
    dxn = _matmul(dproj, w_in, trans_b=True, tm=1024, tn=512, name=f"even_in_dx_{l}", ride=ride)
    dxn, landed = dxn if ride is not None else (dxn, None)
    return dxn, g_w_in, g_w_out, dbias, dsinks[:A_Q_HEADS, 0, 0], landed


def _mla_fwd(xn, h, w_down, q_norm, w_uq, kv_norm, w_ukv, w_o, l):
    S = xn.shape[0]
    T = min(FLASH_T, S)
    down = _matmul(xn, w_down, tm=1024, tn=768, name=f"mla_down_{l}")
    c_q, c_kv, kr_raw = down[:, :C_Q_RANK], down[:, C_Q_RANK:C_Q_RANK + C_KV_RANK], down[:, C_Q_RANK + C_KV_RANK:C_DOWN]
    cqn = _rmsnorm(c_q, q_norm, out_dtype=MXU_DT, name=f"mla_qnorm_{l}")
    ckvn = _rmsnorm(c_kv, kv_norm, out_dtype=MXU_DT, name=f"mla_kvnorm_{l}")
    cos, sin = _rope_tables(S, C_ROPE)
    to_t = lambda t: t.reshape(S // T, T, -1).transpose(0, 2, 1)
    qt = _mla_q_proj(cqn, w_uq.T.reshape(C_HEADS, C_QK, C_Q_RANK), to_t(cos), to_t(sin), name=f"mla_uq_{l}")
    w_kv = w_ukv.reshape(C_KV_RANK, C_HEADS, C_NOPE + C_V).transpose(1, 0, 2)
    kr = _rope(kr_raw[None])[0]
    kt, vt1, kn, vn = _mla_kv_proj(ckvn, w_kv[..., :C_NOPE], w_kv[..., C_NOPE:], kr, to_t(kr), name=f"mla_ukv_{l}")
    ot, lse = _flash_fwd(qt, kn, vt1)
    h1 = _mla_out_proj(ot, w_o.reshape(C_HEADS, C_V, -1), h, name=f"mla_o_{l}")
    return h1, (c_q, c_kv, cqn, ckvn, qt, kn, kt, vn, ot, lse)


def _mla_bwd(dh, xn, ctx, w_down, q_norm, w_uq, kv_norm, w_ukv, w_o, l):
    c_q, c_kv, cqn, ckvn, qt, kn, kt, vn, ot, lse = ctx
    S = xn.shape[0]
    T = qt.shape[-1]
    dot, delta, dw_o = _mla_out_proj_bwd(dh, ot, w_o.reshape(C_HEADS, C_V, -1), name=f"mla_o_bwd_{l}")
    g_w_o = dw_o.reshape(w_o.shape)
    dqt, dkt, dvt = _flash_bwd(qt, kn, kt, vn, dot, lse, delta)
    cos, sin = _rope_tables(S, C_ROPE)
    to_t = lambda t: t.reshape(S // T, T, -1).transpose(0, 2, 1)
    dcqn, dwq_t = _mla_q_proj_bwd(dqt, cqn, w_uq.T.reshape(C_HEADS, C_QK, C_Q_RANK), to_t(cos), to_t(sin),
                                  name=f"mla_uq_bwd_{l}")
    g_w_uq = dwq_t.reshape(C_HEADS * C_QK, C_Q_RANK).T
    w_kv = w_ukv.reshape(C_KV_RANK, C_HEADS, C_NOPE + C_V).transpose(1, 0, 2)
    dckvn, dwk_t, dwv_t, dkr_t = _mla_kv_proj_bwd(dkt, dvt, ckvn, w_kv[..., :C_NOPE], w_kv[..., C_NOPE:],
                                                  name=f"mla_ukv_bwd_{l}")
    g_w_ukv = jnp.concatenate([dwk_t, dwv_t], axis=1).reshape(C_HEADS * (C_NOPE + C_V), C_KV_RANK).T
    _, rope_vjp = jax.vjp(lambda t: _rope(t[None])[0], jnp.zeros((S, C_ROPE), F32))
    (dkr_raw,) = rope_vjp(dkr_t.transpose(0, 2, 1).reshape(S, C_ROPE))
    dc_q, g_q_norm = _rmsnorm_bwd(c_q, q_norm, dcqn, None, name=f"mla_qnorm_bwd_{l}")
    dc_kv, g_kv_norm = _rmsnorm_bwd(c_kv, kv_norm, dckvn, None, name=f"mla_kvnorm_bwd_{l}")
    ddown = jnp.concatenate([dc_q, dc_kv, dkr_raw, jnp.zeros((S, C_DOWN_PAD - C_DOWN), F32)], axis=1).astype(MXU_DT)
    g_w_down = _matmul_tn(xn, ddown, tk=512, tn=768, name=f"mla_down_dw_{l}")[:, :C_DOWN]
    dxn = _matmul(ddown, w_down, trans_b=True, tm=1024, tn=512, name=f"mla_down_dx_{l}")
    return dxn, g_w_down, g_q_norm[0], g_w_uq, g_kv_norm[0], g_w_ukv, g_w_o


def kernel(x, rel_bias, attn_norm, mlp_norm, final_norm, w_in_ab, sinks, w_out_ab, w_down_c, q_norm_c, w_uq_c, kv_norm_c, w_ukv_c, w_o_c, w_mlp_up, w_mlp_down, loss_target, m_rel_bias, m_attn_norm, m_mlp_norm, m_final_norm, m_w_in_ab, m_sinks, m_w_out_ab, m_w_down_c, m_q_norm_c, m_w_uq_c, m_kv_norm_c, m_w_ukv_c, m_w_o_c, m_w_mlp_up, m_w_mlp_down, v_rel_bias, v_attn_norm, v_mlp_norm, v_final_norm, v_w_in_ab, v_sinks, v_w_out_ab, v_w_down_c, v_q_norm_c, v_w_uq_c, v_kv_norm_c, v_w_ukv_c, v_w_o_c, v_w_mlp_up, v_w_mlp_down):
    W = dict(w_in_ab=w_in_ab, w_out_ab=w_out_ab, w_down_c=w_down_c, w_uq_c=w_uq_c, w_ukv_c=w_ukv_c, w_o_c=w_o_c,
             w_mlp_up=w_mlp_up, w_mlp_down=w_mlp_down)
    Mo = dict(w_in_ab=m_w_in_ab, w_out_ab=m_w_out_ab, w_down_c=m_w_down_c, w_uq_c=m_w_uq_c, w_ukv_c=m_w_ukv_c,
              w_o_c=m_w_o_c, w_mlp_up=m_w_mlp_up, w_mlp_down=m_w_mlp_down)
    Vo = dict(w_in_ab=v_w_in_ab, w_out_ab=v_w_out_ab, w_down_c=v_w_down_c, w_uq_c=v_w_uq_c, w_ukv_c=v_w_ukv_c,
              w_o_c=v_w_o_c, w_mlp_up=v_w_mlp_up, w_mlp_down=v_w_mlp_down)
    S = x.shape[1]
    me = 4 * lax.axis_index("x") + 2 * lax.axis_index("y") + lax.axis_index("c")
    axis_of = dict(_BIG)

    groups = _exchange_groups()

    def pack(src, gi):
        return _pack_rows([src[n][i] for n, i in groups[gi]])

    def unpack_group(gathered, gi):
        shapes = [W[n].shape[1:] for n, _ in groups[gi]]
        return {n: _gathered_to_full(g, axis_of[n])
                for (n, _), g in zip(groups[gi], _unpack_rows(gathered, shapes))}

    def send_of(G, gi):
        return _pack_rows_per_device([_full_to_shards(G[n], axis_of[n]) for n, _ in groups[gi]]).astype(MXU_DT)

    w_packs = [pack(W, gi) for gi in range(len(groups))]
    gathered = _exchange(w_packs[0].astype(MXU_DT), False, "gather_weights_0")
    gains = _exchange(jnp.concatenate([_pad_rows8(q_norm_c), _pad_rows8(kv_norm_c)], axis=0), False, "gather_gains")
    n_odd = q_norm_c.shape[0]
    q_norm_full = gains[:, 0].reshape(N_DEV, -1)[:, :q_norm_c.size].reshape(N_DEV, n_odd, -1).transpose(1, 0, 2).reshape(n_odd, C_Q_RANK)
    kv_norm_full = gains[:, 8].reshape(N_DEV, -1)[:, :kv_norm_c.size].reshape(N_DEV, n_odd, -1).transpose(1, 0, 2).reshape(n_odd, C_KV_RANK)

    bias, bias_vjp = jax.vjp(_make_bias, rel_bias)
    sink_rows = [jnp.broadcast_to(jnp.concatenate([sinks[e], jnp.full((B_HEADS,), NEG, F32)])[:, None, None],
                                  (N_BIAS_HEADS, 1, 128)) for e in range(sinks.shape[0])]

    h = x[0]
    saved = []
    for l in range(DEPTH):
        full = unpack_group(gathered, 0 if l == 0 else l + 1)
        if l % 2 == 1:
            full["w_down_c"] = jnp.pad(full["w_down_c"], ((0, 0), (0, C_DOWN_PAD - C_DOWN)))
        xn = _rmsnorm(h, attn_norm[l], out_dtype=MXU_DT, name=f"attn_norm_{l}")
        if l == 0:
            h1, ctx, gathered_mlp = _even_fwd(xn, h, full["w_in_ab"], full["w_out_ab"], bias, sink_rows[0], l,
                                              ride=(w_packs[1].astype(MXU_DT), False))
            full.update(unpack_group(gathered_mlp, 1))
        elif l % 2 == 0:
            h1, ctx, _ = _even_fwd(xn, h, full["w_in_ab"], full["w_out_ab"], bias, sink_rows[l // 2], l)
        else:
            o = l // 2
            h1, ctx = _mla_fwd(xn, h, full["w_down_c"], q_norm_full[o], full["w_uq_c"], kv_norm_full[o],
                               full["w_ukv_c"], full["w_o_c"], l)
        xn2 = _rmsnorm(h1, mlp_norm[l], out_dtype=MXU_DT, name=f"mlp_norm_{l}")
        if l + 1 < DEPTH:
            act, gathered = _matmul(xn2, full["w_mlp_up"], out_dtype=MXU_DT, epi='relu2', tm=1024, tn=512,
                                    name=f"mlp_up_{l}", ride=(w_packs[l + 2].astype(MXU_DT), False))
        else:
            act = _matmul(xn2, full["w_mlp_up"], out_dtype=MXU_DT, epi='relu2', tm=1024, tn=512, name=f"mlp_up_{l}")
        h2 = _matmul(act, full["w_mlp_down"], epi='add', extra=h1, tm=512, tn=512, name=f"mlp_down_{l}")
        saved.append((h, xn, h1, xn2, act, ctx, full))
        h = h2

    loss_row, dh, g_final = _loss_head(h, loss_target[0], final_norm)

    g_attn_norm, g_mlp_norm = [None] * DEPTH, [None] * DEPTH
    g_sinks, g_qn, g_kvn = [None] * sinks.shape[0], [None] * n_odd, [None] * n_odd
    dbias_total = None
    landed = [None] * len(groups)
    send = None
    for l in reversed(range(DEPTH)):
        h0, xn, h1, xn2, act, ctx, full = saved[l]
        G = {}
        if send is None:
            du = _matmul(dh, full["w_mlp_down"], trans_b=True, out_dtype=MXU_DT, epi='dsq', extra=act,
                         tm=1024, tn=512, name=f"mlp_down_dx_{l}")
        else:
            du, landed[l + 2] = _matmul(dh, full["w_mlp_down"], trans_b=True, out_dtype=MXU_DT, epi='dsq', extra=act,
                                        tm=1024, tn=512, name=f"mlp_down_dx_{l}", ride=(send, True))
        G["w_mlp_down"] = _matmul_tn(act, dh, tk=512, tn=1024, name=f"mlp_down_dw_{l}")
        G["w_mlp_up"] = _matmul_tn(xn2, du, tk=512, tn=1024, name=f"mlp_up_dw_{l}")
        dxn2 = _matmul(du, full["w_mlp_up"], trans_b=True, tm=512, tn=512, name=f"mlp_up_dx_{l}")
        dh, g = _rmsnorm_bwd(h1, mlp_norm[l], dxn2, dh, name=f"mlp_norm_bwd_{l}")
        g_mlp_norm[l] = g[0]
        if l % 2 == 0:
            e = l // 2
            dxn, G["w_in_ab"], G["w_out_ab"], dbias, g_sinks[e], landed_mlp = _even_bwd(
                dh, xn, ctx, full["w_in_ab"], full["w_out_ab"], bias, sink_rows[e], l,
                ride=(send_of(G, 1), True) if l == 0 else None)
            if l == 0:
                landed[1] = landed_mlp
            dbias_total = dbias if dbias_total is None else dbias_total + dbias
        else:
            o = l // 2
            dxn, G["w_down_c"], g_qn[o], G["w_uq_c"], g_kvn[o], G["w_ukv_c"], G["w_o_c"] = _mla_bwd(
                dh, xn, ctx, full["w_down_c"], q_norm_full[o], full["w_uq_c"], kv_norm_full[o],
                full["w_ukv_c"], full["w_o_c"], l)
        dh, g = _rmsnorm_bwd(h0, attn_norm[l], dxn, dh, name=f"attn_norm_bwd_{l}")
        g_attn_norm[l] = g[0]
        send = send_of(G, 0 if l == 0 else l + 1)
    landed[0] = _exchange(send, True, "scatter_grads_0")
    grad_x = dh[None]
    (g_rel_bias,) = bias_vjp(dbias_total)

    big = [{}, {}, {}, {}]
    for gi in range(len(groups)):
        outs = _adamw(landed[gi], w_packs[gi], pack(Mo, gi), pack(Vo, gi), name=f"adamw_{gi}")
        shapes = [W[n].shape[1:] for n, _ in groups[gi]]
        for kind, buf in enumerate(outs):
            for (n, _), t in zip(groups[gi], _unpack_rows(buf, shapes)):
                big[kind].setdefault(n, []).append(t)
    big_out = [{n: jnp.stack(ts) for n, ts in d.items()} for d in big]

    small_g = [g_rel_bias, jnp.stack(g_attn_norm), jnp.stack(g_mlp_norm), g_final[0], jnp.stack(g_sinks),
               jnp.stack(g_qn), jnp.stack(g_kvn), loss_row[0, :1]]
    small_w = [rel_bias, attn_norm, mlp_norm, final_norm, sinks, q_norm_c, kv_norm_c, jnp.zeros((1,), F32)]
    small_m = [m_rel_bias, m_attn_norm, m_mlp_norm, m_final_norm, m_sinks, m_q_norm_c, m_kv_norm_c, jnp.zeros((1,), F32)]
    small_v = [v_rel_bias, v_attn_norm, v_mlp_norm, v_final_norm, v_sinks, v_q_norm_c, v_kv_norm_c, jnp.ones((1,), F32)]
    offs = np.cumsum([0] + [-(-a.size // 1024) * 8 for a in small_g])
    partials = _exchange(jnp.concatenate([_pad_rows8(a) for a in small_g], axis=0), False, "gather_small_grads")

    def mine(i, a_full_shape, local):
        p = partials[:, offs[i]:offs[i + 1]].reshape(N_DEV, -1)[:, :math.prod(a_full_shape)]
        p = p.reshape((N_DEV,) + tuple(a_full_shape))
        if local.shape != tuple(a_full_shape):
            width = local.shape[-1]
            p = lax.dynamic_slice_in_dim(p, me * width, width, axis=p.ndim - 1)
        return jnp.stack([_pad_rows8(p[s]) for s in range(N_DEV)])

    parts_small = jnp.concatenate([mine(i, g.shape, w) for i, (g, w) in enumerate(zip(small_g, small_w))], axis=1)
    pk = lambda arrs: jnp.concatenate([_pad_rows8(a) for a in arrs], axis=0)
    small_out = _adamw(parts_small, pk(small_w), pk(small_m), pk(small_v), name="adamw_small", tr=parts_small.shape[1])
    offs2 = np.cumsum([0] + [-(-a.size // 1024) * 8 for a in small_w])

    def unpack_small(buf):
        return [buf[offs2[i]:offs2[i + 1]].reshape(-1)[:a.size].reshape(a.shape) for i, a in enumerate(small_w)]

    sg, sd, sm, sv = (unpack_small(b) for b in small_out)
    loss = sg[7][0]

    order = ['rel_bias', 'attn_norm', 'mlp_norm', 'final_norm', 'w_in_ab', 'sinks', 'w_out_ab', 'w_down_c', 'q_norm_c',
             'w_uq_c', 'kv_norm_c', 'w_ukv_c', 'w_o_c', 'w_mlp_up', 'w_mlp_down']
    small_idx = {'rel_bias': 0, 'attn_norm': 1, 'mlp_norm': 2, 'final_norm': 3, 'sinks': 4, 'q_norm_c': 5, 'kv_norm_c': 6}

    def pick(kind):
        res = []
        for n in order:
            if n in small_idx:
                res.append((sg, sd, sm, sv)[kind][small_idx[n]])
            else:
                res.append(big_out[kind][n])
        return res

    return (loss, grad_x, *pick(0), *pick(1), *pick(2), *pick(3))
```

```python
import math

import numpy as np
import jax
import jax.numpy as jnp
from jax import lax
from jax.experimental import pallas as pl
from jax.experimental.pallas import tpu as pltpu

F32 = jnp.float32
MXU_DT = jnp.bfloat16

N_DEV = 8
D_MODEL = 1024
DEPTH = 4
HEAD_DIM = 64
BLOCK = 128
EPS = 1e-6
NEG = -1e30
A_Q_HEADS = 8
A_KV_HEADS = 2
A_GROUP = A_Q_HEADS // A_KV_HEADS
A_WINDOW = 128
B_BRANCHES = ((128, 1), (512, 4), (2048, 16))
B_HPB = 4
B_HEADS = len(B_BRANCHES) * B_HPB
NUM_BUCKETS = 32
MAX_DISTANCE = 2048
N_BIAS_HEADS = A_Q_HEADS + B_HEADS
N_BAND_KV = A_KV_HEADS + B_HEADS
A_IN = (A_Q_HEADS + 2 * A_KV_HEADS) * HEAD_DIM
C_HEADS = 8
C_NOPE = 64
C_ROPE = 32
C_QK = C_NOPE + C_ROPE
C_V = 64
C_Q_RANK = 384
C_KV_RANK = 256
C_DOWN = C_Q_RANK + C_KV_RANK + C_ROPE
C_DOWN_PAD = 768
ROPE_THETA = 10000.0
N_CHUNKS = 16
FLASH_T = 512

ADAM_LR = 0.001
ADAM_B1 = 0.9
ADAM_B2 = 0.999
ADAM_EPS = 1e-08
ADAM_WD = 0.01
ADAM_STEP = 10

V7X_VMEM_BYTES = 64 * 1024 * 1024
VMEM_LIMIT = V7X_VMEM_BYTES - 8 * 1024 * 1024


def _pcall(body, **kw):
    return pl.pallas_call(body, **kw)


def _cparams(*sem):
    return pltpu.CompilerParams(dimension_semantics=sem, vmem_limit_bytes=VMEM_LIMIT)


def _exchange(src, all_to_all, name):
    def body(src_ref, out_ref, send_sems, recv_sems, local_sem):
        copies = _exchange_copies(src_ref, out_ref, send_sems, recv_sems, local_sem, all_to_all)
        for cp in copies:
            cp.start()
        _exchange_wait(copies)

    return _pcall(
        body, name=name,
        out_shape=_exchange_out(src),
        in_specs=[pl.BlockSpec(memory_space=pl.ANY)],
        out_specs=pl.BlockSpec(memory_space=pl.ANY),
        scratch_shapes=_exchange_sems(),
    )(src)


def _exchange_out(src):
    return jax.ShapeDtypeStruct((N_DEV,) + src.shape[-2:], src.dtype)


def _exchange_sems():
    return [pltpu.SemaphoreType.DMA((N_DEV - 1,)), pltpu.SemaphoreType.DMA((N_DEV - 1,)), pltpu.SemaphoreType.DMA]


def _exchange_copies(src_ref, out_ref, send_sems, recv_sems, local_sem, all_to_all):
    x, y, c = lax.axis_index("x"), lax.axis_index("y"), lax.axis_index("c")
    me = 4 * x + 2 * y + c

    def piece(dev):
        return src_ref.at[dev] if all_to_all else src_ref

    copies = [pltpu.make_async_copy(piece(me), out_ref.at[me], local_sem)]
    for k in range(1, N_DEV):
        px = 1 - x if (k >> 2) & 1 else x
        py = 1 - y if (k >> 1) & 1 else y
        pc = 1 - c if k & 1 else c
        copies.append(pltpu.make_async_remote_copy(
            src_ref=piece(4 * px + 2 * py + pc), dst_ref=out_ref.at[me],
            send_sem=send_sems.at[k - 1], recv_sem=recv_sems.at[k - 1],
            device_id=(px, py, pc), device_id_type=pl.DeviceIdType.MESH))
    return copies


def _exchange_wait(copies):
    for cp in copies[1:]:
        cp.wait()
    copies[0].wait()


def _matmul(a, b, *, trans_b=False, out_dtype=F32, epi=None, extra=None, norm=None, into=None, tm=512, tn=512,
            name, ride=None):
    M, K = a.shape
    N = b.shape[0] if trans_b else b.shape[1]
    tm, tn = min(tm, M), min(tn, N)
    assert M % tm == 0 and N % tn == 0 and (b.shape[1] if trans_b else b.shape[0]) == K
    assert (epi == 'norm_bwd') == (norm is not None) and (norm is None or tn == N)
    dn = (((1,), (1,)), ((), ())) if trans_b else (((1,), (0,)), ((), ()))
    n_i, n_j = M // tm, N // tn

    def body(*refs):
        it = iter(refs)
        a_ref, b_ref = next(it), next(it)
        e_ref = next(it) if extra is not None else None
        x_ref, g_ref, dres_ref = (next(it), next(it), next(it)) if norm is not None else (None, None, None)
        if into is not None:
            next(it)
        src_ref = next(it) if ride is not None else None
        o_ref = next(it)
        dg_ref = next(it) if norm is not None else None
        i, j = pl.program_id(0), pl.program_id(1)
        if ride is not None:
            land_ref = next(it)
            copies = _exchange_copies(src_ref, land_ref, *it, ride[1])

            @pl.when((i == 0) & (j == 0))
            def _():
                for cp in copies:
                    cp.start()

        acc = lax.dot_general(a_ref[...].astype(MXU_DT), b_ref[...].astype(MXU_DT), dn,
                              preferred_element_type=F32)
        if epi == 'relu2':
            r = jnp.maximum(acc, 0.0)
            acc = r * r
        elif epi == 'add':
            acc = acc + e_ref[...].astype(F32)
        elif epi == 'dsq':
            acc = acc * (2.0 * jnp.sqrt(e_ref[...].astype(F32)))
        elif epi == 'norm_bwd':
            @pl.when(i == 0)
            def _():
                dg_ref[...] = jnp.zeros_like(dg_ref)

            xf = x_ref[...]
            r = lax.rsqrt(jnp.mean(xf * xf, axis=-1, keepdims=True) + EPS)
            xhat = xf * r
            dg_ref[...] += jnp.sum(acc * xhat, axis=0, keepdims=True)
            dyg = acc * g_ref[...]
            acc = r * (dyg - xhat * jnp.mean(dyg * xhat, axis=-1, keepdims=True)) + dres_ref[...]
        o_ref[...] = acc.astype(out_dtype)

        if ride is not None:
            @pl.when((i == n_i - 1) & (j == n_j - 1))
            def _():
                _exchange_wait(copies)

    b_spec = pl.BlockSpec((tn, K), lambda i, j: (j, 0)) if trans_b else pl.BlockSpec((K, tn), lambda i, j: (0, j))
    tile = pl.BlockSpec((tm, tn), lambda i, j: (i, j))
    in_specs = [pl.BlockSpec((tm, K), lambda i, j: (i, 0)), b_spec]
    args = [a, b]
    out_shape, out_specs, aliases = [jax.ShapeDtypeStruct((M, N), out_dtype)], [tile], {}
    if extra is not None:
        in_specs.append(tile)
        args.append(extra)
    if norm is not None:
        vec = pl.BlockSpec((1, N), lambda i, j: (0, 0))
        in_specs += [tile, vec, tile]
        args += [norm[0], norm[1].reshape(1, N), norm[2]]
        out_shape.append(jax.ShapeDtypeStruct((1, N), F32))
        out_specs.append(vec)
    if into is not None:
        buf, col0 = into
        assert col0 % tn == 0 and buf.shape[0] == M and buf.dtype == out_dtype
        aliases = {len(args): 0}
        in_specs.append(pl.BlockSpec(memory_space=pl.ANY))
        args.append(buf)
        out_shape[0] = jax.ShapeDtypeStruct(buf.shape, out_dtype)
        out_specs[0] = pl.BlockSpec((tm, tn), lambda i, j: (i, j + col0 // tn))
    scratch = []
    if ride is not None:
        in_specs.append(pl.BlockSpec(memory_space=pl.ANY))
        args.append(ride[0])
        out_shape.append(_exchange_out(ride[0]))
        out_specs.append(pl.BlockSpec(memory_space=pl.ANY))
        scratch = _exchange_sems()
    ordered = ride is not None or norm is not None
    res = _pcall(
        body, name=name, grid=(n_i, n_j), out_shape=tuple(out_shape), in_specs=in_specs, out_specs=tuple(out_specs),
        scratch_shapes=scratch, input_output_aliases=aliases,
        compiler_params=_cparams(*(("arbitrary", "arbitrary") if ordered else ("parallel", "parallel"))),
    )(*args)
    return res[0] if len(res) == 1 else res


def _matmul_tn(a, b, *, tk=512, tn=512, tm=1024, name):
    M, Ka = a.shape
    N = b.shape[1]
    tk, tn, tm = min(tk, Ka), min(tn, N), min(tm, M)
    assert Ka % tk == 0 and N % tn == 0 and M % tm == 0 and b.shape[0] == M

    def body(a_ref, b_ref, o_ref):
        @pl.when(pl.program_id(2) == 0)
        def _():
            o_ref[...] = jnp.zeros_like(o_ref)

        o_ref[...] += lax.dot_general(a_ref[...].astype(MXU_DT), b_ref[...].astype(MXU_DT),
                                      (((0,), (0,)), ((), ())), preferred_element_type=F32)

    return _pcall(
        body, name=name, grid=(Ka // tk, N // tn, M // tm),
        out_shape=jax.ShapeDtypeStruct((Ka, N), F32),
        in_specs=[pl.BlockSpec((tm, tk), lambda i, j, r: (r, i)), pl.BlockSpec((tm, tn), lambda i, j, r: (r, j))],
        out_specs=pl.BlockSpec((tk, tn), lambda i, j, r: (i, j)),
        compiler_params=_cparams("parallel", "parallel", "arbitrary"),
    )(a, b)


def _rmsnorm(x, g, *, out_dtype, name, tr=512):
    S, D = x.shape
    tr = min(tr, S)

    def body(x_ref, g_ref, o_ref):
        xf = x_ref[...].astype(F32)
        r = lax.rsqrt(jnp.mean(xf * xf, axis=-1, keepdims=True) + EPS)
        o_ref[...] = (xf * r * g_ref[...]).astype(out_dtype)

    return _pcall(
        body, name=name, grid=(S // tr,),
        out_shape=jax.ShapeDtypeStruct((S, D), out_dtype),
        in_specs=[pl.BlockSpec((tr, D), lambda i: (i, 0)), pl.BlockSpec((1, D), lambda i: (0, 0))],
        out_specs=pl.BlockSpec((tr, D), lambda i: (i, 0)),
        compiler_params=_cparams("parallel"),
    )(x, g.reshape(1, D))


def _rmsnorm_bwd(x, g, dy, dres, *, name, tr=512):
    S, D = x.shape
    tr = min(tr, S)

    def body(*refs):
        if dres is None:
            x_ref, g_ref, dy_ref, dx_ref, dg_ref = refs
        else:
            x_ref, g_ref, dy_ref, dres_ref, dx_ref, dg_ref = refs

        @pl.when(pl.program_id(0) == 0)
        def _():
            dg_ref[...] = jnp.zeros_like(dg_ref)

        xf = x_ref[...].astype(F32)
        r = lax.rsqrt(jnp.mean(xf * xf, axis=-1, keepdims=True) + EPS)
        xhat = xf * r
        dyf = dy_ref[...].astype(F32)
        dg_ref[...] += jnp.sum(dyf * xhat, axis=0, keepdims=True)
        dyg = dyf * g_ref[...]
        dx = r * (dyg - xhat * jnp.mean(dyg * xhat, axis=-1, keepdims=True))
        if dres is not None:
            dx = dx + dres_ref[...]
        dx_ref[...] = dx

    row = pl.BlockSpec((tr, D), lambda i: (i, 0))
    vec = pl.BlockSpec((1, D), lambda i: (0, 0))
    args = [x, g.reshape(1, D), dy] + ([] if dres is None else [dres])
    return _pcall(
        body, name=name, grid=(S // tr,),
        out_shape=(jax.ShapeDtypeStruct((S, D), F32), jax.ShapeDtypeStruct((1, D), F32)),
        in_specs=[row, vec, row] + ([] if dres is None else [row]),
        out_specs=(row, vec),
        compiler_params=_cparams("arbitrary"),
    )(*args)


def _loss_head(h, t, g, *, tr=512):
    S, D = h.shape
    tr = min(tr, S)

    def body(h_ref, t_ref, g_ref, loss_ref, dh_ref, dg_ref):
        @pl.when(pl.program_id(0) == 0)
        def _():
            dg_ref[...] = jnp.zeros_like(dg_ref)
            loss_ref[...] = jnp.zeros_like(loss_ref)

        xf = h_ref[...]
        r = lax.rsqrt(jnp.mean(xf * xf, axis=-1, keepdims=True) + EPS)
        xhat = xf * r
        e = xhat * g_ref[...] - t_ref[...]
        part = 0.5 * jnp.sum(jnp.mean(e * e, axis=-1, keepdims=True), axis=0, keepdims=True)
        loss_ref[...] += jnp.broadcast_to(part, loss_ref.shape)
        dy = e * (1.0 / D)
        dg_ref[...] += jnp.sum(dy * xhat, axis=0, keepdims=True)
        dyg = dy * g_ref[...]
        dh_ref[...] = r * (dyg - xhat * jnp.mean(dyg * xhat, axis=-1, keepdims=True))

    row = pl.BlockSpec((tr, D), lambda i: (i, 0))
    vec = pl.BlockSpec((1, D), lambda i: (0, 0))
    return _pcall(
        body, name="loss_head", grid=(S // tr,),
        out_shape=(jax.ShapeDtypeStruct((1, 128), F32), jax.ShapeDtypeStruct((S, D), F32),
                   jax.ShapeDtypeStruct((1, D), F32)),
        in_specs=[row, row, vec],
        out_specs=(pl.BlockSpec((1, 128), lambda i: (0, 0)), row, vec),
        compiler_params=_cparams("arbitrary"),
    )(h, t, g.reshape(1, D))


N_PAIRS = N_BIAS_HEADS // 2
A_PAIRS = A_Q_HEADS // 2
PAIR_W = 2 * HEAD_DIM
assert PAIR_W == 128 and A_KV_HEADS * HEAD_DIM == PAIR_W and B_HPB * HEAD_DIM == 2 * PAIR_W


def _pair_period(p):
    return jnp.where(p < A_PAIRS + 2, 16, jnp.where(p < A_PAIRS + 4, 4, 1))


def _pair_cols(p):
    b = jnp.maximum(p - A_PAIRS, 0)
    base, pp = 6 + 6 * (b // 2), b % 2
    is_a = p < A_PAIRS
    return (jnp.where(is_a, p, base + pp), jnp.where(is_a, A_PAIRS, base + 2 + pp),
            jnp.where(is_a, A_PAIRS + 1, base + 4 + pp))


def _band_specs(S):
    ch = S // N_CHUNKS
    nb = ch // BLOCK
    col = lambda i: (lambda p, c: (c, _pair_cols(p)[i]))
    prev = lambda i: (lambda p, c: (jnp.maximum(c * nb - 1, 0), _pair_cols(p)[i]))
    qkv = [pl.BlockSpec((ch, PAIR_W), col(0)), pl.BlockSpec((ch, PAIR_W), col(1)), pl.BlockSpec((BLOCK, PAIR_W), prev(1)),
           pl.BlockSpec((ch, PAIR_W), col(2)), pl.BlockSpec((BLOCK, PAIR_W), prev(2))]
    out_spec = pl.BlockSpec((ch, PAIR_W), lambda p, c: (c, p))
    sink_spec = pl.BlockSpec((2, 1, 128), lambda p, c: (p, 0, 0))
    row_spec = pl.BlockSpec((1, 1, nb, 2, BLOCK), lambda p, c: (p, c, 0, 0, 0))
    return ch, nb, qkv, out_spec, sink_spec, row_spec


def _pair_kv(p, e, ref):
    half = jnp.where(p < A_PAIRS, p // (A_GROUP // 2), e)
    return jnp.where(half == 0, ref[:, :HEAD_DIM], ref[:, HEAD_DIM:])


def _eye():
    return lax.broadcasted_iota(jnp.int32, (BLOCK, BLOCK), 0) == lax.broadcasted_iota(jnp.int32, (BLOCK, BLOCK), 1)


_NT = (((1,), (1,)), ((), ()))
_NN = (((1,), (0,)), ((), ()))
_TN = (((0,), (0,)), ((), ()))


_B_NT = (((2,), (2,)), ((0,), (0,)))
_B_NN = (((2,), (1,)), ((0,), (0,)))


def _bdot(a, b, dn):
    return lax.dot_general(a, b, dn, preferred_element_type=F32)


def _with_prev(first, t3):
    return first[None] if t3.shape[0] == 1 else jnp.concatenate([first[None], t3[:-1]], axis=0)


def _mask_first(s_prev, prev_ok):
    s0 = jnp.where(prev_ok, s_prev[0], NEG)[None]
    return s0 if s_prev.shape[0] == 1 else jnp.concatenate([s0, s_prev[1:]], axis=0)


def _banded_fwd(proj, bias, sinks):
    S = proj.shape[0]
    dh = HEAD_DIM
    ch, nb, qkv, out_spec, sink_spec, row_spec = _band_specs(S)
    bias_spec = pl.BlockSpec((2, BLOCK, 2 * BLOCK), lambda p, c: (p, 0, 0))
    scale = HEAD_DIM ** -0.5

    def body(q_ref, k_ref, kp_ref, v_ref, vp_ref, b_ref, s_ref, o_ref, lse_ref):
        p, c = pl.program_id(0), pl.program_id(1)
        prev_ok = (c % _pair_period(p)) != 0
        for e in range(2):
            lanes = slice(e * dh, (e + 1) * dh)
            q3 = q_ref[:, lanes].reshape(nb, BLOCK, dh)
            k3, v3 = (_pair_kv(p, e, r).reshape(nb, BLOCK, dh) for r in (k_ref, v_ref))
            kp3, vp3 = _with_prev(_pair_kv(p, e, kp_ref), k3), _with_prev(_pair_kv(p, e, vp_ref), v3)
            sink = s_ref[e, :, :1]
            s_cur = _bdot(q3, k3, _B_NT) * scale + b_ref[e, :, BLOCK:][None]
            s_prev = _mask_first(_bdot(q3, kp3, _B_NT) * scale + b_ref[e, :, :BLOCK][None], prev_ok)
            m = jnp.maximum(jnp.max(s_cur, axis=-1, keepdims=True), jnp.max(s_prev, axis=-1, keepdims=True))
            m = jnp.maximum(m, sink)
            p_cur = jnp.exp(s_cur - m)
            p_prev = jnp.exp(s_prev - m)
            l = jnp.sum(p_cur, axis=-1, keepdims=True) + jnp.sum(p_prev, axis=-1, keepdims=True) + jnp.exp(sink - m)
            acc = _bdot(p_cur.astype(MXU_DT), v3, _B_NN) + _bdot(p_prev.astype(MXU_DT), vp3, _B_NN)
            o_ref[:, lanes] = (acc / l).reshape(ch, dh)
            lse = m + jnp.log(l)
            lse_ref[0, 0, :, e:e + 1, :] = jnp.sum(jnp.where(_eye()[None], lse, 0.0), axis=1, keepdims=True)

    return _pcall(
        body, name="banded_fwd", grid=(N_PAIRS, N_CHUNKS),
        out_shape=(jax.ShapeDtypeStruct((S, N_PAIRS * PAIR_W), F32),
                   jax.ShapeDtypeStruct((N_PAIRS, N_CHUNKS, nb, 2, BLOCK), F32)),
        in_specs=qkv + [bias_spec, sink_spec],
        out_specs=(out_spec, row_spec),
        compiler_params=_cparams("parallel", "parallel"),
    )(proj, proj, proj, proj, proj, bias, sinks)


def _banded_bwd(proj, bias_t, sinks, do, dlse):
    S = proj.shape[0]
    dh = HEAD_DIM
    ch, nb, qkv, out_spec, sink_spec, row_spec = _band_specs(S)
    bias_spec = pl.BlockSpec((2, 2 * BLOCK, BLOCK), lambda p, c: (p, 0, 0))
    scale = HEAD_DIM ** -0.5

    def body(q_ref, k_ref, kp_ref, v_ref, vp_ref, b_ref, s_ref, do_ref, dl_ref,
             dq_ref, dk_ref, dv_ref, dkh_ref, dvh_ref, db_ref, ds_ref):
        p, c = pl.program_id(0), pl.program_id(1)

        @pl.when(c == 0)
        def _():
            db_ref[...] = jnp.zeros_like(db_ref)
            ds_ref[...] = jnp.zeros_like(ds_ref)

        prev_ok = (c % _pair_period(p)) != 0
        for e in range(2):
            lanes = slice(e * dh, (e + 1) * dh)
            q3 = q_ref[:, lanes].reshape(nb, BLOCK, dh)
            k3, v3 = (_pair_kv(p, e, r).reshape(nb, BLOCK, dh) for r in (k_ref, v_ref))
            kp3, vp3 = _with_prev(_pair_kv(p, e, kp_ref), k3), _with_prev(_pair_kv(p, e, vp_ref), v3)
            do3 = do_ref[:, lanes].astype(MXU_DT).reshape(nb, BLOCK, dh)
            sink = s_ref[e, :, :1]
            s_cur = _bdot(k3, q3, _B_NT) * scale + b_ref[e, BLOCK:, :][None]
            s_prev = _mask_first(_bdot(kp3, q3, _B_NT) * scale + b_ref[e, :BLOCK, :][None], prev_ok)
            m = jnp.maximum(jnp.max(s_cur, axis=1, keepdims=True), jnp.max(s_prev, axis=1, keepdims=True))
            m = jnp.maximum(m, sink)
            p_cur = jnp.exp(s_cur - m)
            p_prev = jnp.exp(s_prev - m)
            p_sink = jnp.exp(sink - m)
            inv = 1.0 / (jnp.sum(p_cur, axis=1, keepdims=True) + jnp.sum(p_prev, axis=1, keepdims=True) + p_sink)
            p_cur, p_prev, p_sink = p_cur * inv, p_prev * inv, p_sink * inv
            dp_cur = _bdot(v3, do3, _B_NT)
            dp_prev = _bdot(vp3, do3, _B_NT)
            delta = jnp.sum(p_cur * dp_cur, axis=1, keepdims=True) + jnp.sum(p_prev * dp_prev, axis=1, keepdims=True)
            t = dl_ref[0, 0, :, e:e + 1, :] - delta
            ds_cur = p_cur * (dp_cur + t)
            ds_prev = p_prev * (dp_prev + t)
            dsink = jnp.sum(jnp.sum(p_sink * t, axis=0), axis=-1, keepdims=True)
            ds_ref[e] += jnp.broadcast_to(dsink, (1, 128))
            db_ref[e, :BLOCK, :] += jnp.sum(ds_prev, axis=0)
            db_ref[e, BLOCK:, :] += jnp.sum(ds_cur, axis=0)
            dsb_cur = (ds_cur * scale).astype(MXU_DT)
            dsb_prev = (ds_prev * scale).astype(MXU_DT)
            dk_prev = _bdot(dsb_prev, q3, _B_NN)
            dv_prev = _bdot(p_prev.astype(MXU_DT), do3, _B_NN)

            def shifted(t3):
                z = jnp.zeros((1, BLOCK, dh), F32)
                return z if nb == 1 else jnp.concatenate([t3[1:], z], axis=0)

            dk_ref[:, lanes] = (_bdot(dsb_cur, q3, _B_NN) + shifted(dk_prev)).reshape(ch, dh)
            dv_ref[:, lanes] = (_bdot(p_cur.astype(MXU_DT), do3, _B_NN) + shifted(dv_prev)).reshape(ch, dh)
            dkh_ref[0, 0, :, lanes] = dk_prev[0]
            dvh_ref[0, 0, :, lanes] = dv_prev[0]
            for b in range(nb):
                dq_ref[b * BLOCK:(b + 1) * BLOCK, lanes] = (
                    lax.dot_general(dsb_cur[b], k3[b], _TN, preferred_element_type=F32)
                    + lax.dot_general(dsb_prev[b], kp3[b], _TN, preferred_element_type=F32))

    halo_spec = pl.BlockSpec((1, 1, BLOCK, PAIR_W), lambda p, c: (p, c, 0, 0))
    big = jax.ShapeDtypeStruct((S, N_PAIRS * PAIR_W), F32)
    halo = jax.ShapeDtypeStruct((N_PAIRS, N_CHUNKS, BLOCK, PAIR_W), F32)
    return _pcall(
        body, name="banded_bwd", grid=(N_PAIRS, N_CHUNKS),
        out_shape=(big, big, big, halo, halo, jax.ShapeDtypeStruct(bias_t.shape, F32),
                   jax.ShapeDtypeStruct(sinks.shape, F32)),
        in_specs=qkv + [bias_spec, sink_spec, out_spec, row_spec],
        out_specs=(out_spec, out_spec, out_spec, halo_spec, halo_spec, bias_spec, sink_spec),
        compiler_params=_cparams("arbitrary", "arbitrary"),
    )(proj, proj, proj, proj, proj, bias_t, sinks, do, dlse)


def _halo_fold(t, halo, name):
    S, width = t.shape
    nb = S // N_CHUNKS // BLOCK

    def body(t_ref, h_ref, o_ref):
        o_ref[:N_CHUNKS - 1, 0] = t_ref[:N_CHUNKS - 1, 0] + h_ref[0, 1:]
        o_ref[N_CHUNKS - 1:, 0] = t_ref[N_CHUNKS - 1:, 0]

    blk = pl.BlockSpec((N_CHUNKS, 1, BLOCK, PAIR_W), lambda p: (0, nb - 1, 0, p))
    return _pcall(
        body, name=name, grid=(N_PAIRS,),
        out_shape=jax.ShapeDtypeStruct((N_CHUNKS, nb, BLOCK, width), t.dtype),
        in_specs=[blk, pl.BlockSpec((1, N_CHUNKS, BLOCK, PAIR_W), lambda p: (p, 0, 0, 0))],
        out_specs=blk, input_output_aliases={0: 0},
        compiler_params=_cparams("parallel"),
    )(t.reshape(N_CHUNKS, nb, BLOCK, width), halo).reshape(S, width)


def _causal_mask(T):
    return lax.broadcasted_iota(jnp.int32, (T, T), 0) <= lax.broadcasted_iota(jnp.int32, (T, T), 1)


LOG2E = math.log2(math.e)
FLASH_SPLIT = 2
FLASH_ONES_ROWS = 16


def _mla_q_proj(cqn, wq_t, cos_t, sin_t, *, name):
    S, R = cqn.shape
    H, dqk, _ = wq_t.shape
    nq, half, T = cos_t.shape

    def body(x_ref, w_ref, c_ref, s_ref, o_ref):
        x = x_ref[...]
        for h in range(H):
            qt = lax.dot_general(w_ref[h], x, _NT, preferred_element_type=F32)
            t1, t2 = qt[C_NOPE:C_NOPE + half], qt[C_NOPE + half:]
            o_ref[h, 0, :C_NOPE] = qt[:C_NOPE].astype(MXU_DT)
            o_ref[h, 0, C_NOPE:C_NOPE + half] = (t1 * c_ref[0] - t2 * s_ref[0]).astype(MXU_DT)
            o_ref[h, 0, C_NOPE + half:] = (t1 * s_ref[0] + t2 * c_ref[0]).astype(MXU_DT)

    tab = pl.BlockSpec((1, half, T), lambda i: (i, 0, 0))
    return _pcall(
        body, name=name, grid=(nq,),
        out_shape=jax.ShapeDtypeStruct((H, nq, dqk, T), MXU_DT),
        in_specs=[pl.BlockSpec((T, R), lambda i: (i, 0)), pl.BlockSpec((H, dqk, R), lambda i: (0, 0, 0)), tab, tab],
        out_specs=pl.BlockSpec((H, 1, dqk, T), lambda i: (0, i, 0, 0)),
        compiler_params=_cparams("parallel"),
    )(cqn, wq_t, cos_t, sin_t)


def _mla_q_proj_bwd(dqt, cqn, wq_t, cos_t, sin_t, *, name):
    S, R = cqn.shape
    H, dqk, _ = wq_t.shape
    nq, half, T = cos_t.shape

    def body(g_ref, x_ref, w_ref, c_ref, s_ref, dx_ref, dw_ref):
        @pl.when(pl.program_id(0) == 0)
        def _():
            dw_ref[...] = jnp.zeros_like(dw_ref)

        x = x_ref[...]
        acc = jnp.zeros((T, R), F32)
        for h in range(H):
            g = g_ref[h, 0]
            g1, g2 = g[C_NOPE:C_NOPE + half], g[C_NOPE + half:]
            gq = jnp.concatenate([g[:C_NOPE], g1 * c_ref[0] + g2 * s_ref[0], g2 * c_ref[0] - g1 * s_ref[0]],
                                 axis=0).astype(MXU_DT)
            acc = acc + lax.dot_general(gq, w_ref[h], _TN, preferred_element_type=F32)
            dw_ref[h] += lax.dot_general(gq, x, _NN, preferred_element_type=F32)
        dx_ref[...] = acc

    tab = pl.BlockSpec((1, half, T), lambda i: (i, 0, 0))
    whole = pl.BlockSpec((H, dqk, R), lambda i: (0, 0, 0))
    return _pcall(
        body, name=name, grid=(nq,),
        out_shape=(jax.ShapeDtypeStruct((S, R), F32), jax.ShapeDtypeStruct((H, dqk, R), F32)),
        in_specs=[pl.BlockSpec((H, 1, dqk, T), lambda i: (0, i, 0, 0)), pl.BlockSpec((T, R), lambda i: (i, 0)), whole, tab, tab],
        out_specs=(pl.BlockSpec((T, R), lambda i: (i, 0)), whole),
        compiler_params=_cparams("arbitrary"),
    )(dqt, cqn, wq_t, cos_t, sin_t)


def _mla_kv_proj(ckvn, wk, wv, kr, kr_t, *, name):
    S, R = ckvn.shape
    H = wk.shape[0]
    nq, dr, T = kr_t.shape
    dqk = C_NOPE + dr
    wk_t, wv_t = wk.transpose(0, 2, 1), wv.transpose(0, 2, 1)

    def body(x_ref, wk_ref, wv_ref, wkt_ref, wvt_ref, kr_ref, krt_ref, kt_ref, vt_ref, kn_ref, vn_ref):
        x = x_ref[...]
        krt, krn = krt_ref[0].astype(MXU_DT), kr_ref[...].astype(MXU_DT)
        ones = jnp.where(lax.broadcasted_iota(jnp.int32, (FLASH_ONES_ROWS, T), 0) == 0, 1.0, 0.0).astype(MXU_DT)
        for h in range(H):
            kt_ref[h, 0, :C_NOPE] = lax.dot_general(wkt_ref[h], x, _NT, preferred_element_type=F32).astype(MXU_DT)
            kt_ref[h, 0, C_NOPE:] = krt
            vt_ref[h, 0, :C_V] = lax.dot_general(wvt_ref[h], x, _NT, preferred_element_type=F32).astype(MXU_DT)
            vt_ref[h, 0, C_V:] = ones
            kn_ref[h, :, :C_NOPE] = lax.dot_general(x, wk_ref[h], _NN, preferred_element_type=F32).astype(MXU_DT)
            kn_ref[h, :, C_NOPE:] = krn
            vn_ref[h] = lax.dot_general(x, wv_ref[h], _NN, preferred_element_type=F32).astype(MXU_DT)

    w_spec = pl.BlockSpec((H, R, C_NOPE), lambda i: (0, 0, 0))
    wt_spec = pl.BlockSpec((H, C_NOPE, R), lambda i: (0, 0, 0))
    return _pcall(
        body, name=name, grid=(nq,),
        out_shape=(jax.ShapeDtypeStruct((H, nq, dqk, T), MXU_DT),
                   jax.ShapeDtypeStruct((H, nq, C_V + FLASH_ONES_ROWS, T), MXU_DT),
                   jax.ShapeDtypeStruct((H, S, dqk), MXU_DT), jax.ShapeDtypeStruct((H, S, C_V), MXU_DT)),
        in_specs=[pl.BlockSpec((T, R), lambda i: (i, 0)), w_spec, w_spec, wt_spec, wt_spec,
                  pl.BlockSpec((T, dr), lambda i: (i, 0)), pl.BlockSpec((1, dr, T), lambda i: (i, 0, 0))],
        out_specs=(pl.BlockSpec((H, 1, dqk, T), lambda i: (0, i, 0, 0)),
                   pl.BlockSpec((H, 1, C_V + FLASH_ONES_ROWS, T), lambda i: (0, i, 0, 0)),
                   pl.BlockSpec((H, T, dqk), lambda i: (0, i, 0)), pl.BlockSpec((H, T, C_V), lambda i: (0, i, 0))),
        compiler_params=_cparams("parallel"),
    )(ckvn, wk, wv, wk_t, wv_t, kr, kr_t)


def _mla_kv_proj_bwd(dkt, dvt, ckvn, wk, wv, *, name):
    S, R = ckvn.shape
    H = wk.shape[0]
    _, nq, dqk, T = dkt.shape
    dr = dqk - C_NOPE
    wk_t, wv_t = wk.transpose(0, 2, 1), wv.transpose(0, 2, 1)

    def body(gk_ref, gv_ref, x_ref, wkt_ref, wvt_ref, dx_ref, dwk_ref, dwv_ref, dkr_ref):
        @pl.when(pl.program_id(0) == 0)
        def _():
            dwk_ref[...] = jnp.zeros_like(dwk_ref)
            dwv_ref[...] = jnp.zeros_like(dwv_ref)

        x = x_ref[...]
        acc = jnp.zeros((T, R), F32)
        dkr = jnp.zeros((dr, T), F32)
        for h in range(H):
            gk = gk_ref[h, 0, :C_NOPE].astype(MXU_DT)
            gv = gv_ref[h, 0].astype(MXU_DT)
            acc = acc + (lax.dot_general(gk, wkt_ref[h], _TN, preferred_element_type=F32)
                         + lax.dot_general(gv, wvt_ref[h], _TN, preferred_element_type=F32))
            dwk_ref[h] += lax.dot_general(gk, x, _NN, preferred_element_type=F32)
            dwv_ref[h] += lax.dot_general(gv, x, _NN, preferred_element_type=F32)
            dkr = dkr + gk_ref[h, 0, C_NOPE:]
        dx_ref[...] = acc
        dkr_ref[0] = dkr

    wt_spec = pl.BlockSpec((H, C_NOPE, R), lambda i: (0, 0, 0))
    return _pcall(
        body, name=name, grid=(nq,),
        out_shape=(jax.ShapeDtypeStruct((S, R), F32), jax.ShapeDtypeStruct((H, C_NOPE, R), F32),
                   jax.ShapeDtypeStruct((H, C_V, R), F32), jax.ShapeDtypeStruct((nq, dr, T), F32)),
        in_specs=[pl.BlockSpec((H, 1, dqk, T), lambda i: (0, i, 0, 0)), pl.BlockSpec((H, 1, C_V, T), lambda i: (0, i, 0, 0)),
                  pl.BlockSpec((T, R), lambda i: (i, 0)), wt_spec, wt_spec],
        out_specs=(pl.BlockSpec((T, R), lambda i: (i, 0)), wt_spec, wt_spec, pl.BlockSpec((1, dr, T), lambda i: (i, 0, 0))),
        compiler_params=_cparams("arbitrary"),
    )(dkt, dvt, ckvn, wk_t, wv_t)


def _mla_out_proj(ot, w_o, h, *, name):
    H, nq, dv, T = ot.shape
    D = w_o.shape[2]

    def body(o_ref, w_ref, h_ref, out_ref):
        acc = h_ref[...]
        for hd in range(H):
            acc = acc + lax.dot_general(o_ref[hd, 0].astype(MXU_DT), w_ref[hd], _TN, preferred_element_type=F32)
        out_ref[...] = acc

    row = pl.BlockSpec((T, D), lambda i: (i, 0))
    return _pcall(
        body, name=name, grid=(nq,),
        out_shape=jax.ShapeDtypeStruct(h.shape, F32),
        in_specs=[pl.BlockSpec((H, 1, dv, T), lambda i: (0, i, 0, 0)), pl.BlockSpec((H, dv, D), lambda i: (0, 0, 0)), row],
        out_specs=row,
        compiler_params=_cparams("parallel"),
    )(ot, w_o, h)


def _mla_out_proj_bwd(dh, ot, w_o, *, name):
    H, nq, dv, T = ot.shape
    D = w_o.shape[2]

    def body(dh_ref, o_ref, w_ref, dot_ref, del_ref, dw_ref):
        @pl.when(pl.program_id(0) == 0)
        def _():
            dw_ref[...] = jnp.zeros_like(dw_ref)

        dhb = dh_ref[...].astype(MXU_DT)
        for hd in range(H):
            o = o_ref[hd, 0]
            d = lax.dot_general(w_ref[hd], dhb, _NT, preferred_element_type=F32)
            dot_ref[hd, 0] = d
            del_ref[hd, 0] = jnp.sum(d * o, axis=0, keepdims=True)
            dw_ref[hd] += lax.dot_general(o.astype(MXU_DT), dhb, _NN, preferred_element_type=F32)

    tile = pl.BlockSpec((H, 1, dv, T), lambda i: (0, i, 0, 0))
    whole = pl.BlockSpec((H, dv, D), lambda i: (0, 0, 0))
    return _pcall(
        body, name=name, grid=(nq,),
        out_shape=(jax.ShapeDtypeStruct(ot.shape, F32), jax.ShapeDtypeStruct((H, nq, 1, T), F32),
                   jax.ShapeDtypeStruct(w_o.shape, F32)),
        in_specs=[pl.BlockSpec((T, D), lambda i: (i, 0)), tile, whole],
        out_specs=(tile, pl.BlockSpec((H, 1, 1, T), lambda i: (0, i, 0, 0)), whole),
        compiler_params=_cparams("arbitrary"),
    )(dh, ot, w_o)


def _flash_fwd(qt, k, vt1):
    H, nq, dqk, T = qt.shape
    S = k.shape[1]
    dva = vt1.shape[2]
    dv = dva - FLASH_ONES_ROWS
    scale = dqk ** -0.5
    c = scale * LOG2E
    th = T // FLASH_SPLIT

    def body(qt_ref, k_ref, vt_ref, ot_ref, lse_ref, sa_ref, sb_ref):
        i = pl.program_id(1)

        def scores(j):
            kb = k_ref[0, pl.ds(pl.multiple_of(j * T, T), T), :]
            return lax.dot_general(kb, qt_ref[0, 0], _NN, preferred_element_type=F32)

        def softmax_pv(s_ref, j, carry, masked):
            m, acc = carry
            raw = s_ref[...]
            if masked:
                raw = jnp.where(_causal_mask(T), raw, NEG)
            m_new = jnp.maximum(m, jnp.max(raw, axis=0, keepdims=True))
            alpha = jnp.exp2((m - m_new) * c)
            pb = jnp.exp2((raw - m_new) * c).astype(MXU_DT)
            acc = acc * alpha + lax.dot_general(vt_ref[0, j], pb, _NN, preferred_element_type=F32)
            return m_new, acc

        def pair(p, carry):
            j = 2 * p
            sb_ref[...] = scores(j + 1)
            carry = softmax_pv(sa_ref, j, carry, False)
            sa_ref[...] = scores(j + 2)
            return softmax_pv(sb_ref, j + 1, carry, False)

        def even_tail(carry):
            return softmax_pv(sa_ref, i, carry, True)

        def odd_tail(carry):
            sb_ref[...] = scores(i)
            carry = softmax_pv(sa_ref, i - 1, carry, False)
            return softmax_pv(sb_ref, i, carry, True)

        sa_ref[...] = scores(0)
        carry = lax.fori_loop(0, i // 2, pair, (jnp.full((1, T), NEG, F32), jnp.zeros((dva, T), F32)))
        m, acc = lax.cond(i % 2 == 0, even_tail, odd_tail, carry)
        l = acc[dv:dv + 1]
        ot_ref[0, 0] = acc[:dv] / l
        lse_ref[0, 0] = m * scale + jnp.log(l)

    return _pcall(
        body, name="flash_fwd", grid=(H, nq),
        out_shape=(jax.ShapeDtypeStruct((H, nq, dv, T), F32), jax.ShapeDtypeStruct((H, nq, 1, T), F32)),
        in_specs=[pl.BlockSpec((1, 1, dqk, T), lambda h, i: (h, i, 0, 0)),
                  pl.BlockSpec((1, S, dqk), lambda h, i: (h, 0, 0)),
                  pl.BlockSpec((1, nq, dva, T), lambda h, i: (h, 0, 0, 0))],
        out_specs=(pl.BlockSpec((1, 1, dv, T), lambda h, i: (h, i, 0, 0)),
                   pl.BlockSpec((1, 1, 1, T), lambda h, i: (h, i, 0, 0))),
        scratch_shapes=[pltpu.VMEM((T, T), F32), pltpu.VMEM((T, T), F32)],
        compiler_params=_cparams("parallel", "parallel"),
    )(qt, k, vt1)


def _flash_delta(ot, dot):
    H, nq, dv, T = ot.shape

    def body(o_ref, do_ref, d_ref):
        d_ref[0, 0] = jnp.sum(o_ref[0, 0] * do_ref[0, 0], axis=0, keepdims=True)

    spec = pl.BlockSpec((1, 1, dv, T), lambda h, i: (h, i, 0, 0))
    return _pcall(
        body, name="flash_delta", grid=(H, nq),
        out_shape=jax.ShapeDtypeStruct((H, nq, 1, T), F32),
        in_specs=[spec, spec], out_specs=pl.BlockSpec((1, 1, 1, T), lambda h, i: (h, i, 0, 0)),
        compiler_params=_cparams("parallel", "parallel"),
    )(ot, dot)


def _flash_bwd(qt, k, kt, v, dot, lse, delta):
    H, nq, dqk, T = qt.shape
    dv_ = v.shape[2]
    scale = dqk ** -0.5
    c = scale * LOG2E
    th = T // FLASH_SPLIT

    def body(qt_ref, k_ref, kt_ref, v_ref, dot_ref, lse_ref, del_ref, dqt_ref, dkt_ref, dvt_ref,
             sa_ref, pa_ref, sb_ref, pb_ref):
        j = pl.program_id(1)

        @pl.when(j == 0)
        def _():
            dqt_ref[...] = jnp.zeros_like(dqt_ref)

        n_un = nq - 1 - j

        def issue(i, s_ref, dp_ref):
            s_ref[...] = lax.dot_general(k_ref[0], qt_ref[0, i], _NN, preferred_element_type=F32)
            dp_ref[...] = lax.dot_general(v_ref[0], dot_ref[0, i].astype(MXU_DT), _NN, preferred_element_type=F32)

        def consume(i, s_ref, dp_ref, carry, masked):
            dkt, dvt = carry
            raw = s_ref[...]
            if masked:
                raw = jnp.where(_causal_mask(T), raw, NEG)
            p = jnp.exp2(raw * c - lse_ref[0, i] * LOG2E)
            dsb = (p * (dp_ref[...] - del_ref[0, i])).astype(MXU_DT)
            dvt = dvt + lax.dot_general(dot_ref[0, i].astype(MXU_DT), p.astype(MXU_DT), _NT, preferred_element_type=F32)
            dkt = dkt + lax.dot_general(qt_ref[0, i], dsb, _NT, preferred_element_type=F32)
            dqt_ref[0, i] += lax.dot_general(kt_ref[0, 0], dsb, _NN, preferred_element_type=F32) * scale
            return dkt, dvt

        def pair(p, carry):
            i0 = j + 1 + 2 * p
            issue(i0 + 1, sb_ref, pb_ref)
            carry = consume(i0, sa_ref, pa_ref, carry, False)
            issue(jnp.where(2 * p + 2 < n_un, i0 + 2, j), sa_ref, pa_ref)
            return consume(i0 + 1, sb_ref, pb_ref, carry, False)

        def even_tail(carry):
            return consume(j, sa_ref, pa_ref, carry, True)

        def odd_tail(carry):
            issue(j, sb_ref, pb_ref)
            carry = consume(nq - 1, sa_ref, pa_ref, carry, False)
            return consume(j, sb_ref, pb_ref, carry, True)

        issue(jnp.where(n_un > 0, j + 1, j), sa_ref, pa_ref)
        carry = lax.fori_loop(0, n_un // 2, pair, (jnp.zeros((dqk, T), F32), jnp.zeros((dv_, T), F32)))
        dkt, dvt = lax.cond(n_un % 2 == 0, even_tail, odd_tail, carry)
        dkt_ref[0, 0] = dkt * scale
        dvt_ref[0, 0] = dvt

    whole = lambda d: pl.BlockSpec((1, nq, d, T), lambda h, j: (h, 0, 0, 0))
    tile_t = lambda d: pl.BlockSpec((1, 1, d, T), lambda h, j: (h, j, 0, 0))
    return _pcall(
        body, name="flash_bwd", grid=(H, nq),
        out_shape=(jax.ShapeDtypeStruct((H, nq, dqk, T), F32), jax.ShapeDtypeStruct((H, nq, dqk, T), F32),
                   jax.ShapeDtypeStruct((H, nq, dv_, T), F32)),
        in_specs=[whole(dqk),
                  pl.BlockSpec((1, T, dqk), lambda h, j: (h, j, 0)),
                  tile_t(dqk),
                  pl.BlockSpec((1, T, dv_), lambda h, j: (h, j, 0)),
                  whole(dv_), whole(1), whole(1)],
        out_specs=(whole(dqk), tile_t(dqk), tile_t(dv_)),
        scratch_shapes=[pltpu.VMEM((T, T), F32) for _ in range(4)],
        compiler_params=_cparams("arbitrary", "arbitrary"),
    )(qt, k, kt, v, dot, lse, delta)


def _adamw(parts, w, m, v, *, name, tr=512):
    P, R, C = parts.shape
    tr = min(tr, R)
    assert R % tr == 0

    def body(p_ref, w_ref, m_ref, v_ref, g_ref, d_ref, m2_ref, v2_ref):
        g = p_ref[0].astype(F32)
        for s in range(1, P):
            g = g + p_ref[s].astype(F32)
        m2 = ADAM_B1 * m_ref[...] + (1.0 - ADAM_B1) * g
        v2 = ADAM_B2 * v_ref[...] + (1.0 - ADAM_B2) * jnp.square(g)
        m_hat = m2 / (1.0 - ADAM_B1 ** ADAM_STEP)
        v_hat = v2 / (1.0 - ADAM_B2 ** ADAM_STEP)
        g_ref[...] = g
        d_ref[...] = -ADAM_LR * (m_hat / (jnp.sqrt(v_hat) + ADAM_EPS) + ADAM_WD * w_ref[...])
        m2_ref[...] = m2
        v2_ref[...] = v2

    row = pl.BlockSpec((tr, C), lambda i: (i, 0))
    out = jax.ShapeDtypeStruct((R, C), F32)
    return _pcall(
        body, name=name, grid=(R // tr,),
        out_shape=(out, out, out, out),
        in_specs=[pl.BlockSpec((P, tr, C), lambda i: (0, i, 0)), row, row, row],
        out_specs=(row, row, row, row),
        compiler_params=_cparams("parallel"),
    )(parts, w, m, v)


def _bias_tables():
    i = np.arange(BLOCK)[:, None]
    j = np.arange(2 * BLOCK)[None, :]
    dist = i + BLOCK - j
    out = []
    for dil, max_dist in [(1, A_WINDOW - 1)] + [(d, w // d) for w, d in B_BRANCHES]:
        n = np.maximum(dist, 0) * dil
        max_exact = NUM_BUCKETS // 2
        nf = np.maximum(n, 1).astype(np.float64)
        val = np.log(nf / max_exact) / math.log(MAX_DISTANCE / max_exact) * (NUM_BUCKETS - max_exact)
        inband = (dist >= 0) & (dist <= max_dist)
        frac = np.abs(val - np.round(val))
        last = NUM_BUCKETS - 1 - max_exact
        assert np.all((frac > 2e-5) | (n <= max_exact) | (val >= last) | ~inband)
        large = max_exact + val.astype(np.int64)
        bucket = np.where(n < max_exact, n, np.minimum(large, NUM_BUCKETS - 1))
        onehot = (bucket[..., None] == np.arange(NUM_BUCKETS)).astype(np.float32)
        out.append((onehot.reshape(-1, NUM_BUCKETS), inband))
    return out


def _make_bias(rel_bias):
    tabs = _bias_tables()
    groups = [(0, A_Q_HEADS)] + [(A_Q_HEADS + g * B_HPB, B_HPB) for g in range(len(B_BRANCHES))]
    parts = []
    for (onehot, inband), (h0, nh) in zip(tabs, groups):
        b = jnp.dot(jnp.asarray(onehot), rel_bias[:, h0:h0 + nh], precision=lax.Precision.HIGHEST)
        b = b.reshape(BLOCK, 2 * BLOCK, nh)
        b = jnp.where(jnp.asarray(inband)[..., None], b, NEG)
        parts.append(b.transpose(2, 0, 1))
    return jnp.concatenate(parts, axis=0)


A_W = A_Q_HEADS * HEAD_DIM
B_W = B_HPB * HEAD_DIM


def _perm_rows(x, d):
    return x if d == 1 else x.reshape(x.shape[0] // d, d, -1).transpose(1, 0, 2).reshape(x.shape)


def _unperm_rows(x, d):
    return x if d == 1 else x.reshape(d, x.shape[0] // d, -1).transpose(1, 0, 2).reshape(x.shape)


def _even_post(o_all, lse):
    S = o_all.shape[0]
    outs, lses = [], []
    for g, (_, d) in enumerate(B_BRANCHES):
        w0 = A_W + g * B_W
        outs.append(_unperm_rows(o_all[:, w0:w0 + B_W], d))
        lg = lse[A_PAIRS + 2 * g:A_PAIRS + 2 * g + 2].transpose(1, 2, 4, 0, 3).reshape(S, B_HPB)
        lses.append(_unperm_rows(lg, d))
    wts = jax.nn.softmax(jnp.stack(lses), axis=0)
    widen = jnp.asarray(np.kron(np.eye(B_HPB), np.ones((1, HEAD_DIM))), F32)
    out_b = sum(jnp.dot(wts[g], widen, precision=lax.Precision.HIGHEST) * outs[g] for g in range(len(B_BRANCHES)))
    return jnp.concatenate([o_all[:, :A_W], out_b], axis=-1)


def _rope_tables(S, r):
    inv = ROPE_THETA ** (-jnp.arange(0, r, 2, dtype=jnp.float32) / r)
    ang = jnp.arange(S, dtype=jnp.float32)[:, None] * inv[None, :]
    return jnp.cos(ang), jnp.sin(ang)


def _rope(t):
    S, r = t.shape[1], t.shape[-1]
    shape = (1, S) + (1,) * (t.ndim - 3) + (r // 2,)
    cos, sin = (a.reshape(shape) for a in _rope_tables(S, r))
    t1, t2 = t[..., :r // 2], t[..., r // 2:]
    return jnp.concatenate([t1 * cos - t2 * sin, t1 * sin + t2 * cos], axis=-1)


def _mla_pre(q_lin, kv_lin, kr_raw):
    S = q_lin.shape[0]
    q = q_lin.reshape(1, S, C_HEADS, C_QK)
    qf = jnp.concatenate([q[..., :C_NOPE], _rope(q[..., C_NOPE:])], axis=-1)[0]
    kv = kv_lin.reshape(S, C_HEADS, C_NOPE + C_V)
    kr = _rope(kr_raw[None])[0]
    kf = jnp.concatenate([kv[..., :C_NOPE], jnp.broadcast_to(kr[:, None, :], (S, C_HEADS, C_ROPE))], axis=-1)
    return qf, kf, kv[..., C_NOPE:]


def _to_tiles_t(t, T):
    S, H, d = t.shape
    return t.reshape(S // T, T, H, d).transpose(2, 0, 3, 1)


def _from_tiles_t(t):
    H, n, d, T = t.shape
    return t.transpose(1, 3, 0, 2).reshape(n * T, H, d)


_BIG = (("w_in_ab", 2), ("w_out_ab", 2), ("w_down_c", 1), ("w_uq_c", 2), ("w_ukv_c", 2), ("w_o_c", 2),
        ("w_mlp_up", 2), ("w_mlp_down", 1))
_ROW_ALIGN = 512


def _pack_rows(arrs):
    rows = [a.reshape(-1, 128) for a in arrs]
    n = sum(r.shape[0] for r in rows)
    pad = (-n) % _ROW_ALIGN
    if pad:
        rows.append(jnp.zeros((pad, 128), rows[0].dtype))
    return jnp.concatenate(rows, axis=0)


def _pack_rows_per_device(arrs):
    rows = [a.reshape(N_DEV, -1, 128) for a in arrs]
    n = sum(r.shape[1] for r in rows)
    pad = (-n) % _ROW_ALIGN
    if pad:
        rows.append(jnp.zeros((N_DEV, pad, 128), rows[0].dtype))
    return jnp.concatenate(rows, axis=1)


def _unpack_rows(buf, shapes):
    out, r0 = [], 0
    for shp in shapes:
        n = math.prod(shp) // 128
        out.append(buf[..., r0:r0 + n, :].reshape(buf.shape[:-2] + tuple(shp)))
        r0 += n
    return out


def _layer_tensors(l):
    att = [("w_in_ab", l // 2), ("w_out_ab", l // 2)] if l % 2 == 0 else \
          [("w_down_c", l // 2), ("w_uq_c", l // 2), ("w_ukv_c", l // 2), ("w_o_c", l // 2)]
    return att + [("w_mlp_up", l), ("w_mlp_down", l)]


def _exchange_groups():
    first = _layer_tensors(0)
    return [first[:-2], first[-2:]] + [_layer_tensors(l) for l in range(1, DEPTH)]


def _gathered_to_full(g, axis):
    if axis == 2:
        return g.transpose(1, 0, 2).reshape(g.shape[1], N_DEV * g.shape[2])
    return g.reshape(N_DEV * g.shape[1], g.shape[2])


def _full_to_shards(t, axis):
    a, b = t.shape
    if axis == 2:
        return t.reshape(a, N_DEV, b // N_DEV).transpose(1, 0, 2)
    return t.reshape(N_DEV, a // N_DEV, b)


def _pad_rows8(a):
    flat = a.reshape(-1)
    n = -(-flat.shape[0] // 1024) * 1024
    return jnp.pad(flat, (0, n - flat.shape[0])).reshape(-1, 128)


def _even_fwd(xn, h, w_in, w_out, bias, sinks_row, l, ride=None):
    c0 = A_IN + 3 * B_W
    proj = lax.empty((xn.shape[0], w_in.shape[1]), MXU_DT)
    proj = _matmul(xn, w_in[:, :c0], out_dtype=MXU_DT, tm=1024, tn=512, name=f"even_in_{l}", ride=ride, into=(proj, 0))
    proj, landed = proj if ride is not None else (proj, None)
    for g, (_, d) in list(enumerate(B_BRANCHES))[1:]:
        col0 = A_IN + 3 * B_W * g
        proj = _matmul(_perm_rows(xn, d), w_in[:, col0:col0 + 3 * B_W], out_dtype=MXU_DT, tm=1024, tn=3 * B_W,
                       name=f"even_in_dil{d}_{l}", into=(proj, col0))
    o_all, lse = _banded_fwd(proj, bias, sinks_row)
    attn, post_vjp = jax.vjp(_even_post, o_all, lse)
    attn = attn.astype(MXU_DT)
    h1 = _matmul(attn, w_out, epi='add', extra=h, tm=1024, tn=512, name=f"even_out_{l}")
    return h1, (proj, attn, post_vjp), landed


def _even_bwd(dh, xn, ctx, w_in, w_out, bias, sinks_row, l, norm, ride=None):
    proj, attn, post_vjp = ctx
    d_attn = _matmul(dh, w_out, trans_b=True, tm=1024, tn=768, name=f"even_out_dx_{l}")
    g_w_out = _matmul_tn(attn, dh, tk=768, tn=512, name=f"even_out_dw_{l}")
    do_all, dlse = post_vjp(d_attn)
    dq, dk, dv, dkh, dvh, dbias_t, dsinks = _banded_bwd(proj, bias.transpose(0, 2, 1), sinks_row, do_all, dlse)
    dbias = dbias_t.transpose(0, 2, 1)
    dk = _halo_fold(dk, dkh, f"halo_k_{l}")
    dv = _halo_fold(dv, dvh, f"halo_v_{l}")

    def kv_sum(t):
        heads = [t[:, i * HEAD_DIM:(i + 1) * HEAD_DIM] for i in range(A_Q_HEADS)]
        return jnp.concatenate([sum(heads[j * A_GROUP:(j + 1) * A_GROUP]) for j in range(A_KV_HEADS)], axis=1)

    groups = [jnp.concatenate([dq[:, :A_W], kv_sum(dk), kv_sum(dv)], axis=1).astype(MXU_DT)]
    for g, (_, d) in enumerate(B_BRANCHES):
        cols = slice(A_W + g * B_W, A_W + (g + 1) * B_W)
        grp = jnp.concatenate([dq[:, cols], dk[:, cols], dv[:, cols]], axis=1).astype(MXU_DT)
        groups.append(_unperm_rows(grp, d))
    dproj = jnp.concatenate(groups, axis=1)
    g_w_in = _matmul_tn(xn, dproj, tk=512, tn=1024, name=f"even_in_dw_{l}")
    res = _matmul(dproj, w_in, trans_b=True, epi='norm_bwd', norm=norm, tm=512, tn=w_in.shape[0],
                  name=f"even_in_dx_{l}", ride=ride)
    dh_new, g_norm, landed = res if ride is not None else (*res, None)
    return dh_new, g_norm[0], g_w_in, g_w_out, dbias, dsinks[:A_Q_HEADS, 0, 0], landed


def _mla_fwd(xn, h, w_down, q_norm, w_uq, kv_norm, w_ukv, w_o, l):
    S = xn.shape[0]
    T = min(FLASH_T, S)
    down = _matmul(xn, w_down, tm=1024, tn=768, name=f"mla_down_{l}")
    c_q, c_kv, kr_raw = down[:, :C_Q_RANK], down[:, C_Q_RANK:C_Q_RANK + C_KV_RANK], down[:, C_Q_RANK + C_KV_RANK:C_DOWN]
    cqn = _rmsnorm(c_q, q_norm, out_dtype=MXU_DT, name=f"mla_qnorm_{l}")
    ckvn = _rmsnorm(c_kv, kv_norm, out_dtype=MXU_DT, name=f"mla_kvnorm_{l}")
    cos, sin = _rope_tables(S, C_ROPE)
    to_t = lambda t: t.reshape(S // T, T, -1).transpose(0, 2, 1)
    qt = _mla_q_proj(cqn, w_uq.T.reshape(C_HEADS, C_QK, C_Q_RANK), to_t(cos), to_t(sin), name=f"mla_uq_{l}")
    w_kv = w_ukv.reshape(C_KV_RANK, C_HEADS, C_NOPE + C_V).transpose(1, 0, 2)
    kr = _rope(kr_raw[None])[0]
    kt, vt1, kn, vn = _mla_kv_proj(ckvn, w_kv[..., :C_NOPE], w_kv[..., C_NOPE:], kr, to_t(kr), name=f"mla_ukv_{l}")
    ot, lse = _flash_fwd(qt, kn, vt1)
    h1 = _mla_out_proj(ot, w_o.reshape(C_HEADS, C_V, -1), h, name=f"mla_o_{l}")
    return h1, (c_q, c_kv, cqn, ckvn, qt, kn, kt, vn, ot, lse)


def _mla_bwd(dh, xn, ctx, w_down, q_norm, w_uq, kv_norm, w_ukv, w_o, l, norm):
    c_q, c_kv, cqn, ckvn, qt, kn, kt, vn, ot, lse = ctx
    S = xn.shape[0]
    T = qt.shape[-1]
    dot, delta, dw_o = _mla_out_proj_bwd(dh, ot, w_o.reshape(C_HEADS, C_V, -1), name=f"mla_o_bwd_{l}")
    g_w_o = dw_o.reshape(w_o.shape)
    dqt, dkt, dvt = _flash_bwd(qt, kn, kt, vn, dot, lse, delta)
    cos, sin = _rope_tables(S, C_ROPE)
    to_t = lambda t: t.reshape(S // T, T, -1).transpose(0, 2, 1)
    dcqn, dwq_t = _mla_q_proj_bwd(dqt, cqn, w_uq.T.reshape(C_HEADS, C_QK, C_Q_RANK), to_t(cos), to_t(sin),
                                  name=f"mla_uq_bwd_{l}")
    g_w_uq = dwq_t.reshape(C_HEADS * C_QK, C_Q_RANK).T
    w_kv = w_ukv.reshape(C_KV_RANK, C_HEADS, C_NOPE + C_V).transpose(1, 0, 2)
    dckvn, dwk_t, dwv_t, dkr_t = _mla_kv_proj_bwd(dkt, dvt, ckvn, w_kv[..., :C_NOPE], w_kv[..., C_NOPE:],
                                                  name=f"mla_ukv_bwd_{l}")
    g_w_ukv = jnp.concatenate([dwk_t, dwv_t], axis=1).reshape(C_HEADS * (C_NOPE + C_V), C_KV_RANK).T
    _, rope_vjp = jax.vjp(lambda t: _rope(t[None])[0], jnp.zeros((S, C_ROPE), F32))
    (dkr_raw,) = rope_vjp(dkr_t.transpose(0, 2, 1).reshape(S, C_ROPE))
    dc_q, g_q_norm = _rmsnorm_bwd(c_q, q_norm, dcqn, None, name=f"mla_qnorm_bwd_{l}")
    dc_kv, g_kv_norm = _rmsnorm_bwd(c_kv, kv_norm, dckvn, None, name=f"mla_kvnorm_bwd_{l}")
    ddown = jnp.concatenate([dc_q, dc_kv, dkr_raw, jnp.zeros((S, C_DOWN_PAD - C_DOWN), F32)], axis=1).astype(MXU_DT)
    g_w_down = _matmul_tn(xn, ddown, tk=512, tn=768, name=f"mla_down_dw_{l}")[:, :C_DOWN]
    dh_new, g_norm = _matmul(ddown, w_down, trans_b=True, epi='norm_bwd', norm=norm, tm=512, tn=w_down.shape[0],
                             name=f"mla_down_dx_{l}")
    return dh_new, g_norm[0], g_w_down, g_q_norm[0], g_w_uq, g_kv_norm[0], g_w_ukv, g_w_o


def kernel(x, rel_bias, attn_norm, mlp_norm, final_norm, w_in_ab, sinks, w_out_ab, w_down_c, q_norm_c, w_uq_c, kv_norm_c, w_ukv_c, w_o_c, w_mlp_up, w_mlp_down, loss_target, m_rel_bias, m_attn_norm, m_mlp_norm, m_final_norm, m_w_in_ab, m_sinks, m_w_out_ab, m_w_down_c, m_q_norm_c, m_w_uq_c, m_kv_norm_c, m_w_ukv_c, m_w_o_c, m_w_mlp_up, m_w_mlp_down, v_rel_bias, v_attn_norm, v_mlp_norm, v_final_norm, v_w_in_ab, v_sinks, v_w_out_ab, v_w_down_c, v_q_norm_c, v_w_uq_c, v_kv_norm_c, v_w_ukv_c, v_w_o_c, v_w_mlp_up, v_w_mlp_down):
    W = dict(w_in_ab=w_in_ab, w_out_ab=w_out_ab, w_down_c=w_down_c, w_uq_c=w_uq_c, w_ukv_c=w_ukv_c, w_o_c=w_o_c,
             w_mlp_up=w_mlp_up, w_mlp_down=w_mlp_down)
    Mo = dict(w_in_ab=m_w_in_ab, w_out_ab=m_w_out_ab, w_down_c=m_w_down_c, w_uq_c=m_w_uq_c, w_ukv_c=m_w_ukv_c,
              w_o_c=m_w_o_c, w_mlp_up=m_w_mlp_up, w_mlp_down=m_w_mlp_down)
    Vo = dict(w_in_ab=v_w_in_ab, w_out_ab=v_w_out_ab, w_down_c=v_w_down_c, w_uq_c=v_w_uq_c, w_ukv_c=v_w_ukv_c,
              w_o_c=v_w_o_c, w_mlp_up=v_w_mlp_up, w_mlp_down=v_w_mlp_down)
    S = x.shape[1]
    me = 4 * lax.axis_index("x") + 2 * lax.axis_index("y") + lax.axis_index("c")
    axis_of = dict(_BIG)

    groups = _exchange_groups()

    def pack(src, gi):
        return _pack_rows([src[n][i] for n, i in groups[gi]])

    def unpack_group(gathered, gi):
        shapes = [W[n].shape[1:] for n, _ in groups[gi]]
        return {n: _gathered_to_full(g, axis_of[n])
                for (n, _), g in zip(groups[gi], _unpack_rows(gathered, shapes))}

    def send_of(G, gi):
        return _pack_rows_per_device([_full_to_shards(G[n], axis_of[n]) for n, _ in groups[gi]]).astype(MXU_DT)

    w_packs = [pack(W, gi) for gi in range(len(groups))]
    gathered = _exchange(w_packs[0].astype(MXU_DT), False, "gather_weights_0")
    gains = _exchange(jnp.concatenate([_pad_rows8(q_norm_c), _pad_rows8(kv_norm_c)], axis=0), False, "gather_gains")
    n_odd = q_norm_c.shape[0]
    q_norm_full = gains[:, 0].reshape(N_DEV, -1)[:, :q_norm_c.size].reshape(N_DEV, n_odd, -1).transpose(1, 0, 2).reshape(n_odd, C_Q_RANK)
    kv_norm_full = gains[:, 8].reshape(N_DEV, -1)[:, :kv_norm_c.size].reshape(N_DEV, n_odd, -1).transpose(1, 0, 2).reshape(n_odd, C_KV_RANK)

    bias, bias_vjp = jax.vjp(_make_bias, rel_bias)
    sink_rows = [jnp.broadcast_to(jnp.concatenate([sinks[e], jnp.full((B_HEADS,), NEG, F32)])[:, None, None],
                                  (N_BIAS_HEADS, 1, 128)) for e in range(sinks.shape[0])]

    h = x[0]
    saved = []
    for l in range(DEPTH):
        full = unpack_group(gathered, 0 if l == 0 else l + 1)
        if l % 2 == 1:
            full["w_down_c"] = jnp.pad(full["w_down_c"], ((0, 0), (0, C_DOWN_PAD - C_DOWN)))
        xn = _rmsnorm(h, attn_norm[l], out_dtype=MXU_DT, name=f"attn_norm_{l}")
        if l == 0:
            h1, ctx, gathered_mlp = _even_fwd(xn, h, full["w_in_ab"], full["w_out_ab"], bias, sink_rows[0], l,
                                              ride=(w_packs[1].astype(MXU_DT), False))
            full.update(unpack_group(gathered_mlp, 1))
        elif l % 2 == 0:
            h1, ctx, _ = _even_fwd(xn, h, full["w_in_ab"], full["w_out_ab"], bias, sink_rows[l // 2], l)
        else:
            o = l // 2
            h1, ctx = _mla_fwd(xn, h, full["w_down_c"], q_norm_full[o], full["w_uq_c"], kv_norm_full[o],
                               full["w_ukv_c"], full["w_o_c"], l)
        xn2 = _rmsnorm(h1, mlp_norm[l], out_dtype=MXU_DT, name=f"mlp_norm_{l}")
        if l + 1 < DEPTH:
            act, gathered = _matmul(xn2, full["w_mlp_up"], out_dtype=MXU_DT, epi='relu2', tm=1024, tn=512,
                                    name=f"mlp_up_{l}", ride=(w_packs[l + 2].astype(MXU_DT), False))
        else:
            act = _matmul(xn2, full["w_mlp_up"], out_dtype=MXU_DT, epi='relu2', tm=1024, tn=512, name=f"mlp_up_{l}")
        h2 = _matmul(act, full["w_mlp_down"], epi='add', extra=h1, tm=512, tn=512, name=f"mlp_down_{l}")
        saved.append((h, xn, h1, xn2, act, ctx, full))
        h = h2

    loss_row, dh, g_final = _loss_head(h, loss_target[0], final_norm)

    g_attn_norm, g_mlp_norm = [None] * DEPTH, [None] * DEPTH
    g_sinks, g_qn, g_kvn = [None] * sinks.shape[0], [None] * n_odd, [None] * n_odd
    dbias_total = None
    landed = [None] * len(groups)
    send = None
    for l in reversed(range(DEPTH)):
        h0, xn, h1, xn2, act, ctx, full = saved[l]
        G = {}
        if send is None:
            du = _matmul(dh, full["w_mlp_down"], trans_b=True, out_dtype=MXU_DT, epi='dsq', extra=act,
                         tm=1024, tn=512, name=f"mlp_down_dx_{l}")
        else:
            du, landed[l + 2] = _matmul(dh, full["w_mlp_down"], trans_b=True, out_dtype=MXU_DT, epi='dsq', extra=act,
                                        tm=1024, tn=512, name=f"mlp_down_dx_{l}", ride=(send, True))
        G["w_mlp_down"] = _matmul_tn(act, dh, tk=512, tn=1024, name=f"mlp_down_dw_{l}")
        G["w_mlp_up"] = _matmul_tn(xn2, du, tk=512, tn=1024, name=f"mlp_up_dw_{l}")
        dh, g = _matmul(du, full["w_mlp_up"], trans_b=True, epi='norm_bwd', norm=(h1, mlp_norm[l], dh),
                        tm=512, tn=D_MODEL, name=f"mlp_up_dx_{l}")
        g_mlp_norm[l] = g[0]
        if l % 2 == 0:
            e = l // 2
            dh, g_attn_norm[l], G["w_in_ab"], G["w_out_ab"], dbias, g_sinks[e], landed_mlp = _even_bwd(
                dh, xn, ctx, full["w_in_ab"], full["w_out_ab"], bias, sink_rows[e], l, (h0, attn_norm[l], dh),
                ride=(send_of(G, 1), True) if l == 0 else None)
            if l == 0:
                landed[1] = landed_mlp
            dbias_total = dbias if dbias_total is None else dbias_total + dbias
        else:
            o = l // 2
            dh, g_attn_norm[l], G["w_down_c"], g_qn[o], G["w_uq_c"], g_kvn[o], G["w_ukv_c"], G["w_o_c"] = _mla_bwd(
                dh, xn, ctx, full["w_down_c"], q_norm_full[o], full["w_uq_c"], kv_norm_full[o],
                full["w_ukv_c"], full["w_o_c"], l, (h0, attn_norm[l], dh))
        send = send_of(G, 0 if l == 0 else l + 1)
    landed[0] = _exchange(send, True, "scatter_grads_0")
    grad_x = dh[None]
    (g_rel_bias,) = bias_vjp(dbias_total)

    big = [{}, {}, {}, {}]
    for gi in range(len(groups)):
        outs = _adamw(landed[gi], w_packs[gi], pack(Mo, gi), pack(Vo, gi), name=f"adamw_{gi}")
        shapes = [W[n].shape[1:] for n, _ in groups[gi]]
        for kind, buf in enumerate(outs):
            for (n, _), t in zip(groups[gi], _unpack_rows(buf, shapes)):
                big[kind].setdefault(n, []).append(t)
    big_out = [{n: jnp.stack(ts) for n, ts in d.items()} for d in big]

    small_g = [g_rel_bias, jnp.stack(g_attn_norm), jnp.stack(g_mlp_norm), g_final[0], jnp.stack(g_sinks),
               jnp.stack(g_qn), jnp.stack(g_kvn), loss_row[0, :1]]
    small_w = [rel_bias, attn_norm, mlp_norm, final_norm, sinks, q_norm_c, kv_norm_c, jnp.zeros((1,), F32)]
    small_m = [m_rel_bias, m_attn_norm, m_mlp_norm, m_final_norm, m_sinks, m_q_norm_c, m_kv_norm_c, jnp.zeros((1,), F32)]
    small_v = [v_rel_bias, v_attn_norm, v_mlp_norm, v_final_norm, v_sinks, v_q_norm_c, v_kv_norm_c, jnp.ones((1,), F32)]
    offs = np.cumsum([0] + [-(-a.size // 1024) * 8 for a in small_g])
    partials = _exchange(jnp.concatenate([_pad_rows8(a) for a in small_g], axis=0), False, "gather_small_grads")

    def mine(i, a_full_shape, local):
        p = partials[:, offs[i]:offs[i + 1]].reshape(N_DEV, -1)[:, :math.prod(a_full_shape)]
        p = p.reshape((N_DEV,) + tuple(a_full_shape))
        if local.shape != tuple(a_full_shape):
            width = local.shape[-1]
            p = lax.dynamic_slice_in_dim(p, me * width, width, axis=p.ndim - 1)
        return jnp.stack([_pad_rows8(p[s]) for s in range(N_DEV)])

    parts_small = jnp.concatenate([mine(i, g.shape, w) for i, (g, w) in enumerate(zip(small_g, small_w))], axis=1)
    pk = lambda arrs: jnp.concatenate([_pad_rows8(a) for a in arrs], axis=0)
    small_out = _adamw(parts_small, pk(small_w), pk(small_m), pk(small_v), name="adamw_small", tr=parts_small.shape[1])
    offs2 = np.cumsum([0] + [-(-a.size // 1024) * 8 for a in small_w])

    def unpack_small(buf):
        return [buf[offs2[i]:offs2[i + 1]].reshape(-1)[:a.size].reshape(a.shape) for i, a in enumerate(small_w)]

    sg, sd, sm, sv = (unpack_small(b) for b in small_out)
    loss = sg[7][0]

    order = ['rel_bias', 'attn_norm', 'mlp_norm', 'final_norm', 'w_in_ab', 'sinks', 'w_out_ab', 'w_down_c', 'q_norm_c',
             'w_uq_c', 'kv_norm_c', 'w_ukv_c', 'w_o_c', 'w_mlp_up', 'w_mlp_down']
    small_idx = {'rel_bias': 0, 'attn_norm': 1, 'mlp_norm': 2, 'final_norm': 3, 'sinks': 4, 'q_norm_c': 5, 'kv_norm_c': 6}

    def pick(kind):
        res = []
        for n in order:
            if n in small_idx:
                res.append((sg, sd, sm, sv)[kind][small_idx[n]])
            else:
                res.append(big_out[kind][n])
        return res

    return (loss, grad_x, *pick(0), *pick(1), *pick(2), *pick(3))
```

```python
import math

import numpy as np
import jax
import jax.numpy as jnp
from jax import lax
from jax.experimental import pallas as pl
from jax.experimental.pallas import tpu as pltpu

F32 = jnp.float32
MXU_DT = jnp.bfloat16

N_DEV = 8
D_MODEL = 1024
DEPTH = 4
HEAD_DIM = 64
BLOCK = 128
EPS = 1e-6
NEG = -1e30
A_Q_HEADS = 8
A_KV_HEADS = 2
A_GROUP = A_Q_HEADS // A_KV_HEADS
A_WINDOW = 128
B_BRANCHES = ((128, 1), (512, 4), (2048, 16))
B_HPB = 4
B_HEADS = len(B_BRANCHES) * B_HPB
NUM_BUCKETS = 32
MAX_DISTANCE = 2048
N_BIAS_HEADS = A_Q_HEADS + B_HEADS
N_BAND_KV = A_KV_HEADS + B_HEADS
A_IN = (A_Q_HEADS + 2 * A_KV_HEADS) * HEAD_DIM
C_HEADS = 8
C_NOPE = 64
C_ROPE = 32
C_QK = C_NOPE + C_ROPE
C_V = 64
C_Q_RANK = 384
C_KV_RANK = 256
C_DOWN = C_Q_RANK + C_KV_RANK + C_ROPE
C_DOWN_PAD = 768
ROPE_THETA = 10000.0
N_CHUNKS = 16
FLASH_T = 512

ADAM_LR = 0.001
ADAM_B1 = 0.9
ADAM_B2 = 0.999
ADAM_EPS = 1e-08
ADAM_WD = 0.01
ADAM_STEP = 10

V7X_VMEM_BYTES = 64 * 1024 * 1024
VMEM_LIMIT = V7X_VMEM_BYTES - 8 * 1024 * 1024


def _pcall(body, **kw):
    return pl.pallas_call(body, **kw)


def _cparams(*sem):
    return pltpu.CompilerParams(dimension_semantics=sem, vmem_limit_bytes=VMEM_LIMIT)


def _exchange(src, all_to_all, name):
    def body(src_ref, out_ref, send_sems, recv_sems, local_sem):
        copies = _exchange_copies(src_ref, out_ref, send_sems, recv_sems, local_sem, all_to_all)
        for cp in copies:
            cp.start()
        _exchange_wait(copies)

    return _pcall(
        body, name=name,
        out_shape=_exchange_out(src),
        in_specs=[pl.BlockSpec(memory_space=pl.ANY)],
        out_specs=pl.BlockSpec(memory_space=pl.ANY),
        scratch_shapes=_exchange_sems(),
    )(src)


def _exchange_out(src):
    return jax.ShapeDtypeStruct((N_DEV,) + src.shape[-2:], src.dtype)


def _exchange_sems():
    return [pltpu.SemaphoreType.DMA((N_DEV - 1,)), pltpu.SemaphoreType.DMA((N_DEV - 1,)), pltpu.SemaphoreType.DMA]


def _exchange_copies(src_ref, out_ref, send_sems, recv_sems, local_sem, all_to_all):
    x, y, c = lax.axis_index("x"), lax.axis_index("y"), lax.axis_index("c")
    me = 4 * x + 2 * y + c

    def piece(dev):
        return src_ref.at[dev] if all_to_all else src_ref

    copies = [pltpu.make_async_copy(piece(me), out_ref.at[me], local_sem)]
    for k in range(1, N_DEV):
        px = 1 - x if (k >> 2) & 1 else x
        py = 1 - y if (k >> 1) & 1 else y
        pc = 1 - c if k & 1 else c
        copies.append(pltpu.make_async_remote_copy(
            src_ref=piece(4 * px + 2 * py + pc), dst_ref=out_ref.at[me],
            send_sem=send_sems.at[k - 1], recv_sem=recv_sems.at[k - 1],
            device_id=(px, py, pc), device_id_type=pl.DeviceIdType.MESH))
    return copies


def _exchange_wait(copies):
    for cp in copies[1:]:
        cp.wait()
    copies[0].wait()


def _matmul(a, b, *, trans_b=False, out_dtype=F32, epi=None, extra=None, norm=None, into=None, tm=512, tn=512,
            name, ride=None):
    M, K = a.shape
    N = b.shape[0] if trans_b else b.shape[1]
    tm, tn = min(tm, M), min(tn, N)
    assert M % tm == 0 and N % tn == 0 and (b.shape[1] if trans_b else b.shape[0]) == K
    assert (epi == 'norm_bwd') == (norm is not None) and (norm is None or tn == N)
    dn = (((1,), (1,)), ((), ())) if trans_b else (((1,), (0,)), ((), ()))
    n_i, n_j = M // tm, N // tn

    def body(*refs):
        it = iter(refs)
        a_ref, b_ref = next(it), next(it)
        e_ref = next(it) if extra is not None else None
        x_ref, g_ref, dres_ref = (next(it), next(it), next(it)) if norm is not None else (None, None, None)
        if into is not None:
            next(it)
        src_ref = next(it) if ride is not None else None
        o_ref = next(it)
        slope_ref = next(it) if epi == 'relu2' else None
        dg_ref = next(it) if norm is not None else None
        i, j = pl.program_id(0), pl.program_id(1)
        if ride is not None:
            land_ref = next(it)
            copies = _exchange_copies(src_ref, land_ref, *it, ride[1])

            @pl.when((i == 0) & (j == 0))
            def _():
                for cp in copies:
                    cp.start()

        acc = lax.dot_general(a_ref[...].astype(MXU_DT), b_ref[...].astype(MXU_DT), dn,
                              preferred_element_type=F32)
        if epi == 'relu2':
            r = jnp.maximum(acc, 0.0)
            acc = r * r
            slope_ref[...] = (2.0 * r).astype(slope_ref.dtype)
        elif epi == 'add':
            acc = acc + e_ref[...].astype(F32)
        elif epi == 'mul':
            acc = acc * e_ref[...].astype(F32)
        elif epi == 'norm_bwd':
            @pl.when(i == 0)
            def _():
                dg_ref[...] = jnp.zeros_like(dg_ref)

            xf = x_ref[...]
            r = lax.rsqrt(jnp.mean(xf * xf, axis=-1, keepdims=True) + EPS)
            xhat = xf * r
            dg_ref[...] += jnp.sum(acc * xhat, axis=0, keepdims=True)
            dyg = acc * g_ref[...]
            acc = r * (dyg - xhat * jnp.mean(dyg * xhat, axis=-1, keepdims=True)) + dres_ref[...]
        o_ref[...] = acc.astype(out_dtype)

        if ride is not None:
            @pl.when((i == n_i - 1) & (j == n_j - 1))
            def _():
                _exchange_wait(copies)

    b_spec = pl.BlockSpec((tn, K), lambda i, j: (j, 0)) if trans_b else pl.BlockSpec((K, tn), lambda i, j: (0, j))
    tile = pl.BlockSpec((tm, tn), lambda i, j: (i, j))
    in_specs = [pl.BlockSpec((tm, K), lambda i, j: (i, 0)), b_spec]
    args = [a, b]
    out_shape, out_specs, aliases = [jax.ShapeDtypeStruct((M, N), out_dtype)], [tile], {}
    if extra is not None:
        in_specs.append(tile)
        args.append(extra)
    if epi == 'relu2':
        out_shape.append(jax.ShapeDtypeStruct((M, N), out_dtype))
        out_specs.append(tile)
    if norm is not None:
        vec = pl.BlockSpec((1, N), lambda i, j: (0, 0))
        in_specs += [tile, vec, tile]
        args += [norm[0], norm[1].reshape(1, N), norm[2]]
        out_shape.append(jax.ShapeDtypeStruct((1, N), F32))
        out_specs.append(vec)
    if into is not None:
        buf, col0 = into
        assert col0 % tn == 0 and buf.shape[0] == M and buf.dtype == out_dtype
        aliases = {len(args): 0}
        in_specs.append(pl.BlockSpec(memory_space=pl.ANY))
        args.append(buf)
        out_shape[0] = jax.ShapeDtypeStruct(buf.shape, out_dtype)
        out_specs[0] = pl.BlockSpec((tm, tn), lambda i, j: (i, j + col0 // tn))
    scratch = []
    if ride is not None:
        in_specs.append(pl.BlockSpec(memory_space=pl.ANY))
        args.append(ride[0])
        out_shape.append(_exchange_out(ride[0]))
        out_specs.append(pl.BlockSpec(memory_space=pl.ANY))
        scratch = _exchange_sems()
    ordered = ride is not None or norm is not None
    res = _pcall(
        body, name=name, grid=(n_i, n_j), out_shape=tuple(out_shape), in_specs=in_specs, out_specs=tuple(out_specs),
        scratch_shapes=scratch, input_output_aliases=aliases,
        compiler_params=_cparams(*(("arbitrary", "arbitrary") if ordered else ("parallel", "parallel"))),
    )(*args)
    return res[0] if len(res) == 1 else res


def _matmul_tn(a, b, *, tk=512, tn=512, tm=1024, name):
    M, Ka = a.shape
    N = b.shape[1]
    tk, tn, tm = min(tk, Ka), min(tn, N), min(tm, M)
    assert Ka % tk == 0 and N % tn == 0 and M % tm == 0 and b.shape[0] == M

    def body(a_ref, b_ref, o_ref):
        @pl.when(pl.program_id(2) == 0)
        def _():
            o_ref[...] = jnp.zeros_like(o_ref)

        o_ref[...] += lax.dot_general(a_ref[...].astype(MXU_DT), b_ref[...].astype(MXU_DT),
                                      (((0,), (0,)), ((), ())), preferred_element_type=F32)

    return _pcall(
        body, name=name, grid=(Ka // tk, N // tn, M // tm),
        out_shape=jax.ShapeDtypeStruct((Ka, N), F32),
        in_specs=[pl.BlockSpec((tm, tk), lambda i, j, r: (r, i)), pl.BlockSpec((tm, tn), lambda i, j, r: (r, j))],
        out_specs=pl.BlockSpec((tk, tn), lambda i, j, r: (i, j)),
        compiler_params=_cparams("parallel", "parallel", "arbitrary"),
    )(a, b)


def _rmsnorm(x, g, *, out_dtype, name, tr=512):
    S, D = x.shape
    tr = min(tr, S)

    def body(x_ref, g_ref, o_ref):
        xf = x_ref[...].astype(F32)
        r = lax.rsqrt(jnp.mean(xf * xf, axis=-1, keepdims=True) + EPS)
        o_ref[...] = (xf * r * g_ref[...]).astype(out_dtype)

    return _pcall(
        body, name=name, grid=(S // tr,),
        out_shape=jax.ShapeDtypeStruct((S, D), out_dtype),
        in_specs=[pl.BlockSpec((tr, D), lambda i: (i, 0)), pl.BlockSpec((1, D), lambda i: (0, 0))],
        out_specs=pl.BlockSpec((tr, D), lambda i: (i, 0)),
        compiler_params=_cparams("parallel"),
    )(x, g.reshape(1, D))


def _rmsnorm_bwd(x, g, dy, dres, *, name, tr=512):
    S, D = x.shape
    tr = min(tr, S)

    def body(*refs):
        if dres is None:
            x_ref, g_ref, dy_ref, dx_ref, dg_ref = refs
        else:
            x_ref, g_ref, dy_ref, dres_ref, dx_ref, dg_ref = refs

        @pl.when(pl.program_id(0) == 0)
        def _():
            dg_ref[...] = jnp.zeros_like(dg_ref)

        xf = x_ref[...].astype(F32)
        r = lax.rsqrt(jnp.mean(xf * xf, axis=-1, keepdims=True) + EPS)
        xhat = xf * r
        dyf = dy_ref[...].astype(F32)
        dg_ref[...] += jnp.sum(dyf * xhat, axis=0, keepdims=True)
        dyg = dyf * g_ref[...]
        dx = r * (dyg - xhat * jnp.mean(dyg * xhat, axis=-1, keepdims=True))
        if dres is not None:
            dx = dx + dres_ref[...]
        dx_ref[...] = dx

    row = pl.BlockSpec((tr, D), lambda i: (i, 0))
    vec = pl.BlockSpec((1, D), lambda i: (0, 0))
    args = [x, g.reshape(1, D), dy] + ([] if dres is None else [dres])
    return _pcall(
        body, name=name, grid=(S // tr,),
        out_shape=(jax.ShapeDtypeStruct((S, D), F32), jax.ShapeDtypeStruct((1, D), F32)),
        in_specs=[row, vec, row] + ([] if dres is None else [row]),
        out_specs=(row, vec),
        compiler_params=_cparams("arbitrary"),
    )(*args)


def _loss_head(h, t, g, *, tr=512):
    S, D = h.shape
    tr = min(tr, S)

    def body(h_ref, t_ref, g_ref, loss_ref, dh_ref, dg_ref):
        @pl.when(pl.program_id(0) == 0)
        def _():
            dg_ref[...] = jnp.zeros_like(dg_ref)
            loss_ref[...] = jnp.zeros_like(loss_ref)

        xf = h_ref[...]
        r = lax.rsqrt(jnp.mean(xf * xf, axis=-1, keepdims=True) + EPS)
        xhat = xf * r
        e = xhat * g_ref[...] - t_ref[...]
        part = 0.5 * jnp.sum(jnp.mean(e * e, axis=-1, keepdims=True), axis=0, keepdims=True)
        loss_ref[...] += jnp.broadcast_to(part, loss_ref.shape)
        dy = e * (1.0 / D)
        dg_ref[...] += jnp.sum(dy * xhat, axis=0, keepdims=True)
        dyg = dy * g_ref[...]
        dh_ref[...] = r * (dyg - xhat * jnp.mean(dyg * xhat, axis=-1, keepdims=True))

    row = pl.BlockSpec((tr, D), lambda i: (i, 0))
    vec = pl.BlockSpec((1, D), lambda i: (0, 0))
    return _pcall(
        body, name="loss_head", grid=(S // tr,),
        out_shape=(jax.ShapeDtypeStruct((1, 128), F32), jax.ShapeDtypeStruct((S, D), F32),
                   jax.ShapeDtypeStruct((1, D), F32)),
        in_specs=[row, row, vec],
        out_specs=(pl.BlockSpec((1, 128), lambda i: (0, 0)), row, vec),
        compiler_params=_cparams("arbitrary"),
    )(h, t, g.reshape(1, D))


N_PAIRS = N_BIAS_HEADS // 2
A_PAIRS = A_Q_HEADS // 2
PAIR_W = 2 * HEAD_DIM
assert PAIR_W == 128 and A_KV_HEADS * HEAD_DIM == PAIR_W and B_HPB * HEAD_DIM == 2 * PAIR_W


def _pair_period(p):
    return jnp.where(p < A_PAIRS + 2, 16, jnp.where(p < A_PAIRS + 4, 4, 1))


def _pair_cols(p):
    b = jnp.maximum(p - A_PAIRS, 0)
    base, pp = 6 + 6 * (b // 2), b % 2
    is_a = p < A_PAIRS
    return (jnp.where(is_a, p, base + pp), jnp.where(is_a, A_PAIRS, base + 2 + pp),
            jnp.where(is_a, A_PAIRS + 1, base + 4 + pp))


def _band_specs(S):
    ch = S // N_CHUNKS
    nb = ch // BLOCK
    col = lambda i: (lambda p, c: (c, _pair_cols(p)[i]))
    prev = lambda i: (lambda p, c: (jnp.maximum(c * nb - 1, 0), _pair_cols(p)[i]))
    qkv = [pl.BlockSpec((ch, PAIR_W), col(0)), pl.BlockSpec((ch, PAIR_W), col(1)), pl.BlockSpec((BLOCK, PAIR_W), prev(1)),
           pl.BlockSpec((ch, PAIR_W), col(2)), pl.BlockSpec((BLOCK, PAIR_W), prev(2))]
    out_spec = pl.BlockSpec((ch, PAIR_W), lambda p, c: (c, p))
    sink_spec = pl.BlockSpec((2, 1, 128), lambda p, c: (p, 0, 0))
    row_spec = pl.BlockSpec((1, 1, nb, 2, BLOCK), lambda p, c: (p, c, 0, 0, 0))
    return ch, nb, qkv, out_spec, sink_spec, row_spec


def _pair_kv(p, e, ref):
    half = jnp.where(p < A_PAIRS, p // (A_GROUP // 2), e)
    return jnp.where(half == 0, ref[:, :HEAD_DIM], ref[:, HEAD_DIM:])


def _eye():
    return lax.broadcasted_iota(jnp.int32, (BLOCK, BLOCK), 0) == lax.broadcasted_iota(jnp.int32, (BLOCK, BLOCK), 1)


_NT = (((1,), (1,)), ((), ()))
_NN = (((1,), (0,)), ((), ()))
_TN = (((0,), (0,)), ((), ()))


_B_NT = (((2,), (2,)), ((0,), (0,)))
_B_NN = (((2,), (1,)), ((0,), (0,)))


def _bdot(a, b, dn):
    return lax.dot_general(a, b, dn, preferred_element_type=F32)


def _with_prev(first, t3):
    return first[None] if t3.shape[0] == 1 else jnp.concatenate([first[None], t3[:-1]], axis=0)


def _mask_first(s_prev, prev_ok):
    s0 = jnp.where(prev_ok, s_prev[0], NEG)[None]
    return s0 if s_prev.shape[0] == 1 else jnp.concatenate([s0, s_prev[1:]], axis=0)


def _banded_fwd(proj, bias, sinks):
    S = proj.shape[0]
    dh = HEAD_DIM
    ch, nb, qkv, out_spec, sink_spec, row_spec = _band_specs(S)
    bias_spec = pl.BlockSpec((2, BLOCK, 2 * BLOCK), lambda p, c: (p, 0, 0))
    scale = HEAD_DIM ** -0.5

    def body(q_ref, k_ref, kp_ref, v_ref, vp_ref, b_ref, s_ref, o_ref, lse_ref):
        p, c = pl.program_id(0), pl.program_id(1)
        prev_ok = (c % _pair_period(p)) != 0
        for e in range(2):
            lanes = slice(e * dh, (e + 1) * dh)
            q3 = q_ref[:, lanes].reshape(nb, BLOCK, dh)
            k3, v3 = (_pair_kv(p, e, r).reshape(nb, BLOCK, dh) for r in (k_ref, v_ref))
            kp3, vp3 = _with_prev(_pair_kv(p, e, kp_ref), k3), _with_prev(_pair_kv(p, e, vp_ref), v3)
            sink = s_ref[e, :, :1]
            s_cur = _bdot(q3, k3, _B_NT) * scale + b_ref[e, :, BLOCK:][None]
            s_prev = _mask_first(_bdot(q3, kp3, _B_NT) * scale + b_ref[e, :, :BLOCK][None], prev_ok)
            m = jnp.maximum(jnp.max(s_cur, axis=-1, keepdims=True), jnp.max(s_prev, axis=-1, keepdims=True))
            m = jnp.maximum(m, sink)
            p_cur = jnp.exp(s_cur - m)
            p_prev = jnp.exp(s_prev - m)
            l = jnp.sum(p_cur, axis=-1, keepdims=True) + jnp.sum(p_prev, axis=-1, keepdims=True) + jnp.exp(sink - m)
            acc = _bdot(p_cur.astype(MXU_DT), v3, _B_NN) + _bdot(p_prev.astype(MXU_DT), vp3, _B_NN)
            o_ref[:, lanes] = (acc / l).reshape(ch, dh)
            lse = m + jnp.log(l)
            lse_ref[0, 0, :, e:e + 1, :] = jnp.sum(jnp.where(_eye()[None], lse, 0.0), axis=1, keepdims=True)

    return _pcall(
        body, name="banded_fwd", grid=(N_PAIRS, N_CHUNKS),
        out_shape=(jax.ShapeDtypeStruct((S, N_PAIRS * PAIR_W), F32),
                   jax.ShapeDtypeStruct((N_PAIRS, N_CHUNKS, nb, 2, BLOCK), F32)),
        in_specs=qkv + [bias_spec, sink_spec],
        out_specs=(out_spec, row_spec),
        compiler_params=_cparams("parallel", "parallel"),
    )(proj, proj, proj, proj, proj, bias, sinks)


def _banded_bwd(proj, bias_t, sinks, do, dlse):
    S = proj.shape[0]
    dh = HEAD_DIM
    ch, nb, qkv, out_spec, sink_spec, row_spec = _band_specs(S)
    bias_spec = pl.BlockSpec((2, 2 * BLOCK, BLOCK), lambda p, c: (p, 0, 0))
    scale = HEAD_DIM ** -0.5

    def body(q_ref, k_ref, kp_ref, v_ref, vp_ref, b_ref, s_ref, do_ref, dl_ref,
             dq_ref, dk_ref, dv_ref, dkh_ref, dvh_ref, db_ref, ds_ref):
        p, c = pl.program_id(0), pl.program_id(1)

        @pl.when(c == 0)
        def _():
            db_ref[...] = jnp.zeros_like(db_ref)
            ds_ref[...] = jnp.zeros_like(ds_ref)

        prev_ok = (c % _pair_period(p)) != 0
        for e in range(2):
            lanes = slice(e * dh, (e + 1) * dh)
            q3 = q_ref[:, lanes].reshape(nb, BLOCK, dh)
            k3, v3 = (_pair_kv(p, e, r).reshape(nb, BLOCK, dh) for r in (k_ref, v_ref))
            kp3, vp3 = _with_prev(_pair_kv(p, e, kp_ref), k3), _with_prev(_pair_kv(p, e, vp_ref), v3)
            do3 = do_ref[:, lanes].astype(MXU_DT).reshape(nb, BLOCK, dh)
            sink = s_ref[e, :, :1]
            s_cur = _bdot(k3, q3, _B_NT) * scale + b_ref[e, BLOCK:, :][None]
            s_prev = _mask_first(_bdot(kp3, q3, _B_NT) * scale + b_ref[e, :BLOCK, :][None], prev_ok)
            m = jnp.maximum(jnp.max(s_cur, axis=1, keepdims=True), jnp.max(s_prev, axis=1, keepdims=True))
            m = jnp.maximum(m, sink)
            p_cur = jnp.exp(s_cur - m)
            p_prev = jnp.exp(s_prev - m)
            p_sink = jnp.exp(sink - m)
            inv = 1.0 / (jnp.sum(p_cur, axis=1, keepdims=True) + jnp.sum(p_prev, axis=1, keepdims=True) + p_sink)
            p_cur, p_prev, p_sink = p_cur * inv, p_prev * inv, p_sink * inv
            dp_cur = _bdot(v3, do3, _B_NT)
            dp_prev = _bdot(vp3, do3, _B_NT)
            delta = jnp.sum(p_cur * dp_cur, axis=1, keepdims=True) + jnp.sum(p_prev * dp_prev, axis=1, keepdims=True)
            t = dl_ref[0, 0, :, e:e + 1, :] - delta
            ds_cur = p_cur * (dp_cur + t)
            ds_prev = p_prev * (dp_prev + t)
            dsink = jnp.sum(jnp.sum(p_sink * t, axis=0), axis=-1, keepdims=True)
            ds_ref[e] += jnp.broadcast_to(dsink, (1, 128))
            db_ref[e, :BLOCK, :] += jnp.sum(ds_prev, axis=0)
            db_ref[e, BLOCK:, :] += jnp.sum(ds_cur, axis=0)
            dsb_cur = (ds_cur * scale).astype(MXU_DT)
            dsb_prev = (ds_prev * scale).astype(MXU_DT)
            dk_prev = _bdot(dsb_prev, q3, _B_NN)
            dv_prev = _bdot(p_prev.astype(MXU_DT), do3, _B_NN)

            def shifted(t3):
                z = jnp.zeros((1, BLOCK, dh), F32)
                return z if nb == 1 else jnp.concatenate([t3[1:], z], axis=0)

            dk_ref[:, lanes] = (_bdot(dsb_cur, q3, _B_NN) + shifted(dk_prev)).reshape(ch, dh)
            dv_ref[:, lanes] = (_bdot(p_cur.astype(MXU_DT), do3, _B_NN) + shifted(dv_prev)).reshape(ch, dh)
            dkh_ref[0, 0, :, lanes] = dk_prev[0]
            dvh_ref[0, 0, :, lanes] = dv_prev[0]
            for b in range(nb):
                dq_ref[b * BLOCK:(b + 1) * BLOCK, lanes] = (
                    lax.dot_general(dsb_cur[b], k3[b], _TN, preferred_element_type=F32)
                    + lax.dot_general(dsb_prev[b], kp3[b], _TN, preferred_element_type=F32))

    halo_spec = pl.BlockSpec((1, 1, BLOCK, PAIR_W), lambda p, c: (p, c, 0, 0))
    big = jax.ShapeDtypeStruct((S, N_PAIRS * PAIR_W), F32)
    halo = jax.ShapeDtypeStruct((N_PAIRS, N_CHUNKS, BLOCK, PAIR_W), F32)
    return _pcall(
        body, name="banded_bwd", grid=(N_PAIRS, N_CHUNKS),
        out_shape=(big, big, big, halo, halo, jax.ShapeDtypeStruct(bias_t.shape, F32),
                   jax.ShapeDtypeStruct(sinks.shape, F32)),
        in_specs=qkv + [bias_spec, sink_spec, out_spec, row_spec],
        out_specs=(out_spec, out_spec, out_spec, halo_spec, halo_spec, bias_spec, sink_spec),
        compiler_params=_cparams("arbitrary", "arbitrary"),
    )(proj, proj, proj, proj, proj, bias_t, sinks, do, dlse)


def _halo_fold(t, halo, name):
    S, width = t.shape
    nb = S // N_CHUNKS // BLOCK

    def body(t_ref, h_ref, o_ref):
        o_ref[:N_CHUNKS - 1, 0] = t_ref[:N_CHUNKS - 1, 0] + h_ref[0, 1:]
        o_ref[N_CHUNKS - 1:, 0] = t_ref[N_CHUNKS - 1:, 0]

    blk = pl.BlockSpec((N_CHUNKS, 1, BLOCK, PAIR_W), lambda p: (0, nb - 1, 0, p))
    return _pcall(
        body, name=name, grid=(N_PAIRS,),
        out_shape=jax.ShapeDtypeStruct((N_CHUNKS, nb, BLOCK, width), t.dtype),
        in_specs=[blk, pl.BlockSpec((1, N_CHUNKS, BLOCK, PAIR_W), lambda p: (p, 0, 0, 0))],
        out_specs=blk, input_output_aliases={0: 0},
        compiler_params=_cparams("parallel"),
    )(t.reshape(N_CHUNKS, nb, BLOCK, width), halo).reshape(S, width)


def _causal_mask(T):
    return lax.broadcasted_iota(jnp.int32, (T, T), 0) <= lax.broadcasted_iota(jnp.int32, (T, T), 1)


LOG2E = math.log2(math.e)
FLASH_SPLIT = 2
FLASH_ONES_ROWS = 16


def _mla_q_proj(cqn, wq_t, cos_t, sin_t, *, name):
    S, R = cqn.shape
    H, dqk, _ = wq_t.shape
    nq, half, T = cos_t.shape

    def body(x_ref, w_ref, c_ref, s_ref, o_ref):
        x = x_ref[...]
        for h in range(H):
            qt = lax.dot_general(w_ref[h], x, _NT, preferred_element_type=F32)
            t1, t2 = qt[C_NOPE:C_NOPE + half], qt[C_NOPE + half:]
            o_ref[h, 0, :C_NOPE] = qt[:C_NOPE].astype(MXU_DT)
            o_ref[h, 0, C_NOPE:C_NOPE + half] = (t1 * c_ref[0] - t2 * s_ref[0]).astype(MXU_DT)
            o_ref[h, 0, C_NOPE + half:] = (t1 * s_ref[0] + t2 * c_ref[0]).astype(MXU_DT)

    tab = pl.BlockSpec((1, half, T), lambda i: (i, 0, 0))
    return _pcall(
        body, name=name, grid=(nq,),
        out_shape=jax.ShapeDtypeStruct((H, nq, dqk, T), MXU_DT),
        in_specs=[pl.BlockSpec((T, R), lambda i: (i, 0)), pl.BlockSpec((H, dqk, R), lambda i: (0, 0, 0)), tab, tab],
        out_specs=pl.BlockSpec((H, 1, dqk, T), lambda i: (0, i, 0, 0)),
        compiler_params=_cparams("parallel"),
    )(cqn, wq_t, cos_t, sin_t)


def _mla_q_proj_bwd(dqt, cqn, wq_t, cos_t, sin_t, *, name):
    S, R = cqn.shape
    H, dqk, _ = wq_t.shape
    nq, half, T = cos_t.shape

    def body(g_ref, x_ref, w_ref, c_ref, s_ref, dx_ref, dw_ref):
        @pl.when(pl.program_id(0) == 0)
        def _():
            dw_ref[...] = jnp.zeros_like(dw_ref)

        x = x_ref[...]
        acc = jnp.zeros((T, R), F32)
        for h in range(H):
            g = g_ref[h, 0]
            g1, g2 = g[C_NOPE:C_NOPE + half], g[C_NOPE + half:]
            gq = jnp.concatenate([g[:C_NOPE], g1 * c_ref[0] + g2 * s_ref[0], g2 * c_ref[0] - g1 * s_ref[0]],
                                 axis=0).astype(MXU_DT)
            acc = acc + lax.dot_general(gq, w_ref[h], _TN, preferred_element_type=F32)
            dw_ref[h] += lax.dot_general(gq, x, _NN, preferred_element_type=F32)
        dx_ref[...] = acc

    tab = pl.BlockSpec((1, half, T), lambda i: (i, 0, 0))
    whole = pl.BlockSpec((H, dqk, R), lambda i: (0, 0, 0))
    return _pcall(
        body, name=name, grid=(nq,),
        out_shape=(jax.ShapeDtypeStruct((S, R), F32), jax.ShapeDtypeStruct((H, dqk, R), F32)),
        in_specs=[pl.BlockSpec((H, 1, dqk, T), lambda i: (0, i, 0, 0)), pl.BlockSpec((T, R), lambda i: (i, 0)), whole, tab, tab],
        out_specs=(pl.BlockSpec((T, R), lambda i: (i, 0)), whole),
        compiler_params=_cparams("arbitrary"),
    )(dqt, cqn, wq_t, cos_t, sin_t)


def _mla_kv_proj(ckvn, wk, wv, kr, kr_t, *, name):
    S, R = ckvn.shape
    H = wk.shape[0]
    nq, dr, T = kr_t.shape
    dqk = C_NOPE + dr
    wk_t, wv_t = wk.transpose(0, 2, 1), wv.transpose(0, 2, 1)

    def body(x_ref, wk_ref, wv_ref, wkt_ref, wvt_ref, kr_ref, krt_ref, kt_ref, vt_ref, kn_ref, vn_ref):
        x = x_ref[...]
        krt, krn = krt_ref[0].astype(MXU_DT), kr_ref[...].astype(MXU_DT)
        ones = jnp.where(lax.broadcasted_iota(jnp.int32, (FLASH_ONES_ROWS, T), 0) == 0, 1.0, 0.0).astype(MXU_DT)
        for h in range(H):
            kt_ref[h, 0, :C_NOPE] = lax.dot_general(wkt_ref[h], x, _NT, preferred_element_type=F32).astype(MXU_DT)
            kt_ref[h, 0, C_NOPE:] = krt
            vt_ref[h, 0, :C_V] = lax.dot_general(wvt_ref[h], x, _NT, preferred_element_type=F32).astype(MXU_DT)
            vt_ref[h, 0, C_V:] = ones
            kn_ref[h, :, :C_NOPE] = lax.dot_general(x, wk_ref[h], _NN, preferred_element_type=F32).astype(MXU_DT)
            kn_ref[h, :, C_NOPE:] = krn
            vn_ref[h] = lax.dot_general(x, wv_ref[h], _NN, preferred_element_type=F32).astype(MXU_DT)

    w_spec = pl.BlockSpec((H, R, C_NOPE), lambda i: (0, 0, 0))
    wt_spec = pl.BlockSpec((H, C_NOPE, R), lambda i: (0, 0, 0))
    return _pcall(
        body, name=name, grid=(nq,),
        out_shape=(jax.ShapeDtypeStruct((H, nq, dqk, T), MXU_DT),
                   jax.ShapeDtypeStruct((H, nq, C_V + FLASH_ONES_ROWS, T), MXU_DT),
                   jax.ShapeDtypeStruct((H, S, dqk), MXU_DT), jax.ShapeDtypeStruct((H, S, C_V), MXU_DT)),
        in_specs=[pl.BlockSpec((T, R), lambda i: (i, 0)), w_spec, w_spec, wt_spec, wt_spec,
                  pl.BlockSpec((T, dr), lambda i: (i, 0)), pl.BlockSpec((1, dr, T), lambda i: (i, 0, 0))],
        out_specs=(pl.BlockSpec((H, 1, dqk, T), lambda i: (0, i, 0, 0)),
                   pl.BlockSpec((H, 1, C_V + FLASH_ONES_ROWS, T), lambda i: (0, i, 0, 0)),
                   pl.BlockSpec((H, T, dqk), lambda i: (0, i, 0)), pl.BlockSpec((H, T, C_V), lambda i: (0, i, 0))),
        compiler_params=_cparams("parallel"),
    )(ckvn, wk, wv, wk_t, wv_t, kr, kr_t)


def _mla_kv_proj_bwd(dkt, dvt, ckvn, wk, wv, *, name):
    S, R = ckvn.shape
    H = wk.shape[0]
    _, nq, dqk, T = dkt.shape
    dr = dqk - C_NOPE
    wk_t, wv_t = wk.transpose(0, 2, 1), wv.transpose(0, 2, 1)

    def body(gk_ref, gv_ref, x_ref, wkt_ref, wvt_ref, dx_ref, dwk_ref, dwv_ref, dkr_ref):
        @pl.when(pl.program_id(0) == 0)
        def _():
            dwk_ref[...] = jnp.zeros_like(dwk_ref)
            dwv_ref[...] = jnp.zeros_like(dwv_ref)

        x = x_ref[...]
        acc = jnp.zeros((T, R), F32)
        dkr = jnp.zeros((dr, T), F32)
        for h in range(H):
            gk = gk_ref[h, 0, :C_NOPE].astype(MXU_DT)
            gv = gv_ref[h, 0].astype(MXU_DT)
            acc = acc + (lax.dot_general(gk, wkt_ref[h], _TN, preferred_element_type=F32)
                         + lax.dot_general(gv, wvt_ref[h], _TN, preferred_element_type=F32))
            dwk_ref[h] += lax.dot_general(gk, x, _NN, preferred_element_type=F32)
            dwv_ref[h] += lax.dot_general(gv, x, _NN, preferred_element_type=F32)
            dkr = dkr + gk_ref[h, 0, C_NOPE:]
        dx_ref[...] = acc
        dkr_ref[0] = dkr

    wt_spec = pl.BlockSpec((H, C_NOPE, R), lambda i: (0, 0, 0))
    return _pcall(
        body, name=name, grid=(nq,),
        out_shape=(jax.ShapeDtypeStruct((S, R), F32), jax.ShapeDtypeStruct((H, C_NOPE, R), F32),
                   jax.ShapeDtypeStruct((H, C_V, R), F32), jax.ShapeDtypeStruct((nq, dr, T), F32)),
        in_specs=[pl.BlockSpec((H, 1, dqk, T), lambda i: (0, i, 0, 0)), pl.BlockSpec((H, 1, C_V, T), lambda i: (0, i, 0, 0)),
                  pl.BlockSpec((T, R), lambda i: (i, 0)), wt_spec, wt_spec],
        out_specs=(pl.BlockSpec((T, R), lambda i: (i, 0)), wt_spec, wt_spec, pl.BlockSpec((1, dr, T), lambda i: (i, 0, 0))),
        compiler_params=_cparams("arbitrary"),
    )(dkt, dvt, ckvn, wk_t, wv_t)


def _mla_out_proj(ot, w_o, h, *, name):
    H, nq, dv, T = ot.shape
    D = w_o.shape[2]

    def body(o_ref, w_ref, h_ref, out_ref):
        acc = h_ref[...]
        for hd in range(H):
            acc = acc + lax.dot_general(o_ref[hd, 0].astype(MXU_DT), w_ref[hd], _TN, preferred_element_type=F32)
        out_ref[...] = acc

    row = pl.BlockSpec((T, D), lambda i: (i, 0))
    return _pcall(
        body, name=name, grid=(nq,),
        out_shape=jax.ShapeDtypeStruct(h.shape, F32),
        in_specs=[pl.BlockSpec((H, 1, dv, T), lambda i: (0, i, 0, 0)), pl.BlockSpec((H, dv, D), lambda i: (0, 0, 0)), row],
        out_specs=row,
        compiler_params=_cparams("parallel"),
    )(ot, w_o, h)


def _mla_out_proj_bwd(dh, ot, w_o, *, name):
    H, nq, dv, T = ot.shape
    D = w_o.shape[2]

    def body(dh_ref, o_ref, w_ref, dot_ref, del_ref, dw_ref):
        @pl.when(pl.program_id(0) == 0)
        def _():
            dw_ref[...] = jnp.zeros_like(dw_ref)

        dhb = dh_ref[...].astype(MXU_DT)
        for hd in range(H):
            o = o_ref[hd, 0]
            d = lax.dot_general(w_ref[hd], dhb, _NT, preferred_element_type=F32)
            dot_ref[hd, 0] = d
            del_ref[hd, 0] = jnp.sum(d * o, axis=0, keepdims=True)
            dw_ref[hd] += lax.dot_general(o.astype(MXU_DT), dhb, _NN, preferred_element_type=F32)

    tile = pl.BlockSpec((H, 1, dv, T), lambda i: (0, i, 0, 0))
    whole = pl.BlockSpec((H, dv, D), lambda i: (0, 0, 0))
    return _pcall(
        body, name=name, grid=(nq,),
        out_shape=(jax.ShapeDtypeStruct(ot.shape, F32), jax.ShapeDtypeStruct((H, nq, 1, T), F32),
                   jax.ShapeDtypeStruct(w_o.shape, F32)),
        in_specs=[pl.BlockSpec((T, D), lambda i: (i, 0)), tile, whole],
        out_specs=(tile, pl.BlockSpec((H, 1, 1, T), lambda i: (0, i, 0, 0)), whole),
        compiler_params=_cparams("arbitrary"),
    )(dh, ot, w_o)


def _flash_fwd(qt, k, vt1):
    H, nq, dqk, T = qt.shape
    S = k.shape[1]
    dva = vt1.shape[2]
    dv = dva - FLASH_ONES_ROWS
    scale = dqk ** -0.5
    c = scale * LOG2E
    th = T // FLASH_SPLIT

    def body(qt_ref, k_ref, vt_ref, ot_ref, lse_ref, sa_ref, sb_ref):
        i = pl.program_id(1)

        def scores(j):
            kb = k_ref[0, pl.ds(pl.multiple_of(j * T, T), T), :]
            return lax.dot_general(kb, qt_ref[0, 0], _NN, preferred_element_type=F32)

        def softmax_pv(s_ref, j, carry, masked):
            m, acc = carry
            raw = s_ref[...]
            if masked:
                raw = jnp.where(_causal_mask(T), raw, NEG)
            m_new = jnp.maximum(m, jnp.max(raw, axis=0, keepdims=True))
            alpha = jnp.exp2((m - m_new) * c)
            pb = jnp.exp2((raw - m_new) * c).astype(MXU_DT)
            acc = acc * alpha + lax.dot_general(vt_ref[0, j], pb, _NN, preferred_element_type=F32)
            return m_new, acc

        def pair(p, carry):
            j = 2 * p
            sb_ref[...] = scores(j + 1)
            carry = softmax_pv(sa_ref, j, carry, False)
            sa_ref[...] = scores(j + 2)
            return softmax_pv(sb_ref, j + 1, carry, False)

        def even_tail(carry):
            return softmax_pv(sa_ref, i, carry, True)

        def odd_tail(carry):
            sb_ref[...] = scores(i)
            carry = softmax_pv(sa_ref, i - 1, carry, False)
            return softmax_pv(sb_ref, i, carry, True)

        sa_ref[...] = scores(0)
        carry = lax.fori_loop(0, i // 2, pair, (jnp.full((1, T), NEG, F32), jnp.zeros((dva, T), F32)))
        m, acc = lax.cond(i % 2 == 0, even_tail, odd_tail, carry)
        l = acc[dv:dv + 1]
        ot_ref[0, 0] = acc[:dv] / l
        lse_ref[0, 0] = m * scale + jnp.log(l)

    return _pcall(
        body, name="flash_fwd", grid=(H, nq),
        out_shape=(jax.ShapeDtypeStruct((H, nq, dv, T), F32), jax.ShapeDtypeStruct((H, nq, 1, T), F32)),
        in_specs=[pl.BlockSpec((1, 1, dqk, T), lambda h, i: (h, i, 0, 0)),
                  pl.BlockSpec((1, S, dqk), lambda h, i: (h, 0, 0)),
                  pl.BlockSpec((1, nq, dva, T), lambda h, i: (h, 0, 0, 0))],
        out_specs=(pl.BlockSpec((1, 1, dv, T), lambda h, i: (h, i, 0, 0)),
                   pl.BlockSpec((1, 1, 1, T), lambda h, i: (h, i, 0, 0))),
        scratch_shapes=[pltpu.VMEM((T, T), F32), pltpu.VMEM((T, T), F32)],
        compiler_params=_cparams("parallel", "parallel"),
    )(qt, k, vt1)


def _flash_delta(ot, dot):
    H, nq, dv, T = ot.shape

    def body(o_ref, do_ref, d_ref):
        d_ref[0, 0] = jnp.sum(o_ref[0, 0] * do_ref[0, 0], axis=0, keepdims=True)

    spec = pl.BlockSpec((1, 1, dv, T), lambda h, i: (h, i, 0, 0))
    return _pcall(
        body, name="flash_delta", grid=(H, nq),
        out_shape=jax.ShapeDtypeStruct((H, nq, 1, T), F32),
        in_specs=[spec, spec], out_specs=pl.BlockSpec((1, 1, 1, T), lambda h, i: (h, i, 0, 0)),
        compiler_params=_cparams("parallel", "parallel"),
    )(ot, dot)


def _flash_bwd(qt, k, kt, v, dot, lse, delta):
    H, nq, dqk, T = qt.shape
    dv_ = v.shape[2]
    scale = dqk ** -0.5
    c = scale * LOG2E
    th = T // FLASH_SPLIT

    def body(qt_ref, k_ref, kt_ref, v_ref, dot_ref, lse_ref, del_ref, dqt_ref, dkt_ref, dvt_ref,
             sa_ref, pa_ref, sb_ref, pb_ref):
        j = pl.program_id(1)

        @pl.when(j == 0)
        def _():
            dqt_ref[...] = jnp.zeros_like(dqt_ref)

        n_un = nq - 1 - j

        def issue(i, s_ref, dp_ref):
            s_ref[...] = lax.dot_general(k_ref[0], qt_ref[0, i], _NN, preferred_element_type=F32)
            dp_ref[...] = lax.dot_general(v_ref[0], dot_ref[0, i].astype(MXU_DT), _NN, preferred_element_type=F32)

        def consume(i, s_ref, dp_ref, carry, masked):
            dkt, dvt = carry
            raw = s_ref[...]
            if masked:
                raw = jnp.where(_causal_mask(T), raw, NEG)
            p = jnp.exp2(raw * c - lse_ref[0, i] * LOG2E)
            dsb = (p * (dp_ref[...] - del_ref[0, i])).astype(MXU_DT)
            dvt = dvt + lax.dot_general(dot_ref[0, i].astype(MXU_DT), p.astype(MXU_DT), _NT, preferred_element_type=F32)
            dkt = dkt + lax.dot_general(qt_ref[0, i], dsb, _NT, preferred_element_type=F32)
            dqt_ref[0, i] += lax.dot_general(kt_ref[0, 0], dsb, _NN, preferred_element_type=F32) * scale
            return dkt, dvt

        def pair(p, carry):
            i0 = j + 1 + 2 * p
            issue(i0 + 1, sb_ref, pb_ref)
            carry = consume(i0, sa_ref, pa_ref, carry, False)
            issue(jnp.where(2 * p + 2 < n_un, i0 + 2, j), sa_ref, pa_ref)
            return consume(i0 + 1, sb_ref, pb_ref, carry, False)

        def even_tail(carry):
            return consume(j, sa_ref, pa_ref, carry, True)

        def odd_tail(carry):
            issue(j, sb_ref, pb_ref)
            carry = consume(nq - 1, sa_ref, pa_ref, carry, False)
            return consume(j, sb_ref, pb_ref, carry, True)

        issue(jnp.where(n_un > 0, j + 1, j), sa_ref, pa_ref)
        carry = lax.fori_loop(0, n_un // 2, pair, (jnp.zeros((dqk, T), F32), jnp.zeros((dv_, T), F32)))
        dkt, dvt = lax.cond(n_un % 2 == 0, even_tail, odd_tail, carry)
        dkt_ref[0, 0] = dkt * scale
        dvt_ref[0, 0] = dvt

    whole = lambda d: pl.BlockSpec((1, nq, d, T), lambda h, j: (h, 0, 0, 0))
    tile_t = lambda d: pl.BlockSpec((1, 1, d, T), lambda h, j: (h, j, 0, 0))
    return _pcall(
        body, name="flash_bwd", grid=(H, nq),
        out_shape=(jax.ShapeDtypeStruct((H, nq, dqk, T), F32), jax.ShapeDtypeStruct((H, nq, dqk, T), F32),
                   jax.ShapeDtypeStruct((H, nq, dv_, T), F32)),
        in_specs=[whole(dqk),
                  pl.BlockSpec((1, T, dqk), lambda h, j: (h, j, 0)),
                  tile_t(dqk),
                  pl.BlockSpec((1, T, dv_), lambda h, j: (h, j, 0)),
                  whole(dv_), whole(1), whole(1)],
        out_specs=(whole(dqk), tile_t(dqk), tile_t(dv_)),
        scratch_shapes=[pltpu.VMEM((T, T), F32) for _ in range(4)],
        compiler_params=_cparams("arbitrary", "arbitrary"),
    )(qt, k, kt, v, dot, lse, delta)


def _adamw(parts, w, m, v, *, name, tr=512):
    P, R, C = parts.shape
    tr = min(tr, R)
    assert R % tr == 0

    def body(p_ref, w_ref, m_ref, v_ref, g_ref, d_ref, m2_ref, v2_ref):
        g = p_ref[0].astype(F32)
        for s in range(1, P):
            g = g + p_ref[s].astype(F32)
        m2 = ADAM_B1 * m_ref[...] + (1.0 - ADAM_B1) * g
        v2 = ADAM_B2 * v_ref[...] + (1.0 - ADAM_B2) * jnp.square(g)
        m_hat = m2 / (1.0 - ADAM_B1 ** ADAM_STEP)
        v_hat = v2 / (1.0 - ADAM_B2 ** ADAM_STEP)
        g_ref[...] = g
        d_ref[...] = -ADAM_LR * (m_hat / (jnp.sqrt(v_hat) + ADAM_EPS) + ADAM_WD * w_ref[...])
        m2_ref[...] = m2
        v2_ref[...] = v2

    row = pl.BlockSpec((tr, C), lambda i: (i, 0))
    out = jax.ShapeDtypeStruct((R, C), F32)
    return _pcall(
        body, name=name, grid=(R // tr,),
        out_shape=(out, out, out, out),
        in_specs=[pl.BlockSpec((P, tr, C), lambda i: (0, i, 0)), row, row, row],
        out_specs=(row, row, row, row),
        compiler_params=_cparams("parallel"),
    )(parts, w, m, v)


def _bias_tables():
    i = np.arange(BLOCK)[:, None]
    j = np.arange(2 * BLOCK)[None, :]
    dist = i + BLOCK - j
    out = []
    for dil, max_dist in [(1, A_WINDOW - 1)] + [(d, w // d) for w, d in B_BRANCHES]:
        n = np.maximum(dist, 0) * dil
        max_exact = NUM_BUCKETS // 2
        nf = np.maximum(n, 1).astype(np.float64)
        val = np.log(nf / max_exact) / math.log(MAX_DISTANCE / max_exact) * (NUM_BUCKETS - max_exact)
        inband = (dist >= 0) & (dist <= max_dist)
        frac = np.abs(val - np.round(val))
        last = NUM_BUCKETS - 1 - max_exact
        assert np.all((frac > 2e-5) | (n <= max_exact) | (val >= last) | ~inband)
        large = max_exact + val.astype(np.int64)
        bucket = np.where(n < max_exact, n, np.minimum(large, NUM_BUCKETS - 1))
        onehot = (bucket[..., None] == np.arange(NUM_BUCKETS)).astype(np.float32)
        out.append((onehot.reshape(-1, NUM_BUCKETS), inband))
    return out


def _make_bias(rel_bias):
    tabs = _bias_tables()
    groups = [(0, A_Q_HEADS)] + [(A_Q_HEADS + g * B_HPB, B_HPB) for g in range(len(B_BRANCHES))]
    parts = []
    for (onehot, inband), (h0, nh) in zip(tabs, groups):
        b = jnp.dot(jnp.asarray(onehot), rel_bias[:, h0:h0 + nh], precision=lax.Precision.HIGHEST)
        b = b.reshape(BLOCK, 2 * BLOCK, nh)
        b = jnp.where(jnp.asarray(inband)[..., None], b, NEG)
        parts.append(b.transpose(2, 0, 1))
    return jnp.concatenate(parts, axis=0)


A_W = A_Q_HEADS * HEAD_DIM
B_W = B_HPB * HEAD_DIM


def _perm_rows(x, d):
    return x if d == 1 else x.reshape(x.shape[0] // d, d, -1).transpose(1, 0, 2).reshape(x.shape)


def _unperm_rows(x, d):
    return x if d == 1 else x.reshape(d, x.shape[0] // d, -1).transpose(1, 0, 2).reshape(x.shape)


def _even_post(o_all, lse):
    S = o_all.shape[0]
    outs, lses = [], []
    for g, (_, d) in enumerate(B_BRANCHES):
        w0 = A_W + g * B_W
        outs.append(_unperm_rows(o_all[:, w0:w0 + B_W], d))
        lg = lse[A_PAIRS + 2 * g:A_PAIRS + 2 * g + 2].transpose(1, 2, 4, 0, 3).reshape(S, B_HPB)
        lses.append(_unperm_rows(lg, d))
    wts = jax.nn.softmax(jnp.stack(lses), axis=0)
    widen = jnp.asarray(np.kron(np.eye(B_HPB), np.ones((1, HEAD_DIM))), F32)
    out_b = sum(jnp.dot(wts[g], widen, precision=lax.Precision.HIGHEST) * outs[g] for g in range(len(B_BRANCHES)))
    return jnp.concatenate([o_all[:, :A_W], out_b], axis=-1)


def _rope_tables(S, r):
    inv = ROPE_THETA ** (-jnp.arange(0, r, 2, dtype=jnp.float32) / r)
    ang = jnp.arange(S, dtype=jnp.float32)[:, None] * inv[None, :]
    return jnp.cos(ang), jnp.sin(ang)


def _rope(t):
    S, r = t.shape[1], t.shape[-1]
    shape = (1, S) + (1,) * (t.ndim - 3) + (r // 2,)
    cos, sin = (a.reshape(shape) for a in _rope_tables(S, r))
    t1, t2 = t[..., :r // 2], t[..., r // 2:]
    return jnp.concatenate([t1 * cos - t2 * sin, t1 * sin + t2 * cos], axis=-1)


def _mla_pre(q_lin, kv_lin, kr_raw):
    S = q_lin.shape[0]
    q = q_lin.reshape(1, S, C_HEADS, C_QK)
    qf = jnp.concatenate([q[..., :C_NOPE], _rope(q[..., C_NOPE:])], axis=-1)[0]
    kv = kv_lin.reshape(S, C_HEADS, C_NOPE + C_V)
    kr = _rope(kr_raw[None])[0]
    kf = jnp.concatenate([kv[..., :C_NOPE], jnp.broadcast_to(kr[:, None, :], (S, C_HEADS, C_ROPE))], axis=-1)
    return qf, kf, kv[..., C_NOPE:]


def _to_tiles_t(t, T):
    S, H, d = t.shape
    return t.reshape(S // T, T, H, d).transpose(2, 0, 3, 1)


def _from_tiles_t(t):
    H, n, d, T = t.shape
    return t.transpose(1, 3, 0, 2).reshape(n * T, H, d)


_BIG = (("w_in_ab", 2), ("w_out_ab", 2), ("w_down_c", 1), ("w_uq_c", 2), ("w_ukv_c", 2), ("w_o_c", 2),
        ("w_mlp_up", 2), ("w_mlp_down", 1))
_ROW_ALIGN = 512


def _pack_rows(arrs):
    rows = [a.reshape(-1, 128) for a in arrs]
    n = sum(r.shape[0] for r in rows)
    pad = (-n) % _ROW_ALIGN
    if pad:
        rows.append(jnp.zeros((pad, 128), rows[0].dtype))
    return jnp.concatenate(rows, axis=0)


def _pack_rows_per_device(arrs):
    rows = [a.reshape(N_DEV, -1, 128) for a in arrs]
    n = sum(r.shape[1] for r in rows)
    pad = (-n) % _ROW_ALIGN
    if pad:
        rows.append(jnp.zeros((N_DEV, pad, 128), rows[0].dtype))
    return jnp.concatenate(rows, axis=1)


def _unpack_rows(buf, shapes):
    out, r0 = [], 0
    for shp in shapes:
        n = math.prod(shp) // 128
        out.append(buf[..., r0:r0 + n, :].reshape(buf.shape[:-2] + tuple(shp)))
        r0 += n
    return out


def _layer_tensors(l):
    att = [("w_in_ab", l // 2), ("w_out_ab", l // 2)] if l % 2 == 0 else \
          [("w_down_c", l // 2), ("w_uq_c", l // 2), ("w_ukv_c", l // 2), ("w_o_c", l // 2)]
    return att + [("w_mlp_up", l), ("w_mlp_down", l)]


def _exchange_groups():
    first = _layer_tensors(0)
    return [first[:-2], first[-2:]] + [_layer_tensors(l) for l in range(1, DEPTH)]


def _gathered_to_full(g, axis):
    if axis == 2:
        return g.transpose(1, 0, 2).reshape(g.shape[1], N_DEV * g.shape[2])
    return g.reshape(N_DEV * g.shape[1], g.shape[2])


def _full_to_shards(t, axis):
    a, b = t.shape
    if axis == 2:
        return t.reshape(a, N_DEV, b // N_DEV).transpose(1, 0, 2)
    return t.reshape(N_DEV, a // N_DEV, b)


def _pad_rows8(a):
    flat = a.reshape(-1)
    n = -(-flat.shape[0] // 1024) * 1024
    return jnp.pad(flat, (0, n - flat.shape[0])).reshape(-1, 128)


def _even_fwd(xn, h, w_in, w_out, bias, sinks_row, l, ride=None):
    c0 = A_IN + 3 * B_W
    proj = lax.empty((xn.shape[0], w_in.shape[1]), MXU_DT)
    proj = _matmul(xn, w_in[:, :c0], out_dtype=MXU_DT, tm=1024, tn=512, name=f"even_in_{l}", ride=ride, into=(proj, 0))
    proj, landed = proj if ride is not None else (proj, None)
    for g, (_, d) in list(enumerate(B_BRANCHES))[1:]:
        col0 = A_IN + 3 * B_W * g
        proj = _matmul(_perm_rows(xn, d), w_in[:, col0:col0 + 3 * B_W], out_dtype=MXU_DT, tm=1024, tn=3 * B_W,
                       name=f"even_in_dil{d}_{l}", into=(proj, col0))
    o_all, lse = _banded_fwd(proj, bias, sinks_row)
    attn, post_vjp = jax.vjp(_even_post, o_all, lse)
    attn = attn.astype(MXU_DT)
    h1 = _matmul(attn, w_out, epi='add', extra=h, tm=1024, tn=512, name=f"even_out_{l}")
    return h1, (proj, attn, post_vjp), landed


def _even_bwd(dh, xn, ctx, w_in, w_out, bias, sinks_row, l, norm, ride=None):
    proj, attn, post_vjp = ctx
    d_attn = _matmul(dh, w_out, trans_b=True, tm=1024, tn=768, name=f"even_out_dx_{l}")
    g_w_out = _matmul_tn(attn, dh, tk=768, tn=512, name=f"even_out_dw_{l}")
    do_all, dlse = post_vjp(d_attn)
    dq, dk, dv, dkh, dvh, dbias_t, dsinks = _banded_bwd(proj, bias.transpose(0, 2, 1), sinks_row, do_all, dlse)
    dbias = dbias_t.transpose(0, 2, 1)
    dk = _halo_fold(dk, dkh, f"halo_k_{l}")
    dv = _halo_fold(dv, dvh, f"halo_v_{l}")

    def kv_sum(t):
        heads = [t[:, i * HEAD_DIM:(i + 1) * HEAD_DIM] for i in range(A_Q_HEADS)]
        return jnp.concatenate([sum(heads[j * A_GROUP:(j + 1) * A_GROUP]) for j in range(A_KV_HEADS)], axis=1)

    groups = [jnp.concatenate([dq[:, :A_W], kv_sum(dk), kv_sum(dv)], axis=1).astype(MXU_DT)]
    for g, (_, d) in enumerate(B_BRANCHES):
        cols = slice(A_W + g * B_W, A_W + (g + 1) * B_W)
        grp = jnp.concatenate([dq[:, cols], dk[:, cols], dv[:, cols]], axis=1).astype(MXU_DT)
        groups.append(_unperm_rows(grp, d))
    dproj = jnp.concatenate(groups, axis=1)
    g_w_in = _matmul_tn(xn, dproj, tk=512, tn=1024, name=f"even_in_dw_{l}")
    res = _matmul(dproj, w_in, trans_b=True, epi='norm_bwd', norm=norm, tm=512, tn=w_in.shape[0],
                  name=f"even_in_dx_{l}", ride=ride)
    dh_new, g_norm, landed = res if ride is not None else (*res, None)
    return dh_new, g_norm[0], g_w_in, g_w_out, dbias, dsinks[:A_Q_HEADS, 0, 0], landed


def _mla_fwd(xn, h, w_down, q_norm, w_uq, kv_norm, w_ukv, w_o, l):
    S = xn.shape[0]
    T = min(FLASH_T, S)
    down = _matmul(xn, w_down, tm=1024, tn=768, name=f"mla_down_{l}")
    c_q, c_kv, kr_raw = down[:, :C_Q_RANK], down[:, C_Q_RANK:C_Q_RANK + C_KV_RANK], down[:, C_Q_RANK + C_KV_RANK:C_DOWN]
    cqn = _rmsnorm(c_q, q_norm, out_dtype=MXU_DT, name=f"mla_qnorm_{l}")
    ckvn = _rmsnorm(c_kv, kv_norm, out_dtype=MXU_DT, name=f"mla_kvnorm_{l}")
    cos, sin = _rope_tables(S, C_ROPE)
    to_t = lambda t: t.reshape(S // T, T, -1).transpose(0, 2, 1)
    qt = _mla_q_proj(cqn, w_uq.T.reshape(C_HEADS, C_QK, C_Q_RANK), to_t(cos), to_t(sin), name=f"mla_uq_{l}")
    w_kv = w_ukv.reshape(C_KV_RANK, C_HEADS, C_NOPE + C_V).transpose(1, 0, 2)
    kr = _rope(kr_raw[None])[0]
    kt, vt1, kn, vn = _mla_kv_proj(ckvn, w_kv[..., :C_NOPE], w_kv[..., C_NOPE:], kr, to_t(kr), name=f"mla_ukv_{l}")
    ot, lse = _flash_fwd(qt, kn, vt1)
    h1 = _mla_out_proj(ot, w_o.reshape(C_HEADS, C_V, -1), h, name=f"mla_o_{l}")
    return h1, (c_q, c_kv, cqn, ckvn, qt, kn, kt, vn, ot, lse)


def _mla_bwd(dh, xn, ctx, w_down, q_norm, w_uq, kv_norm, w_ukv, w_o, l, norm):
    c_q, c_kv, cqn, ckvn, qt, kn, kt, vn, ot, lse = ctx
    S = xn.shape[0]
    T = qt.shape[-1]
    dot, delta, dw_o = _mla_out_proj_bwd(dh, ot, w_o.reshape(C_HEADS, C_V, -1), name=f"mla_o_bwd_{l}")
    g_w_o = dw_o.reshape(w_o.shape)
    dqt, dkt, dvt = _flash_bwd(qt, kn, kt, vn, dot, lse, delta)
    cos, sin = _rope_tables(S, C_ROPE)
    to_t = lambda t: t.reshape(S // T, T, -1).transpose(0, 2, 1)
    dcqn, dwq_t = _mla_q_proj_bwd(dqt, cqn, w_uq.T.reshape(C_HEADS, C_QK, C_Q_RANK), to_t(cos), to_t(sin),
                                  name=f"mla_uq_bwd_{l}")
    g_w_uq = dwq_t.reshape(C_HEADS * C_QK, C_Q_RANK).T
    w_kv = w_ukv.reshape(C_KV_RANK, C_HEADS, C_NOPE + C_V).transpose(1, 0, 2)
    dckvn, dwk_t, dwv_t, dkr_t = _mla_kv_proj_bwd(dkt, dvt, ckvn, w_kv[..., :C_NOPE], w_kv[..., C_NOPE:],
                                                  name=f"mla_ukv_bwd_{l}")
    g_w_ukv = jnp.concatenate([dwk_t, dwv_t], axis=1).reshape(C_HEADS * (C_NOPE + C_V), C_KV_RANK).T
    _, rope_vjp = jax.vjp(lambda t: _rope(t[None])[0], jnp.zeros((S, C_ROPE), F32))
    (dkr_raw,) = rope_vjp(dkr_t.transpose(0, 2, 1).reshape(S, C_ROPE))
    dc_q, g_q_norm = _rmsnorm_bwd(c_q, q_norm, dcqn, None, name=f"mla_qnorm_bwd_{l}")
    dc_kv, g_kv_norm = _rmsnorm_bwd(c_kv, kv_norm, dckvn, None, name=f"mla_kvnorm_bwd_{l}")
    ddown = jnp.concatenate([dc_q, dc_kv, dkr_raw, jnp.zeros((S, C_DOWN_PAD - C_DOWN), F32)], axis=1).astype(MXU_DT)
    g_w_down = _matmul_tn(xn, ddown, tk=512, tn=768, name=f"mla_down_dw_{l}")[:, :C_DOWN]
    dh_new, g_norm = _matmul(ddown, w_down, trans_b=True, epi='norm_bwd', norm=norm, tm=512, tn=w_down.shape[0],
                             name=f"mla_down_dx_{l}")
    return dh_new, g_norm[0], g_w_down, g_q_norm[0], g_w_uq, g_kv_norm[0], g_w_ukv, g_w_o


def kernel(x, rel_bias, attn_norm, mlp_norm, final_norm, w_in_ab, sinks, w_out_ab, w_down_c, q_norm_c, w_uq_c, kv_norm_c, w_ukv_c, w_o_c, w_mlp_up, w_mlp_down, loss_target, m_rel_bias, m_attn_norm, m_mlp_norm, m_final_norm, m_w_in_ab, m_sinks, m_w_out_ab, m_w_down_c, m_q_norm_c, m_w_uq_c, m_kv_norm_c, m_w_ukv_c, m_w_o_c, m_w_mlp_up, m_w_mlp_down, v_rel_bias, v_attn_norm, v_mlp_norm, v_final_norm, v_w_in_ab, v_sinks, v_w_out_ab, v_w_down_c, v_q_norm_c, v_w_uq_c, v_kv_norm_c, v_w_ukv_c, v_w_o_c, v_w_mlp_up, v_w_mlp_down):
    W = dict(w_in_ab=w_in_ab, w_out_ab=w_out_ab, w_down_c=w_down_c, w_uq_c=w_uq_c, w_ukv_c=w_ukv_c, w_o_c=w_o_c,
             w_mlp_up=w_mlp_up, w_mlp_down=w_mlp_down)
    Mo = dict(w_in_ab=m_w_in_ab, w_out_ab=m_w_out_ab, w_down_c=m_w_down_c, w_uq_c=m_w_uq_c, w_ukv_c=m_w_ukv_c,
              w_o_c=m_w_o_c, w_mlp_up=m_w_mlp_up, w_mlp_down=m_w_mlp_down)
    Vo = dict(w_in_ab=v_w_in_ab, w_out_ab=v_w_out_ab, w_down_c=v_w_down_c, w_uq_c=v_w_uq_c, w_ukv_c=v_w_ukv_c,
              w_o_c=v_w_o_c, w_mlp_up=v_w_mlp_up, w_mlp_down=v_w_mlp_down)
    S = x.shape[1]
    me = 4 * lax.axis_index("x") + 2 * lax.axis_index("y") + lax.axis_index("c")
    axis_of = dict(_BIG)

    groups = _exchange_groups()

    def pack(src, gi):
        return _pack_rows([src[n][i] for n, i in groups[gi]])

    def unpack_group(gathered, gi):
        shapes = [W[n].shape[1:] for n, _ in groups[gi]]
        return {n: _gathered_to_full(g, axis_of[n])
                for (n, _), g in zip(groups[gi], _unpack_rows(gathered, shapes))}

    def send_of(G, gi):
        return _pack_rows_per_device([_full_to_shards(G[n], axis_of[n]) for n, _ in groups[gi]]).astype(MXU_DT)

    w_packs = [pack(W, gi) for gi in range(len(groups))]
    gathered = _exchange(w_packs[0].astype(MXU_DT), False, "gather_weights_0")
    gains = _exchange(jnp.concatenate([_pad_rows8(q_norm_c), _pad_rows8(kv_norm_c)], axis=0), False, "gather_gains")
    n_odd = q_norm_c.shape[0]
    q_norm_full = gains[:, 0].reshape(N_DEV, -1)[:, :q_norm_c.size].reshape(N_DEV, n_odd, -1).transpose(1, 0, 2).reshape(n_odd, C_Q_RANK)
    kv_norm_full = gains[:, 8].reshape(N_DEV, -1)[:, :kv_norm_c.size].reshape(N_DEV, n_odd, -1).transpose(1, 0, 2).reshape(n_odd, C_KV_RANK)

    bias, bias_vjp = jax.vjp(_make_bias, rel_bias)
    sink_rows = [jnp.broadcast_to(jnp.concatenate([sinks[e], jnp.full((B_HEADS,), NEG, F32)])[:, None, None],
                                  (N_BIAS_HEADS, 1, 128)) for e in range(sinks.shape[0])]

    h = x[0]
    saved = []
    for l in range(DEPTH):
        full = unpack_group(gathered, 0 if l == 0 else l + 1)
        if l % 2 == 1:
            full["w_down_c"] = jnp.pad(full["w_down_c"], ((0, 0), (0, C_DOWN_PAD - C_DOWN)))
        xn = _rmsnorm(h, attn_norm[l], out_dtype=MXU_DT, name=f"attn_norm_{l}")
        if l == 0:
            h1, ctx, gathered_mlp = _even_fwd(xn, h, full["w_in_ab"], full["w_out_ab"], bias, sink_rows[0], l,
                                              ride=(w_packs[1].astype(MXU_DT), False))
            full.update(unpack_group(gathered_mlp, 1))
        elif l % 2 == 0:
            h1, ctx, _ = _even_fwd(xn, h, full["w_in_ab"], full["w_out_ab"], bias, sink_rows[l // 2], l)
        else:
            o = l // 2
            h1, ctx = _mla_fwd(xn, h, full["w_down_c"], q_norm_full[o], full["w_uq_c"], kv_norm_full[o],
                               full["w_ukv_c"], full["w_o_c"], l)
        xn2 = _rmsnorm(h1, mlp_norm[l], out_dtype=MXU_DT, name=f"mlp_norm_{l}")
        if l + 1 < DEPTH:
            act, slope, gathered = _matmul(xn2, full["w_mlp_up"], out_dtype=MXU_DT, epi='relu2', tm=1024, tn=512,
                                           name=f"mlp_up_{l}", ride=(w_packs[l + 2].astype(MXU_DT), False))
        else:
            act, slope = _matmul(xn2, full["w_mlp_up"], out_dtype=MXU_DT, epi='relu2', tm=1024, tn=512,
                                 name=f"mlp_up_{l}")
        h2 = _matmul(act, full["w_mlp_down"], epi='add', extra=h1, tm=512, tn=512, name=f"mlp_down_{l}")
        saved.append((h, xn, h1, xn2, act, slope, ctx, full))
        h = h2

    loss_row, dh, g_final = _loss_head(h, loss_target[0], final_norm)

    g_attn_norm, g_mlp_norm = [None] * DEPTH, [None] * DEPTH
    g_sinks, g_qn, g_kvn = [None] * sinks.shape[0], [None] * n_odd, [None] * n_odd
    dbias_total = None
    landed = [None] * len(groups)
    send = None
    for l in reversed(range(DEPTH)):
        h0, xn, h1, xn2, act, slope, ctx, full = saved[l]
        G = {}
        if send is None:
            du = _matmul(dh, full["w_mlp_down"], trans_b=True, out_dtype=MXU_DT, epi='mul', extra=slope,
                         tm=1024, tn=512, name=f"mlp_down_dx_{l}")
        else:
            du, landed[l + 2] = _matmul(dh, full["w_mlp_down"], trans_b=True, out_dtype=MXU_DT, epi='mul', extra=slope,
                                        tm=1024, tn=512, name=f"mlp_down_dx_{l}", ride=(send, True))
        G["w_mlp_down"] = _matmul_tn(act, dh, tk=512, tn=1024, name=f"mlp_down_dw_{l}")
        G["w_mlp_up"] = _matmul_tn(xn2, du, tk=512, tn=1024, name=f"mlp_up_dw_{l}")
        dh, g = _matmul(du, full["w_mlp_up"], trans_b=True, epi='norm_bwd', norm=(h1, mlp_norm[l], dh),
                        tm=512, tn=D_MODEL, name=f"mlp_up_dx_{l}")
        g_mlp_norm[l] = g[0]
        if l % 2 == 0:
            e = l // 2
            dh, g_attn_norm[l], G["w_in_ab"], G["w_out_ab"], dbias, g_sinks[e], landed_mlp = _even_bwd(
                dh, xn, ctx, full["w_in_ab"], full["w_out_ab"], bias, sink_rows[e], l, (h0, attn_norm[l], dh),
                ride=(send_of(G, 1), True) if l == 0 else None)
            if l == 0:
                landed[1] = landed_mlp
            dbias_total = dbias if dbias_total is None else dbias_total + dbias
        else:
            o = l // 2
            dh, g_attn_norm[l], G["w_down_c"], g_qn[o], G["w_uq_c"], g_kvn[o], G["w_ukv_c"], G["w_o_c"] = _mla_bwd(
                dh, xn, ctx, full["w_down_c"], q_norm_full[o], full["w_uq_c"], kv_norm_full[o],
                full["w_ukv_c"], full["w_o_c"], l, (h0, attn_norm[l], dh))
        send = send_of(G, 0 if l == 0 else l + 1)
    landed[0] = _exchange(send, True, "scatter_grads_0")
    grad_x = dh[None]
    (g_rel_bias,) = bias_vjp(dbias_total)

    big = [{}, {}, {}, {}]
    for gi in range(len(groups)):
        outs = _adamw(landed[gi], w_packs[gi], pack(Mo, gi), pack(Vo, gi), name=f"adamw_{gi}")
        shapes = [W[n].shape[1:] for n, _ in groups[gi]]
        for kind, buf in enumerate(outs):
            for (n, _), t in zip(groups[gi], _unpack_rows(buf, shapes)):
                big[kind].setdefault(n, []).append(t)
    big_out = [{n: jnp.stack(ts) for n, ts in d.items()} for d in big]

    small_g = [g_rel_bias, jnp.stack(g_attn_norm), jnp.stack(g_mlp_norm), g_final[0], jnp.stack(g_sinks),
               jnp.stack(g_qn), jnp.stack(g_kvn), loss_row[0, :1]]
    small_w = [rel_bias, attn_norm, mlp_norm, final_norm, sinks, q_norm_c, kv_norm_c, jnp.zeros((1,), F32)]
    small_m = [m_rel_bias, m_attn_norm, m_mlp_norm, m_final_norm, m_sinks, m_q_norm_c, m_kv_norm_c, jnp.zeros((1,), F32)]
    small_v = [v_rel_bias, v_attn_norm, v_mlp_norm, v_final_norm, v_sinks, v_q_norm_c, v_kv_norm_c, jnp.ones((1,), F32)]
    offs = np.cumsum([0] + [-(-a.size // 1024) * 8 for a in small_g])
    partials = _exchange(jnp.concatenate([_pad_rows8(a) for a in small_g], axis=0), False, "gather_small_grads")

    def mine(i, a_full_shape, local):
        p = partials[:, offs[i]:offs[i + 1]].reshape(N_DEV, -1)[:, :math.prod(a_full_shape)]
        p = p.reshape((N_DEV,) + tuple(a_full_shape))
        if local.shape != tuple(a_full_shape):
            width = local.shape[-1]
            p = lax.dynamic_slice_in_dim(p, me * width, width, axis=p.ndim - 1)
        return jnp.stack([_pad_rows8(p[s]) for s in range(N_DEV)])

    parts_small = jnp.concatenate([mine(i, g.shape, w) for i, (g, w) in enumerate(zip(small_g, small_w))], axis=1)
    pk = lambda arrs: jnp.concatenate([_pad_rows8(a) for a in arrs], axis=0)
    small_out = _adamw(parts_small, pk(small_w), pk(small_m), pk(small_v), name="adamw_small", tr=parts_small.shape[1])
    offs2 = np.cumsum([0] + [-(-a.size // 1024) * 8 for a in small_w])

    def unpack_small(buf):
        return [buf[offs2[i]:offs2[i + 1]].reshape(-1)[:a.size].reshape(a.shape) for i, a in enumerate(small_w)]

    sg, sd, sm, sv = (unpack_small(b) for b in small_out)
    loss = sg[7][0]

    order = ['rel_bias', 'attn_norm', 'mlp_norm', 'final_norm', 'w_in_ab', 'sinks', 'w_out_ab', 'w_down_c', 'q_norm_c',
             'w_uq_c', 'kv_norm_c', 'w_ukv_c', 'w_o_c', 'w_mlp_up', 'w_mlp_down']
    small_idx = {'rel_bias': 0, 'attn_norm': 1, 'mlp_norm': 2, 'final_norm': 3, 'sinks': 4, 'q_norm_c': 5, 'kv_norm_c': 6}

    def pick(kind):
        res = []
        for n in order:
            if n in small_idx:
                res.append((sg, sd, sm, sv)[kind][small_idx[n]])
            else:
                res.append(big_out[kind][n])
        return res

    return (loss, grad_x, *pick(0), *pick(1), *pick(2), *pick(3))
```

```python
import math

import numpy as np
import jax
import jax.numpy as jnp
from jax import lax
from jax.experimental import pallas as pl
from jax.experimental.pallas import tpu as pltpu

F32 = jnp.float32
MXU_DT = jnp.bfloat16

N_DEV = 8
D_MODEL = 1024
DEPTH = 4
HEAD_DIM = 64
BLOCK = 128
EPS = 1e-6
NEG = -1e30
A_Q_HEADS = 8
A_KV_HEADS = 2
A_GROUP = A_Q_HEADS // A_KV_HEADS
A_WINDOW = 128
B_BRANCHES = ((128, 1), (512, 4), (2048, 16))
B_HPB = 4
B_HEADS = len(B_BRANCHES) * B_HPB
NUM_BUCKETS = 32
MAX_DISTANCE = 2048
N_BIAS_HEADS = A_Q_HEADS + B_HEADS
N_BAND_KV = A_KV_HEADS + B_HEADS
A_IN = (A_Q_HEADS + 2 * A_KV_HEADS) * HEAD_DIM
C_HEADS = 8
C_NOPE = 64
C_ROPE = 32
C_QK = C_NOPE + C_ROPE
C_V = 64
C_Q_RANK = 384
C_KV_RANK = 256
C_DOWN = C_Q_RANK + C_KV_RANK + C_ROPE
C_DOWN_PAD = 768
ROPE_THETA = 10000.0
N_CHUNKS = 16
FLASH_T = 512

ADAM_LR = 0.001
ADAM_B1 = 0.9
ADAM_B2 = 0.999
ADAM_EPS = 1e-08
ADAM_WD = 0.01
ADAM_STEP = 10

V7X_VMEM_BYTES = 64 * 1024 * 1024
VMEM_LIMIT = V7X_VMEM_BYTES - 8 * 1024 * 1024


def _pcall(body, **kw):
    return pl.pallas_call(body, **kw)


def _cparams(*sem):
    return pltpu.CompilerParams(dimension_semantics=sem, vmem_limit_bytes=VMEM_LIMIT)


def _exchange(src, all_to_all, name):
    def body(src_ref, out_ref, send_sems, recv_sems, local_sem):
        copies = _exchange_copies(src_ref, out_ref, send_sems, recv_sems, local_sem, all_to_all)
        for cp in copies:
            cp.start()
        _exchange_wait(copies)

    return _pcall(
        body, name=name,
        out_shape=_exchange_out(src),
        in_specs=[pl.BlockSpec(memory_space=pl.ANY)],
        out_specs=pl.BlockSpec(memory_space=pl.ANY),
        scratch_shapes=_exchange_sems(),
    )(src)


def _exchange_out(src):
    return jax.ShapeDtypeStruct((N_DEV,) + src.shape[-2:], src.dtype)


def _exchange_sems():
    return [pltpu.SemaphoreType.DMA((N_DEV - 1,)), pltpu.SemaphoreType.DMA((N_DEV - 1,)), pltpu.SemaphoreType.DMA]


def _exchange_copies(src_ref, out_ref, send_sems, recv_sems, local_sem, all_to_all):
    x, y, c = lax.axis_index("x"), lax.axis_index("y"), lax.axis_index("c")
    me = 4 * x + 2 * y + c

    def piece(dev):
        return src_ref.at[dev] if all_to_all else src_ref

    copies = [pltpu.make_async_copy(piece(me), out_ref.at[me], local_sem)]
    for k in range(1, N_DEV):
        px = 1 - x if (k >> 2) & 1 else x
        py = 1 - y if (k >> 1) & 1 else y
        pc = 1 - c if k & 1 else c
        copies.append(pltpu.make_async_remote_copy(
            src_ref=piece(4 * px + 2 * py + pc), dst_ref=out_ref.at[me],
            send_sem=send_sems.at[k - 1], recv_sem=recv_sems.at[k - 1],
            device_id=(px, py, pc), device_id_type=pl.DeviceIdType.MESH))
    return copies


def _exchange_wait(copies):
    for cp in copies[1:]:
        cp.wait()
    copies[0].wait()


def _matmul(a, b, *, trans_b=False, out_dtype=F32, epi=None, extra=None, norm=None, into=None, tm=512, tn=512,
            name, ride=None):
    M, K = a.shape
    N = b.shape[0] if trans_b else b.shape[1]
    tm, tn = min(tm, M), min(tn, N)
    assert M % tm == 0 and N % tn == 0 and (b.shape[1] if trans_b else b.shape[0]) == K
    assert (epi == 'norm_bwd') == (norm is not None) and (norm is None or tn == N)
    dn = (((1,), (1,)), ((), ())) if trans_b else (((1,), (0,)), ((), ()))
    n_i, n_j = M // tm, N // tn

    def body(*refs):
        it = iter(refs)
        a_ref, b_ref = next(it), next(it)
        e_ref = next(it) if extra is not None else None
        x_ref, g_ref, dres_ref = (next(it), next(it), next(it)) if norm is not None else (None, None, None)
        if into is not None:
            next(it)
        src_ref = next(it) if ride is not None else None
        o_ref = next(it)
        slope_ref = next(it) if epi == 'relu2' else None
        dg_ref = next(it) if norm is not None else None
        i, j = pl.program_id(0), pl.program_id(1)
        if ride is not None:
            land_ref = next(it)
            copies = _exchange_copies(src_ref, land_ref, *it, ride[1])

            @pl.when((i == 0) & (j == 0))
            def _():
                for cp in copies:
                    cp.start()

        acc = lax.dot_general(a_ref[...].astype(MXU_DT), b_ref[...].astype(MXU_DT), dn,
                              preferred_element_type=F32)
        if epi == 'relu2':
            r = jnp.maximum(acc, 0.0)
            acc = r * r
            slope_ref[...] = (2.0 * r).astype(slope_ref.dtype)
        elif epi == 'add':
            acc = acc + e_ref[...].astype(F32)
        elif epi == 'mul':
            acc = acc * e_ref[...].astype(F32)
        elif epi == 'norm_bwd':
            @pl.when(i == 0)
            def _():
                dg_ref[...] = jnp.zeros_like(dg_ref)

            xf = x_ref[...]
            r = lax.rsqrt(jnp.mean(xf * xf, axis=-1, keepdims=True) + EPS)
            xhat = xf * r
            dg_ref[...] += jnp.sum(acc * xhat, axis=0, keepdims=True)
            dyg = acc * g_ref[...]
            acc = r * (dyg - xhat * jnp.mean(dyg * xhat, axis=-1, keepdims=True)) + dres_ref[...]
        o_ref[...] = acc.astype(out_dtype)

        if ride is not None:
            @pl.when((i == n_i - 1) & (j == n_j - 1))
            def _():
                _exchange_wait(copies)

    b_spec = pl.BlockSpec((tn, K), lambda i, j: (j, 0)) if trans_b else pl.BlockSpec((K, tn), lambda i, j: (0, j))
    tile = pl.BlockSpec((tm, tn), lambda i, j: (i, j))
    in_specs = [pl.BlockSpec((tm, K), lambda i, j: (i, 0)), b_spec]
    args = [a, b]
    out_shape, out_specs, aliases = [jax.ShapeDtypeStruct((M, N), out_dtype)], [tile], {}
    if extra is not None:
        in_specs.append(tile)
        args.append(extra)
    if epi == 'relu2':
        out_shape.append(jax.ShapeDtypeStruct((M, N), out_dtype))
        out_specs.append(tile)
    if norm is not None:
        vec = pl.BlockSpec((1, N), lambda i, j: (0, 0))
        in_specs += [tile, vec, tile]
        args += [norm[0], norm[1].reshape(1, N), norm[2]]
        out_shape.append(jax.ShapeDtypeStruct((1, N), F32))
        out_specs.append(vec)
    if into is not None:
        buf, col0 = into
        assert col0 % tn == 0 and buf.shape[0] == M and buf.dtype == out_dtype
        aliases = {len(args): 0}
        in_specs.append(pl.BlockSpec(memory_space=pl.ANY))
        args.append(buf)
        out_shape[0] = jax.ShapeDtypeStruct(buf.shape, out_dtype)
        out_specs[0] = pl.BlockSpec((tm, tn), lambda i, j: (i, j + col0 // tn))
    scratch = []
    if ride is not None:
        in_specs.append(pl.BlockSpec(memory_space=pl.ANY))
        args.append(ride[0])
        out_shape.append(_exchange_out(ride[0]))
        out_specs.append(pl.BlockSpec(memory_space=pl.ANY))
        scratch = _exchange_sems()
    ordered = ride is not None or norm is not None
    res = _pcall(
        body, name=name, grid=(n_i, n_j), out_shape=tuple(out_shape), in_specs=in_specs, out_specs=tuple(out_specs),
        scratch_shapes=scratch, input_output_aliases=aliases,
        compiler_params=_cparams(*(("arbitrary", "arbitrary") if ordered else ("parallel", "parallel"))),
    )(*args)
    return res[0] if len(res) == 1 else res


def _matmul_tn(a, b, *, tk=512, tn=512, tm=1024, name):
    M, Ka = a.shape
    N = b.shape[1]
    tk, tn, tm = min(tk, Ka), min(tn, N), min(tm, M)
    assert Ka % tk == 0 and N % tn == 0 and M % tm == 0 and b.shape[0] == M

    def body(a_ref, b_ref, o_ref):
        @pl.when(pl.program_id(2) == 0)
        def _():
            o_ref[...] = jnp.zeros_like(o_ref)

        o_ref[...] += lax.dot_general(a_ref[...].astype(MXU_DT), b_ref[...].astype(MXU_DT),
                                      (((0,), (0,)), ((), ())), preferred_element_type=F32)

    return _pcall(
        body, name=name, grid=(Ka // tk, N // tn, M // tm),
        out_shape=jax.ShapeDtypeStruct((Ka, N), F32),
        in_specs=[pl.BlockSpec((tm, tk), lambda i, j, r: (r, i)), pl.BlockSpec((tm, tn), lambda i, j, r: (r, j))],
        out_specs=pl.BlockSpec((tk, tn), lambda i, j, r: (i, j)),
        compiler_params=_cparams("parallel", "parallel", "arbitrary"),
    )(a, b)


def _rmsnorm(x, g, *, out_dtype, name, tr=512):
    S, D = x.shape
    tr = min(tr, S)

    def body(x_ref, g_ref, o_ref):
        xf = x_ref[...].astype(F32)
        r = lax.rsqrt(jnp.mean(xf * xf, axis=-1, keepdims=True) + EPS)
        o_ref[...] = (xf * r * g_ref[...]).astype(out_dtype)

    return _pcall(
        body, name=name, grid=(S // tr,),
        out_shape=jax.ShapeDtypeStruct((S, D), out_dtype),
        in_specs=[pl.BlockSpec((tr, D), lambda i: (i, 0)), pl.BlockSpec((1, D), lambda i: (0, 0))],
        out_specs=pl.BlockSpec((tr, D), lambda i: (i, 0)),
        compiler_params=_cparams("parallel"),
    )(x, g.reshape(1, D))


def _rmsnorm_bwd(x, g, dy, dres, *, name, tr=512):
    S, D = x.shape
    tr = min(tr, S)

    def body(*refs):
        if dres is None:
            x_ref, g_ref, dy_ref, dx_ref, dg_ref = refs
        else:
            x_ref, g_ref, dy_ref, dres_ref, dx_ref, dg_ref = refs

        @pl.when(pl.program_id(0) == 0)
        def _():
            dg_ref[...] = jnp.zeros_like(dg_ref)

        xf = x_ref[...].astype(F32)
        r = lax.rsqrt(jnp.mean(xf * xf, axis=-1, keepdims=True) + EPS)
        xhat = xf * r
        dyf = dy_ref[...].astype(F32)
        dg_ref[...] += jnp.sum(dyf * xhat, axis=0, keepdims=True)
        dyg = dyf * g_ref[...]
        dx = r * (dyg - xhat * jnp.mean(dyg * xhat, axis=-1, keepdims=True))
        if dres is not None:
            dx = dx + dres_ref[...]
        dx_ref[...] = dx

    row = pl.BlockSpec((tr, D), lambda i: (i, 0))
    vec = pl.BlockSpec((1, D), lambda i: (0, 0))
    args = [x, g.reshape(1, D), dy] + ([] if dres is None else [dres])
    return _pcall(
        body, name=name, grid=(S // tr,),
        out_shape=(jax.ShapeDtypeStruct((S, D), F32), jax.ShapeDtypeStruct((1, D), F32)),
        in_specs=[row, vec, row] + ([] if dres is None else [row]),
        out_specs=(row, vec),
        compiler_params=_cparams("arbitrary"),
    )(*args)


def _loss_head(h, t, g, *, tr=512):
    S, D = h.shape
    tr = min(tr, S)

    def body(h_ref, t_ref, g_ref, loss_ref, dh_ref, dg_ref):
        @pl.when(pl.program_id(0) == 0)
        def _():
            dg_ref[...] = jnp.zeros_like(dg_ref)
            loss_ref[...] = jnp.zeros_like(loss_ref)

        xf = h_ref[...]
        r = lax.rsqrt(jnp.mean(xf * xf, axis=-1, keepdims=True) + EPS)
        xhat = xf * r
        e = xhat * g_ref[...] - t_ref[...]
        part = 0.5 * jnp.sum(jnp.mean(e * e, axis=-1, keepdims=True), axis=0, keepdims=True)
        loss_ref[...] += jnp.broadcast_to(part, loss_ref.shape)
        dy = e * (1.0 / D)
        dg_ref[...] += jnp.sum(dy * xhat, axis=0, keepdims=True)
        dyg = dy * g_ref[...]
        dh_ref[...] = r * (dyg - xhat * jnp.mean(dyg * xhat, axis=-1, keepdims=True))

    row = pl.BlockSpec((tr, D), lambda i: (i, 0))
    vec = pl.BlockSpec((1, D), lambda i: (0, 0))
    return _pcall(
        body, name="loss_head", grid=(S // tr,),
        out_shape=(jax.ShapeDtypeStruct((1, 128), F32), jax.ShapeDtypeStruct((S, D), F32),
                   jax.ShapeDtypeStruct((1, D), F32)),
        in_specs=[row, row, vec],
        out_specs=(pl.BlockSpec((1, 128), lambda i: (0, 0)), row, vec),
        compiler_params=_cparams("arbitrary"),
    )(h, t, g.reshape(1, D))


N_PAIRS = N_BIAS_HEADS // 2
A_PAIRS = A_Q_HEADS // 2
PAIR_W = 2 * HEAD_DIM
assert PAIR_W == 128 and A_KV_HEADS * HEAD_DIM == PAIR_W and B_HPB * HEAD_DIM == 2 * PAIR_W


def _pair_period(p):
    return jnp.where(p < A_PAIRS + 2, 16, jnp.where(p < A_PAIRS + 4, 4, 1))


def _pair_cols(p):
    b = jnp.maximum(p - A_PAIRS, 0)
    base, pp = 6 + 6 * (b // 2), b % 2
    is_a = p < A_PAIRS
    return (jnp.where(is_a, p, base + pp), jnp.where(is_a, A_PAIRS, base + 2 + pp),
            jnp.where(is_a, A_PAIRS + 1, base + 4 + pp))


def _band_specs(S):
    ch = S // N_CHUNKS
    nb = ch // BLOCK
    col = lambda i: (lambda p, c: (c, _pair_cols(p)[i]))
    prev = lambda i: (lambda p, c: (jnp.maximum(c * nb - 1, 0), _pair_cols(p)[i]))
    qkv = [pl.BlockSpec((ch, PAIR_W), col(0)), pl.BlockSpec((ch, PAIR_W), col(1)), pl.BlockSpec((BLOCK, PAIR_W), prev(1)),
           pl.BlockSpec((ch, PAIR_W), col(2)), pl.BlockSpec((BLOCK, PAIR_W), prev(2))]
    out_spec = pl.BlockSpec((ch, PAIR_W), lambda p, c: (c, p))
    sink_spec = pl.BlockSpec((2, 1, 128), lambda p, c: (p, 0, 0))
    row_spec = pl.BlockSpec((1, 1, nb, 2, BLOCK), lambda p, c: (p, c, 0, 0, 0))
    return ch, nb, qkv, out_spec, sink_spec, row_spec


def _pair_kv(p, e, ref):
    half = jnp.where(p < A_PAIRS, p // (A_GROUP // 2), e)
    return jnp.where(half == 0, ref[:, :HEAD_DIM], ref[:, HEAD_DIM:])


def _eye():
    return lax.broadcasted_iota(jnp.int32, (BLOCK, BLOCK), 0) == lax.broadcasted_iota(jnp.int32, (BLOCK, BLOCK), 1)


_NT = (((1,), (1,)), ((), ()))
_NN = (((1,), (0,)), ((), ()))
_TN = (((0,), (0,)), ((), ()))


_B_NT = (((2,), (2,)), ((0,), (0,)))
_B_NN = (((2,), (1,)), ((0,), (0,)))


def _bdot(a, b, dn):
    return lax.dot_general(a, b, dn, preferred_element_type=F32)


def _with_prev(first, t3):
    return first[None] if t3.shape[0] == 1 else jnp.concatenate([first[None], t3[:-1]], axis=0)


def _mask_first(s_prev, prev_ok):
    s0 = jnp.where(prev_ok, s_prev[0], NEG)[None]
    return s0 if s_prev.shape[0] == 1 else jnp.concatenate([s0, s_prev[1:]], axis=0)


def _banded_fwd(proj, bias, sinks):
    S = proj.shape[0]
    dh = HEAD_DIM
    ch, nb, qkv, out_spec, sink_spec, row_spec = _band_specs(S)
    bias_spec = pl.BlockSpec((2, BLOCK, 2 * BLOCK), lambda p, c: (p, 0, 0))
    scale = HEAD_DIM ** -0.5

    def body(q_ref, k_ref, kp_ref, v_ref, vp_ref, b_ref, s_ref, o_ref, lse_ref):
        p, c = pl.program_id(0), pl.program_id(1)
        prev_ok = (c % _pair_period(p)) != 0
        for e in range(2):
            lanes = slice(e * dh, (e + 1) * dh)
            q3 = q_ref[:, lanes].reshape(nb, BLOCK, dh)
            k3, v3 = (_pair_kv(p, e, r).reshape(nb, BLOCK, dh) for r in (k_ref, v_ref))
            kp3, vp3 = _with_prev(_pair_kv(p, e, kp_ref), k3), _with_prev(_pair_kv(p, e, vp_ref), v3)
            sink = s_ref[e, :, :1]
            s_cur = _bdot(q3, k3, _B_NT) * scale + b_ref[e, :, BLOCK:][None]
            s_prev = _mask_first(_bdot(q3, kp3, _B_NT) * scale + b_ref[e, :, :BLOCK][None], prev_ok)
            m = jnp.maximum(jnp.max(s_cur, axis=-1, keepdims=True), jnp.max(s_prev, axis=-1, keepdims=True))
            m = jnp.maximum(m, sink)
            p_cur = jnp.exp(s_cur - m)
            p_prev = jnp.exp(s_prev - m)
            l = jnp.sum(p_cur, axis=-1, keepdims=True) + jnp.sum(p_prev, axis=-1, keepdims=True) + jnp.exp(sink - m)
            acc = _bdot(p_cur.astype(MXU_DT), v3, _B_NN) + _bdot(p_prev.astype(MXU_DT), vp3, _B_NN)
            o_ref[:, lanes] = (acc / l).reshape(ch, dh)
            lse = m + jnp.log(l)
            lse_ref[0, 0, :, e:e + 1, :] = jnp.sum(jnp.where(_eye()[None], lse, 0.0), axis=1, keepdims=True)

    return _pcall(
        body, name="banded_fwd", grid=(N_PAIRS, N_CHUNKS),
        out_shape=(jax.ShapeDtypeStruct((S, N_PAIRS * PAIR_W), F32),
                   jax.ShapeDtypeStruct((N_PAIRS, N_CHUNKS, nb, 2, BLOCK), F32)),
        in_specs=qkv + [bias_spec, sink_spec],
        out_specs=(out_spec, row_spec),
        compiler_params=_cparams("parallel", "parallel"),
    )(proj, proj, proj, proj, proj, bias, sinks)


def _banded_bwd(proj, bias_t, sinks, do, dlse):
    S = proj.shape[0]
    dh = HEAD_DIM
    ch, nb, qkv, out_spec, sink_spec, row_spec = _band_specs(S)
    bias_spec = pl.BlockSpec((2, 2 * BLOCK, BLOCK), lambda p, c: (p, 0, 0))
    scale = HEAD_DIM ** -0.5

    def body(q_ref, k_ref, kp_ref, v_ref, vp_ref, b_ref, s_ref, do_ref, dl_ref,
             dq_ref, dk_ref, dv_ref, dkh_ref, dvh_ref, db_ref, ds_ref):
        p, c = pl.program_id(0), pl.program_id(1)

        @pl.when(c == 0)
        def _():
            db_ref[...] = jnp.zeros_like(db_ref)
            ds_ref[...] = jnp.zeros_like(ds_ref)

        prev_ok = (c % _pair_period(p)) != 0
        for e in range(2):
            lanes = slice(e * dh, (e + 1) * dh)
            q3 = q_ref[:, lanes].reshape(nb, BLOCK, dh)
            k3, v3 = (_pair_kv(p, e, r).reshape(nb, BLOCK, dh) for r in (k_ref, v_ref))
            kp3, vp3 = _with_prev(_pair_kv(p, e, kp_ref), k3), _with_prev(_pair_kv(p, e, vp_ref), v3)
            do3 = do_ref[:, lanes].astype(MXU_DT).reshape(nb, BLOCK, dh)
            sink = s_ref[e, :, :1]
            s_cur = _bdot(k3, q3, _B_NT) * scale + b_ref[e, BLOCK:, :][None]
            s_prev = _mask_first(_bdot(kp3, q3, _B_NT) * scale + b_ref[e, :BLOCK, :][None], prev_ok)
            m = jnp.maximum(jnp.max(s_cur, axis=1, keepdims=True), jnp.max(s_prev, axis=1, keepdims=True))
            m = jnp.maximum(m, sink)
            p_cur = jnp.exp(s_cur - m)
            p_prev = jnp.exp(s_prev - m)
            p_sink = jnp.exp(sink - m)
            inv = 1.0 / (jnp.sum(p_cur, axis=1, keepdims=True) + jnp.sum(p_prev, axis=1, keepdims=True) + p_sink)
            p_cur, p_prev, p_sink = p_cur * inv, p_prev * inv, p_sink * inv
            dp_cur = _bdot(v3, do3, _B_NT)
            dp_prev = _bdot(vp3, do3, _B_NT)
            delta = jnp.sum(p_cur * dp_cur, axis=1, keepdims=True) + jnp.sum(p_prev * dp_prev, axis=1, keepdims=True)
            t = dl_ref[0, 0, :, e:e + 1, :] - delta
            ds_cur = p_cur * (dp_cur + t)
            ds_prev = p_prev * (dp_prev + t)
            dsink = jnp.sum(jnp.sum(p_sink * t, axis=0), axis=-1, keepdims=True)
            ds_ref[e] += jnp.broadcast_to(dsink, (1, 128))
            db_ref[e, :BLOCK, :] += jnp.sum(ds_prev, axis=0)
            db_ref[e, BLOCK:, :] += jnp.sum(ds_cur, axis=0)
            dsb_cur = (ds_cur * scale).astype(MXU_DT)
            dsb_prev = (ds_prev * scale).astype(MXU_DT)
            dk_prev = _bdot(dsb_prev, q3, _B_NN)
            dv_prev = _bdot(p_prev.astype(MXU_DT), do3, _B_NN)

            def shifted(t3):
                z = jnp.zeros((1, BLOCK, dh), F32)
                return z if nb == 1 else jnp.concatenate([t3[1:], z], axis=0)

            dk_ref[:, lanes] = (_bdot(dsb_cur, q3, _B_NN) + shifted(dk_prev)).reshape(ch, dh)
            dv_ref[:, lanes] = (_bdot(p_cur.astype(MXU_DT), do3, _B_NN) + shifted(dv_prev)).reshape(ch, dh)
            dkh_ref[0, 0, :, lanes] = dk_prev[0]
            dvh_ref[0, 0, :, lanes] = dv_prev[0]
            for b in range(nb):
                dq_ref[b * BLOCK:(b + 1) * BLOCK, lanes] = (
                    lax.dot_general(dsb_cur[b], k3[b], _TN, preferred_element_type=F32)
                    + lax.dot_general(dsb_prev[b], kp3[b], _TN, preferred_element_type=F32))

    halo_spec = pl.BlockSpec((1, 1, BLOCK, PAIR_W), lambda p, c: (p, c, 0, 0))
    big = jax.ShapeDtypeStruct((S, N_PAIRS * PAIR_W), F32)
    halo = jax.ShapeDtypeStruct((N_PAIRS, N_CHUNKS, BLOCK, PAIR_W), F32)
    return _pcall(
        body, name="banded_bwd", grid=(N_PAIRS, N_CHUNKS),
        out_shape=(big, big, big, halo, halo, jax.ShapeDtypeStruct(bias_t.shape, F32),
                   jax.ShapeDtypeStruct(sinks.shape, F32)),
        in_specs=qkv + [bias_spec, sink_spec, out_spec, row_spec],
        out_specs=(out_spec, out_spec, out_spec, halo_spec, halo_spec, bias_spec, sink_spec),
        compiler_params=_cparams("arbitrary", "arbitrary"),
    )(proj, proj, proj, proj, proj, bias_t, sinks, do, dlse)


def _halo_fold(t, halo, name):
    S, width = t.shape
    nb = S // N_CHUNKS // BLOCK

    def body(t_ref, h_ref, o_ref):
        o_ref[:N_CHUNKS - 1, 0] = t_ref[:N_CHUNKS - 1, 0] + h_ref[0, 1:]
        o_ref[N_CHUNKS - 1:, 0] = t_ref[N_CHUNKS - 1:, 0]

    blk = pl.BlockSpec((N_CHUNKS, 1, BLOCK, PAIR_W), lambda p: (0, nb - 1, 0, p))
    return _pcall(
        body, name=name, grid=(N_PAIRS,),
        out_shape=jax.ShapeDtypeStruct((N_CHUNKS, nb, BLOCK, width), t.dtype),
        in_specs=[blk, pl.BlockSpec((1, N_CHUNKS, BLOCK, PAIR_W), lambda p: (p, 0, 0, 0))],
        out_specs=blk, input_output_aliases={0: 0},
        compiler_params=_cparams("parallel"),
    )(t.reshape(N_CHUNKS, nb, BLOCK, width), halo).reshape(S, width)


def _causal_mask(T):
    return lax.broadcasted_iota(jnp.int32, (T, T), 0) <= lax.broadcasted_iota(jnp.int32, (T, T), 1)


LOG2E = math.log2(math.e)
FLASH_SPLIT = 2
FLASH_ONES_ROWS = 16


def _mla_q_proj(cqn, wq_t, cos_t, sin_t, *, name):
    S, R = cqn.shape
    H, dqk, _ = wq_t.shape
    nq, half, T = cos_t.shape

    def body(x_ref, w_ref, c_ref, s_ref, o_ref):
        x = x_ref[...]
        for h in range(H):
            qt = lax.dot_general(w_ref[h], x, _NT, preferred_element_type=F32)
            t1, t2 = qt[C_NOPE:C_NOPE + half], qt[C_NOPE + half:]
            o_ref[h, 0, :C_NOPE] = qt[:C_NOPE].astype(MXU_DT)
            o_ref[h, 0, C_NOPE:C_NOPE + half] = (t1 * c_ref[0] - t2 * s_ref[0]).astype(MXU_DT)
            o_ref[h, 0, C_NOPE + half:] = (t1 * s_ref[0] + t2 * c_ref[0]).astype(MXU_DT)

    tab = pl.BlockSpec((1, half, T), lambda i: (i, 0, 0))
    return _pcall(
        body, name=name, grid=(nq,),
        out_shape=jax.ShapeDtypeStruct((H, nq, dqk, T), MXU_DT),
        in_specs=[pl.BlockSpec((T, R), lambda i: (i, 0)), pl.BlockSpec((H, dqk, R), lambda i: (0, 0, 0)), tab, tab],
        out_specs=pl.BlockSpec((H, 1, dqk, T), lambda i: (0, i, 0, 0)),
        compiler_params=_cparams("parallel"),
    )(cqn, wq_t, cos_t, sin_t)


def _mla_q_proj_bwd(dqt, cqn, wq_t, cos_t, sin_t, *, name):
    S, R = cqn.shape
    H, dqk, _ = wq_t.shape
    nq, half, T = cos_t.shape

    def body(g_ref, x_ref, w_ref, c_ref, s_ref, dx_ref, dw_ref):
        @pl.when(pl.program_id(0) == 0)
        def _():
            dw_ref[...] = jnp.zeros_like(dw_ref)

        x = x_ref[...]
        acc = jnp.zeros((T, R), F32)
        for h in range(H):
            g = g_ref[h, 0]
            g1, g2 = g[C_NOPE:C_NOPE + half], g[C_NOPE + half:]
            gq = jnp.concatenate([g[:C_NOPE], g1 * c_ref[0] + g2 * s_ref[0], g2 * c_ref[0] - g1 * s_ref[0]],
                                 axis=0).astype(MXU_DT)
            acc = acc + lax.dot_general(gq, w_ref[h], _TN, preferred_element_type=F32)
            dw_ref[h] += lax.dot_general(gq, x, _NN, preferred_element_type=F32)
        dx_ref[...] = acc

    tab = pl.BlockSpec((1, half, T), lambda i: (i, 0, 0))
    whole = pl.BlockSpec((H, dqk, R), lambda i: (0, 0, 0))
    return _pcall(
        body, name=name, grid=(nq,),
        out_shape=(jax.ShapeDtypeStruct((S, R), F32), jax.ShapeDtypeStruct((H, dqk, R), F32)),
        in_specs=[pl.BlockSpec((H, 1, dqk, T), lambda i: (0, i, 0, 0)), pl.BlockSpec((T, R), lambda i: (i, 0)), whole, tab, tab],
        out_specs=(pl.BlockSpec((T, R), lambda i: (i, 0)), whole),
        compiler_params=_cparams("arbitrary"),
    )(dqt, cqn, wq_t, cos_t, sin_t)


def _mla_kv_proj(ckvn, wk, wv, kr, kr_t, *, name):
    S, R = ckvn.shape
    H = wk.shape[0]
    nq, dr, T = kr_t.shape
    dqk = C_NOPE + dr
    wk_t, wv_t = wk.transpose(0, 2, 1), wv.transpose(0, 2, 1)

    def body(x_ref, wk_ref, wv_ref, wkt_ref, wvt_ref, kr_ref, krt_ref, kt_ref, vt_ref, kn_ref, vn_ref):
        x = x_ref[...]
        krt, krn = krt_ref[0].astype(MXU_DT), kr_ref[...].astype(MXU_DT)
        ones = jnp.where(lax.broadcasted_iota(jnp.int32, (FLASH_ONES_ROWS, T), 0) == 0, 1.0, 0.0).astype(MXU_DT)
        for h in range(H):
            kt_ref[h, 0, :C_NOPE] = lax.dot_general(wkt_ref[h], x, _NT, preferred_element_type=F32).astype(MXU_DT)
            kt_ref[h, 0, C_NOPE:] = krt
            vt_ref[h, 0, :C_V] = lax.dot_general(wvt_ref[h], x, _NT, preferred_element_type=F32).astype(MXU_DT)
            vt_ref[h, 0, C_V:] = ones
            kn_ref[h, :, :C_NOPE] = lax.dot_general(x, wk_ref[h], _NN, preferred_element_type=F32).astype(MXU_DT)
            kn_ref[h, :, C_NOPE:] = krn
            vn_ref[h] = lax.dot_general(x, wv_ref[h], _NN, preferred_element_type=F32).astype(MXU_DT)

    w_spec = pl.BlockSpec((H, R, C_NOPE), lambda i: (0, 0, 0))
    wt_spec = pl.BlockSpec((H, C_NOPE, R), lambda i: (0, 0, 0))
    return _pcall(
        body, name=name, grid=(nq,),
        out_shape=(jax.ShapeDtypeStruct((H, nq, dqk, T), MXU_DT),
                   jax.ShapeDtypeStruct((H, nq, C_V + FLASH_ONES_ROWS, T), MXU_DT),
                   jax.ShapeDtypeStruct((H, S, dqk), MXU_DT), jax.ShapeDtypeStruct((H, S, C_V), MXU_DT)),
        in_specs=[pl.BlockSpec((T, R), lambda i: (i, 0)), w_spec, w_spec, wt_spec, wt_spec,
                  pl.BlockSpec((T, dr), lambda i: (i, 0)), pl.BlockSpec((1, dr, T), lambda i: (i, 0, 0))],
        out_specs=(pl.BlockSpec((H, 1, dqk, T), lambda i: (0, i, 0, 0)),
                   pl.BlockSpec((H, 1, C_V + FLASH_ONES_ROWS, T), lambda i: (0, i, 0, 0)),
                   pl.BlockSpec((H, T, dqk), lambda i: (0, i, 0)), pl.BlockSpec((H, T, C_V), lambda i: (0, i, 0))),
        compiler_params=_cparams("parallel"),
    )(ckvn, wk, wv, wk_t, wv_t, kr, kr_t)


def _mla_kv_proj_bwd(dkt, dvt, ckvn, wk, wv, *, name):
    S, R = ckvn.shape
    H = wk.shape[0]
    _, nq, dqk, T = dkt.shape
    dr = dqk - C_NOPE
    wk_t, wv_t = wk.transpose(0, 2, 1), wv.transpose(0, 2, 1)

    def body(gk_ref, gv_ref, x_ref, wkt_ref, wvt_ref, dx_ref, dwk_ref, dwv_ref, dkr_ref):
        @pl.when(pl.program_id(0) == 0)
        def _():
            dwk_ref[...] = jnp.zeros_like(dwk_ref)
            dwv_ref[...] = jnp.zeros_like(dwv_ref)

        x = x_ref[...]
        acc = jnp.zeros((T, R), F32)
        dkr = jnp.zeros((dr, T), F32)
        for h in range(H):
            gk = gk_ref[h, 0, :C_NOPE].astype(MXU_DT)
            gv = gv_ref[h, 0].astype(MXU_DT)
            acc = acc + (lax.dot_general(gk, wkt_ref[h], _TN, preferred_element_type=F32)
                         + lax.dot_general(gv, wvt_ref[h], _TN, preferred_element_type=F32))
            dwk_ref[h] += lax.dot_general(gk, x, _NN, preferred_element_type=F32)
            dwv_ref[h] += lax.dot_general(gv, x, _NN, preferred_element_type=F32)
            dkr = dkr + gk_ref[h, 0, C_NOPE:]
        dx_ref[...] = acc
        dkr_ref[0] = dkr

    wt_spec = pl.BlockSpec((H, C_NOPE, R), lambda i: (0, 0, 0))
    return _pcall(
        body, name=name, grid=(nq,),
        out_shape=(jax.ShapeDtypeStruct((S, R), F32), jax.ShapeDtypeStruct((H, C_NOPE, R), F32),
                   jax.ShapeDtypeStruct((H, C_V, R), F32), jax.ShapeDtypeStruct((nq, dr, T), F32)),
        in_specs=[pl.BlockSpec((H, 1, dqk, T), lambda i: (0, i, 0, 0)), pl.BlockSpec((H, 1, C_V, T), lambda i: (0, i, 0, 0)),
                  pl.BlockSpec((T, R), lambda i: (i, 0)), wt_spec, wt_spec],
        out_specs=(pl.BlockSpec((T, R), lambda i: (i, 0)), wt_spec, wt_spec, pl.BlockSpec((1, dr, T), lambda i: (i, 0, 0))),
        compiler_params=_cparams("arbitrary"),
    )(dkt, dvt, ckvn, wk_t, wv_t)


def _mla_out_proj(ot, w_o, h, *, name):
    H, nq, dv, T = ot.shape
    D = w_o.shape[2]

    def body(o_ref, w_ref, h_ref, out_ref):
        acc = h_ref[...]
        for hd in range(H):
            acc = acc + lax.dot_general(o_ref[hd, 0].astype(MXU_DT), w_ref[hd], _TN, preferred_element_type=F32)
        out_ref[...] = acc

    row = pl.BlockSpec((T, D), lambda i: (i, 0))
    return _pcall(
        body, name=name, grid=(nq,),
        out_shape=jax.ShapeDtypeStruct(h.shape, F32),
        in_specs=[pl.BlockSpec((H, 1, dv, T), lambda i: (0, i, 0, 0)), pl.BlockSpec((H, dv, D), lambda i: (0, 0, 0)), row],
        out_specs=row,
        compiler_params=_cparams("parallel"),
    )(ot, w_o, h)


def _mla_out_proj_bwd(dh, ot, w_o, *, name):
    H, nq, dv, T = ot.shape
    D = w_o.shape[2]

    def body(dh_ref, o_ref, w_ref, dot_ref, del_ref, dw_ref):
        @pl.when(pl.program_id(0) == 0)
        def _():
            dw_ref[...] = jnp.zeros_like(dw_ref)

        dhb = dh_ref[...].astype(MXU_DT)
        for hd in range(H):
            o = o_ref[hd, 0]
            d = lax.dot_general(w_ref[hd], dhb, _NT, preferred_element_type=F32)
            dot_ref[hd, 0] = d
            del_ref[hd, 0] = jnp.sum(d * o, axis=0, keepdims=True)
            dw_ref[hd] += lax.dot_general(o.astype(MXU_DT), dhb, _NN, preferred_element_type=F32)

    tile = pl.BlockSpec((H, 1, dv, T), lambda i: (0, i, 0, 0))
    whole = pl.BlockSpec((H, dv, D), lambda i: (0, 0, 0))
    return _pcall(
        body, name=name, grid=(nq,),
        out_shape=(jax.ShapeDtypeStruct(ot.shape, F32), jax.ShapeDtypeStruct((H, nq, 1, T), F32),
                   jax.ShapeDtypeStruct(w_o.shape, F32)),
        in_specs=[pl.BlockSpec((T, D), lambda i: (i, 0)), tile, whole],
        out_specs=(tile, pl.BlockSpec((H, 1, 1, T), lambda i: (0, i, 0, 0)), whole),
        compiler_params=_cparams("arbitrary"),
    )(dh, ot, w_o)


def _flash_fwd(qt, k, vt1):
    H, nq, dqk, T = qt.shape
    S = k.shape[1]
    dva = vt1.shape[2]
    dv = dva - FLASH_ONES_ROWS
    scale = dqk ** -0.5
    c = scale * LOG2E
    th = T // FLASH_SPLIT

    def body(qt_ref, k_ref, vt_ref, ot_ref, lse_ref, sa_ref, sb_ref):
        i = pl.program_id(1)

        def scores(j):
            kb = k_ref[0, pl.ds(pl.multiple_of(j * T, T), T), :]
            return lax.dot_general(kb, qt_ref[0, 0], _NN, preferred_element_type=F32)

        def softmax_pv(s_ref, j, carry, masked):
            m, acc = carry
            raw = s_ref[...]
            if masked:
                raw = jnp.where(_causal_mask(T), raw, NEG)
            m_new = jnp.maximum(m, jnp.max(raw, axis=0, keepdims=True))
            alpha = jnp.exp2((m - m_new) * c)
            pb = jnp.exp2((raw - m_new) * c).astype(MXU_DT)
            acc = acc * alpha + lax.dot_general(vt_ref[0, j], pb, _NN, preferred_element_type=F32)
            return m_new, acc

        def pair(p, carry):
            j = 2 * p
            sb_ref[...] = scores(j + 1)
            carry = softmax_pv(sa_ref, j, carry, False)
            sa_ref[...] = scores(j + 2)
            return softmax_pv(sb_ref, j + 1, carry, False)

        def even_tail(carry):
            return softmax_pv(sa_ref, i, carry, True)

        def odd_tail(carry):
            sb_ref[...] = scores(i)
            carry = softmax_pv(sa_ref, i - 1, carry, False)
            return softmax_pv(sb_ref, i, carry, True)

        sa_ref[...] = scores(0)
        carry = lax.fori_loop(0, i // 2, pair, (jnp.full((1, T), NEG, F32), jnp.zeros((dva, T), F32)))
        m, acc = lax.cond(i % 2 == 0, even_tail, odd_tail, carry)
        l = acc[dv:dv + 1]
        ot_ref[0, 0] = acc[:dv] / l
        lse_ref[0, 0] = m * scale + jnp.log(l)

    return _pcall(
        body, name="flash_fwd", grid=(H, nq),
        out_shape=(jax.ShapeDtypeStruct((H, nq, dv, T), F32), jax.ShapeDtypeStruct((H, nq, 1, T), F32)),
        in_specs=[pl.BlockSpec((1, 1, dqk, T), lambda h, i: (h, i, 0, 0)),
                  pl.BlockSpec((1, S, dqk), lambda h, i: (h, 0, 0)),
                  pl.BlockSpec((1, nq, dva, T), lambda h, i: (h, 0, 0, 0))],
        out_specs=(pl.BlockSpec((1, 1, dv, T), lambda h, i: (h, i, 0, 0)),
                   pl.BlockSpec((1, 1, 1, T), lambda h, i: (h, i, 0, 0))),
        scratch_shapes=[pltpu.VMEM((T, T), F32), pltpu.VMEM((T, T), F32)],
        compiler_params=_cparams("parallel", "parallel"),
    )(qt, k, vt1)


def _flash_delta(ot, dot):
    H, nq, dv, T = ot.shape

    def body(o_ref, do_ref, d_ref):
        d_ref[0, 0] = jnp.sum(o_ref[0, 0] * do_ref[0, 0], axis=0, keepdims=True)

    spec = pl.BlockSpec((1, 1, dv, T), lambda h, i: (h, i, 0, 0))
    return _pcall(
        body, name="flash_delta", grid=(H, nq),
        out_shape=jax.ShapeDtypeStruct((H, nq, 1, T), F32),
        in_specs=[spec, spec], out_specs=pl.BlockSpec((1, 1, 1, T), lambda h, i: (h, i, 0, 0)),
        compiler_params=_cparams("parallel", "parallel"),
    )(ot, dot)


def _flash_bwd(qt, k, kt, v, dot, lse, delta):
    H, nq, dqk, T = qt.shape
    dv_ = v.shape[2]
    scale = dqk ** -0.5
    c = scale * LOG2E
    th = T // FLASH_SPLIT

    def body(qt_ref, k_ref, kt_ref, v_ref, dot_ref, lse_ref, del_ref, dqt_ref, dkt_ref, dvt_ref,
             sa_ref, pa_ref, sb_ref, pb_ref):
        j = pl.program_id(1)

        @pl.when(j == 0)
        def _():
            dqt_ref[...] = jnp.zeros_like(dqt_ref)

        n_un = nq - 1 - j

        def issue(i, s_ref, dp_ref):
            s_ref[...] = lax.dot_general(k_ref[0], qt_ref[0, i], _NN, preferred_element_type=F32)
            dp_ref[...] = lax.dot_general(v_ref[0], dot_ref[0, i].astype(MXU_DT), _NN, preferred_element_type=F32)

        def consume(i, s_ref, dp_ref, carry, masked):
            dkt, dvt = carry
            raw = s_ref[...]
            if masked:
                raw = jnp.where(_causal_mask(T), raw, NEG)
            p = jnp.exp2(raw * c - lse_ref[0, i] * LOG2E)
            dsb = (p * (dp_ref[...] - del_ref[0, i])).astype(MXU_DT)
            dvt = dvt + lax.dot_general(dot_ref[0, i].astype(MXU_DT), p.astype(MXU_DT), _NT, preferred_element_type=F32)
            dkt = dkt + lax.dot_general(qt_ref[0, i], dsb, _NT, preferred_element_type=F32)
            dqt_ref[0, i] += lax.dot_general(kt_ref[0, 0], dsb, _NN, preferred_element_type=F32) * scale
            return dkt, dvt

        def pair(p, carry):
            i0 = j + 1 + 2 * p
            issue(i0 + 1, sb_ref, pb_ref)
            carry = consume(i0, sa_ref, pa_ref, carry, False)
            issue(jnp.where(2 * p + 2 < n_un, i0 + 2, j), sa_ref, pa_ref)
            return consume(i0 + 1, sb_ref, pb_ref, carry, False)

        def even_tail(carry):
            return consume(j, sa_ref, pa_ref, carry, True)

        def odd_tail(carry):
            issue(j, sb_ref, pb_ref)
            carry = consume(nq - 1, sa_ref, pa_ref, carry, False)
            return consume(j, sb_ref, pb_ref, carry, True)

        issue(jnp.where(n_un > 0, j + 1, j), sa_ref, pa_ref)
        carry = lax.fori_loop(0, n_un // 2, pair, (jnp.zeros((dqk, T), F32), jnp.zeros((dv_, T), F32)))
        dkt, dvt = lax.cond(n_un % 2 == 0, even_tail, odd_tail, carry)
        dkt_ref[0, 0] = dkt * scale
        dvt_ref[0, 0] = dvt

    whole = lambda d: pl.BlockSpec((1, nq, d, T), lambda h, j: (h, 0, 0, 0))
    tile_t = lambda d: pl.BlockSpec((1, 1, d, T), lambda h, j: (h, j, 0, 0))
    return _pcall(
        body, name="flash_bwd", grid=(H, nq),
        out_shape=(jax.ShapeDtypeStruct((H, nq, dqk, T), F32), jax.ShapeDtypeStruct((H, nq, dqk, T), F32),
                   jax.ShapeDtypeStruct((H, nq, dv_, T), F32)),
        in_specs=[whole(dqk),
                  pl.BlockSpec((1, T, dqk), lambda h, j: (h, j, 0)),
                  tile_t(dqk),
                  pl.BlockSpec((1, T, dv_), lambda h, j: (h, j, 0)),
                  whole(dv_), whole(1), whole(1)],
        out_specs=(whole(dqk), tile_t(dqk), tile_t(dv_)),
        scratch_shapes=[pltpu.VMEM((T, T), F32) for _ in range(4)],
        compiler_params=_cparams("arbitrary", "arbitrary"),
    )(qt, k, kt, v, dot, lse, delta)


def _adamw(parts, w, m, v, *, name, tr=512):
    P, R, C = parts.shape
    tr = min(tr, R)
    assert R % tr == 0

    def body(p_ref, w_ref, m_ref, v_ref, g_ref, d_ref, m2_ref, v2_ref):
        g = p_ref[0].astype(F32)
        for s in range(1, P):
            g = g + p_ref[s].astype(F32)
        m2 = ADAM_B1 * m_ref[...] + (1.0 - ADAM_B1) * g
        v2 = ADAM_B2 * v_ref[...] + (1.0 - ADAM_B2) * jnp.square(g)
        m_hat = m2 / (1.0 - ADAM_B1 ** ADAM_STEP)
        v_hat = v2 / (1.0 - ADAM_B2 ** ADAM_STEP)
        g_ref[...] = g
        d_ref[...] = -ADAM_LR * (m_hat / (jnp.sqrt(v_hat) + ADAM_EPS) + ADAM_WD * w_ref[...])
        m2_ref[...] = m2
        v2_ref[...] = v2

    row = pl.BlockSpec((tr, C), lambda i: (i, 0))
    out = jax.ShapeDtypeStruct((R, C), F32)
    return _pcall(
        body, name=name, grid=(R // tr,),
        out_shape=(out, out, out, out),
        in_specs=[pl.BlockSpec((P, tr, C), lambda i: (0, i, 0)), row, row, row],
        out_specs=(row, row, row, row),
        compiler_params=_cparams("parallel"),
    )(parts, w, m, v)


def _bias_tables():
    i = np.arange(BLOCK)[:, None]
    j = np.arange(2 * BLOCK)[None, :]
    dist = i + BLOCK - j
    out = []
    for dil, max_dist in [(1, A_WINDOW - 1)] + [(d, w // d) for w, d in B_BRANCHES]:
        n = np.maximum(dist, 0) * dil
        max_exact = NUM_BUCKETS // 2
        nf = np.maximum(n, 1).astype(np.float64)
        val = np.log(nf / max_exact) / math.log(MAX_DISTANCE / max_exact) * (NUM_BUCKETS - max_exact)
        inband = (dist >= 0) & (dist <= max_dist)
        frac = np.abs(val - np.round(val))
        last = NUM_BUCKETS - 1 - max_exact
        assert np.all((frac > 2e-5) | (n <= max_exact) | (val >= last) | ~inband)
        large = max_exact + val.astype(np.int64)
        bucket = np.where(n < max_exact, n, np.minimum(large, NUM_BUCKETS - 1))
        onehot = (bucket[..., None] == np.arange(NUM_BUCKETS)).astype(np.float32)
        out.append((onehot.reshape(-1, NUM_BUCKETS), inband))
    return out


def _make_bias(rel_bias):
    tabs = _bias_tables()
    groups = [(0, A_Q_HEADS)] + [(A_Q_HEADS + g * B_HPB, B_HPB) for g in range(len(B_BRANCHES))]
    parts = []
    for (onehot, inband), (h0, nh) in zip(tabs, groups):
        b = jnp.dot(jnp.asarray(onehot), rel_bias[:, h0:h0 + nh], precision=lax.Precision.HIGHEST)
        b = b.reshape(BLOCK, 2 * BLOCK, nh)
        b = jnp.where(jnp.asarray(inband)[..., None], b, NEG)
        parts.append(b.transpose(2, 0, 1))
    return jnp.concatenate(parts, axis=0)


A_W = A_Q_HEADS * HEAD_DIM
B_W = B_HPB * HEAD_DIM


def _perm_rows(x, d):
    return x if d == 1 else x.reshape(x.shape[0] // d, d, -1).transpose(1, 0, 2).reshape(x.shape)


def _unperm_rows(x, d):
    return x if d == 1 else x.reshape(d, x.shape[0] // d, -1).transpose(1, 0, 2).reshape(x.shape)


def _even_post(o_all, lse):
    S = o_all.shape[0]
    outs, lses = [], []
    for g, (_, d) in enumerate(B_BRANCHES):
        w0 = A_W + g * B_W
        outs.append(_unperm_rows(o_all[:, w0:w0 + B_W], d))
        lg = lse[A_PAIRS + 2 * g:A_PAIRS + 2 * g + 2].transpose(1, 2, 4, 0, 3).reshape(S, B_HPB)
        lses.append(_unperm_rows(lg, d))
    wts = jax.nn.softmax(jnp.stack(lses), axis=0)
    widen = jnp.asarray(np.kron(np.eye(B_HPB), np.ones((1, HEAD_DIM))), F32)
    out_b = sum(jnp.dot(wts[g], widen, precision=lax.Precision.HIGHEST) * outs[g] for g in range(len(B_BRANCHES)))
    return jnp.concatenate([o_all[:, :A_W], out_b], axis=-1)


def _rope_tables(S, r):
    inv = ROPE_THETA ** (-jnp.arange(0, r, 2, dtype=jnp.float32) / r)
    ang = jnp.arange(S, dtype=jnp.float32)[:, None] * inv[None, :]
    return jnp.cos(ang), jnp.sin(ang)


def _rope(t):
    S, r = t.shape[1], t.shape[-1]
    shape = (1, S) + (1,) * (t.ndim - 3) + (r // 2,)
    cos, sin = (a.reshape(shape) for a in _rope_tables(S, r))
    t1, t2 = t[..., :r // 2], t[..., r // 2:]
    return jnp.concatenate([t1 * cos - t2 * sin, t1 * sin + t2 * cos], axis=-1)


def _mla_pre(q_lin, kv_lin, kr_raw):
    S = q_lin.shape[0]
    q = q_lin.reshape(1, S, C_HEADS, C_QK)
    qf = jnp.concatenate([q[..., :C_NOPE], _rope(q[..., C_NOPE:])], axis=-1)[0]
    kv = kv_lin.reshape(S, C_HEADS, C_NOPE + C_V)
    kr = _rope(kr_raw[None])[0]
    kf = jnp.concatenate([kv[..., :C_NOPE], jnp.broadcast_to(kr[:, None, :], (S, C_HEADS, C_ROPE))], axis=-1)
    return qf, kf, kv[..., C_NOPE:]


def _to_tiles_t(t, T):
    S, H, d = t.shape
    return t.reshape(S // T, T, H, d).transpose(2, 0, 3, 1)


def _from_tiles_t(t):
    H, n, d, T = t.shape
    return t.transpose(1, 3, 0, 2).reshape(n * T, H, d)


_BIG = (("w_in_ab", 2), ("w_out_ab", 2), ("w_down_c", 1), ("w_uq_c", 2), ("w_ukv_c", 2), ("w_o_c", 2),
        ("w_mlp_up", 2), ("w_mlp_down", 1))
_ROW_ALIGN = 512


def _pack_rows(arrs):
    rows = [a.reshape(-1, 128) for a in arrs]
    n = sum(r.shape[0] for r in rows)
    pad = (-n) % _ROW_ALIGN
    if pad:
        rows.append(jnp.zeros((pad, 128), rows[0].dtype))
    return jnp.concatenate(rows, axis=0)


def _pack_rows_per_device(arrs):
    rows = [a.reshape(N_DEV, -1, 128) for a in arrs]
    n = sum(r.shape[1] for r in rows)
    pad = (-n) % _ROW_ALIGN
    if pad:
        rows.append(jnp.zeros((N_DEV, pad, 128), rows[0].dtype))
    return jnp.concatenate(rows, axis=1)


def _unpack_rows(buf, shapes):
    out, r0 = [], 0
    for shp in shapes:
        n = math.prod(shp) // 128
        out.append(buf[..., r0:r0 + n, :].reshape(buf.shape[:-2] + tuple(shp)))
        r0 += n
    return out


def _layer_tensors(l):
    att = [("w_in_ab", l // 2), ("w_out_ab", l // 2)] if l % 2 == 0 else \
          [("w_down_c", l // 2), ("w_uq_c", l // 2), ("w_ukv_c", l // 2), ("w_o_c", l // 2)]
    return att + [("w_mlp_up", l), ("w_mlp_down", l)]


def _exchange_groups():
    first = _layer_tensors(0)
    return [first[:-2], first[-2:]] + [_layer_tensors(l) for l in range(1, DEPTH)]


def _gathered_to_full(g, axis):
    if axis == 2:
        return g.transpose(1, 0, 2).reshape(g.shape[1], N_DEV * g.shape[2])
    return g.reshape(N_DEV * g.shape[1], g.shape[2])


def _full_to_shards(t, axis):
    a, b = t.shape
    if axis == 2:
        return t.reshape(a, N_DEV, b // N_DEV).transpose(1, 0, 2)
    return t.reshape(N_DEV, a // N_DEV, b)


def _pad_rows8(a):
    flat = a.reshape(-1)
    n = -(-flat.shape[0] // 1024) * 1024
    return jnp.pad(flat, (0, n - flat.shape[0])).reshape(-1, 128)


def _even_fwd(xn, h, w_in, w_out, bias, sinks_row, l, ride=None):
    c0 = A_IN + 3 * B_W
    proj = lax.empty((xn.shape[0], w_in.shape[1]), MXU_DT)
    proj = _matmul(xn, w_in[:, :c0], out_dtype=MXU_DT, tm=1024, tn=512, name=f"even_in_{l}", ride=ride, into=(proj, 0))
    proj, landed = proj if ride is not None else (proj, None)
    for g, (_, d) in list(enumerate(B_BRANCHES))[1:]:
        col0 = A_IN + 3 * B_W * g
        proj = _matmul(_perm_rows(xn, d), w_in[:, col0:col0 + 3 * B_W], out_dtype=MXU_DT, tm=1024, tn=3 * B_W,
                       name=f"even_in_dil{d}_{l}", into=(proj, col0))
    o_all, lse = _banded_fwd(proj, bias, sinks_row)
    attn, post_vjp = jax.vjp(_even_post, o_all, lse)
    attn = attn.astype(MXU_DT)
    h1 = _matmul(attn, w_out, epi='add', extra=h, tm=1024, tn=512, name=f"even_out_{l}")
    return h1, (proj, attn, post_vjp), landed


def _even_bwd(dh, xn, ctx, w_in, w_out, bias, sinks_row, l, norm, ride=None):
    proj, attn, post_vjp = ctx
    d_attn = _matmul(dh, w_out, trans_b=True, tm=1024, tn=768, name=f"even_out_dx_{l}")
    g_w_out = _matmul_tn(attn, dh, tk=768, tn=512, name=f"even_out_dw_{l}")
    do_all, dlse = post_vjp(d_attn)
    dq, dk, dv, dkh, dvh, dbias_t, dsinks = _banded_bwd(proj, bias.transpose(0, 2, 1), sinks_row, do_all, dlse)
    dbias = dbias_t.transpose(0, 2, 1)
    dk = _halo_fold(dk, dkh, f"halo_k_{l}")
    dv = _halo_fold(dv, dvh, f"halo_v_{l}")

    def kv_sum(t):
        heads = [t[:, i * HEAD_DIM:(i + 1) * HEAD_DIM] for i in range(A_Q_HEADS)]
        return jnp.concatenate([sum(heads[j * A_GROUP:(j + 1) * A_GROUP]) for j in range(A_KV_HEADS)], axis=1)

    groups = [jnp.concatenate([dq[:, :A_W], kv_sum(dk), kv_sum(dv)], axis=1).astype(MXU_DT)]
    for g, (_, d) in enumerate(B_BRANCHES):
        cols = slice(A_W + g * B_W, A_W + (g + 1) * B_W)
        grp = jnp.concatenate([dq[:, cols], dk[:, cols], dv[:, cols]], axis=1).astype(MXU_DT)
        groups.append(_unperm_rows(grp, d))
    dproj = jnp.concatenate(groups, axis=1)
    g_w_in = _matmul_tn(xn, dproj, tk=512, tn=1024, name=f"even_in_dw_{l}")
    res = _matmul(dproj, w_in, trans_b=True, epi='norm_bwd', norm=norm, tm=512, tn=w_in.shape[0],
                  name=f"even_in_dx_{l}", ride=ride)
    dh_new, g_norm, landed = res if ride is not None else (*res, None)
    return dh_new, g_norm[0], g_w_in, g_w_out, dbias, dsinks[:A_Q_HEADS, 0, 0], landed


def _mla_fwd(xn, h, w_down, q_norm, w_uq, kv_norm, w_ukv, w_o, l):
    S = xn.shape[0]
    T = min(FLASH_T, S)
    down = _matmul(xn, w_down, tm=1024, tn=768, name=f"mla_down_{l}")
    c_q, c_kv, kr_raw = down[:, :C_Q_RANK], down[:, C_Q_RANK:C_Q_RANK + C_KV_RANK], down[:, C_Q_RANK + C_KV_RANK:C_DOWN]
    cqn = _rmsnorm(c_q, q_norm, out_dtype=MXU_DT, name=f"mla_qnorm_{l}")
    ckvn = _rmsnorm(c_kv, kv_norm, out_dtype=MXU_DT, name=f"mla_kvnorm_{l}")
    cos, sin = _rope_tables(S, C_ROPE)
    to_t = lambda t: t.reshape(S // T, T, -1).transpose(0, 2, 1)
    qt = _mla_q_proj(cqn, w_uq.T.reshape(C_HEADS, C_QK, C_Q_RANK), to_t(cos), to_t(sin), name=f"mla_uq_{l}")
    w_kv = w_ukv.reshape(C_KV_RANK, C_HEADS, C_NOPE + C_V).transpose(1, 0, 2)
    kr = _rope(kr_raw[None])[0]
    kt, vt1, kn, vn = _mla_kv_proj(ckvn, w_kv[..., :C_NOPE], w_kv[..., C_NOPE:], kr, to_t(kr), name=f"mla_ukv_{l}")
    ot, lse = _flash_fwd(qt, kn, vt1)
    h1 = _mla_out_proj(ot, w_o.reshape(C_HEADS, C_V, -1), h, name=f"mla_o_{l}")
    return h1, (c_q, c_kv, cqn, ckvn, qt, kn, kt, vn, ot, lse)


def _mla_bwd(dh, xn, ctx, w_down, q_norm, w_uq, kv_norm, w_ukv, w_o, l, norm):
    c_q, c_kv, cqn, ckvn, qt, kn, kt, vn, ot, lse = ctx
    S = xn.shape[0]
    T = qt.shape[-1]
    dot, delta, dw_o = _mla_out_proj_bwd(dh, ot, w_o.reshape(C_HEADS, C_V, -1), name=f"mla_o_bwd_{l}")
    g_w_o = dw_o.reshape(w_o.shape)
    dqt, dkt, dvt = _flash_bwd(qt, kn, kt, vn, dot, lse, delta)
    cos, sin = _rope_tables(S, C_ROPE)
    to_t = lambda t: t.reshape(S // T, T, -1).transpose(0, 2, 1)
    dcqn, dwq_t = _mla_q_proj_bwd(dqt, cqn, w_uq.T.reshape(C_HEADS, C_QK, C_Q_RANK), to_t(cos), to_t(sin),
                                  name=f"mla_uq_bwd_{l}")
    g_w_uq = dwq_t.reshape(C_HEADS * C_QK, C_Q_RANK).T
    w_kv = w_ukv.reshape(C_KV_RANK, C_HEADS, C_NOPE + C_V).transpose(1, 0, 2)
    dckvn, dwk_t, dwv_t, dkr_t = _mla_kv_proj_bwd(dkt, dvt, ckvn, w_kv[..., :C_NOPE], w_kv[..., C_NOPE:],
                                                  name=f"mla_ukv_bwd_{l}")
    g_w_ukv = jnp.concatenate([dwk_t, dwv_t], axis=1).reshape(C_HEADS * (C_NOPE + C_V), C_KV_RANK).T
    _, rope_vjp = jax.vjp(lambda t: _rope(t[None])[0], jnp.zeros((S, C_ROPE), F32))
    (dkr_raw,) = rope_vjp(dkr_t.transpose(0, 2, 1).reshape(S, C_ROPE))
    dc_q, g_q_norm = _rmsnorm_bwd(c_q, q_norm, dcqn, None, name=f"mla_qnorm_bwd_{l}")
    dc_kv, g_kv_norm = _rmsnorm_bwd(c_kv, kv_norm, dckvn, None, name=f"mla_kvnorm_bwd_{l}")
    ddown = jnp.concatenate([dc_q, dc_kv, dkr_raw, jnp.zeros((S, C_DOWN_PAD - C_DOWN), F32)], axis=1).astype(MXU_DT)
    g_w_down = _matmul_tn(xn, ddown, tk=512, tn=768, name=f"mla_down_dw_{l}")[:, :C_DOWN]
    dh_new, g_norm = _matmul(ddown, w_down, trans_b=True, epi='norm_bwd', norm=norm, tm=512, tn=w_down.shape[0],
                             name=f"mla_down_dx_{l}")
    return dh_new, g_norm[0], g_w_down, g_q_norm[0], g_w_uq, g_kv_norm[0], g_w_ukv, g_w_o


def kernel(x, rel_bias, attn_norm, mlp_norm, final_norm, w_in_ab, sinks, w_out_ab, w_down_c, q_norm_c, w_uq_c, kv_norm_c, w_ukv_c, w_o_c, w_mlp_up, w_mlp_down, loss_target, m_rel_bias, m_attn_norm, m_mlp_norm, m_final_norm, m_w_in_ab, m_sinks, m_w_out_ab, m_w_down_c, m_q_norm_c, m_w_uq_c, m_kv_norm_c, m_w_ukv_c, m_w_o_c, m_w_mlp_up, m_w_mlp_down, v_rel_bias, v_attn_norm, v_mlp_norm, v_final_norm, v_w_in_ab, v_sinks, v_w_out_ab, v_w_down_c, v_q_norm_c, v_w_uq_c, v_kv_norm_c, v_w_ukv_c, v_w_o_c, v_w_mlp_up, v_w_mlp_down):
    W = dict(w_in_ab=w_in_ab, w_out_ab=w_out_ab, w_down_c=w_down_c, w_uq_c=w_uq_c, w_ukv_c=w_ukv_c, w_o_c=w_o_c,
             w_mlp_up=w_mlp_up, w_mlp_down=w_mlp_down)
    Mo = dict(w_in_ab=m_w_in_ab, w_out_ab=m_w_out_ab, w_down_c=m_w_down_c, w_uq_c=m_w_uq_c, w_ukv_c=m_w_ukv_c,
              w_o_c=m_w_o_c, w_mlp_up=m_w_mlp_up, w_mlp_down=m_w_mlp_down)
    Vo = dict(w_in_ab=v_w_in_ab, w_out_ab=v_w_out_ab, w_down_c=v_w_down_c, w_uq_c=v_w_uq_c, w_ukv_c=v_w_ukv_c,
              w_o_c=v_w_o_c, w_mlp_up=v_w_mlp_up, w_mlp_down=v_w_mlp_down)
    S = x.shape[1]
    me = 4 * lax.axis_index("x") + 2 * lax.axis_index("y") + lax.axis_index("c")
    axis_of = dict(_BIG)

    groups = _exchange_groups()

    def pack(src, gi):
        return _pack_rows([src[n][i] for n, i in groups[gi]])

    def unpack_group(gathered, gi):
        shapes = [W[n].shape[1:] for n, _ in groups[gi]]
        return {n: _gathered_to_full(g, axis_of[n])
                for (n, _), g in zip(groups[gi], _unpack_rows(gathered, shapes))}

    def send_of(G, gi):
        return _pack_rows_per_device([_full_to_shards(G[n], axis_of[n]) for n, _ in groups[gi]]).astype(MXU_DT)

    w_packs = [pack(W, gi) for gi in range(len(groups))]
    gathered = _exchange(w_packs[0].astype(MXU_DT), False, "gather_weights_0")
    gains = _exchange(jnp.concatenate([_pad_rows8(q_norm_c), _pad_rows8(kv_norm_c)], axis=0), False, "gather_gains")
    n_odd = q_norm_c.shape[0]
    q_norm_full = gains[:, 0].reshape(N_DEV, -1)[:, :q_norm_c.size].reshape(N_DEV, n_odd, -1).transpose(1, 0, 2).reshape(n_odd, C_Q_RANK)
    kv_norm_full = gains[:, 8].reshape(N_DEV, -1)[:, :kv_norm_c.size].reshape(N_DEV, n_odd, -1).transpose(1, 0, 2).reshape(n_odd, C_KV_RANK)

    bias, bias_vjp = jax.vjp(_make_bias, rel_bias)
    sink_rows = [jnp.broadcast_to(jnp.concatenate([sinks[e], jnp.full((B_HEADS,), NEG, F32)])[:, None, None],
                                  (N_BIAS_HEADS, 1, 128)) for e in range(sinks.shape[0])]

    h = x[0]
    saved = []
    for l in range(DEPTH):
        full = unpack_group(gathered, 0 if l == 0 else l + 1)
        if l % 2 == 1:
            full["w_down_c"] = jnp.pad(full["w_down_c"], ((0, 0), (0, C_DOWN_PAD - C_DOWN)))
        xn = _rmsnorm(h, attn_norm[l], out_dtype=MXU_DT, name=f"attn_norm_{l}")
        if l == 0:
            h1, ctx, gathered_mlp = _even_fwd(xn, h, full["w_in_ab"], full["w_out_ab"], bias, sink_rows[0], l,
                                              ride=(w_packs[1].astype(MXU_DT), False))
            full.update(unpack_group(gathered_mlp, 1))
        elif l % 2 == 0:
            h1, ctx, _ = _even_fwd(xn, h, full["w_in_ab"], full["w_out_ab"], bias, sink_rows[l // 2], l)
        else:
            o = l // 2
            h1, ctx = _mla_fwd(xn, h, full["w_down_c"], q_norm_full[o], full["w_uq_c"], kv_norm_full[o],
                               full["w_ukv_c"], full["w_o_c"], l)
        xn2 = _rmsnorm(h1, mlp_norm[l], out_dtype=MXU_DT, name=f"mlp_norm_{l}")
        if l + 1 < DEPTH:
            act, slope, gathered = _matmul(xn2, full["w_mlp_up"], out_dtype=MXU_DT, epi='relu2', tm=2048, tn=512,
                                           name=f"mlp_up_{l}", ride=(w_packs[l + 2].astype(MXU_DT), False))
        else:
            act, slope = _matmul(xn2, full["w_mlp_up"], out_dtype=MXU_DT, epi='relu2', tm=2048, tn=512,
                                 name=f"mlp_up_{l}")
        h2 = _matmul(act, full["w_mlp_down"], epi='add', extra=h1, tm=1024, tn=512, name=f"mlp_down_{l}")
        saved.append((h, xn, h1, xn2, act, slope, ctx, full))
        h = h2

    loss_row, dh, g_final = _loss_head(h, loss_target[0], final_norm)

    g_attn_norm, g_mlp_norm = [None] * DEPTH, [None] * DEPTH
    g_sinks, g_qn, g_kvn = [None] * sinks.shape[0], [None] * n_odd, [None] * n_odd
    dbias_total = None
    landed = [None] * len(groups)
    send = None
    for l in reversed(range(DEPTH)):
        h0, xn, h1, xn2, act, slope, ctx, full = saved[l]
        G = {}
        if send is None:
            du = _matmul(dh, full["w_mlp_down"], trans_b=True, out_dtype=MXU_DT, epi='mul', extra=slope,
                         tm=2048, tn=512, name=f"mlp_down_dx_{l}")
        else:
            du, landed[l + 2] = _matmul(dh, full["w_mlp_down"], trans_b=True, out_dtype=MXU_DT, epi='mul', extra=slope,
                                        tm=2048, tn=512, name=f"mlp_down_dx_{l}", ride=(send, True))
        G["w_mlp_down"] = _matmul_tn(act, dh, tk=512, tn=1024, name=f"mlp_down_dw_{l}")
        G["w_mlp_up"] = _matmul_tn(xn2, du, tk=512, tn=1024, name=f"mlp_up_dw_{l}")
        dh, g = _matmul(du, full["w_mlp_up"], trans_b=True, epi='norm_bwd', norm=(h1, mlp_norm[l], dh),
                        tm=512, tn=D_MODEL, name=f"mlp_up_dx_{l}")
        g_mlp_norm[l] = g[0]
        if l % 2 == 0:
            e = l // 2
            dh, g_attn_norm[l], G["w_in_ab"], G["w_out_ab"], dbias, g_sinks[e], landed_mlp = _even_bwd(
                dh, xn, ctx, full["w_in_ab"], full["w_out_ab"], bias, sink_rows[e], l, (h0, attn_norm[l], dh),
                ride=(send_of(G, 1), True) if l == 0 else None)
            if l == 0:
                landed[1] = landed_mlp
            dbias_total = dbias if dbias_total is None else dbias_total + dbias
        else:
            o = l // 2
            dh, g_attn_norm[l], G["w_down_c"], g_qn[o], G["w_uq_c"], g_kvn[o], G["w_ukv_c"], G["w_o_c"] = _mla_bwd(
                dh, xn, ctx, full["w_down_c"], q_norm_full[o], full["w_uq_c"], kv_norm_full[o],
                full["w_ukv_c"], full["w_o_c"], l, (h0, attn_norm[l], dh))
        send = send_of(G, 0 if l == 0 else l + 1)
    landed[0] = _exchange(send, True, "scatter_grads_0")
    grad_x = dh[None]
    (g_rel_bias,) = bias_vjp(dbias_total)

    big = [{}, {}, {}, {}]
    for gi in range(len(groups)):
        outs = _adamw(landed[gi], w_packs[gi], pack(Mo, gi), pack(Vo, gi), name=f"adamw_{gi}")
        shapes = [W[n].shape[1:] for n, _ in groups[gi]]
        for kind, buf in enumerate(outs):
            for (n, _), t in zip(groups[gi], _unpack_rows(buf, shapes)):
                big[kind].setdefault(n, []).append(t)
    big_out = [{n: jnp.stack(ts) for n, ts in d.items()} for d in big]

    small_g = [g_rel_bias, jnp.stack(g_attn_norm), jnp.stack(g_mlp_norm), g_final[0], jnp.stack(g_sinks),
               jnp.stack(g_qn), jnp.stack(g_kvn), loss_row[0, :1]]
    small_w = [rel_bias, attn_norm, mlp_norm, final_norm, sinks, q_norm_c, kv_norm_c, jnp.zeros((1,), F32)]
    small_m = [m_rel_bias, m_attn_norm, m_mlp_norm, m_final_norm, m_sinks, m_q_norm_c, m_kv_norm_c, jnp.zeros((1,), F32)]
    small_v = [v_rel_bias, v_attn_norm, v_mlp_norm, v_final_norm, v_sinks, v_q_norm_c, v_kv_norm_c, jnp.ones((1,), F32)]
    offs = np.cumsum([0] + [-(-a.size // 1024) * 8 for a in small_g])
    partials = _exchange(jnp.concatenate([_pad_rows8(a) for a in small_g], axis=0), False, "gather_small_grads")

    def mine(i, a_full_shape, local):
        p = partials[:, offs[i]:offs[i + 1]].reshape(N_DEV, -1)[:, :math.prod(a_full_shape)]
        p = p.reshape((N_DEV,) + tuple(a_full_shape))
        if local.shape != tuple(a_full_shape):
            width = local.shape[-1]
            p = lax.dynamic_slice_in_dim(p, me * width, width, axis=p.ndim - 1)
        return jnp.stack([_pad_rows8(p[s]) for s in range(N_DEV)])

    parts_small = jnp.concatenate([mine(i, g.shape, w) for i, (g, w) in enumerate(zip(small_g, small_w))], axis=1)
    pk = lambda arrs: jnp.concatenate([_pad_rows8(a) for a in arrs], axis=0)
    small_out = _adamw(parts_small, pk(small_w), pk(small_m), pk(small_v), name="adamw_small", tr=parts_small.shape[1])
    offs2 = np.cumsum([0] + [-(-a.size // 1024) * 8 for a in small_w])

    def unpack_small(buf):
        return [buf[offs2[i]:offs2[i + 1]].reshape(-1)[:a.size].reshape(a.shape) for i, a in enumerate(small_w)]

    sg, sd, sm, sv = (unpack_small(b) for b in small_out)
    loss = sg[7][0]

    order = ['rel_bias', 'attn_norm', 'mlp_norm', 'final_norm', 'w_in_ab', 'sinks', 'w_out_ab', 'w_down_c', 'q_norm_c',
             'w_uq_c', 'kv_norm_c', 'w_ukv_c', 'w_o_c', 'w_mlp_up', 'w_mlp_down']
    small_idx = {'rel_bias': 0, 'attn_norm': 1, 'mlp_norm': 2, 'final_norm': 3, 'sinks': 4, 'q_norm_c': 5, 'kv_norm_c': 6}

    def pick(kind):
        res = []
        for n in order:
            if n in small_idx:
                res.append((sg, sd, sm, sv)[kind][small_idx[n]])
            else:
                res.append(big_out[kind][n])
        return res

    return (loss, grad_x, *pick(0), *pick(1), *pick(2), *pick(3))
```

```python
import math

import numpy as np
import jax
import jax.numpy as jnp
from jax import lax
from jax.experimental import pallas as pl
from jax.experimental.pallas import tpu as pltpu

F32 = jnp.float32
MXU_DT = jnp.bfloat16

N_DEV = 8
D_MODEL = 1024
DEPTH = 4
HEAD_DIM = 64
BLOCK = 128
EPS = 1e-6
NEG = -1e30
A_Q_HEADS = 8
A_KV_HEADS = 2
A_GROUP = A_Q_HEADS // A_KV_HEADS
A_WINDOW = 128
B_BRANCHES = ((128, 1), (512, 4), (2048, 16))
B_HPB = 4
B_HEADS = len(B_BRANCHES) * B_HPB
NUM_BUCKETS = 32
MAX_DISTANCE = 2048
N_BIAS_HEADS = A_Q_HEADS + B_HEADS
N_BAND_KV = A_KV_HEADS + B_HEADS
A_IN = (A_Q_HEADS + 2 * A_KV_HEADS) * HEAD_DIM
C_HEADS = 8
C_NOPE = 64
C_ROPE = 32
C_QK = C_NOPE + C_ROPE
C_V = 64
C_Q_RANK = 384
C_KV_RANK = 256
C_DOWN = C_Q_RANK + C_KV_RANK + C_ROPE
C_DOWN_PAD = 768
ROPE_THETA = 10000.0
N_CHUNKS = 16
FLASH_T = 512

ADAM_LR = 0.001
ADAM_B1 = 0.9
ADAM_B2 = 0.999
ADAM_EPS = 1e-08
ADAM_WD = 0.01
ADAM_STEP = 10

V7X_VMEM_BYTES = 64 * 1024 * 1024
VMEM_LIMIT = V7X_VMEM_BYTES - 8 * 1024 * 1024


def _pcall(body, **kw):
    return pl.pallas_call(body, **kw)


def _cparams(*sem):
    return pltpu.CompilerParams(dimension_semantics=sem, vmem_limit_bytes=VMEM_LIMIT)


def _exchange(src, all_to_all, name):
    def body(src_ref, out_ref, send_sems, recv_sems, local_sem):
        copies = _exchange_copies(src_ref, out_ref, send_sems, recv_sems, local_sem, all_to_all)
        for cp in copies:
            cp.start()
        _exchange_wait(copies)

    return _pcall(
        body, name=name,
        out_shape=_exchange_out(src),
        in_specs=[pl.BlockSpec(memory_space=pl.ANY)],
        out_specs=pl.BlockSpec(memory_space=pl.ANY),
        scratch_shapes=_exchange_sems(),
    )(src)


def _exchange_out(src):
    return jax.ShapeDtypeStruct((N_DEV,) + src.shape[-2:], src.dtype)


def _exchange_sems():
    return [pltpu.SemaphoreType.DMA((N_DEV - 1,)), pltpu.SemaphoreType.DMA((N_DEV - 1,)), pltpu.SemaphoreType.DMA]


def _exchange_copies(src_ref, out_ref, send_sems, recv_sems, local_sem, all_to_all):
    x, y, c = lax.axis_index("x"), lax.axis_index("y"), lax.axis_index("c")
    me = 4 * x + 2 * y + c

    def piece(dev):
        return src_ref.at[dev] if all_to_all else src_ref

    copies = [pltpu.make_async_copy(piece(me), out_ref.at[me], local_sem)]
    for k in range(1, N_DEV):
        px = 1 - x if (k >> 2) & 1 else x
        py = 1 - y if (k >> 1) & 1 else y
        pc = 1 - c if k & 1 else c
        copies.append(pltpu.make_async_remote_copy(
            src_ref=piece(4 * px + 2 * py + pc), dst_ref=out_ref.at[me],
            send_sem=send_sems.at[k - 1], recv_sem=recv_sems.at[k - 1],
            device_id=(px, py, pc), device_id_type=pl.DeviceIdType.MESH))
    return copies


def _exchange_wait(copies):
    for cp in copies[1:]:
        cp.wait()
    copies[0].wait()


def _matmul(a, b, *, trans_b=False, out_dtype=F32, epi=None, extra=None, norm=None, into=None, tm=512, tn=512,
            name, ride=None):
    M, K = a.shape
    N = b.shape[0] if trans_b else b.shape[1]
    tm, tn = min(tm, M), min(tn, N)
    assert M % tm == 0 and N % tn == 0 and (b.shape[1] if trans_b else b.shape[0]) == K
    assert (epi == 'norm_bwd') == (norm is not None) and (norm is None or tn == N)
    dn = (((1,), (1,)), ((), ())) if trans_b else (((1,), (0,)), ((), ()))
    n_i, n_j = M // tm, N // tn

    def body(*refs):
        it = iter(refs)
        a_ref, b_ref = next(it), next(it)
        e_ref = next(it) if extra is not None else None
        x_ref, g_ref, dres_ref = (next(it), next(it), next(it)) if norm is not None else (None, None, None)
        if into is not None:
            next(it)
        src_ref = next(it) if ride is not None else None
        o_ref = next(it)
        slope_ref = next(it) if epi == 'relu2' else None
        dg_ref = next(it) if norm is not None else None
        i, j = pl.program_id(0), pl.program_id(1)
        if ride is not None:
            land_ref = next(it)
            copies = _exchange_copies(src_ref, land_ref, *it, ride[1])

            @pl.when((i == 0) & (j == 0))
            def _():
                for cp in copies:
                    cp.start()

        acc = lax.dot_general(a_ref[...].astype(MXU_DT), b_ref[...].astype(MXU_DT), dn,
                              preferred_element_type=F32)
        if epi == 'relu2':
            r = jnp.maximum(acc, 0.0)
            acc = r * r
            slope_ref[...] = (2.0 * r).astype(slope_ref.dtype)
        elif epi == 'add':
            acc = acc + e_ref[...].astype(F32)
        elif epi == 'mul':
            acc = acc * e_ref[...].astype(F32)
        elif epi == 'norm_bwd':
            @pl.when(i == 0)
            def _():
                dg_ref[...] = jnp.zeros_like(dg_ref)

            xf = x_ref[...]
            r = lax.rsqrt(jnp.mean(xf * xf, axis=-1, keepdims=True) + EPS)
            xhat = xf * r
            dg_ref[...] += jnp.sum(acc * xhat, axis=0, keepdims=True)
            dyg = acc * g_ref[...]
            acc = r * (dyg - xhat * jnp.mean(dyg * xhat, axis=-1, keepdims=True)) + dres_ref[...]
        o_ref[...] = acc.astype(out_dtype)

        if ride is not None:
            @pl.when((i == n_i - 1) & (j == n_j - 1))
            def _():
                _exchange_wait(copies)

    b_spec = pl.BlockSpec((tn, K), lambda i, j: (j, 0)) if trans_b else pl.BlockSpec((K, tn), lambda i, j: (0, j))
    tile = pl.BlockSpec((tm, tn), lambda i, j: (i, j))
    in_specs = [pl.BlockSpec((tm, K), lambda i, j: (i, 0)), b_spec]
    args = [a, b]
    out_shape, out_specs, aliases = [jax.ShapeDtypeStruct((M, N), out_dtype)], [tile], {}
    if extra is not None:
        in_specs.append(tile)
        args.append(extra)
    if epi == 'relu2':
        out_shape.append(jax.ShapeDtypeStruct((M, N), out_dtype))
        out_specs.append(tile)
    if norm is not None:
        vec = pl.BlockSpec((1, N), lambda i, j: (0, 0))
        in_specs += [tile, vec, tile]
        args += [norm[0], norm[1].reshape(1, N), norm[2]]
        out_shape.append(jax.ShapeDtypeStruct((1, N), F32))
        out_specs.append(vec)
    if into is not None:
        buf, col0 = into
        assert col0 % tn == 0 and buf.shape[0] == M and buf.dtype == out_dtype
        aliases = {len(args): 0}
        in_specs.append(pl.BlockSpec(memory_space=pl.ANY))
        args.append(buf)
        out_shape[0] = jax.ShapeDtypeStruct(buf.shape, out_dtype)
        out_specs[0] = pl.BlockSpec((tm, tn), lambda i, j: (i, j + col0 // tn))
    scratch = []
    if ride is not None:
        in_specs.append(pl.BlockSpec(memory_space=pl.ANY))
        args.append(ride[0])
        out_shape.append(_exchange_out(ride[0]))
        out_specs.append(pl.BlockSpec(memory_space=pl.ANY))
        scratch = _exchange_sems()
    ordered = ride is not None or norm is not None
    res = _pcall(
        body, name=name, grid=(n_i, n_j), out_shape=tuple(out_shape), in_specs=in_specs, out_specs=tuple(out_specs),
        scratch_shapes=scratch, input_output_aliases=aliases,
        compiler_params=_cparams(*(("arbitrary", "arbitrary") if ordered else ("parallel", "parallel"))),
    )(*args)
    return res[0] if len(res) == 1 else res


def _matmul_tn(a, b, *, tk=512, tn=512, tm=1024, name):
    M, Ka = a.shape
    N = b.shape[1]
    tk, tn, tm = min(tk, Ka), min(tn, N), min(tm, M)
    assert Ka % tk == 0 and N % tn == 0 and M % tm == 0 and b.shape[0] == M

    def body(a_ref, b_ref, o_ref):
        @pl.when(pl.program_id(2) == 0)
        def _():
            o_ref[...] = jnp.zeros_like(o_ref)

        o_ref[...] += lax.dot_general(a_ref[...].astype(MXU_DT), b_ref[...].astype(MXU_DT),
                                      (((0,), (0,)), ((), ())), preferred_element_type=F32)

    return _pcall(
        body, name=name, grid=(Ka // tk, N // tn, M // tm),
        out_shape=jax.ShapeDtypeStruct((Ka, N), F32),
        in_specs=[pl.BlockSpec((tm, tk), lambda i, j, r: (r, i)), pl.BlockSpec((tm, tn), lambda i, j, r: (r, j))],
        out_specs=pl.BlockSpec((tk, tn), lambda i, j, r: (i, j)),
        compiler_params=_cparams("parallel", "parallel", "arbitrary"),
    )(a, b)


def _rmsnorm(x, g, *, out_dtype, name, tr=512):
    S, D = x.shape
    tr = min(tr, S)

    def body(x_ref, g_ref, o_ref):
        xf = x_ref[...].astype(F32)
        r = lax.rsqrt(jnp.mean(xf * xf, axis=-1, keepdims=True) + EPS)
        o_ref[...] = (xf * r * g_ref[...]).astype(out_dtype)

    return _pcall(
        body, name=name, grid=(S // tr,),
        out_shape=jax.ShapeDtypeStruct((S, D), out_dtype),
        in_specs=[pl.BlockSpec((tr, D), lambda i: (i, 0)), pl.BlockSpec((1, D), lambda i: (0, 0))],
        out_specs=pl.BlockSpec((tr, D), lambda i: (i, 0)),
        compiler_params=_cparams("parallel"),
    )(x, g.reshape(1, D))


def _rmsnorm_bwd(x, g, dy, dres, *, name, tr=512):
    S, D = x.shape
    tr = min(tr, S)

    def body(*refs):
        if dres is None:
            x_ref, g_ref, dy_ref, dx_ref, dg_ref = refs
        else:
            x_ref, g_ref, dy_ref, dres_ref, dx_ref, dg_ref = refs

        @pl.when(pl.program_id(0) == 0)
        def _():
            dg_ref[...] = jnp.zeros_like(dg_ref)

        xf = x_ref[...].astype(F32)
        r = lax.rsqrt(jnp.mean(xf * xf, axis=-1, keepdims=True) + EPS)
        xhat = xf * r
        dyf = dy_ref[...].astype(F32)
        dg_ref[...] += jnp.sum(dyf * xhat, axis=0, keepdims=True)
        dyg = dyf * g_ref[...]
        dx = r * (dyg - xhat * jnp.mean(dyg * xhat, axis=-1, keepdims=True))
        if dres is not None:
            dx = dx + dres_ref[...]
        dx_ref[...] = dx

    row = pl.BlockSpec((tr, D), lambda i: (i, 0))
    vec = pl.BlockSpec((1, D), lambda i: (0, 0))
    args = [x, g.reshape(1, D), dy] + ([] if dres is None else [dres])
    return _pcall(
        body, name=name, grid=(S // tr,),
        out_shape=(jax.ShapeDtypeStruct((S, D), F32), jax.ShapeDtypeStruct((1, D), F32)),
        in_specs=[row, vec, row] + ([] if dres is None else [row]),
        out_specs=(row, vec),
        compiler_params=_cparams("arbitrary"),
    )(*args)


def _loss_head(h, t, g, *, tr=512):
    S, D = h.shape
    tr = min(tr, S)

    def body(h_ref, t_ref, g_ref, loss_ref, dh_ref, dg_ref):
        @pl.when(pl.program_id(0) == 0)
        def _():
            dg_ref[...] = jnp.zeros_like(dg_ref)
            loss_ref[...] = jnp.zeros_like(loss_ref)

        xf = h_ref[...]
        r = lax.rsqrt(jnp.mean(xf * xf, axis=-1, keepdims=True) + EPS)
        xhat = xf * r
        e = xhat * g_ref[...] - t_ref[...]
        part = 0.5 * jnp.sum(jnp.mean(e * e, axis=-1, keepdims=True), axis=0, keepdims=True)
        loss_ref[...] += jnp.broadcast_to(part, loss_ref.shape)
        dy = e * (1.0 / D)
        dg_ref[...] += jnp.sum(dy * xhat, axis=0, keepdims=True)
        dyg = dy * g_ref[...]
        dh_ref[...] = r * (dyg - xhat * jnp.mean(dyg * xhat, axis=-1, keepdims=True))

    row = pl.BlockSpec((tr, D), lambda i: (i, 0))
    vec = pl.BlockSpec((1, D), lambda i: (0, 0))
    return _pcall(
        body, name="loss_head", grid=(S // tr,),
        out_shape=(jax.ShapeDtypeStruct((1, 128), F32), jax.ShapeDtypeStruct((S, D), F32),
                   jax.ShapeDtypeStruct((1, D), F32)),
        in_specs=[row, row, vec],
        out_specs=(pl.BlockSpec((1, 128), lambda i: (0, 0)), row, vec),
        compiler_params=_cparams("arbitrary"),
    )(h, t, g.reshape(1, D))


N_PAIRS = N_BIAS_HEADS // 2
A_PAIRS = A_Q_HEADS // 2
PAIR_W = 2 * HEAD_DIM
assert PAIR_W == 128 and A_KV_HEADS * HEAD_DIM == PAIR_W and B_HPB * HEAD_DIM == 2 * PAIR_W


def _pair_period(p):
    return jnp.where(p < A_PAIRS + 2, 16, jnp.where(p < A_PAIRS + 4, 4, 1))


def _pair_cols(p):
    b = jnp.maximum(p - A_PAIRS, 0)
    base, pp = 6 + 6 * (b // 2), b % 2
    is_a = p < A_PAIRS
    return (jnp.where(is_a, p, base + pp), jnp.where(is_a, A_PAIRS, base + 2 + pp),
            jnp.where(is_a, A_PAIRS + 1, base + 4 + pp))


def _band_specs(S):
    ch = S // N_CHUNKS
    nb = ch // BLOCK
    col = lambda i: (lambda p, c: (c, _pair_cols(p)[i]))
    prev = lambda i: (lambda p, c: (jnp.maximum(c * nb - 1, 0), _pair_cols(p)[i]))
    qkv = [pl.BlockSpec((ch, PAIR_W), col(0)), pl.BlockSpec((ch, PAIR_W), col(1)), pl.BlockSpec((BLOCK, PAIR_W), prev(1)),
           pl.BlockSpec((ch, PAIR_W), col(2)), pl.BlockSpec((BLOCK, PAIR_W), prev(2))]
    out_spec = pl.BlockSpec((ch, PAIR_W), lambda p, c: (c, p))
    sink_spec = pl.BlockSpec((2, 1, 128), lambda p, c: (p, 0, 0))
    row_spec = pl.BlockSpec((1, 1, nb, 2, BLOCK), lambda p, c: (p, c, 0, 0, 0))
    return ch, nb, qkv, out_spec, sink_spec, row_spec


def _pair_kv(p, e, ref):
    half = jnp.where(p < A_PAIRS, p // (A_GROUP // 2), e)
    return jnp.where(half == 0, ref[:, :HEAD_DIM], ref[:, HEAD_DIM:])


def _eye():
    return lax.broadcasted_iota(jnp.int32, (BLOCK, BLOCK), 0) == lax.broadcasted_iota(jnp.int32, (BLOCK, BLOCK), 1)


_NT = (((1,), (1,)), ((), ()))
_NN = (((1,), (0,)), ((), ()))
_TN = (((0,), (0,)), ((), ()))


_B_NT = (((2,), (2,)), ((0,), (0,)))
_B_NN = (((2,), (1,)), ((0,), (0,)))


def _bdot(a, b, dn):
    return lax.dot_general(a, b, dn, preferred_element_type=F32)


def _with_prev(first, t3):
    return first[None] if t3.shape[0] == 1 else jnp.concatenate([first[None], t3[:-1]], axis=0)


def _mask_first(s_prev, prev_ok):
    s0 = jnp.where(prev_ok, s_prev[0], NEG)[None]
    return s0 if s_prev.shape[0] == 1 else jnp.concatenate([s0, s_prev[1:]], axis=0)


def _banded_fwd(proj, bias, sinks):
    S = proj.shape[0]
    dh = HEAD_DIM
    ch, nb, qkv, out_spec, sink_spec, row_spec = _band_specs(S)
    bias_spec = pl.BlockSpec((2, BLOCK, 2 * BLOCK), lambda p, c: (p, 0, 0))
    scale = HEAD_DIM ** -0.5

    def body(q_ref, k_ref, kp_ref, v_ref, vp_ref, b_ref, s_ref, o_ref, lse_ref):
        p, c = pl.program_id(0), pl.program_id(1)
        prev_ok = (c % _pair_period(p)) != 0
        for e in range(2):
            lanes = slice(e * dh, (e + 1) * dh)
            q3 = q_ref[:, lanes].reshape(nb, BLOCK, dh)
            k3, v3 = (_pair_kv(p, e, r).reshape(nb, BLOCK, dh) for r in (k_ref, v_ref))
            kp3, vp3 = _with_prev(_pair_kv(p, e, kp_ref), k3), _with_prev(_pair_kv(p, e, vp_ref), v3)
            sink = s_ref[e, :, :1]
            s_cur = _bdot(q3, k3, _B_NT) * scale + b_ref[e, :, BLOCK:][None]
            s_prev = _mask_first(_bdot(q3, kp3, _B_NT) * scale + b_ref[e, :, :BLOCK][None], prev_ok)
            m = jnp.maximum(jnp.max(s_cur, axis=-1, keepdims=True), jnp.max(s_prev, axis=-1, keepdims=True))
            m = jnp.maximum(m, sink)
            p_cur = jnp.exp(s_cur - m)
            p_prev = jnp.exp(s_prev - m)
            l = jnp.sum(p_cur, axis=-1, keepdims=True) + jnp.sum(p_prev, axis=-1, keepdims=True) + jnp.exp(sink - m)
            acc = _bdot(p_cur.astype(MXU_DT), v3, _B_NN) + _bdot(p_prev.astype(MXU_DT), vp3, _B_NN)
            o_ref[:, lanes] = (acc / l).reshape(ch, dh)
            lse = m + jnp.log(l)
            lse_ref[0, 0, :, e:e + 1, :] = jnp.sum(jnp.where(_eye()[None], lse, 0.0), axis=1, keepdims=True)

    return _pcall(
        body, name="banded_fwd", grid=(N_PAIRS, N_CHUNKS),
        out_shape=(jax.ShapeDtypeStruct((S, N_PAIRS * PAIR_W), F32),
                   jax.ShapeDtypeStruct((N_PAIRS, N_CHUNKS, nb, 2, BLOCK), F32)),
        in_specs=qkv + [bias_spec, sink_spec],
        out_specs=(out_spec, row_spec),
        compiler_params=_cparams("parallel", "parallel"),
    )(proj, proj, proj, proj, proj, bias, sinks)


def _banded_bwd(proj, bias_t, sinks, do, dlse):
    S = proj.shape[0]
    dh = HEAD_DIM
    ch, nb, qkv, out_spec, sink_spec, row_spec = _band_specs(S)
    bias_spec = pl.BlockSpec((2, 2 * BLOCK, BLOCK), lambda p, c: (p, 0, 0))
    scale = HEAD_DIM ** -0.5

    def body(q_ref, k_ref, kp_ref, v_ref, vp_ref, b_ref, s_ref, do_ref, dl_ref,
             dq_ref, dk_ref, dv_ref, dkh_ref, dvh_ref, db_ref, ds_ref):
        p, c = pl.program_id(0), pl.program_id(1)

        @pl.when(c == 0)
        def _():
            db_ref[...] = jnp.zeros_like(db_ref)
            ds_ref[...] = jnp.zeros_like(ds_ref)

        prev_ok = (c % _pair_period(p)) != 0
        for e in range(2):
            lanes = slice(e * dh, (e + 1) * dh)
            q3 = q_ref[:, lanes].reshape(nb, BLOCK, dh)
            k3, v3 = (_pair_kv(p, e, r).reshape(nb, BLOCK, dh) for r in (k_ref, v_ref))
            kp3, vp3 = _with_prev(_pair_kv(p, e, kp_ref), k3), _with_prev(_pair_kv(p, e, vp_ref), v3)
            do3 = do_ref[:, lanes].astype(MXU_DT).reshape(nb, BLOCK, dh)
            sink = s_ref[e, :, :1]
            s_cur = _bdot(k3, q3, _B_NT) * scale + b_ref[e, BLOCK:, :][None]
            s_prev = _mask_first(_bdot(kp3, q3, _B_NT) * scale + b_ref[e, :BLOCK, :][None], prev_ok)
            m = jnp.maximum(jnp.max(s_cur, axis=1, keepdims=True), jnp.max(s_prev, axis=1, keepdims=True))
            m = jnp.maximum(m, sink)
            p_cur = jnp.exp(s_cur - m)
            p_prev = jnp.exp(s_prev - m)
            p_sink = jnp.exp(sink - m)
            inv = 1.0 / (jnp.sum(p_cur, axis=1, keepdims=True) + jnp.sum(p_prev, axis=1, keepdims=True) + p_sink)
            p_cur, p_prev, p_sink = p_cur * inv, p_prev * inv, p_sink * inv
            dp_cur = _bdot(v3, do3, _B_NT)
            dp_prev = _bdot(vp3, do3, _B_NT)
            delta = jnp.sum(p_cur * dp_cur, axis=1, keepdims=True) + jnp.sum(p_prev * dp_prev, axis=1, keepdims=True)
            t = dl_ref[0, 0, :, e:e + 1, :] - delta
            ds_cur = p_cur * (dp_cur + t)
            ds_prev = p_prev * (dp_prev + t)
            dsink = jnp.sum(jnp.sum(p_sink * t, axis=0), axis=-1, keepdims=True)
            ds_ref[e] += jnp.broadcast_to(dsink, (1, 128))
            db_ref[e, :BLOCK, :] += jnp.sum(ds_prev, axis=0)
            db_ref[e, BLOCK:, :] += jnp.sum(ds_cur, axis=0)
            dsb_cur = (ds_cur * scale).astype(MXU_DT)
            dsb_prev = (ds_prev * scale).astype(MXU_DT)
            dk_prev = _bdot(dsb_prev, q3, _B_NN)
            dv_prev = _bdot(p_prev.astype(MXU_DT), do3, _B_NN)

            def shifted(t3):
                z = jnp.zeros((1, BLOCK, dh), F32)
                return z if nb == 1 else jnp.concatenate([t3[1:], z], axis=0)

            dk_ref[:, lanes] = (_bdot(dsb_cur, q3, _B_NN) + shifted(dk_prev)).reshape(ch, dh)
            dv_ref[:, lanes] = (_bdot(p_cur.astype(MXU_DT), do3, _B_NN) + shifted(dv_prev)).reshape(ch, dh)
            dkh_ref[0, 0, :, lanes] = dk_prev[0]
            dvh_ref[0, 0, :, lanes] = dv_prev[0]
            for b in range(nb):
                dq_ref[b * BLOCK:(b + 1) * BLOCK, lanes] = (
                    lax.dot_general(dsb_cur[b], k3[b], _TN, preferred_element_type=F32)
                    + lax.dot_general(dsb_prev[b], kp3[b], _TN, preferred_element_type=F32))

    halo_spec = pl.BlockSpec((1, 1, BLOCK, PAIR_W), lambda p, c: (p, c, 0, 0))
    big = jax.ShapeDtypeStruct((S, N_PAIRS * PAIR_W), F32)
    halo = jax.ShapeDtypeStruct((N_PAIRS, N_CHUNKS, BLOCK, PAIR_W), F32)
    return _pcall(
        body, name="banded_bwd", grid=(N_PAIRS, N_CHUNKS),
        out_shape=(big, big, big, halo, halo, jax.ShapeDtypeStruct(bias_t.shape, F32),
                   jax.ShapeDtypeStruct(sinks.shape, F32)),
        in_specs=qkv + [bias_spec, sink_spec, out_spec, row_spec],
        out_specs=(out_spec, out_spec, out_spec, halo_spec, halo_spec, bias_spec, sink_spec),
        compiler_params=_cparams("arbitrary", "arbitrary"),
    )(proj, proj, proj, proj, proj, bias_t, sinks, do, dlse)


def _halo_fold(t, halo, name):
    S, width = t.shape
    nb = S // N_CHUNKS // BLOCK

    def body(t_ref, h_ref, o_ref):
        o_ref[:N_CHUNKS - 1, 0] = t_ref[:N_CHUNKS - 1, 0] + h_ref[0, 1:]
        o_ref[N_CHUNKS - 1:, 0] = t_ref[N_CHUNKS - 1:, 0]

    blk = pl.BlockSpec((N_CHUNKS, 1, BLOCK, PAIR_W), lambda p: (0, nb - 1, 0, p))
    return _pcall(
        body, name=name, grid=(N_PAIRS,),
        out_shape=jax.ShapeDtypeStruct((N_CHUNKS, nb, BLOCK, width), t.dtype),
        in_specs=[blk, pl.BlockSpec((1, N_CHUNKS, BLOCK, PAIR_W), lambda p: (p, 0, 0, 0))],
        out_specs=blk, input_output_aliases={0: 0},
        compiler_params=_cparams("parallel"),
    )(t.reshape(N_CHUNKS, nb, BLOCK, width), halo).reshape(S, width)


def _causal_mask(T):
    return lax.broadcasted_iota(jnp.int32, (T, T), 0) <= lax.broadcasted_iota(jnp.int32, (T, T), 1)


LOG2E = math.log2(math.e)
FLASH_SPLIT = 2
FLASH_ONES_ROWS = 16


def _mla_q_proj(cqn, wq_t, cos_t, sin_t, *, name):
    S, R = cqn.shape
    H, dqk, _ = wq_t.shape
    nq, half, T = cos_t.shape

    def body(x_ref, w_ref, c_ref, s_ref, o_ref):
        x = x_ref[...]
        for h in range(H):
            qt = lax.dot_general(w_ref[h], x, _NT, preferred_element_type=F32)
            t1, t2 = qt[C_NOPE:C_NOPE + half], qt[C_NOPE + half:]
            o_ref[h, 0, :C_NOPE] = qt[:C_NOPE].astype(MXU_DT)
            o_ref[h, 0, C_NOPE:C_NOPE + half] = (t1 * c_ref[0] - t2 * s_ref[0]).astype(MXU_DT)
            o_ref[h, 0, C_NOPE + half:] = (t1 * s_ref[0] + t2 * c_ref[0]).astype(MXU_DT)

    tab = pl.BlockSpec((1, half, T), lambda i: (i, 0, 0))
    return _pcall(
        body, name=name, grid=(nq,),
        out_shape=jax.ShapeDtypeStruct((H, nq, dqk, T), MXU_DT),
        in_specs=[pl.BlockSpec((T, R), lambda i: (i, 0)), pl.BlockSpec((H, dqk, R), lambda i: (0, 0, 0)), tab, tab],
        out_specs=pl.BlockSpec((H, 1, dqk, T), lambda i: (0, i, 0, 0)),
        compiler_params=_cparams("parallel"),
    )(cqn, wq_t, cos_t, sin_t)


def _mla_q_proj_bwd(dqt, cqn, wq_t, cos_t, sin_t, *, name):
    S, R = cqn.shape
    H, dqk, _ = wq_t.shape
    nq, half, T = cos_t.shape

    def body(g_ref, x_ref, w_ref, c_ref, s_ref, dx_ref, dw_ref):
        @pl.when(pl.program_id(0) == 0)
        def _():
            dw_ref[...] = jnp.zeros_like(dw_ref)

        x = x_ref[...]
        acc = jnp.zeros((T, R), F32)
        for h in range(H):
            g = g_ref[h, 0]
            g1, g2 = g[C_NOPE:C_NOPE + half], g[C_NOPE + half:]
            gq = jnp.concatenate([g[:C_NOPE], g1 * c_ref[0] + g2 * s_ref[0], g2 * c_ref[0] - g1 * s_ref[0]],
                                 axis=0).astype(MXU_DT)
            acc = acc + lax.dot_general(gq, w_ref[h], _TN, preferred_element_type=F32)
            dw_ref[h] += lax.dot_general(gq, x, _NN, preferred_element_type=F32)
        dx_ref[...] = acc

    tab = pl.BlockSpec((1, half, T), lambda i: (i, 0, 0))
    whole = pl.BlockSpec((H, dqk, R), lambda i: (0, 0, 0))
    return _pcall(
        body, name=name, grid=(nq,),
        out_shape=(jax.ShapeDtypeStruct((S, R), F32), jax.ShapeDtypeStruct((H, dqk, R), F32)),
        in_specs=[pl.BlockSpec((H, 1, dqk, T), lambda i: (0, i, 0, 0)), pl.BlockSpec((T, R), lambda i: (i, 0)), whole, tab, tab],
        out_specs=(pl.BlockSpec((T, R), lambda i: (i, 0)), whole),
        compiler_params=_cparams("arbitrary"),
    )(dqt, cqn, wq_t, cos_t, sin_t)


def _mla_kv_proj(ckvn, wk, wv, kr, kr_t, *, name):
    S, R = ckvn.shape
    H = wk.shape[0]
    nq, dr, T = kr_t.shape
    dqk = C_NOPE + dr
    wk_t, wv_t = wk.transpose(0, 2, 1), wv.transpose(0, 2, 1)

    def body(x_ref, wk_ref, wv_ref, wkt_ref, wvt_ref, kr_ref, krt_ref, kt_ref, vt_ref, kn_ref, vn_ref):
        x = x_ref[...]
        krt, krn = krt_ref[0].astype(MXU_DT), kr_ref[...].astype(MXU_DT)
        ones = jnp.where(lax.broadcasted_iota(jnp.int32, (FLASH_ONES_ROWS, T), 0) == 0, 1.0, 0.0).astype(MXU_DT)
        for h in range(H):
            kt_ref[h, 0, :C_NOPE] = lax.dot_general(wkt_ref[h], x, _NT, preferred_element_type=F32).astype(MXU_DT)
            kt_ref[h, 0, C_NOPE:] = krt
            vt_ref[h, 0, :C_V] = lax.dot_general(wvt_ref[h], x, _NT, preferred_element_type=F32).astype(MXU_DT)
            vt_ref[h, 0, C_V:] = ones
            kn_ref[h, :, :C_NOPE] = lax.dot_general(x, wk_ref[h], _NN, preferred_element_type=F32).astype(MXU_DT)
            kn_ref[h, :, C_NOPE:] = krn
            vn_ref[h] = lax.dot_general(x, wv_ref[h], _NN, preferred_element_type=F32).astype(MXU_DT)

    w_spec = pl.BlockSpec((H, R, C_NOPE), lambda i: (0, 0, 0))
    wt_spec = pl.BlockSpec((H, C_NOPE, R), lambda i: (0, 0, 0))
    return _pcall(
        body, name=name, grid=(nq,),
        out_shape=(jax.ShapeDtypeStruct((H, nq, dqk, T), MXU_DT),
                   jax.ShapeDtypeStruct((H, nq, C_V + FLASH_ONES_ROWS, T), MXU_DT),
                   jax.ShapeDtypeStruct((H, S, dqk), MXU_DT), jax.ShapeDtypeStruct((H, S, C_V), MXU_DT)),
        in_specs=[pl.BlockSpec((T, R), lambda i: (i, 0)), w_spec, w_spec, wt_spec, wt_spec,
                  pl.BlockSpec((T, dr), lambda i: (i, 0)), pl.BlockSpec((1, dr, T), lambda i: (i, 0, 0))],
        out_specs=(pl.BlockSpec((H, 1, dqk, T), lambda i: (0, i, 0, 0)),
                   pl.BlockSpec((H, 1, C_V + FLASH_ONES_ROWS, T), lambda i: (0, i, 0, 0)),
                   pl.BlockSpec((H, T, dqk), lambda i: (0, i, 0)), pl.BlockSpec((H, T, C_V), lambda i: (0, i, 0))),
        compiler_params=_cparams("parallel"),
    )(ckvn, wk, wv, wk_t, wv_t, kr, kr_t)


def _mla_kv_proj_bwd(dkt, dvt, ckvn, wk, wv, *, name):
    S, R = ckvn.shape
    H = wk.shape[0]
    _, nq, dqk, T = dkt.shape
    dr = dqk - C_NOPE
    wk_t, wv_t = wk.transpose(0, 2, 1), wv.transpose(0, 2, 1)

    def body(gk_ref, gv_ref, x_ref, wkt_ref, wvt_ref, dx_ref, dwk_ref, dwv_ref, dkr_ref):
        @pl.when(pl.program_id(0) == 0)
        def _():
            dwk_ref[...] = jnp.zeros_like(dwk_ref)
            dwv_ref[...] = jnp.zeros_like(dwv_ref)

        x = x_ref[...]
        acc = jnp.zeros((T, R), F32)
        dkr = jnp.zeros((dr, T), F32)
        for h in range(H):
            gk = gk_ref[h, 0, :C_NOPE].astype(MXU_DT)
            gv = gv_ref[h, 0].astype(MXU_DT)
            acc = acc + (lax.dot_general(gk, wkt_ref[h], _TN, preferred_element_type=F32)
                         + lax.dot_general(gv, wvt_ref[h], _TN, preferred_element_type=F32))
            dwk_ref[h] += lax.dot_general(gk, x, _NN, preferred_element_type=F32)
            dwv_ref[h] += lax.dot_general(gv, x, _NN, preferred_element_type=F32)
            dkr = dkr + gk_ref[h, 0, C_NOPE:]
        dx_ref[...] = acc
        dkr_ref[0] = dkr

    wt_spec = pl.BlockSpec((H, C_NOPE, R), lambda i: (0, 0, 0))
    return _pcall(
        body, name=name, grid=(nq,),
        out_shape=(jax.ShapeDtypeStruct((S, R), F32), jax.ShapeDtypeStruct((H, C_NOPE, R), F32),
                   jax.ShapeDtypeStruct((H, C_V, R), F32), jax.ShapeDtypeStruct((nq, dr, T), F32)),
        in_specs=[pl.BlockSpec((H, 1, dqk, T), lambda i: (0, i, 0, 0)), pl.BlockSpec((H, 1, C_V, T), lambda i: (0, i, 0, 0)),
                  pl.BlockSpec((T, R), lambda i: (i, 0)), wt_spec, wt_spec],
        out_specs=(pl.BlockSpec((T, R), lambda i: (i, 0)), wt_spec, wt_spec, pl.BlockSpec((1, dr, T), lambda i: (i, 0, 0))),
        compiler_params=_cparams("arbitrary"),
    )(dkt, dvt, ckvn, wk_t, wv_t)


def _mla_out_proj(ot, w_o, h, *, name):
    H, nq, dv, T = ot.shape
    D = w_o.shape[2]

    def body(o_ref, w_ref, h_ref, out_ref):
        acc = h_ref[...]
        for hd in range(H):
            acc = acc + lax.dot_general(o_ref[hd, 0].astype(MXU_DT), w_ref[hd], _TN, preferred_element_type=F32)
        out_ref[...] = acc

    row = pl.BlockSpec((T, D), lambda i: (i, 0))
    return _pcall(
        body, name=name, grid=(nq,),
        out_shape=jax.ShapeDtypeStruct(h.shape, F32),
        in_specs=[pl.BlockSpec((H, 1, dv, T), lambda i: (0, i, 0, 0)), pl.BlockSpec((H, dv, D), lambda i: (0, 0, 0)), row],
        out_specs=row,
        compiler_params=_cparams("parallel"),
    )(ot, w_o, h)


def _mla_out_proj_bwd(dh, ot, w_o, *, name):
    H, nq, dv, T = ot.shape
    D = w_o.shape[2]

    def body(dh_ref, o_ref, w_ref, dot_ref, del_ref, dw_ref):
        @pl.when(pl.program_id(0) == 0)
        def _():
            dw_ref[...] = jnp.zeros_like(dw_ref)

        dhb = dh_ref[...].astype(MXU_DT)
        for hd in range(H):
            o = o_ref[hd, 0]
            d = lax.dot_general(w_ref[hd], dhb, _NT, preferred_element_type=F32)
            dot_ref[hd, 0] = d
            del_ref[hd, 0] = jnp.sum(d * o, axis=0, keepdims=True)
            dw_ref[hd] += lax.dot_general(o.astype(MXU_DT), dhb, _NN, preferred_element_type=F32)

    tile = pl.BlockSpec((H, 1, dv, T), lambda i: (0, i, 0, 0))
    whole = pl.BlockSpec((H, dv, D), lambda i: (0, 0, 0))
    return _pcall(
        body, name=name, grid=(nq,),
        out_shape=(jax.ShapeDtypeStruct(ot.shape, F32), jax.ShapeDtypeStruct((H, nq, 1, T), F32),
                   jax.ShapeDtypeStruct(w_o.shape, F32)),
        in_specs=[pl.BlockSpec((T, D), lambda i: (i, 0)), tile, whole],
        out_specs=(tile, pl.BlockSpec((H, 1, 1, T), lambda i: (0, i, 0, 0)), whole),
        compiler_params=_cparams("arbitrary"),
    )(dh, ot, w_o)


def _flash_fwd(qt, k, vt1):
    H, nq, dqk, T = qt.shape
    S = k.shape[1]
    dva = vt1.shape[2]
    dv = dva - FLASH_ONES_ROWS
    scale = dqk ** -0.5
    c = scale * LOG2E
    th = T // FLASH_SPLIT

    def body(qt_ref, k_ref, vt_ref, ot_ref, lse_ref, sa_ref, sb_ref):
        i = pl.program_id(1)

        def scores(j):
            kb = k_ref[0, pl.ds(pl.multiple_of(j * T, T), T), :]
            return lax.dot_general(kb, qt_ref[0, 0], _NN, preferred_element_type=F32)

        def softmax_pv(s_ref, j, carry, masked):
            m, acc = carry
            raw = s_ref[...]
            if masked:
                raw = jnp.where(_causal_mask(T), raw, NEG)
            m_new = jnp.maximum(m, jnp.max(raw, axis=0, keepdims=True))
            alpha = jnp.exp2((m - m_new) * c)
            pb = jnp.exp2((raw - m_new) * c).astype(MXU_DT)
            acc = acc * alpha + lax.dot_general(vt_ref[0, j], pb, _NN, preferred_element_type=F32)
            return m_new, acc

        def pair(p, carry):
            j = 2 * p
            sb_ref[...] = scores(j + 1)
            carry = softmax_pv(sa_ref, j, carry, False)
            sa_ref[...] = scores(j + 2)
            return softmax_pv(sb_ref, j + 1, carry, False)

        def even_tail(carry):
            return softmax_pv(sa_ref, i, carry, True)

        def odd_tail(carry):
            sb_ref[...] = scores(i)
            carry = softmax_pv(sa_ref, i - 1, carry, False)
            return softmax_pv(sb_ref, i, carry, True)

        sa_ref[...] = scores(0)
        carry = lax.fori_loop(0, i // 2, pair, (jnp.full((1, T), NEG, F32), jnp.zeros((dva, T), F32)))
        m, acc = lax.cond(i % 2 == 0, even_tail, odd_tail, carry)
        l = acc[dv:dv + 1]
        ot_ref[0, 0] = acc[:dv] / l
        lse_ref[0, 0] = m * scale + jnp.log(l)

    return _pcall(
        body, name="flash_fwd", grid=(H, nq),
        out_shape=(jax.ShapeDtypeStruct((H, nq, dv, T), F32), jax.ShapeDtypeStruct((H, nq, 1, T), F32)),
        in_specs=[pl.BlockSpec((1, 1, dqk, T), lambda h, i: (h, i, 0, 0)),
                  pl.BlockSpec((1, S, dqk), lambda h, i: (h, 0, 0)),
                  pl.BlockSpec((1, nq, dva, T), lambda h, i: (h, 0, 0, 0))],
        out_specs=(pl.BlockSpec((1, 1, dv, T), lambda h, i: (h, i, 0, 0)),
                   pl.BlockSpec((1, 1, 1, T), lambda h, i: (h, i, 0, 0))),
        scratch_shapes=[pltpu.VMEM((T, T), F32), pltpu.VMEM((T, T), F32)],
        compiler_params=_cparams("parallel", "parallel"),
    )(qt, k, vt1)


def _flash_delta(ot, dot):
    H, nq, dv, T = ot.shape

    def body(o_ref, do_ref, d_ref):
        d_ref[0, 0] = jnp.sum(o_ref[0, 0] * do_ref[0, 0], axis=0, keepdims=True)

    spec = pl.BlockSpec((1, 1, dv, T), lambda h, i: (h, i, 0, 0))
    return _pcall(
        body, name="flash_delta", grid=(H, nq),
        out_shape=jax.ShapeDtypeStruct((H, nq, 1, T), F32),
        in_specs=[spec, spec], out_specs=pl.BlockSpec((1, 1, 1, T), lambda h, i: (h, i, 0, 0)),
        compiler_params=_cparams("parallel", "parallel"),
    )(ot, dot)


def _flash_bwd(qt, k, kt, v, dot, lse, delta):
    H, nq, dqk, T = qt.shape
    dv_ = v.shape[2]
    scale = dqk ** -0.5
    c = scale * LOG2E
    th = T // FLASH_SPLIT

    def body(qt_ref, k_ref, kt_ref, v_ref, dot_ref, lse_ref, del_ref, dqt_ref, dkt_ref, dvt_ref,
             sa_ref, pa_ref, sb_ref, pb_ref):
        j = pl.program_id(1)

        @pl.when(j == 0)
        def _():
            dqt_ref[...] = jnp.zeros_like(dqt_ref)

        n_un = nq - 1 - j

        def issue(i, s_ref, dp_ref):
            s_ref[...] = lax.dot_general(k_ref[0], qt_ref[0, i], _NN, preferred_element_type=F32)
            dp_ref[...] = lax.dot_general(v_ref[0], dot_ref[0, i].astype(MXU_DT), _NN, preferred_element_type=F32)

        def consume(i, s_ref, dp_ref, carry, masked):
            dkt, dvt = carry
            raw = s_ref[...]
            if masked:
                raw = jnp.where(_causal_mask(T), raw, NEG)
            p = jnp.exp2(raw * c - lse_ref[0, i] * LOG2E)
            dsb = (p * (dp_ref[...] - del_ref[0, i])).astype(MXU_DT)
            dvt = dvt + lax.dot_general(dot_ref[0, i].astype(MXU_DT), p.astype(MXU_DT), _NT, preferred_element_type=F32)
            dkt = dkt + lax.dot_general(qt_ref[0, i], dsb, _NT, preferred_element_type=F32)
            dqt_ref[0, i] += lax.dot_general(kt_ref[0, 0], dsb, _NN, preferred_element_type=F32) * scale
            return dkt, dvt

        def pair(p, carry):
            i0 = j + 1 + 2 * p
            issue(i0 + 1, sb_ref, pb_ref)
            carry = consume(i0, sa_ref, pa_ref, carry, False)
            issue(jnp.where(2 * p + 2 < n_un, i0 + 2, j), sa_ref, pa_ref)
            return consume(i0 + 1, sb_ref, pb_ref, carry, False)

        def even_tail(carry):
            return consume(j, sa_ref, pa_ref, carry, True)

        def odd_tail(carry):
            issue(j, sb_ref, pb_ref)
            carry = consume(nq - 1, sa_ref, pa_ref, carry, False)
            return consume(j, sb_ref, pb_ref, carry, True)

        issue(jnp.where(n_un > 0, j + 1, j), sa_ref, pa_ref)
        carry = lax.fori_loop(0, n_un // 2, pair, (jnp.zeros((dqk, T), F32), jnp.zeros((dv_, T), F32)))
        dkt, dvt = lax.cond(n_un % 2 == 0, even_tail, odd_tail, carry)
        dkt_ref[0, 0] = dkt * scale
        dvt_ref[0, 0] = dvt

    whole = lambda d: pl.BlockSpec((1, nq, d, T), lambda h, j: (h, 0, 0, 0))
    tile_t = lambda d: pl.BlockSpec((1, 1, d, T), lambda h, j: (h, j, 0, 0))
    return _pcall(
        body, name="flash_bwd", grid=(H, nq),
        out_shape=(jax.ShapeDtypeStruct((H, nq, dqk, T), F32), jax.ShapeDtypeStruct((H, nq, dqk, T), F32),
                   jax.ShapeDtypeStruct((H, nq, dv_, T), F32)),
        in_specs=[whole(dqk),
                  pl.BlockSpec((1, T, dqk), lambda h, j: (h, j, 0)),
                  tile_t(dqk),
                  pl.BlockSpec((1, T, dv_), lambda h, j: (h, j, 0)),
                  whole(dv_), whole(1), whole(1)],
        out_specs=(whole(dqk), tile_t(dqk), tile_t(dv_)),
        scratch_shapes=[pltpu.VMEM((T, T), F32) for _ in range(4)],
        compiler_params=_cparams("arbitrary", "arbitrary"),
    )(qt, k, kt, v, dot, lse, delta)


def _adamw(parts, w, m, v, *, name, tr=512):
    P, R, C = parts.shape
    tr = min(tr, R)
    assert R % tr == 0

    def body(p_ref, w_ref, m_ref, v_ref, g_ref, d_ref, m2_ref, v2_ref):
        g = p_ref[0].astype(F32)
        for s in range(1, P):
            g = g + p_ref[s].astype(F32)
        m2 = ADAM_B1 * m_ref[...] + (1.0 - ADAM_B1) * g
        v2 = ADAM_B2 * v_ref[...] + (1.0 - ADAM_B2) * jnp.square(g)
        m_hat = m2 / (1.0 - ADAM_B1 ** ADAM_STEP)
        v_hat = v2 / (1.0 - ADAM_B2 ** ADAM_STEP)
        g_ref[...] = g
        d_ref[...] = -ADAM_LR * (m_hat / (jnp.sqrt(v_hat) + ADAM_EPS) + ADAM_WD * w_ref[...])
        m2_ref[...] = m2
        v2_ref[...] = v2

    row = pl.BlockSpec((tr, C), lambda i: (i, 0))
    out = jax.ShapeDtypeStruct((R, C), F32)
    return _pcall(
        body, name=name, grid=(R // tr,),
        out_shape=(out, out, out, out),
        in_specs=[pl.BlockSpec((P, tr, C), lambda i: (0, i, 0)), row, row, row],
        out_specs=(row, row, row, row),
        compiler_params=_cparams("parallel"),
    )(parts, w, m, v)


def _bias_tables():
    i = np.arange(BLOCK)[:, None]
    j = np.arange(2 * BLOCK)[None, :]
    dist = i + BLOCK - j
    out = []
    for dil, max_dist in [(1, A_WINDOW - 1)] + [(d, w // d) for w, d in B_BRANCHES]:
        n = np.maximum(dist, 0) * dil
        max_exact = NUM_BUCKETS // 2
        nf = np.maximum(n, 1).astype(np.float64)
        val = np.log(nf / max_exact) / math.log(MAX_DISTANCE / max_exact) * (NUM_BUCKETS - max_exact)
        inband = (dist >= 0) & (dist <= max_dist)
        frac = np.abs(val - np.round(val))
        last = NUM_BUCKETS - 1 - max_exact
        assert np.all((frac > 2e-5) | (n <= max_exact) | (val >= last) | ~inband)
        large = max_exact + val.astype(np.int64)
        bucket = np.where(n < max_exact, n, np.minimum(large, NUM_BUCKETS - 1))
        onehot = (bucket[..., None] == np.arange(NUM_BUCKETS)).astype(np.float32)
        out.append((onehot.reshape(-1, NUM_BUCKETS), inband))
    return out


def _make_bias(rel_bias):
    tabs = _bias_tables()
    groups = [(0, A_Q_HEADS)] + [(A_Q_HEADS + g * B_HPB, B_HPB) for g in range(len(B_BRANCHES))]
    parts = []
    for (onehot, inband), (h0, nh) in zip(tabs, groups):
        b = jnp.dot(jnp.asarray(onehot), rel_bias[:, h0:h0 + nh], precision=lax.Precision.HIGHEST)
        b = b.reshape(BLOCK, 2 * BLOCK, nh)
        b = jnp.where(jnp.asarray(inband)[..., None], b, NEG)
        parts.append(b.transpose(2, 0, 1))
    return jnp.concatenate(parts, axis=0)


A_W = A_Q_HEADS * HEAD_DIM
B_W = B_HPB * HEAD_DIM


def _perm_rows(x, d):
    return x if d == 1 else x.reshape(x.shape[0] // d, d, -1).transpose(1, 0, 2).reshape(x.shape)


def _unperm_rows(x, d):
    return x if d == 1 else x.reshape(d, x.shape[0] // d, -1).transpose(1, 0, 2).reshape(x.shape)


def _even_post(o_all, lse):
    S = o_all.shape[0]
    outs, lses = [], []
    for g, (_, d) in enumerate(B_BRANCHES):
        w0 = A_W + g * B_W
        outs.append(_unperm_rows(o_all[:, w0:w0 + B_W], d))
        lg = lse[A_PAIRS + 2 * g:A_PAIRS + 2 * g + 2].transpose(1, 2, 4, 0, 3).reshape(S, B_HPB)
        lses.append(_unperm_rows(lg, d))
    wts = jax.nn.softmax(jnp.stack(lses), axis=0)
    widen = jnp.asarray(np.kron(np.eye(B_HPB), np.ones((1, HEAD_DIM))), F32)
    out_b = sum(jnp.dot(wts[g], widen, precision=lax.Precision.HIGHEST) * outs[g] for g in range(len(B_BRANCHES)))
    return jnp.concatenate([o_all[:, :A_W], out_b], axis=-1)


def _rope_tables(S, r):
    inv = ROPE_THETA ** (-jnp.arange(0, r, 2, dtype=jnp.float32) / r)
    ang = jnp.arange(S, dtype=jnp.float32)[:, None] * inv[None, :]
    return jnp.cos(ang), jnp.sin(ang)


def _rope(t):
    S, r = t.shape[1], t.shape[-1]
    shape = (1, S) + (1,) * (t.ndim - 3) + (r // 2,)
    cos, sin = (a.reshape(shape) for a in _rope_tables(S, r))
    t1, t2 = t[..., :r // 2], t[..., r // 2:]
    return jnp.concatenate([t1 * cos - t2 * sin, t1 * sin + t2 * cos], axis=-1)


def _mla_pre(q_lin, kv_lin, kr_raw):
    S = q_lin.shape[0]
    q = q_lin.reshape(1, S, C_HEADS, C_QK)
    qf = jnp.concatenate([q[..., :C_NOPE], _rope(q[..., C_NOPE:])], axis=-1)[0]
    kv = kv_lin.reshape(S, C_HEADS, C_NOPE + C_V)
    kr = _rope(kr_raw[None])[0]
    kf = jnp.concatenate([kv[..., :C_NOPE], jnp.broadcast_to(kr[:, None, :], (S, C_HEADS, C_ROPE))], axis=-1)
    return qf, kf, kv[..., C_NOPE:]


def _to_tiles_t(t, T):
    S, H, d = t.shape
    return t.reshape(S // T, T, H, d).transpose(2, 0, 3, 1)


def _from_tiles_t(t):
    H, n, d, T = t.shape
    return t.transpose(1, 3, 0, 2).reshape(n * T, H, d)


_BIG = (("w_in_ab", 2), ("w_out_ab", 2), ("w_down_c", 1), ("w_uq_c", 2), ("w_ukv_c", 2), ("w_o_c", 2),
        ("w_mlp_up", 2), ("w_mlp_down", 1))
_ROW_ALIGN = 512


def _pack_rows(arrs):
    rows = [a.reshape(-1, 128) for a in arrs]
    n = sum(r.shape[0] for r in rows)
    pad = (-n) % _ROW_ALIGN
    if pad:
        rows.append(jnp.zeros((pad, 128), rows[0].dtype))
    return jnp.concatenate(rows, axis=0)


def _pack_rows_per_device(arrs):
    rows = [a.reshape(N_DEV, -1, 128) for a in arrs]
    n = sum(r.shape[1] for r in rows)
    pad = (-n) % _ROW_ALIGN
    if pad:
        rows.append(jnp.zeros((N_DEV, pad, 128), rows[0].dtype))
    return jnp.concatenate(rows, axis=1)


def _unpack_rows(buf, shapes):
    out, r0 = [], 0
    for shp in shapes:
        n = math.prod(shp) // 128
        out.append(buf[..., r0:r0 + n, :].reshape(buf.shape[:-2] + tuple(shp)))
        r0 += n
    return out


def _layer_tensors(l):
    att = [("w_in_ab", l // 2), ("w_out_ab", l // 2)] if l % 2 == 0 else \
          [("w_down_c", l // 2), ("w_uq_c", l // 2), ("w_ukv_c", l // 2), ("w_o_c", l // 2)]
    return att + [("w_mlp_up", l), ("w_mlp_down", l)]


def _exchange_groups():
    first = _layer_tensors(0)
    return [first[:-2], first[-2:]] + [_layer_tensors(l) for l in range(1, DEPTH)]


def _gathered_to_full(g, axis):
    if axis == 2:
        return g.transpose(1, 0, 2).reshape(g.shape[1], N_DEV * g.shape[2])
    return g.reshape(N_DEV * g.shape[1], g.shape[2])


def _full_to_shards(t, axis):
    a, b = t.shape
    if axis == 2:
        return t.reshape(a, N_DEV, b // N_DEV).transpose(1, 0, 2)
    return t.reshape(N_DEV, a // N_DEV, b)


def _pad_rows8(a):
    flat = a.reshape(-1)
    n = -(-flat.shape[0] // 1024) * 1024
    return jnp.pad(flat, (0, n - flat.shape[0])).reshape(-1, 128)


def _even_fwd(xn, h, w_in, w_out, bias, sinks_row, l, ride=None):
    c0 = A_IN + 3 * B_W
    proj = lax.empty((xn.shape[0], w_in.shape[1]), MXU_DT)
    proj = _matmul(xn, w_in[:, :c0], out_dtype=MXU_DT, tm=1024, tn=512, name=f"even_in_{l}", ride=ride, into=(proj, 0))
    proj, landed = proj if ride is not None else (proj, None)
    for g, (_, d) in list(enumerate(B_BRANCHES))[1:]:
        col0 = A_IN + 3 * B_W * g
        proj = _matmul(_perm_rows(xn, d), w_in[:, col0:col0 + 3 * B_W], out_dtype=MXU_DT, tm=1024, tn=3 * B_W,
                       name=f"even_in_dil{d}_{l}", into=(proj, col0))
    o_all, lse = _banded_fwd(proj, bias, sinks_row)
    attn, post_vjp = jax.vjp(_even_post, o_all, lse)
    attn = attn.astype(MXU_DT)
    h1 = _matmul(attn, w_out, epi='add', extra=h, tm=1024, tn=512, name=f"even_out_{l}")
    return h1, (proj, attn, post_vjp), landed


def _even_bwd(dh, xn, ctx, w_in, w_out, bias, sinks_row, l, norm, ride=None):
    proj, attn, post_vjp = ctx
    d_attn = _matmul(dh, w_out, trans_b=True, tm=1024, tn=768, name=f"even_out_dx_{l}")
    g_w_out = _matmul_tn(attn, dh, tk=768, tn=512, name=f"even_out_dw_{l}")
    do_all, dlse = post_vjp(d_attn)
    dq, dk, dv, dkh, dvh, dbias_t, dsinks = _banded_bwd(proj, bias.transpose(0, 2, 1), sinks_row, do_all, dlse)
    dbias = dbias_t.transpose(0, 2, 1)
    dk = _halo_fold(dk, dkh, f"halo_k_{l}")
    dv = _halo_fold(dv, dvh, f"halo_v_{l}")

    def kv_sum(t):
        heads = [t[:, i * HEAD_DIM:(i + 1) * HEAD_DIM] for i in range(A_Q_HEADS)]
        return jnp.concatenate([sum(heads[j * A_GROUP:(j + 1) * A_GROUP]) for j in range(A_KV_HEADS)], axis=1)

    groups = [jnp.concatenate([dq[:, :A_W], kv_sum(dk), kv_sum(dv)], axis=1).astype(MXU_DT)]
    for g, (_, d) in enumerate(B_BRANCHES):
        cols = slice(A_W + g * B_W, A_W + (g + 1) * B_W)
        grp = jnp.concatenate([dq[:, cols], dk[:, cols], dv[:, cols]], axis=1).astype(MXU_DT)
        groups.append(_unperm_rows(grp, d))
    dproj = jnp.concatenate(groups, axis=1)
    g_w_in = _matmul_tn(xn, dproj, tk=1024, tn=1024, name=f"even_in_dw_{l}")
    res = _matmul(dproj, w_in, trans_b=True, epi='norm_bwd', norm=norm, tm=512, tn=w_in.shape[0],
                  name=f"even_in_dx_{l}", ride=ride)
    dh_new, g_norm, landed = res if ride is not None else (*res, None)
    return dh_new, g_norm[0], g_w_in, g_w_out, dbias, dsinks[:A_Q_HEADS, 0, 0], landed


def _mla_fwd(xn, h, w_down, q_norm, w_uq, kv_norm, w_ukv, w_o, l):
    S = xn.shape[0]
    T = min(FLASH_T, S)
    down = _matmul(xn, w_down, tm=1024, tn=768, name=f"mla_down_{l}")
    c_q, c_kv, kr_raw = down[:, :C_Q_RANK], down[:, C_Q_RANK:C_Q_RANK + C_KV_RANK], down[:, C_Q_RANK + C_KV_RANK:C_DOWN]
    cqn = _rmsnorm(c_q, q_norm, out_dtype=MXU_DT, name=f"mla_qnorm_{l}")
    ckvn = _rmsnorm(c_kv, kv_norm, out_dtype=MXU_DT, name=f"mla_kvnorm_{l}")
    cos, sin = _rope_tables(S, C_ROPE)
    to_t = lambda t: t.reshape(S // T, T, -1).transpose(0, 2, 1)
    qt = _mla_q_proj(cqn, w_uq.T.reshape(C_HEADS, C_QK, C_Q_RANK), to_t(cos), to_t(sin), name=f"mla_uq_{l}")
    w_kv = w_ukv.reshape(C_KV_RANK, C_HEADS, C_NOPE + C_V).transpose(1, 0, 2)
    kr = _rope(kr_raw[None])[0]
    kt, vt1, kn, vn = _mla_kv_proj(ckvn, w_kv[..., :C_NOPE], w_kv[..., C_NOPE:], kr, to_t(kr), name=f"mla_ukv_{l}")
    ot, lse = _flash_fwd(qt, kn, vt1)
    h1 = _mla_out_proj(ot, w_o.reshape(C_HEADS, C_V, -1), h, name=f"mla_o_{l}")
    return h1, (c_q, c_kv, cqn, ckvn, qt, kn, kt, vn, ot, lse)


def _mla_bwd(dh, xn, ctx, w_down, q_norm, w_uq, kv_norm, w_ukv, w_o, l, norm):
    c_q, c_kv, cqn, ckvn, qt, kn, kt, vn, ot, lse = ctx
    S = xn.shape[0]
    T = qt.shape[-1]
    dot, delta, dw_o = _mla_out_proj_bwd(dh, ot, w_o.reshape(C_HEADS, C_V, -1), name=f"mla_o_bwd_{l}")
    g_w_o = dw_o.reshape(w_o.shape)
    dqt, dkt, dvt = _flash_bwd(qt, kn, kt, vn, dot, lse, delta)
    cos, sin = _rope_tables(S, C_ROPE)
    to_t = lambda t: t.reshape(S // T, T, -1).transpose(0, 2, 1)
    dcqn, dwq_t = _mla_q_proj_bwd(dqt, cqn, w_uq.T.reshape(C_HEADS, C_QK, C_Q_RANK), to_t(cos), to_t(sin),
                                  name=f"mla_uq_bwd_{l}")
    g_w_uq = dwq_t.reshape(C_HEADS * C_QK, C_Q_RANK).T
    w_kv = w_ukv.reshape(C_KV_RANK, C_HEADS, C_NOPE + C_V).transpose(1, 0, 2)
    dckvn, dwk_t, dwv_t, dkr_t = _mla_kv_proj_bwd(dkt, dvt, ckvn, w_kv[..., :C_NOPE], w_kv[..., C_NOPE:],
                                                  name=f"mla_ukv_bwd_{l}")
    g_w_ukv = jnp.concatenate([dwk_t, dwv_t], axis=1).reshape(C_HEADS * (C_NOPE + C_V), C_KV_RANK).T
    _, rope_vjp = jax.vjp(lambda t: _rope(t[None])[0], jnp.zeros((S, C_ROPE), F32))
    (dkr_raw,) = rope_vjp(dkr_t.transpose(0, 2, 1).reshape(S, C_ROPE))
    dc_q, g_q_norm = _rmsnorm_bwd(c_q, q_norm, dcqn, None, name=f"mla_qnorm_bwd_{l}")
    dc_kv, g_kv_norm = _rmsnorm_bwd(c_kv, kv_norm, dckvn, None, name=f"mla_kvnorm_bwd_{l}")
    ddown = jnp.concatenate([dc_q, dc_kv, dkr_raw, jnp.zeros((S, C_DOWN_PAD - C_DOWN), F32)], axis=1).astype(MXU_DT)
    g_w_down = _matmul_tn(xn, ddown, tk=512, tn=768, name=f"mla_down_dw_{l}")[:, :C_DOWN]
    dh_new, g_norm = _matmul(ddown, w_down, trans_b=True, epi='norm_bwd', norm=norm, tm=512, tn=w_down.shape[0],
                             name=f"mla_down_dx_{l}")
    return dh_new, g_norm[0], g_w_down, g_q_norm[0], g_w_uq, g_kv_norm[0], g_w_ukv, g_w_o


def kernel(x, rel_bias, attn_norm, mlp_norm, final_norm, w_in_ab, sinks, w_out_ab, w_down_c, q_norm_c, w_uq_c, kv_norm_c, w_ukv_c, w_o_c, w_mlp_up, w_mlp_down, loss_target, m_rel_bias, m_attn_norm, m_mlp_norm, m_final_norm, m_w_in_ab, m_sinks, m_w_out_ab, m_w_down_c, m_q_norm_c, m_w_uq_c, m_kv_norm_c, m_w_ukv_c, m_w_o_c, m_w_mlp_up, m_w_mlp_down, v_rel_bias, v_attn_norm, v_mlp_norm, v_final_norm, v_w_in_ab, v_sinks, v_w_out_ab, v_w_down_c, v_q_norm_c, v_w_uq_c, v_kv_norm_c, v_w_ukv_c, v_w_o_c, v_w_mlp_up, v_w_mlp_down):
    W = dict(w_in_ab=w_in_ab, w_out_ab=w_out_ab, w_down_c=w_down_c, w_uq_c=w_uq_c, w_ukv_c=w_ukv_c, w_o_c=w_o_c,
             w_mlp_up=w_mlp_up, w_mlp_down=w_mlp_down)
    Mo = dict(w_in_ab=m_w_in_ab, w_out_ab=m_w_out_ab, w_down_c=m_w_down_c, w_uq_c=m_w_uq_c, w_ukv_c=m_w_ukv_c,
              w_o_c=m_w_o_c, w_mlp_up=m_w_mlp_up, w_mlp_down=m_w_mlp_down)
    Vo = dict(w_in_ab=v_w_in_ab, w_out_ab=v_w_out_ab, w_down_c=v_w_down_c, w_uq_c=v_w_uq_c, w_ukv_c=v_w_ukv_c,
              w_o_c=v_w_o_c, w_mlp_up=v_w_mlp_up, w_mlp_down=v_w_mlp_down)
    S = x.shape[1]
    me = 4 * lax.axis_index("x") + 2 * lax.axis_index("y") + lax.axis_index("c")
    axis_of = dict(_BIG)

    groups = _exchange_groups()

    def pack(src, gi):
        return _pack_rows([src[n][i] for n, i in groups[gi]])

    def unpack_group(gathered, gi):
        shapes = [W[n].shape[1:] for n, _ in groups[gi]]
        return {n: _gathered_to_full(g, axis_of[n])
                for (n, _), g in zip(groups[gi], _unpack_rows(gathered, shapes))}

    def send_of(G, gi):
        return _pack_rows_per_device([_full_to_shards(G[n], axis_of[n]) for n, _ in groups[gi]]).astype(MXU_DT)

    w_packs = [pack(W, gi) for gi in range(len(groups))]
    gathered = _exchange(w_packs[0].astype(MXU_DT), False, "gather_weights_0")
    gains = _exchange(jnp.concatenate([_pad_rows8(q_norm_c), _pad_rows8(kv_norm_c)], axis=0), False, "gather_gains")
    n_odd = q_norm_c.shape[0]
    q_norm_full = gains[:, 0].reshape(N_DEV, -1)[:, :q_norm_c.size].reshape(N_DEV, n_odd, -1).transpose(1, 0, 2).reshape(n_odd, C_Q_RANK)
    kv_norm_full = gains[:, 8].reshape(N_DEV, -1)[:, :kv_norm_c.size].reshape(N_DEV, n_odd, -1).transpose(1, 0, 2).reshape(n_odd, C_KV_RANK)

    bias, bias_vjp = jax.vjp(_make_bias, rel_bias)
    sink_rows = [jnp.broadcast_to(jnp.concatenate([sinks[e], jnp.full((B_HEADS,), NEG, F32)])[:, None, None],
                                  (N_BIAS_HEADS, 1, 128)) for e in range(sinks.shape[0])]

    h = x[0]
    saved = []
    for l in range(DEPTH):
        full = unpack_group(gathered, 0 if l == 0 else l + 1)
        if l % 2 == 1:
            full["w_down_c"] = jnp.pad(full["w_down_c"], ((0, 0), (0, C_DOWN_PAD - C_DOWN)))
        xn = _rmsnorm(h, attn_norm[l], out_dtype=MXU_DT, name=f"attn_norm_{l}")
        if l == 0:
            h1, ctx, gathered_mlp = _even_fwd(xn, h, full["w_in_ab"], full["w_out_ab"], bias, sink_rows[0], l,
                                              ride=(w_packs[1].astype(MXU_DT), False))
            full.update(unpack_group(gathered_mlp, 1))
        elif l % 2 == 0:
            h1, ctx, _ = _even_fwd(xn, h, full["w_in_ab"], full["w_out_ab"], bias, sink_rows[l // 2], l)
        else:
            o = l // 2
            h1, ctx = _mla_fwd(xn, h, full["w_down_c"], q_norm_full[o], full["w_uq_c"], kv_norm_full[o],
                               full["w_ukv_c"], full["w_o_c"], l)
        xn2 = _rmsnorm(h1, mlp_norm[l], out_dtype=MXU_DT, name=f"mlp_norm_{l}")
        if l + 1 < DEPTH:
            act, slope, gathered = _matmul(xn2, full["w_mlp_up"], out_dtype=MXU_DT, epi='relu2', tm=2048, tn=512,
                                           name=f"mlp_up_{l}", ride=(w_packs[l + 2].astype(MXU_DT), False))
        else:
            act, slope = _matmul(xn2, full["w_mlp_up"], out_dtype=MXU_DT, epi='relu2', tm=2048, tn=512,
                                 name=f"mlp_up_{l}")
        h2 = _matmul(act, full["w_mlp_down"], epi='add', extra=h1, tm=1024, tn=512, name=f"mlp_down_{l}")
        saved.append((h, xn, h1, xn2, act, slope, ctx, full))
        h = h2

    loss_row, dh, g_final = _loss_head(h, loss_target[0], final_norm)

    g_attn_norm, g_mlp_norm = [None] * DEPTH, [None] * DEPTH
    g_sinks, g_qn, g_kvn = [None] * sinks.shape[0], [None] * n_odd, [None] * n_odd
    dbias_total = None
    landed = [None] * len(groups)
    send = None
    for l in reversed(range(DEPTH)):
        h0, xn, h1, xn2, act, slope, ctx, full = saved[l]
        G = {}
        if send is None:
            du = _matmul(dh, full["w_mlp_down"], trans_b=True, out_dtype=MXU_DT, epi='mul', extra=slope,
                         tm=2048, tn=512, name=f"mlp_down_dx_{l}")
        else:
            du, landed[l + 2] = _matmul(dh, full["w_mlp_down"], trans_b=True, out_dtype=MXU_DT, epi='mul', extra=slope,
                                        tm=2048, tn=512, name=f"mlp_down_dx_{l}", ride=(send, True))
        G["w_mlp_down"] = _matmul_tn(act, dh, tk=2048, tn=1024, name=f"mlp_down_dw_{l}")
        G["w_mlp_up"] = _matmul_tn(xn2, du, tk=1024, tn=1024, name=f"mlp_up_dw_{l}")
        dh, g = _matmul(du, full["w_mlp_up"], trans_b=True, epi='norm_bwd', norm=(h1, mlp_norm[l], dh),
                        tm=512, tn=D_MODEL, name=f"mlp_up_dx_{l}")
        g_mlp_norm[l] = g[0]
        if l % 2 == 0:
            e = l // 2
            dh, g_attn_norm[l], G["w_in_ab"], G["w_out_ab"], dbias, g_sinks[e], landed_mlp = _even_bwd(
                dh, xn, ctx, full["w_in_ab"], full["w_out_ab"], bias, sink_rows[e], l, (h0, attn_norm[l], dh),
                ride=(send_of(G, 1), True) if l == 0 else None)
            if l == 0:
                landed[1] = landed_mlp
            dbias_total = dbias if dbias_total is None else dbias_total + dbias
        else:
            o = l // 2
            dh, g_attn_norm[l], G["w_down_c"], g_qn[o], G["w_uq_c"], g_kvn[o], G["w_ukv_c"], G["w_o_c"] = _mla_bwd(
                dh, xn, ctx, full["w_down_c"], q_norm_full[o], full["w_uq_c"], kv_norm_full[o],
                full["w_ukv_c"], full["w_o_c"], l, (h0, attn_norm[l], dh))
        send = send_of(G, 0 if l == 0 else l + 1)
    landed[0] = _exchange(send, True, "scatter_grads_0")
    grad_x = dh[None]
    (g_rel_bias,) = bias_vjp(dbias_total)

    big = [{}, {}, {}, {}]
    for gi in range(len(groups)):
        outs = _adamw(landed[gi], w_packs[gi], pack(Mo, gi), pack(Vo, gi), name=f"adamw_{gi}")
        shapes = [W[n].shape[1:] for n, _ in groups[gi]]
        for kind, buf in enumerate(outs):
            for (n, _), t in zip(groups[gi], _unpack_rows(buf, shapes)):
                big[kind].setdefault(n, []).append(t)
    big_out = [{n: jnp.stack(ts) for n, ts in d.items()} for d in big]

    small_g = [g_rel_bias, jnp.stack(g_attn_norm), jnp.stack(g_mlp_norm), g_final[0], jnp.stack(g_sinks),
               jnp.stack(g_qn), jnp.stack(g_kvn), loss_row[0, :1]]
    small_w = [rel_bias, attn_norm, mlp_norm, final_norm, sinks, q_norm_c, kv_norm_c, jnp.zeros((1,), F32)]
    small_m = [m_rel_bias, m_attn_norm, m_mlp_norm, m_final_norm, m_sinks, m_q_norm_c, m_kv_norm_c, jnp.zeros((1,), F32)]
    small_v = [v_rel_bias, v_attn_norm, v_mlp_norm, v_final_norm, v_sinks, v_q_norm_c, v_kv_norm_c, jnp.ones((1,), F32)]
    offs = np.cumsum([0] + [-(-a.size // 1024) * 8 for a in small_g])
    partials = _exchange(jnp.concatenate([_pad_rows8(a) for a in small_g], axis=0), False, "gather_small_grads")

    def mine(i, a_full_shape, local):
        p = partials[:, offs[i]:offs[i + 1]].reshape(N_DEV, -1)[:, :math.prod(a_full_shape)]
        p = p.reshape((N_DEV,) + tuple(a_full_shape))
        if local.shape != tuple(a_full_shape):
            width = local.shape[-1]
            p = lax.dynamic_slice_in_dim(p, me * width, width, axis=p.ndim - 1)
        return jnp.stack([_pad_rows8(p[s]) for s in range(N_DEV)])

    parts_small = jnp.concatenate([mine(i, g.shape, w) for i, (g, w) in enumerate(zip(small_g, small_w))], axis=1)
    pk = lambda arrs: jnp.concatenate([_pad_rows8(a) for a in arrs], axis=0)
    small_out = _adamw(parts_small, pk(small_w), pk(small_m), pk(small_v), name="adamw_small", tr=parts_small.shape[1])
    offs2 = np.cumsum([0] + [-(-a.size // 1024) * 8 for a in small_w])

    def unpack_small(buf):
        return [buf[offs2[i]:offs2[i + 1]].reshape(-1)[:a.size].reshape(a.shape) for i, a in enumerate(small_w)]

    sg, sd, sm, sv = (unpack_small(b) for b in small_out)
    loss = sg[7][0]

    order = ['rel_bias', 'attn_norm', 'mlp_norm', 'final_norm', 'w_in_ab', 'sinks', 'w_out_ab', 'w_down_c', 'q_norm_c',
             'w_uq_c', 'kv_norm_c', 'w_ukv_c', 'w_o_c', 'w_mlp_up', 'w_mlp_down']
    small_idx = {'rel_bias': 0, 'attn_norm': 1, 'mlp_norm': 2, 'final_norm': 3, 'sinks': 4, 'q_norm_c': 5, 'kv_norm_c': 6}

    def pick(kind):
        res = []
        for n in order:
            if n in small_idx:
                res.append((sg, sd, sm, sv)[kind][small_idx[n]])
            else:
                res.append(big_out[kind][n])
        return res

    return (loss, grad_x, *pick(0), *pick(1), *pick(2), *pick(3))
```

```python
import math

import numpy as np
import jax
import jax.numpy as jnp
from jax import lax
from jax.experimental import pallas as pl
from jax.experimental.pallas import tpu as pltpu

F32 = jnp.float32
MXU_DT = jnp.bfloat16

N_DEV = 8
D_MODEL = 1024
DEPTH = 4
HEAD_DIM = 64
BLOCK = 128
EPS = 1e-6
NEG = -1e30
A_Q_HEADS = 8
A_KV_HEADS = 2
A_GROUP = A_Q_HEADS // A_KV_HEADS
A_WINDOW = 128
B_BRANCHES = ((128, 1), (512, 4), (2048, 16))
B_HPB = 4
B_HEADS = len(B_BRANCHES) * B_HPB
NUM_BUCKETS = 32
MAX_DISTANCE = 2048
N_BIAS_HEADS = A_Q_HEADS + B_HEADS
N_BAND_KV = A_KV_HEADS + B_HEADS
A_IN = (A_Q_HEADS + 2 * A_KV_HEADS) * HEAD_DIM
C_HEADS = 8
C_NOPE = 64
C_ROPE = 32
C_QK = C_NOPE + C_ROPE
C_V = 64
C_Q_RANK = 384
C_KV_RANK = 256
C_DOWN = C_Q_RANK + C_KV_RANK + C_ROPE
C_DOWN_PAD = 768
ROPE_THETA = 10000.0
N_CHUNKS = 16
FLASH_T = 512

ADAM_LR = 0.001
ADAM_B1 = 0.9
ADAM_B2 = 0.999
ADAM_EPS = 1e-08
ADAM_WD = 0.01
ADAM_STEP = 10

V7X_VMEM_BYTES = 64 * 1024 * 1024
VMEM_LIMIT = V7X_VMEM_BYTES - 8 * 1024 * 1024


def _pcall(body, **kw):
    return pl.pallas_call(body, **kw)


def _cparams(*sem):
    return pltpu.CompilerParams(dimension_semantics=sem, vmem_limit_bytes=VMEM_LIMIT)


def _exchange(src, all_to_all, name):
    def body(src_ref, out_ref, send_sems, recv_sems, local_sem):
        copies = _exchange_copies(src_ref, out_ref, send_sems, recv_sems, local_sem, all_to_all)
        for cp in copies:
            cp.start()
        _exchange_wait(copies)

    return _pcall(
        body, name=name,
        out_shape=_exchange_out(src),
        in_specs=[pl.BlockSpec(memory_space=pl.ANY)],
        out_specs=pl.BlockSpec(memory_space=pl.ANY),
        scratch_shapes=_exchange_sems(),
    )(src)


def _exchange_out(src):
    return jax.ShapeDtypeStruct((N_DEV,) + src.shape[-2:], src.dtype)


def _exchange_sems():
    return [pltpu.SemaphoreType.DMA((N_DEV - 1,)), pltpu.SemaphoreType.DMA((N_DEV - 1,)), pltpu.SemaphoreType.DMA]


def _exchange_copies(src_ref, out_ref, send_sems, recv_sems, local_sem, all_to_all):
    x, y, c = lax.axis_index("x"), lax.axis_index("y"), lax.axis_index("c")
    me = 4 * x + 2 * y + c

    def piece(dev):
        return src_ref.at[dev] if all_to_all else src_ref

    copies = [pltpu.make_async_copy(piece(me), out_ref.at[me], local_sem)]
    for k in range(1, N_DEV):
        px = 1 - x if (k >> 2) & 1 else x
        py = 1 - y if (k >> 1) & 1 else y
        pc = 1 - c if k & 1 else c
        copies.append(pltpu.make_async_remote_copy(
            src_ref=piece(4 * px + 2 * py + pc), dst_ref=out_ref.at[me],
            send_sem=send_sems.at[k - 1], recv_sem=recv_sems.at[k - 1],
            device_id=(px, py, pc), device_id_type=pl.DeviceIdType.MESH))
    return copies


def _exchange_wait(copies):
    for cp in copies[1:]:
        cp.wait()
    copies[0].wait()


def _matmul(a, b, *, trans_b=False, out_dtype=F32, epi=None, extra=None, norm=None, into=None, tm=512, tn=512,
            name, ride=None):
    M, K = a.shape
    N = b.shape[0] if trans_b else b.shape[1]
    tm, tn = min(tm, M), min(tn, N)
    assert M % tm == 0 and N % tn == 0 and (b.shape[1] if trans_b else b.shape[0]) == K
    assert (epi == 'norm_bwd') == (norm is not None) and (norm is None or tn == N)
    dn = (((1,), (1,)), ((), ())) if trans_b else (((1,), (0,)), ((), ()))
    n_i, n_j = M // tm, N // tn

    def body(*refs):
        it = iter(refs)
        a_ref, b_ref = next(it), next(it)
        e_ref = next(it) if extra is not None else None
        x_ref, g_ref, dres_ref = (next(it), next(it), next(it)) if norm is not None else (None, None, None)
        if into is not None:
            next(it)
        src_ref = next(it) if ride is not None else None
        o_ref = next(it)
        slope_ref = next(it) if epi == 'relu2' else None
        dg_ref = next(it) if norm is not None else None
        i, j = pl.program_id(0), pl.program_id(1)
        if ride is not None:
            land_ref = next(it)
            copies = _exchange_copies(src_ref, land_ref, *it, ride[1])

            @pl.when((i == 0) & (j == 0))
            def _():
                for cp in copies:
                    cp.start()

        acc = lax.dot_general(a_ref[...].astype(MXU_DT), b_ref[...].astype(MXU_DT), dn,
                              preferred_element_type=F32)
        if epi == 'relu2':
            r = jnp.maximum(acc, 0.0)
            acc = r * r
            slope_ref[...] = (2.0 * r).astype(slope_ref.dtype)
        elif epi == 'add':
            acc = acc + e_ref[...].astype(F32)
        elif epi == 'mul':
            acc = acc * e_ref[...].astype(F32)
        elif epi == 'norm_bwd':
            @pl.when(i == 0)
            def _():
                dg_ref[...] = jnp.zeros_like(dg_ref)

            xf = x_ref[...]
            r = lax.rsqrt(jnp.mean(xf * xf, axis=-1, keepdims=True) + EPS)
            xhat = xf * r
            dg_ref[...] += jnp.sum(acc * xhat, axis=0, keepdims=True)
            dyg = acc * g_ref[...]
            acc = r * (dyg - xhat * jnp.mean(dyg * xhat, axis=-1, keepdims=True)) + dres_ref[...]
        o_ref[...] = acc.astype(out_dtype)

        if ride is not None:
            @pl.when((i == n_i - 1) & (j == n_j - 1))
            def _():
                _exchange_wait(copies)

    b_spec = pl.BlockSpec((tn, K), lambda i, j: (j, 0)) if trans_b else pl.BlockSpec((K, tn), lambda i, j: (0, j))
    tile = pl.BlockSpec((tm, tn), lambda i, j: (i, j))
    in_specs = [pl.BlockSpec((tm, K), lambda i, j: (i, 0)), b_spec]
    args = [a, b]
    out_shape, out_specs, aliases = [jax.ShapeDtypeStruct((M, N), out_dtype)], [tile], {}
    if extra is not None:
        in_specs.append(tile)
        args.append(extra)
    if epi == 'relu2':
        out_shape.append(jax.ShapeDtypeStruct((M, N), out_dtype))
        out_specs.append(tile)
    if norm is not None:
        vec = pl.BlockSpec((1, N), lambda i, j: (0, 0))
        in_specs += [tile, vec, tile]
        args += [norm[0], norm[1].reshape(1, N), norm[2]]
        out_shape.append(jax.ShapeDtypeStruct((1, N), F32))
        out_specs.append(vec)
    if into is not None:
        buf, col0 = into
        assert col0 % tn == 0 and buf.shape[0] == M and buf.dtype == out_dtype
        aliases = {len(args): 0}
        in_specs.append(pl.BlockSpec(memory_space=pl.ANY))
        args.append(buf)
        out_shape[0] = jax.ShapeDtypeStruct(buf.shape, out_dtype)
        out_specs[0] = pl.BlockSpec((tm, tn), lambda i, j: (i, j + col0 // tn))
    scratch = []
    if ride is not None:
        in_specs.append(pl.BlockSpec(memory_space=pl.ANY))
        args.append(ride[0])
        out_shape.append(_exchange_out(ride[0]))
        out_specs.append(pl.BlockSpec(memory_space=pl.ANY))
        scratch = _exchange_sems()
    ordered = ride is not None or norm is not None
    res = _pcall(
        body, name=name, grid=(n_i, n_j), out_shape=tuple(out_shape), in_specs=in_specs, out_specs=tuple(out_specs),
        scratch_shapes=scratch, input_output_aliases=aliases,
        compiler_params=_cparams(*(("arbitrary", "arbitrary") if ordered else ("parallel", "parallel"))),
    )(*args)
    return res[0] if len(res) == 1 else res


def _matmul_tn(a, b, *, tk=512, tn=512, tm=1024, name):
    M, Ka = a.shape
    N = b.shape[1]
    tk, tn, tm = min(tk, Ka), min(tn, N), min(tm, M)
    assert Ka % tk == 0 and N % tn == 0 and M % tm == 0 and b.shape[0] == M

    def body(a_ref, b_ref, o_ref):
        @pl.when(pl.program_id(2) == 0)
        def _():
            o_ref[...] = jnp.zeros_like(o_ref)

        o_ref[...] += lax.dot_general(a_ref[...].astype(MXU_DT), b_ref[...].astype(MXU_DT),
                                      (((0,), (0,)), ((), ())), preferred_element_type=F32)

    return _pcall(
        body, name=name, grid=(Ka // tk, N // tn, M // tm),
        out_shape=jax.ShapeDtypeStruct((Ka, N), F32),
        in_specs=[pl.BlockSpec((tm, tk), lambda i, j, r: (r, i)), pl.BlockSpec((tm, tn), lambda i, j, r: (r, j))],
        out_specs=pl.BlockSpec((tk, tn), lambda i, j, r: (i, j)),
        compiler_params=_cparams("parallel", "parallel", "arbitrary"),
    )(a, b)


def _rmsnorm(x, g, *, out_dtype, name, tr=512):
    S, D = x.shape
    tr = min(tr, S)

    def body(x_ref, g_ref, o_ref):
        xf = x_ref[...].astype(F32)
        r = lax.rsqrt(jnp.mean(xf * xf, axis=-1, keepdims=True) + EPS)
        o_ref[...] = (xf * r * g_ref[...]).astype(out_dtype)

    return _pcall(
        body, name=name, grid=(S // tr,),
        out_shape=jax.ShapeDtypeStruct((S, D), out_dtype),
        in_specs=[pl.BlockSpec((tr, D), lambda i: (i, 0)), pl.BlockSpec((1, D), lambda i: (0, 0))],
        out_specs=pl.BlockSpec((tr, D), lambda i: (i, 0)),
        compiler_params=_cparams("parallel"),
    )(x, g.reshape(1, D))


def _rmsnorm_bwd(x, g, dy, dres, *, name, tr=512):
    S, D = x.shape
    tr = min(tr, S)

    def body(*refs):
        if dres is None:
            x_ref, g_ref, dy_ref, dx_ref, dg_ref = refs
        else:
            x_ref, g_ref, dy_ref, dres_ref, dx_ref, dg_ref = refs

        @pl.when(pl.program_id(0) == 0)
        def _():
            dg_ref[...] = jnp.zeros_like(dg_ref)

        xf = x_ref[...].astype(F32)
        r = lax.rsqrt(jnp.mean(xf * xf, axis=-1, keepdims=True) + EPS)
        xhat = xf * r
        dyf = dy_ref[...].astype(F32)
        dg_ref[...] += jnp.sum(dyf * xhat, axis=0, keepdims=True)
        dyg = dyf * g_ref[...]
        dx = r * (dyg - xhat * jnp.mean(dyg * xhat, axis=-1, keepdims=True))
        if dres is not None:
            dx = dx + dres_ref[...]
        dx_ref[...] = dx

    row = pl.BlockSpec((tr, D), lambda i: (i, 0))
    vec = pl.BlockSpec((1, D), lambda i: (0, 0))
    args = [x, g.reshape(1, D), dy] + ([] if dres is None else [dres])
    return _pcall(
        body, name=name, grid=(S // tr,),
        out_shape=(jax.ShapeDtypeStruct((S, D), F32), jax.ShapeDtypeStruct((1, D), F32)),
        in_specs=[row, vec, row] + ([] if dres is None else [row]),
        out_specs=(row, vec),
        compiler_params=_cparams("arbitrary"),
    )(*args)


def _loss_head(h, t, g, *, tr=512):
    S, D = h.shape
    tr = min(tr, S)

    def body(h_ref, t_ref, g_ref, loss_ref, dh_ref, dg_ref):
        @pl.when(pl.program_id(0) == 0)
        def _():
            dg_ref[...] = jnp.zeros_like(dg_ref)
            loss_ref[...] = jnp.zeros_like(loss_ref)

        xf = h_ref[...]
        r = lax.rsqrt(jnp.mean(xf * xf, axis=-1, keepdims=True) + EPS)
        xhat = xf * r
        e = xhat * g_ref[...] - t_ref[...]
        part = 0.5 * jnp.sum(jnp.mean(e * e, axis=-1, keepdims=True), axis=0, keepdims=True)
        loss_ref[...] += jnp.broadcast_to(part, loss_ref.shape)
        dy = e * (1.0 / D)
        dg_ref[...] += jnp.sum(dy * xhat, axis=0, keepdims=True)
        dyg = dy * g_ref[...]
        dh_ref[...] = r * (dyg - xhat * jnp.mean(dyg * xhat, axis=-1, keepdims=True))

    row = pl.BlockSpec((tr, D), lambda i: (i, 0))
    vec = pl.BlockSpec((1, D), lambda i: (0, 0))
    return _pcall(
        body, name="loss_head", grid=(S // tr,),
        out_shape=(jax.ShapeDtypeStruct((1, 128), F32), jax.ShapeDtypeStruct((S, D), F32),
                   jax.ShapeDtypeStruct((1, D), F32)),
        in_specs=[row, row, vec],
        out_specs=(pl.BlockSpec((1, 128), lambda i: (0, 0)), row, vec),
        compiler_params=_cparams("arbitrary"),
    )(h, t, g.reshape(1, D))


N_PAIRS = N_BIAS_HEADS // 2
A_PAIRS = A_Q_HEADS // 2
PAIR_W = 2 * HEAD_DIM
assert PAIR_W == 128 and A_KV_HEADS * HEAD_DIM == PAIR_W and B_HPB * HEAD_DIM == 2 * PAIR_W


def _pair_period(p):
    return jnp.where(p < A_PAIRS + 2, 16, jnp.where(p < A_PAIRS + 4, 4, 1))


def _pair_cols(p):
    b = jnp.maximum(p - A_PAIRS, 0)
    base, pp = 6 + 6 * (b // 2), b % 2
    is_a = p < A_PAIRS
    return (jnp.where(is_a, p, base + pp), jnp.where(is_a, A_PAIRS, base + 2 + pp),
            jnp.where(is_a, A_PAIRS + 1, base + 4 + pp))


def _band_specs(S):
    ch = S // N_CHUNKS
    nb = ch // BLOCK
    col = lambda i: (lambda p, c: (c, _pair_cols(p)[i]))
    prev = lambda i: (lambda p, c: (jnp.maximum(c * nb - 1, 0), _pair_cols(p)[i]))
    qkv = [pl.BlockSpec((ch, PAIR_W), col(0)), pl.BlockSpec((ch, PAIR_W), col(1)), pl.BlockSpec((BLOCK, PAIR_W), prev(1)),
           pl.BlockSpec((ch, PAIR_W), col(2)), pl.BlockSpec((BLOCK, PAIR_W), prev(2))]
    out_spec = pl.BlockSpec((ch, PAIR_W), lambda p, c: (c, p))
    sink_spec = pl.BlockSpec((2, 1, 128), lambda p, c: (p, 0, 0))
    row_spec = pl.BlockSpec((1, 1, nb, 2, BLOCK), lambda p, c: (p, c, 0, 0, 0))
    return ch, nb, qkv, out_spec, sink_spec, row_spec


def _pair_kv(p, e, ref):
    half = jnp.where(p < A_PAIRS, p // (A_GROUP // 2), e)
    return jnp.where(half == 0, ref[:, :HEAD_DIM], ref[:, HEAD_DIM:])


def _eye():
    return lax.broadcasted_iota(jnp.int32, (BLOCK, BLOCK), 0) == lax.broadcasted_iota(jnp.int32, (BLOCK, BLOCK), 1)


_NT = (((1,), (1,)), ((), ()))
_NN = (((1,), (0,)), ((), ()))
_TN = (((0,), (0,)), ((), ()))


_B_NT = (((2,), (2,)), ((0,), (0,)))
_B_NN = (((2,), (1,)), ((0,), (0,)))


def _bdot(a, b, dn):
    return lax.dot_general(a, b, dn, preferred_element_type=F32)


def _with_prev(first, t3):
    return first[None] if t3.shape[0] == 1 else jnp.concatenate([first[None], t3[:-1]], axis=0)


def _mask_first(s_prev, prev_ok):
    s0 = jnp.where(prev_ok, s_prev[0], NEG)[None]
    return s0 if s_prev.shape[0] == 1 else jnp.concatenate([s0, s_prev[1:]], axis=0)


def _banded_fwd(proj, bias, sinks):
    S = proj.shape[0]
    dh = HEAD_DIM
    ch, nb, qkv, out_spec, sink_spec, row_spec = _band_specs(S)
    bias_spec = pl.BlockSpec((2, BLOCK, 2 * BLOCK), lambda p, c: (p, 0, 0))
    scale = HEAD_DIM ** -0.5

    def body(q_ref, k_ref, kp_ref, v_ref, vp_ref, b_ref, s_ref, o_ref, lse_ref):
        p, c = pl.program_id(0), pl.program_id(1)
        prev_ok = (c % _pair_period(p)) != 0
        for e in range(2):
            lanes = slice(e * dh, (e + 1) * dh)
            q3 = q_ref[:, lanes].reshape(nb, BLOCK, dh)
            k3, v3 = (_pair_kv(p, e, r).reshape(nb, BLOCK, dh) for r in (k_ref, v_ref))
            kp3, vp3 = _with_prev(_pair_kv(p, e, kp_ref), k3), _with_prev(_pair_kv(p, e, vp_ref), v3)
            sink = s_ref[e, :, :1]
            s_cur = _bdot(q3, k3, _B_NT) * scale + b_ref[e, :, BLOCK:][None]
            s_prev = _mask_first(_bdot(q3, kp3, _B_NT) * scale + b_ref[e, :, :BLOCK][None], prev_ok)
            m = jnp.maximum(jnp.max(s_cur, axis=-1, keepdims=True), jnp.max(s_prev, axis=-1, keepdims=True))
            m = jnp.maximum(m, sink)
            p_cur = jnp.exp(s_cur - m)
            p_prev = jnp.exp(s_prev - m)
            l = jnp.sum(p_cur, axis=-1, keepdims=True) + jnp.sum(p_prev, axis=-1, keepdims=True) + jnp.exp(sink - m)
            acc = _bdot(p_cur.astype(MXU_DT), v3, _B_NN) + _bdot(p_prev.astype(MXU_DT), vp3, _B_NN)
            o_ref[:, lanes] = (acc / l).reshape(ch, dh)
            lse = m + jnp.log(l)
            lse_ref[0, 0, :, e:e + 1, :] = jnp.sum(jnp.where(_eye()[None], lse, 0.0), axis=1, keepdims=True)

    return _pcall(
        body, name="banded_fwd", grid=(N_PAIRS, N_CHUNKS),
        out_shape=(jax.ShapeDtypeStruct((S, N_PAIRS * PAIR_W), F32),
                   jax.ShapeDtypeStruct((N_PAIRS, N_CHUNKS, nb, 2, BLOCK), F32)),
        in_specs=qkv + [bias_spec, sink_spec],
        out_specs=(out_spec, row_spec),
        compiler_params=_cparams("parallel", "parallel"),
    )(proj, proj, proj, proj, proj, bias, sinks)


def _banded_bwd(proj, bias_t, sinks, do, dlse):
    S = proj.shape[0]
    dh = HEAD_DIM
    ch, nb, qkv, out_spec, sink_spec, row_spec = _band_specs(S)
    bias_spec = pl.BlockSpec((2, 2 * BLOCK, BLOCK), lambda p, c: (p, 0, 0))
    scale = HEAD_DIM ** -0.5

    def body(q_ref, k_ref, kp_ref, v_ref, vp_ref, b_ref, s_ref, do_ref, dl_ref,
             dq_ref, dk_ref, dv_ref, dkh_ref, dvh_ref, db_ref, ds_ref):
        p, c = pl.program_id(0), pl.program_id(1)

        @pl.when(c == 0)
        def _():
            db_ref[...] = jnp.zeros_like(db_ref)
            ds_ref[...] = jnp.zeros_like(ds_ref)

        prev_ok = (c % _pair_period(p)) != 0
        for e in range(2):
            lanes = slice(e * dh, (e + 1) * dh)
            q3 = q_ref[:, lanes].reshape(nb, BLOCK, dh)
            k3, v3 = (_pair_kv(p, e, r).reshape(nb, BLOCK, dh) for r in (k_ref, v_ref))
            kp3, vp3 = _with_prev(_pair_kv(p, e, kp_ref), k3), _with_prev(_pair_kv(p, e, vp_ref), v3)
            do3 = do_ref[:, lanes].astype(MXU_DT).reshape(nb, BLOCK, dh)
            sink = s_ref[e, :, :1]
            s_cur = _bdot(k3, q3, _B_NT) * scale + b_ref[e, BLOCK:, :][None]
            s_prev = _mask_first(_bdot(kp3, q3, _B_NT) * scale + b_ref[e, :BLOCK, :][None], prev_ok)
            m = jnp.maximum(jnp.max(s_cur, axis=1, keepdims=True), jnp.max(s_prev, axis=1, keepdims=True))
            m = jnp.maximum(m, sink)
            p_cur = jnp.exp(s_cur - m)
            p_prev = jnp.exp(s_prev - m)
            p_sink = jnp.exp(sink - m)
            inv = 1.0 / (jnp.sum(p_cur, axis=1, keepdims=True) + jnp.sum(p_prev, axis=1, keepdims=True) + p_sink)
            p_cur, p_prev, p_sink = p_cur * inv, p_prev * inv, p_sink * inv
            dp_cur = _bdot(v3, do3, _B_NT)
            dp_prev = _bdot(vp3, do3, _B_NT)
            delta = jnp.sum(p_cur * dp_cur, axis=1, keepdims=True) + jnp.sum(p_prev * dp_prev, axis=1, keepdims=True)
            t = dl_ref[0, 0, :, e:e + 1, :] - delta
            ds_cur = p_cur * (dp_cur + t)
            ds_prev = p_prev * (dp_prev + t)
            dsink = jnp.sum(jnp.sum(p_sink * t, axis=0), axis=-1, keepdims=True)
            ds_ref[e] += jnp.broadcast_to(dsink, (1, 128))
            db_ref[e, :BLOCK, :] += jnp.sum(ds_prev, axis=0)
            db_ref[e, BLOCK:, :] += jnp.sum(ds_cur, axis=0)
            dsb_cur = (ds_cur * scale).astype(MXU_DT)
            dsb_prev = (ds_prev * scale).astype(MXU_DT)
            dk_prev = _bdot(dsb_prev, q3, _B_NN)
            dv_prev = _bdot(p_prev.astype(MXU_DT), do3, _B_NN)

            def shifted(t3):
                z = jnp.zeros((1, BLOCK, dh), F32)
                return z if nb == 1 else jnp.concatenate([t3[1:], z], axis=0)

            dk_ref[:, lanes] = (_bdot(dsb_cur, q3, _B_NN) + shifted(dk_prev)).reshape(ch, dh)
            dv_ref[:, lanes] = (_bdot(p_cur.astype(MXU_DT), do3, _B_NN) + shifted(dv_prev)).reshape(ch, dh)
            dkh_ref[0, 0, :, lanes] = dk_prev[0]
            dvh_ref[0, 0, :, lanes] = dv_prev[0]
            for b in range(nb):
                dq_ref[b * BLOCK:(b + 1) * BLOCK, lanes] = (
                    lax.dot_general(dsb_cur[b], k3[b], _TN, preferred_element_type=F32)
                    + lax.dot_general(dsb_prev[b], kp3[b], _TN, preferred_element_type=F32))

    halo_spec = pl.BlockSpec((1, 1, BLOCK, PAIR_W), lambda p, c: (p, c, 0, 0))
    big = jax.ShapeDtypeStruct((S, N_PAIRS * PAIR_W), F32)
    halo = jax.ShapeDtypeStruct((N_PAIRS, N_CHUNKS, BLOCK, PAIR_W), F32)
    return _pcall(
        body, name="banded_bwd", grid=(N_PAIRS, N_CHUNKS),
        out_shape=(big, big, big, halo, halo, jax.ShapeDtypeStruct(bias_t.shape, F32),
                   jax.ShapeDtypeStruct(sinks.shape, F32)),
        in_specs=qkv + [bias_spec, sink_spec, out_spec, row_spec],
        out_specs=(out_spec, out_spec, out_spec, halo_spec, halo_spec, bias_spec, sink_spec),
        compiler_params=_cparams("arbitrary", "arbitrary"),
    )(proj, proj, proj, proj, proj, bias_t, sinks, do, dlse)


def _halo_fold(t, halo, name):
    S, width = t.shape
    nb = S // N_CHUNKS // BLOCK

    def body(t_ref, h_ref, o_ref):
        o_ref[:N_CHUNKS - 1, 0] = t_ref[:N_CHUNKS - 1, 0] + h_ref[0, 1:]
        o_ref[N_CHUNKS - 1:, 0] = t_ref[N_CHUNKS - 1:, 0]

    blk = pl.BlockSpec((N_CHUNKS, 1, BLOCK, PAIR_W), lambda p: (0, nb - 1, 0, p))
    return _pcall(
        body, name=name, grid=(N_PAIRS,),
        out_shape=jax.ShapeDtypeStruct((N_CHUNKS, nb, BLOCK, width), t.dtype),
        in_specs=[blk, pl.BlockSpec((1, N_CHUNKS, BLOCK, PAIR_W), lambda p: (p, 0, 0, 0))],
        out_specs=blk, input_output_aliases={0: 0},
        compiler_params=_cparams("parallel"),
    )(t.reshape(N_CHUNKS, nb, BLOCK, width), halo).reshape(S, width)


def _causal_mask(T):
    return lax.broadcasted_iota(jnp.int32, (T, T), 0) <= lax.broadcasted_iota(jnp.int32, (T, T), 1)


LOG2E = math.log2(math.e)
FLASH_SPLIT = 2
FLASH_ONES_ROWS = 16


def _mla_q_proj(cqn, wq_t, cos_t, sin_t, *, name):
    S, R = cqn.shape
    H, dqk, _ = wq_t.shape
    nq, half, T = cos_t.shape

    def body(x_ref, w_ref, c_ref, s_ref, o_ref):
        x = x_ref[...]
        for h in range(H):
            qt = lax.dot_general(w_ref[h], x, _NT, preferred_element_type=F32)
            t1, t2 = qt[C_NOPE:C_NOPE + half], qt[C_NOPE + half:]
            o_ref[h, 0, :C_NOPE] = qt[:C_NOPE].astype(MXU_DT)
            o_ref[h, 0, C_NOPE:C_NOPE + half] = (t1 * c_ref[0] - t2 * s_ref[0]).astype(MXU_DT)
            o_ref[h, 0, C_NOPE + half:] = (t1 * s_ref[0] + t2 * c_ref[0]).astype(MXU_DT)

    tab = pl.BlockSpec((1, half, T), lambda i: (i, 0, 0))
    return _pcall(
        body, name=name, grid=(nq,),
        out_shape=jax.ShapeDtypeStruct((H, nq, dqk, T), MXU_DT),
        in_specs=[pl.BlockSpec((T, R), lambda i: (i, 0)), pl.BlockSpec((H, dqk, R), lambda i: (0, 0, 0)), tab, tab],
        out_specs=pl.BlockSpec((H, 1, dqk, T), lambda i: (0, i, 0, 0)),
        compiler_params=_cparams("parallel"),
    )(cqn, wq_t, cos_t, sin_t)


def _mla_q_proj_bwd(dqt, cqn, wq_t, cos_t, sin_t, *, name):
    S, R = cqn.shape
    H, dqk, _ = wq_t.shape
    nq, half, T = cos_t.shape

    def body(g_ref, x_ref, w_ref, c_ref, s_ref, dx_ref, dw_ref):
        @pl.when(pl.program_id(0) == 0)
        def _():
            dw_ref[...] = jnp.zeros_like(dw_ref)

        x = x_ref[...]
        acc = jnp.zeros((T, R), F32)
        for h in range(H):
            g = g_ref[h, 0]
            g1, g2 = g[C_NOPE:C_NOPE + half], g[C_NOPE + half:]
            gq = jnp.concatenate([g[:C_NOPE], g1 * c_ref[0] + g2 * s_ref[0], g2 * c_ref[0] - g1 * s_ref[0]],
                                 axis=0).astype(MXU_DT)
            acc = acc + lax.dot_general(gq, w_ref[h], _TN, preferred_element_type=F32)
            dw_ref[h] += lax.dot_general(gq, x, _NN, preferred_element_type=F32)
        dx_ref[...] = acc

    tab = pl.BlockSpec((1, half, T), lambda i: (i, 0, 0))
    whole = pl.BlockSpec((H, dqk, R), lambda i: (0, 0, 0))
    return _pcall(
        body, name=name, grid=(nq,),
        out_shape=(jax.ShapeDtypeStruct((S, R), F32), jax.ShapeDtypeStruct((H, dqk, R), F32)),
        in_specs=[pl.BlockSpec((H, 1, dqk, T), lambda i: (0, i, 0, 0)), pl.BlockSpec((T, R), lambda i: (i, 0)), whole, tab, tab],
        out_specs=(pl.BlockSpec((T, R), lambda i: (i, 0)), whole),
        compiler_params=_cparams("arbitrary"),
    )(dqt, cqn, wq_t, cos_t, sin_t)


def _mla_kv_proj(ckvn, wk, wv, kr, kr_t, *, name):
    S, R = ckvn.shape
    H = wk.shape[0]
    nq, dr, T = kr_t.shape
    dqk = C_NOPE + dr
    wk_t, wv_t = wk.transpose(0, 2, 1), wv.transpose(0, 2, 1)

    def body(x_ref, wk_ref, wv_ref, wkt_ref, wvt_ref, kr_ref, krt_ref, kt_ref, vt_ref, kn_ref, vn_ref):
        x = x_ref[...]
        krt, krn = krt_ref[0].astype(MXU_DT), kr_ref[...].astype(MXU_DT)
        ones = jnp.where(lax.broadcasted_iota(jnp.int32, (FLASH_ONES_ROWS, T), 0) == 0, 1.0, 0.0).astype(MXU_DT)
        for h in range(H):
            kt_ref[h, 0, :C_NOPE] = lax.dot_general(wkt_ref[h], x, _NT, preferred_element_type=F32).astype(MXU_DT)
            kt_ref[h, 0, C_NOPE:] = krt
            vt_ref[h, 0, :C_V] = lax.dot_general(wvt_ref[h], x, _NT, preferred_element_type=F32).astype(MXU_DT)
            vt_ref[h, 0, C_V:] = ones
            kn_ref[h, :, :C_NOPE] = lax.dot_general(x, wk_ref[h], _NN, preferred_element_type=F32).astype(MXU_DT)
            kn_ref[h, :, C_NOPE:] = krn
            vn_ref[h] = lax.dot_general(x, wv_ref[h], _NN, preferred_element_type=F32).astype(MXU_DT)

    w_spec = pl.BlockSpec((H, R, C_NOPE), lambda i: (0, 0, 0))
    wt_spec = pl.BlockSpec((H, C_NOPE, R), lambda i: (0, 0, 0))
    return _pcall(
        body, name=name, grid=(nq,),
        out_shape=(jax.ShapeDtypeStruct((H, nq, dqk, T), MXU_DT),
                   jax.ShapeDtypeStruct((H, nq, C_V + FLASH_ONES_ROWS, T), MXU_DT),
                   jax.ShapeDtypeStruct((H, S, dqk), MXU_DT), jax.ShapeDtypeStruct((H, S, C_V), MXU_DT)),
        in_specs=[pl.BlockSpec((T, R), lambda i: (i, 0)), w_spec, w_spec, wt_spec, wt_spec,
                  pl.BlockSpec((T, dr), lambda i: (i, 0)), pl.BlockSpec((1, dr, T), lambda i: (i, 0, 0))],
        out_specs=(pl.BlockSpec((H, 1, dqk, T), lambda i: (0, i, 0, 0)),
                   pl.BlockSpec((H, 1, C_V + FLASH_ONES_ROWS, T), lambda i: (0, i, 0, 0)),
                   pl.BlockSpec((H, T, dqk), lambda i: (0, i, 0)), pl.BlockSpec((H, T, C_V), lambda i: (0, i, 0))),
        compiler_params=_cparams("parallel"),
    )(ckvn, wk, wv, wk_t, wv_t, kr, kr_t)


def _mla_kv_proj_bwd(dkt, dvt, ckvn, wk, wv, *, name):
    S, R = ckvn.shape
    H = wk.shape[0]
    _, nq, dqk, T = dkt.shape
    dr = dqk - C_NOPE
    wk_t, wv_t = wk.transpose(0, 2, 1), wv.transpose(0, 2, 1)

    def body(gk_ref, gv_ref, x_ref, wkt_ref, wvt_ref, dx_ref, dwk_ref, dwv_ref, dkr_ref):
        @pl.when(pl.program_id(0) == 0)
        def _():
            dwk_ref[...] = jnp.zeros_like(dwk_ref)
            dwv_ref[...] = jnp.zeros_like(dwv_ref)

        x = x_ref[...]
        acc = jnp.zeros((T, R), F32)
        dkr = jnp.zeros((dr, T), F32)
        for h in range(H):
            gk = gk_ref[h, 0, :C_NOPE].astype(MXU_DT)
            gv = gv_ref[h, 0].astype(MXU_DT)
            acc = acc + (lax.dot_general(gk, wkt_ref[h], _TN, preferred_element_type=F32)
                         + lax.dot_general(gv, wvt_ref[h], _TN, preferred_element_type=F32))
            dwk_ref[h] += lax.dot_general(gk, x, _NN, preferred_element_type=F32)
            dwv_ref[h] += lax.dot_general(gv, x, _NN, preferred_element_type=F32)
            dkr = dkr + gk_ref[h, 0, C_NOPE:]
        dx_ref[...] = acc
        dkr_ref[0] = dkr

    wt_spec = pl.BlockSpec((H, C_NOPE, R), lambda i: (0, 0, 0))
    return _pcall(
        body, name=name, grid=(nq,),
        out_shape=(jax.ShapeDtypeStruct((S, R), F32), jax.ShapeDtypeStruct((H, C_NOPE, R), F32),
                   jax.ShapeDtypeStruct((H, C_V, R), F32), jax.ShapeDtypeStruct((nq, dr, T), F32)),
        in_specs=[pl.BlockSpec((H, 1, dqk, T), lambda i: (0, i, 0, 0)), pl.BlockSpec((H, 1, C_V, T), lambda i: (0, i, 0, 0)),
                  pl.BlockSpec((T, R), lambda i: (i, 0)), wt_spec, wt_spec],
        out_specs=(pl.BlockSpec((T, R), lambda i: (i, 0)), wt_spec, wt_spec, pl.BlockSpec((1, dr, T), lambda i: (i, 0, 0))),
        compiler_params=_cparams("arbitrary"),
    )(dkt, dvt, ckvn, wk_t, wv_t)


def _mla_out_proj(ot, w_o, h, *, name):
    H, nq, dv, T = ot.shape
    D = w_o.shape[2]

    def body(o_ref, w_ref, h_ref, out_ref):
        acc = h_ref[...]
        for hd in range(H):
            acc = acc + lax.dot_general(o_ref[hd, 0].astype(MXU_DT), w_ref[hd], _TN, preferred_element_type=F32)
        out_ref[...] = acc

    row = pl.BlockSpec((T, D), lambda i: (i, 0))
    return _pcall(
        body, name=name, grid=(nq,),
        out_shape=jax.ShapeDtypeStruct(h.shape, F32),
        in_specs=[pl.BlockSpec((H, 1, dv, T), lambda i: (0, i, 0, 0)), pl.BlockSpec((H, dv, D), lambda i: (0, 0, 0)), row],
        out_specs=row,
        compiler_params=_cparams("parallel"),
    )(ot, w_o, h)


def _mla_out_proj_bwd(dh, ot, w_o, *, name):
    H, nq, dv, T = ot.shape
    D = w_o.shape[2]

    def body(dh_ref, o_ref, w_ref, dot_ref, del_ref, dw_ref):
        @pl.when(pl.program_id(0) == 0)
        def _():
            dw_ref[...] = jnp.zeros_like(dw_ref)

        dhb = dh_ref[...].astype(MXU_DT)
        for hd in range(H):
            o = o_ref[hd, 0]
            d = lax.dot_general(w_ref[hd], dhb, _NT, preferred_element_type=F32)
            dot_ref[hd, 0] = d
            del_ref[hd, 0] = jnp.sum(d * o, axis=0, keepdims=True)
            dw_ref[hd] += lax.dot_general(o.astype(MXU_DT), dhb, _NN, preferred_element_type=F32)

    tile = pl.BlockSpec((H, 1, dv, T), lambda i: (0, i, 0, 0))
    whole = pl.BlockSpec((H, dv, D), lambda i: (0, 0, 0))
    return _pcall(
        body, name=name, grid=(nq,),
        out_shape=(jax.ShapeDtypeStruct(ot.shape, F32), jax.ShapeDtypeStruct((H, nq, 1, T), F32),
                   jax.ShapeDtypeStruct(w_o.shape, F32)),
        in_specs=[pl.BlockSpec((T, D), lambda i: (i, 0)), tile, whole],
        out_specs=(tile, pl.BlockSpec((H, 1, 1, T), lambda i: (0, i, 0, 0)), whole),
        compiler_params=_cparams("arbitrary"),
    )(dh, ot, w_o)


def _flash_fwd(qt, k, vt1):
    H, nq, dqk, T = qt.shape
    S = k.shape[1]
    dva = vt1.shape[2]
    dv = dva - FLASH_ONES_ROWS
    scale = dqk ** -0.5
    c = scale * LOG2E
    th = T // FLASH_SPLIT

    def body(qt_ref, k_ref, vt_ref, ot_ref, lse_ref, sa_ref, sb_ref):
        i = pl.program_id(1)

        def scores(j):
            kb = k_ref[0, pl.ds(pl.multiple_of(j * T, T), T), :]
            return lax.dot_general(kb, qt_ref[0, 0], _NN, preferred_element_type=F32)

        def softmax_pv(s_ref, j, carry, masked):
            m, acc = carry
            raw = s_ref[...]
            if masked:
                raw = jnp.where(_causal_mask(T), raw, NEG)
            m_new = jnp.maximum(m, jnp.max(raw, axis=0, keepdims=True))
            alpha = jnp.exp2((m - m_new) * c)
            pb = jnp.exp2((raw - m_new) * c).astype(MXU_DT)
            acc = acc * alpha + lax.dot_general(vt_ref[0, j], pb, _NN, preferred_element_type=F32)
            return m_new, acc

        def pair(p, carry):
            j = 2 * p
            sb_ref[...] = scores(j + 1)
            carry = softmax_pv(sa_ref, j, carry, False)
            sa_ref[...] = scores(j + 2)
            return softmax_pv(sb_ref, j + 1, carry, False)

        def even_tail(carry):
            return softmax_pv(sa_ref, i, carry, True)

        def odd_tail(carry):
            sb_ref[...] = scores(i)
            carry = softmax_pv(sa_ref, i - 1, carry, False)
            return softmax_pv(sb_ref, i, carry, True)

        sa_ref[...] = scores(0)
        carry = lax.fori_loop(0, i // 2, pair, (jnp.full((1, T), NEG, F32), jnp.zeros((dva, T), F32)))
        m, acc = lax.cond(i % 2 == 0, even_tail, odd_tail, carry)
        l = acc[dv:dv + 1]
        ot_ref[0, 0] = acc[:dv] / l
        lse_ref[0, 0] = m * scale + jnp.log(l)

    return _pcall(
        body, name="flash_fwd", grid=(H, nq),
        out_shape=(jax.ShapeDtypeStruct((H, nq, dv, T), F32), jax.ShapeDtypeStruct((H, nq, 1, T), F32)),
        in_specs=[pl.BlockSpec((1, 1, dqk, T), lambda h, i: (h, i, 0, 0)),
                  pl.BlockSpec((1, S, dqk), lambda h, i: (h, 0, 0)),
                  pl.BlockSpec((1, nq, dva, T), lambda h, i: (h, 0, 0, 0))],
        out_specs=(pl.BlockSpec((1, 1, dv, T), lambda h, i: (h, i, 0, 0)),
                   pl.BlockSpec((1, 1, 1, T), lambda h, i: (h, i, 0, 0))),
        scratch_shapes=[pltpu.VMEM((T, T), F32), pltpu.VMEM((T, T), F32)],
        compiler_params=_cparams("parallel", "parallel"),
    )(qt, k, vt1)


def _flash_delta(ot, dot):
    H, nq, dv, T = ot.shape

    def body(o_ref, do_ref, d_ref):
        d_ref[0, 0] = jnp.sum(o_ref[0, 0] * do_ref[0, 0], axis=0, keepdims=True)

    spec = pl.BlockSpec((1, 1, dv, T), lambda h, i: (h, i, 0, 0))
    return _pcall(
        body, name="flash_delta", grid=(H, nq),
        out_shape=jax.ShapeDtypeStruct((H, nq, 1, T), F32),
        in_specs=[spec, spec], out_specs=pl.BlockSpec((1, 1, 1, T), lambda h, i: (h, i, 0, 0)),
        compiler_params=_cparams("parallel", "parallel"),
    )(ot, dot)


def _flash_bwd(qt, k, kt, v, dot, lse, delta):
    H, nq, dqk, T = qt.shape
    dv_ = v.shape[2]
    scale = dqk ** -0.5
    c = scale * LOG2E
    th = T // FLASH_SPLIT

    def body(qt_ref, k_ref, kt_ref, v_ref, dot_ref, lse_ref, del_ref, dqt_ref, dkt_ref, dvt_ref,
             sa_ref, pa_ref, sb_ref, pb_ref):
        j = pl.program_id(1)

        @pl.when(j == 0)
        def _():
            dqt_ref[...] = jnp.zeros_like(dqt_ref)

        n_un = nq - 1 - j

        def issue(i, s_ref, dp_ref):
            s_ref[...] = lax.dot_general(k_ref[0], qt_ref[0, i], _NN, preferred_element_type=F32)
            dp_ref[...] = lax.dot_general(v_ref[0], dot_ref[0, i].astype(MXU_DT), _NN, preferred_element_type=F32)

        def consume(i, s_ref, dp_ref, carry, masked):
            dkt, dvt = carry
            dob = dot_ref[0, i].astype(MXU_DT)
            lse2, dlt = lse_ref[0, i] * LOG2E, del_ref[0, i]
            dk_parts, dv_parts, dq = [], [], None
            for hk in range(FLASH_SPLIT):
                rows = slice(hk * th, (hk + 1) * th)
                raw = s_ref[rows, :]
                if masked:
                    raw = jnp.where(_causal_mask(T)[rows, :], raw, NEG)
                p = jnp.exp2(raw * c - lse2)
                dsb = (p * (dp_ref[rows, :] - dlt)).astype(MXU_DT)
                dv_parts.append(lax.dot_general(dob, p.astype(MXU_DT), _NT, preferred_element_type=F32))
                dk_parts.append(lax.dot_general(qt_ref[0, i], dsb, _NT, preferred_element_type=F32))
                part = lax.dot_general(kt_ref[0, 0, :, rows], dsb, _NN, preferred_element_type=F32)
                dq = part if dq is None else dq + part
            dqt_ref[0, i] += dq * scale
            return dkt + jnp.concatenate(dk_parts, axis=1), dvt + jnp.concatenate(dv_parts, axis=1)

        def pair(p, carry):
            i0 = j + 1 + 2 * p
            issue(i0 + 1, sb_ref, pb_ref)
            carry = consume(i0, sa_ref, pa_ref, carry, False)
            issue(jnp.where(2 * p + 2 < n_un, i0 + 2, j), sa_ref, pa_ref)
            return consume(i0 + 1, sb_ref, pb_ref, carry, False)

        def even_tail(carry):
            return consume(j, sa_ref, pa_ref, carry, True)

        def odd_tail(carry):
            issue(j, sb_ref, pb_ref)
            carry = consume(nq - 1, sa_ref, pa_ref, carry, False)
            return consume(j, sb_ref, pb_ref, carry, True)

        issue(jnp.where(n_un > 0, j + 1, j), sa_ref, pa_ref)
        carry = lax.fori_loop(0, n_un // 2, pair, (jnp.zeros((dqk, T), F32), jnp.zeros((dv_, T), F32)))
        dkt, dvt = lax.cond(n_un % 2 == 0, even_tail, odd_tail, carry)
        dkt_ref[0, 0] = dkt * scale
        dvt_ref[0, 0] = dvt

    whole = lambda d: pl.BlockSpec((1, nq, d, T), lambda h, j: (h, 0, 0, 0))
    tile_t = lambda d: pl.BlockSpec((1, 1, d, T), lambda h, j: (h, j, 0, 0))
    return _pcall(
        body, name="flash_bwd", grid=(H, nq),
        out_shape=(jax.ShapeDtypeStruct((H, nq, dqk, T), F32), jax.ShapeDtypeStruct((H, nq, dqk, T), F32),
                   jax.ShapeDtypeStruct((H, nq, dv_, T), F32)),
        in_specs=[whole(dqk),
                  pl.BlockSpec((1, T, dqk), lambda h, j: (h, j, 0)),
                  tile_t(dqk),
                  pl.BlockSpec((1, T, dv_), lambda h, j: (h, j, 0)),
                  whole(dv_), whole(1), whole(1)],
        out_specs=(whole(dqk), tile_t(dqk), tile_t(dv_)),
        scratch_shapes=[pltpu.VMEM((T, T), F32) for _ in range(4)],
        compiler_params=_cparams("arbitrary", "arbitrary"),
    )(qt, k, kt, v, dot, lse, delta)


def _adamw(parts, w, m, v, *, name, tr=512):
    P, R, C = parts.shape
    tr = min(tr, R)
    assert R % tr == 0

    def body(p_ref, w_ref, m_ref, v_ref, g_ref, d_ref, m2_ref, v2_ref):
        g = p_ref[0].astype(F32)
        for s in range(1, P):
            g = g + p_ref[s].astype(F32)
        m2 = ADAM_B1 * m_ref[...] + (1.0 - ADAM_B1) * g
        v2 = ADAM_B2 * v_ref[...] + (1.0 - ADAM_B2) * jnp.square(g)
        m_hat = m2 / (1.0 - ADAM_B1 ** ADAM_STEP)
        v_hat = v2 / (1.0 - ADAM_B2 ** ADAM_STEP)
        g_ref[...] = g
        d_ref[...] = -ADAM_LR * (m_hat / (jnp.sqrt(v_hat) + ADAM_EPS) + ADAM_WD * w_ref[...])
        m2_ref[...] = m2
        v2_ref[...] = v2

    row = pl.BlockSpec((tr, C), lambda i: (i, 0))
    out = jax.ShapeDtypeStruct((R, C), F32)
    return _pcall(
        body, name=name, grid=(R // tr,),
        out_shape=(out, out, out, out),
        in_specs=[pl.BlockSpec((P, tr, C), lambda i: (0, i, 0)), row, row, row],
        out_specs=(row, row, row, row),
        compiler_params=_cparams("parallel"),
    )(parts, w, m, v)


def _bias_tables():
    i = np.arange(BLOCK)[:, None]
    j = np.arange(2 * BLOCK)[None, :]
    dist = i + BLOCK - j
    out = []
    for dil, max_dist in [(1, A_WINDOW - 1)] + [(d, w // d) for w, d in B_BRANCHES]:
        n = np.maximum(dist, 0) * dil
        max_exact = NUM_BUCKETS // 2
        nf = np.maximum(n, 1).astype(np.float64)
        val = np.log(nf / max_exact) / math.log(MAX_DISTANCE / max_exact) * (NUM_BUCKETS - max_exact)
        inband = (dist >= 0) & (dist <= max_dist)
        frac = np.abs(val - np.round(val))
        last = NUM_BUCKETS - 1 - max_exact
        assert np.all((frac > 2e-5) | (n <= max_exact) | (val >= last) | ~inband)
        large = max_exact + val.astype(np.int64)
        bucket = np.where(n < max_exact, n, np.minimum(large, NUM_BUCKETS - 1))
        onehot = (bucket[..., None] == np.arange(NUM_BUCKETS)).astype(np.float32)
        out.append((onehot.reshape(-1, NUM_BUCKETS), inband))
    return out


def _make_bias(rel_bias):
    tabs = _bias_tables()
    groups = [(0, A_Q_HEADS)] + [(A_Q_HEADS + g * B_HPB, B_HPB) for g in range(len(B_BRANCHES))]
    parts = []
    for (onehot, inband), (h0, nh) in zip(tabs, groups):
        b = jnp.dot(jnp.asarray(onehot), rel_bias[:, h0:h0 + nh], precision=lax.Precision.HIGHEST)
        b = b.reshape(BLOCK, 2 * BLOCK, nh)
        b = jnp.where(jnp.asarray(inband)[..., None], b, NEG)
        parts.append(b.transpose(2, 0, 1))
    return jnp.concatenate(parts, axis=0)


A_W = A_Q_HEADS * HEAD_DIM
B_W = B_HPB * HEAD_DIM


def _perm_rows(x, d):
    return x if d == 1 else x.reshape(x.shape[0] // d, d, -1).transpose(1, 0, 2).reshape(x.shape)


def _unperm_rows(x, d):
    return x if d == 1 else x.reshape(d, x.shape[0] // d, -1).transpose(1, 0, 2).reshape(x.shape)


def _even_post(o_all, lse):
    S = o_all.shape[0]
    outs, lses = [], []
    for g, (_, d) in enumerate(B_BRANCHES):
        w0 = A_W + g * B_W
        outs.append(_unperm_rows(o_all[:, w0:w0 + B_W], d))
        lg = lse[A_PAIRS + 2 * g:A_PAIRS + 2 * g + 2].transpose(1, 2, 4, 0, 3).reshape(S, B_HPB)
        lses.append(_unperm_rows(lg, d))
    wts = jax.nn.softmax(jnp.stack(lses), axis=0)
    widen = jnp.asarray(np.kron(np.eye(B_HPB), np.ones((1, HEAD_DIM))), F32)
    out_b = sum(jnp.dot(wts[g], widen, precision=lax.Precision.HIGHEST) * outs[g] for g in range(len(B_BRANCHES)))
    return jnp.concatenate([o_all[:, :A_W], out_b], axis=-1)


def _rope_tables(S, r):
    inv = ROPE_THETA ** (-jnp.arange(0, r, 2, dtype=jnp.float32) / r)
    ang = jnp.arange(S, dtype=jnp.float32)[:, None] * inv[None, :]
    return jnp.cos(ang), jnp.sin(ang)


def _rope(t):
    S, r = t.shape[1], t.shape[-1]
    shape = (1, S) + (1,) * (t.ndim - 3) + (r // 2,)
    cos, sin = (a.reshape(shape) for a in _rope_tables(S, r))
    t1, t2 = t[..., :r // 2], t[..., r // 2:]
    return jnp.concatenate([t1 * cos - t2 * sin, t1 * sin + t2 * cos], axis=-1)


def _mla_pre(q_lin, kv_lin, kr_raw):
    S = q_lin.shape[0]
    q = q_lin.reshape(1, S, C_HEADS, C_QK)
    qf = jnp.concatenate([q[..., :C_NOPE], _rope(q[..., C_NOPE:])], axis=-1)[0]
    kv = kv_lin.reshape(S, C_HEADS, C_NOPE + C_V)
    kr = _rope(kr_raw[None])[0]
    kf = jnp.concatenate([kv[..., :C_NOPE], jnp.broadcast_to(kr[:, None, :], (S, C_HEADS, C_ROPE))], axis=-1)
    return qf, kf, kv[..., C_NOPE:]


def _to_tiles_t(t, T):
    S, H, d = t.shape
    return t.reshape(S // T, T, H, d).transpose(2, 0, 3, 1)


def _from_tiles_t(t):
    H, n, d, T = t.shape
    return t.transpose(1, 3, 0, 2).reshape(n * T, H, d)


_BIG = (("w_in_ab", 2), ("w_out_ab", 2), ("w_down_c", 1), ("w_uq_c", 2), ("w_ukv_c", 2), ("w_o_c", 2),
        ("w_mlp_up", 2), ("w_mlp_down", 1))
_ROW_ALIGN = 512


def _pack_rows(arrs):
    rows = [a.reshape(-1, 128) for a in arrs]
    n = sum(r.shape[0] for r in rows)
    pad = (-n) % _ROW_ALIGN
    if pad:
        rows.append(jnp.zeros((pad, 128), rows[0].dtype))
    return jnp.concatenate(rows, axis=0)


def _pack_rows_per_device(arrs):
    rows = [a.reshape(N_DEV, -1, 128) for a in arrs]
    n = sum(r.shape[1] for r in rows)
    pad = (-n) % _ROW_ALIGN
    if pad:
        rows.append(jnp.zeros((N_DEV, pad, 128), rows[0].dtype))
    return jnp.concatenate(rows, axis=1)


def _unpack_rows(buf, shapes):
    out, r0 = [], 0
    for shp in shapes:
        n = math.prod(shp) // 128
        out.append(buf[..., r0:r0 + n, :].reshape(buf.shape[:-2] + tuple(shp)))
        r0 += n
    return out


def _layer_tensors(l):
    att = [("w_in_ab", l // 2), ("w_out_ab", l // 2)] if l % 2 == 0 else \
          [("w_down_c", l // 2), ("w_uq_c", l // 2), ("w_ukv_c", l // 2), ("w_o_c", l // 2)]
    return att + [("w_mlp_up", l), ("w_mlp_down", l)]


def _exchange_groups():
    first = _layer_tensors(0)
    return [first[:-2], first[-2:]] + [_layer_tensors(l) for l in range(1, DEPTH)]


def _gathered_to_full(g, axis):
    if axis == 2:
        return g.transpose(1, 0, 2).reshape(g.shape[1], N_DEV * g.shape[2])
    return g.reshape(N_DEV * g.shape[1], g.shape[2])


def _full_to_shards(t, axis):
    a, b = t.shape
    if axis == 2:
        return t.reshape(a, N_DEV, b // N_DEV).transpose(1, 0, 2)
    return t.reshape(N_DEV, a // N_DEV, b)


def _pad_rows8(a):
    flat = a.reshape(-1)
    n = -(-flat.shape[0] // 1024) * 1024
    return jnp.pad(flat, (0, n - flat.shape[0])).reshape(-1, 128)


def _even_fwd(xn, h, w_in, w_out, bias, sinks_row, l, ride=None):
    c0 = A_IN + 3 * B_W
    proj = lax.empty((xn.shape[0], w_in.shape[1]), MXU_DT)
    proj = _matmul(xn, w_in[:, :c0], out_dtype=MXU_DT, tm=1024, tn=512, name=f"even_in_{l}", ride=ride, into=(proj, 0))
    proj, landed = proj if ride is not None else (proj, None)
    for g, (_, d) in list(enumerate(B_BRANCHES))[1:]:
        col0 = A_IN + 3 * B_W * g
        proj = _matmul(_perm_rows(xn, d), w_in[:, col0:col0 + 3 * B_W], out_dtype=MXU_DT, tm=1024, tn=3 * B_W,
                       name=f"even_in_dil{d}_{l}", into=(proj, col0))
    o_all, lse = _banded_fwd(proj, bias, sinks_row)
    attn, post_vjp = jax.vjp(_even_post, o_all, lse)
    attn = attn.astype(MXU_DT)
    h1 = _matmul(attn, w_out, epi='add', extra=h, tm=1024, tn=512, name=f"even_out_{l}")
    return h1, (proj, attn, post_vjp), landed


def _even_bwd(dh, xn, ctx, w_in, w_out, bias, sinks_row, l, norm, ride=None):
    proj, attn, post_vjp = ctx
    d_attn = _matmul(dh, w_out, trans_b=True, tm=1024, tn=768, name=f"even_out_dx_{l}")
    g_w_out = _matmul_tn(attn, dh, tk=768, tn=512, name=f"even_out_dw_{l}")
    do_all, dlse = post_vjp(d_attn)
    dq, dk, dv, dkh, dvh, dbias_t, dsinks = _banded_bwd(proj, bias.transpose(0, 2, 1), sinks_row, do_all, dlse)
    dbias = dbias_t.transpose(0, 2, 1)
    dk = _halo_fold(dk, dkh, f"halo_k_{l}")
    dv = _halo_fold(dv, dvh, f"halo_v_{l}")

    def kv_sum(t):
        heads = [t[:, i * HEAD_DIM:(i + 1) * HEAD_DIM] for i in range(A_Q_HEADS)]
        return jnp.concatenate([sum(heads[j * A_GROUP:(j + 1) * A_GROUP]) for j in range(A_KV_HEADS)], axis=1)

    groups = [jnp.concatenate([dq[:, :A_W], kv_sum(dk), kv_sum(dv)], axis=1).astype(MXU_DT)]
    for g, (_, d) in enumerate(B_BRANCHES):
        cols = slice(A_W + g * B_W, A_W + (g + 1) * B_W)
        grp = jnp.concatenate([dq[:, cols], dk[:, cols], dv[:, cols]], axis=1).astype(MXU_DT)
        groups.append(_unperm_rows(grp, d))
    dproj = jnp.concatenate(groups, axis=1)
    g_w_in = _matmul_tn(xn, dproj, tk=1024, tn=1024, name=f"even_in_dw_{l}")
    res = _matmul(dproj, w_in, trans_b=True, epi='norm_bwd', norm=norm, tm=512, tn=w_in.shape[0],
                  name=f"even_in_dx_{l}", ride=ride)
    dh_new, g_norm, landed = res if ride is not None else (*res, None)
    return dh_new, g_norm[0], g_w_in, g_w_out, dbias, dsinks[:A_Q_HEADS, 0, 0], landed


def _mla_fwd(xn, h, w_down, q_norm, w_uq, kv_norm, w_ukv, w_o, l):
    S = xn.shape[0]
    T = min(FLASH_T, S)
    down = _matmul(xn, w_down, tm=1024, tn=768, name=f"mla_down_{l}")
    c_q, c_kv, kr_raw = down[:, :C_Q_RANK], down[:, C_Q_RANK:C_Q_RANK + C_KV_RANK], down[:, C_Q_RANK + C_KV_RANK:C_DOWN]
    cqn = _rmsnorm(c_q, q_norm, out_dtype=MXU_DT, name=f"mla_qnorm_{l}")
    ckvn = _rmsnorm(c_kv, kv_norm, out_dtype=MXU_DT, name=f"mla_kvnorm_{l}")
    cos, sin = _rope_tables(S, C_ROPE)
    to_t = lambda t: t.reshape(S // T, T, -1).transpose(0, 2, 1)
    qt = _mla_q_proj(cqn, w_uq.T.reshape(C_HEADS, C_QK, C_Q_RANK), to_t(cos), to_t(sin), name=f"mla_uq_{l}")
    w_kv = w_ukv.reshape(C_KV_RANK, C_HEADS, C_NOPE + C_V).transpose(1, 0, 2)
    kr = _rope(kr_raw[None])[0]
    kt, vt1, kn, vn = _mla_kv_proj(ckvn, w_kv[..., :C_NOPE], w_kv[..., C_NOPE:], kr, to_t(kr), name=f"mla_ukv_{l}")
    ot, lse = _flash_fwd(qt, kn, vt1)
    h1 = _mla_out_proj(ot, w_o.reshape(C_HEADS, C_V, -1), h, name=f"mla_o_{l}")
    return h1, (c_q, c_kv, cqn, ckvn, qt, kn, kt, vn, ot, lse)


def _mla_bwd(dh, xn, ctx, w_down, q_norm, w_uq, kv_norm, w_ukv, w_o, l, norm):
    c_q, c_kv, cqn, ckvn, qt, kn, kt, vn, ot, lse = ctx
    S = xn.shape[0]
    T = qt.shape[-1]
    dot, delta, dw_o = _mla_out_proj_bwd(dh, ot, w_o.reshape(C_HEADS, C_V, -1), name=f"mla_o_bwd_{l}")
    g_w_o = dw_o.reshape(w_o.shape)
    dqt, dkt, dvt = _flash_bwd(qt, kn, kt, vn, dot, lse, delta)
    cos, sin = _rope_tables(S, C_ROPE)
    to_t = lambda t: t.reshape(S // T, T, -1).transpose(0, 2, 1)
    dcqn, dwq_t = _mla_q_proj_bwd(dqt, cqn, w_uq.T.reshape(C_HEADS, C_QK, C_Q_RANK), to_t(cos), to_t(sin),
                                  name=f"mla_uq_bwd_{l}")
    g_w_uq = dwq_t.reshape(C_HEADS * C_QK, C_Q_RANK).T
    w_kv = w_ukv.reshape(C_KV_RANK, C_HEADS, C_NOPE + C_V).transpose(1, 0, 2)
    dckvn, dwk_t, dwv_t, dkr_t = _mla_kv_proj_bwd(dkt, dvt, ckvn, w_kv[..., :C_NOPE], w_kv[..., C_NOPE:],
                                                  name=f"mla_ukv_bwd_{l}")
    g_w_ukv = jnp.concatenate([dwk_t, dwv_t], axis=1).reshape(C_HEADS * (C_NOPE + C_V), C_KV_RANK).T
    _, rope_vjp = jax.vjp(lambda t: _rope(t[None])[0], jnp.zeros((S, C_ROPE), F32))
    (dkr_raw,) = rope_vjp(dkr_t.transpose(0, 2, 1).reshape(S, C_ROPE))
    dc_q, g_q_norm = _rmsnorm_bwd(c_q, q_norm, dcqn, None, name=f"mla_qnorm_bwd_{l}")
    dc_kv, g_kv_norm = _rmsnorm_bwd(c_kv, kv_norm, dckvn, None, name=f"mla_kvnorm_bwd_{l}")
    ddown = jnp.concatenate([dc_q, dc_kv, dkr_raw, jnp.zeros((S, C_DOWN_PAD - C_DOWN), F32)], axis=1).astype(MXU_DT)
    g_w_down = _matmul_tn(xn, ddown, tk=512, tn=768, name=f"mla_down_dw_{l}")[:, :C_DOWN]
    dh_new, g_norm = _matmul(ddown, w_down, trans_b=True, epi='norm_bwd', norm=norm, tm=512, tn=w_down.shape[0],
                             name=f"mla_down_dx_{l}")
    return dh_new, g_norm[0], g_w_down, g_q_norm[0], g_w_uq, g_kv_norm[0], g_w_ukv, g_w_o


def kernel(x, rel_bias, attn_norm, mlp_norm, final_norm, w_in_ab, sinks, w_out_ab, w_down_c, q_norm_c, w_uq_c, kv_norm_c, w_ukv_c, w_o_c, w_mlp_up, w_mlp_down, loss_target, m_rel_bias, m_attn_norm, m_mlp_norm, m_final_norm, m_w_in_ab, m_sinks, m_w_out_ab, m_w_down_c, m_q_norm_c, m_w_uq_c, m_kv_norm_c, m_w_ukv_c, m_w_o_c, m_w_mlp_up, m_w_mlp_down, v_rel_bias, v_attn_norm, v_mlp_norm, v_final_norm, v_w_in_ab, v_sinks, v_w_out_ab, v_w_down_c, v_q_norm_c, v_w_uq_c, v_kv_norm_c, v_w_ukv_c, v_w_o_c, v_w_mlp_up, v_w_mlp_down):
    W = dict(w_in_ab=w_in_ab, w_out_ab=w_out_ab, w_down_c=w_down_c, w_uq_c=w_uq_c, w_ukv_c=w_ukv_c, w_o_c=w_o_c,
             w_mlp_up=w_mlp_up, w_mlp_down=w_mlp_down)
    Mo = dict(w_in_ab=m_w_in_ab, w_out_ab=m_w_out_ab, w_down_c=m_w_down_c, w_uq_c=m_w_uq_c, w_ukv_c=m_w_ukv_c,
              w_o_c=m_w_o_c, w_mlp_up=m_w_mlp_up, w_mlp_down=m_w_mlp_down)
    Vo = dict(w_in_ab=v_w_in_ab, w_out_ab=v_w_out_ab, w_down_c=v_w_down_c, w_uq_c=v_w_uq_c, w_ukv_c=v_w_ukv_c,
              w_o_c=v_w_o_c, w_mlp_up=v_w_mlp_up, w_mlp_down=v_w_mlp_down)
    S = x.shape[1]
    me = 4 * lax.axis_index("x") + 2 * lax.axis_index("y") + lax.axis_index("c")
    axis_of = dict(_BIG)

    groups = _exchange_groups()

    def pack(src, gi):
        return _pack_rows([src[n][i] for n, i in groups[gi]])

    def unpack_group(gathered, gi):
        shapes = [W[n].shape[1:] for n, _ in groups[gi]]
        return {n: _gathered_to_full(g, axis_of[n])
                for (n, _), g in zip(groups[gi], _unpack_rows(gathered, shapes))}

    def send_of(G, gi):
        return _pack_rows_per_device([_full_to_shards(G[n], axis_of[n]) for n, _ in groups[gi]]).astype(MXU_DT)

    w_packs = [pack(W, gi) for gi in range(len(groups))]
    gathered = _exchange(w_packs[0].astype(MXU_DT), False, "gather_weights_0")
    gains = _exchange(jnp.concatenate([_pad_rows8(q_norm_c), _pad_rows8(kv_norm_c)], axis=0), False, "gather_gains")
    n_odd = q_norm_c.shape[0]
    q_norm_full = gains[:, 0].reshape(N_DEV, -1)[:, :q_norm_c.size].reshape(N_DEV, n_odd, -1).transpose(1, 0, 2).reshape(n_odd, C_Q_RANK)
    kv_norm_full = gains[:, 8].reshape(N_DEV, -1)[:, :kv_norm_c.size].reshape(N_DEV, n_odd, -1).transpose(1, 0, 2).reshape(n_odd, C_KV_RANK)

    bias, bias_vjp = jax.vjp(_make_bias, rel_bias)
    sink_rows = [jnp.broadcast_to(jnp.concatenate([sinks[e], jnp.full((B_HEADS,), NEG, F32)])[:, None, None],
                                  (N_BIAS_HEADS, 1, 128)) for e in range(sinks.shape[0])]

    h = x[0]
    saved = []
    for l in range(DEPTH):
        full = unpack_group(gathered, 0 if l == 0 else l + 1)
        if l % 2 == 1:
            full["w_down_c"] = jnp.pad(full["w_down_c"], ((0, 0), (0, C_DOWN_PAD - C_DOWN)))
        xn = _rmsnorm(h, attn_norm[l], out_dtype=MXU_DT, name=f"attn_norm_{l}")
        if l == 0:
            h1, ctx, gathered_mlp = _even_fwd(xn, h, full["w_in_ab"], full["w_out_ab"], bias, sink_rows[0], l,
                                              ride=(w_packs[1].astype(MXU_DT), False))
            full.update(unpack_group(gathered_mlp, 1))
        elif l % 2 == 0:
            h1, ctx, _ = _even_fwd(xn, h, full["w_in_ab"], full["w_out_ab"], bias, sink_rows[l // 2], l)
        else:
            o = l // 2
            h1, ctx = _mla_fwd(xn, h, full["w_down_c"], q_norm_full[o], full["w_uq_c"], kv_norm_full[o],
                               full["w_ukv_c"], full["w_o_c"], l)
        xn2 = _rmsnorm(h1, mlp_norm[l], out_dtype=MXU_DT, name=f"mlp_norm_{l}")
        if l + 1 < DEPTH:
            act, slope, gathered = _matmul(xn2, full["w_mlp_up"], out_dtype=MXU_DT, epi='relu2', tm=2048, tn=512,
                                           name=f"mlp_up_{l}", ride=(w_packs[l + 2].astype(MXU_DT), False))
        else:
            act, slope = _matmul(xn2, full["w_mlp_up"], out_dtype=MXU_DT, epi='relu2', tm=2048, tn=512,
                                 name=f"mlp_up_{l}")
        h2 = _matmul(act, full["w_mlp_down"], epi='add', extra=h1, tm=1024, tn=512, name=f"mlp_down_{l}")
        saved.append((h, xn, h1, xn2, act, slope, ctx, full))
        h = h2

    loss_row, dh, g_final = _loss_head(h, loss_target[0], final_norm)

    g_attn_norm, g_mlp_norm = [None] * DEPTH, [None] * DEPTH
    g_sinks, g_qn, g_kvn = [None] * sinks.shape[0], [None] * n_odd, [None] * n_odd
    dbias_total = None
    landed = [None] * len(groups)
    send = None
    for l in reversed(range(DEPTH)):
        h0, xn, h1, xn2, act, slope, ctx, full = saved[l]
        G = {}
        if send is None:
            du = _matmul(dh, full["w_mlp_down"], trans_b=True, out_dtype=MXU_DT, epi='mul', extra=slope,
                         tm=2048, tn=512, name=f"mlp_down_dx_{l}")
        else:
            du, landed[l + 2] = _matmul(dh, full["w_mlp_down"], trans_b=True, out_dtype=MXU_DT, epi='mul', extra=slope,
                                        tm=2048, tn=512, name=f"mlp_down_dx_{l}", ride=(send, True))
        G["w_mlp_down"] = _matmul_tn(act, dh, tk=2048, tn=1024, name=f"mlp_down_dw_{l}")
        G["w_mlp_up"] = _matmul_tn(xn2, du, tk=1024, tn=1024, name=f"mlp_up_dw_{l}")
        dh, g = _matmul(du, full["w_mlp_up"], trans_b=True, epi='norm_bwd', norm=(h1, mlp_norm[l], dh),
                        tm=512, tn=D_MODEL, name=f"mlp_up_dx_{l}")
        g_mlp_norm[l] = g[0]
        if l % 2 == 0:
            e = l // 2
            dh, g_attn_norm[l], G["w_in_ab"], G["w_out_ab"], dbias, g_sinks[e], landed_mlp = _even_bwd(
                dh, xn, ctx, full["w_in_ab"], full["w_out_ab"], bias, sink_rows[e], l, (h0, attn_norm[l], dh),
                ride=(send_of(G, 1), True) if l == 0 else None)
            if l == 0:
                landed[1] = landed_mlp
            dbias_total = dbias if dbias_total is None else dbias_total + dbias
        else:
            o = l // 2
            dh, g_attn_norm[l], G["w_down_c"], g_qn[o], G["w_uq_c"], g_kvn[o], G["w_ukv_c"], G["w_o_c"] = _mla_bwd(
                dh, xn, ctx, full["w_down_c"], q_norm_full[o], full["w_uq_c"], kv_norm_full[o],
                full["w_ukv_c"], full["w_o_c"], l, (h0, attn_norm[l], dh))
        send = send_of(G, 0 if l == 0 else l + 1)
    landed[0] = _exchange(send, True, "scatter_grads_0")
    grad_x = dh[None]
    (g_rel_bias,) = bias_vjp(dbias_total)

    big = [{}, {}, {}, {}]
    for gi in range(len(groups)):
        outs = _adamw(landed[gi], w_packs[gi], pack(Mo, gi), pack(Vo, gi), name=f"adamw_{gi}")
        shapes = [W[n].shape[1:] for n, _ in groups[gi]]
        for kind, buf in enumerate(outs):
            for (n, _), t in zip(groups[gi], _unpack_rows(buf, shapes)):
                big[kind].setdefault(n, []).append(t)
    big_out = [{n: jnp.stack(ts) for n, ts in d.items()} for d in big]

    small_g = [g_rel_bias, jnp.stack(g_attn_norm), jnp.stack(g_mlp_norm), g_final[0], jnp.stack(g_sinks),
               jnp.stack(g_qn), jnp.stack(g_kvn), loss_row[0, :1]]
    small_w = [rel_bias, attn_norm, mlp_norm, final_norm, sinks, q_norm_c, kv_norm_c, jnp.zeros((1,), F32)]
    small_m = [m_rel_bias, m_attn_norm, m_mlp_norm, m_final_norm, m_sinks, m_q_norm_c, m_kv_norm_c, jnp.zeros((1,), F32)]
    small_v = [v_rel_bias, v_attn_norm, v_mlp_norm, v_final_norm, v_sinks, v_q_norm_c, v_kv_norm_c, jnp.ones((1,), F32)]
    offs = np.cumsum([0] + [-(-a.size // 1024) * 8 for a in small_g])
    partials = _exchange(jnp.concatenate([_pad_rows8(a) for a in small_g], axis=0), False, "gather_small_grads")

    def mine(i, a_full_shape, local):
        p = partials[:, offs[i]:offs[i + 1]].reshape(N_DEV, -1)[:, :math.prod(a_full_shape)]
        p = p.reshape((N_DEV,) + tuple(a_full_shape))
        if local.shape != tuple(a_full_shape):
            width = local.shape[-1]
            p = lax.dynamic_slice_in_dim(p, me * width, width, axis=p.ndim - 1)
        return jnp.stack([_pad_rows8(p[s]) for s in range(N_DEV)])

    parts_small = jnp.concatenate([mine(i, g.shape, w) for i, (g, w) in enumerate(zip(small_g, small_w))], axis=1)
    pk = lambda arrs: jnp.concatenate([_pad_rows8(a) for a in arrs], axis=0)
    small_out = _adamw(parts_small, pk(small_w), pk(small_m), pk(small_v), name="adamw_small", tr=parts_small.shape[1])
    offs2 = np.cumsum([0] + [-(-a.size // 1024) * 8 for a in small_w])

    def unpack_small(buf):
        return [buf[offs2[i]:offs2[i + 1]].reshape(-1)[:a.size].reshape(a.shape) for i, a in enumerate(small_w)]

    sg, sd, sm, sv = (unpack_small(b) for b in small_out)
    loss = sg[7][0]

    order = ['rel_bias', 'attn_norm', 'mlp_norm', 'final_norm', 'w_in_ab', 'sinks', 'w_out_ab', 'w_down_c', 'q_norm_c',
             'w_uq_c', 'kv_norm_c', 'w_ukv_c', 'w_o_c', 'w_mlp_up', 'w_mlp_down']
    small_idx = {'rel_bias': 0, 'attn_norm': 1, 'mlp_norm': 2, 'final_norm': 3, 'sinks': 4, 'q_norm_c': 5, 'kv_norm_c': 6}

    def pick(kind):
        res = []
        for n in order:
            if n in small_idx:
                res.append((sg, sd, sm, sv)[kind][small_idx[n]])
            else:
                res.append(big_out[kind][n])
        return res

    return (loss, grad_x, *pick(0), *pick(1), *pick(2), *pick(3))
```
